```python
import jax, jax.numpy as jnp
from jax import lax
import numpy as np

D_MODEL = 1024
BATCH = 8
SEQ = 2048
DEPTH = 1
DEC_BATCH = 128
DEC_SEQ = 4
PAST_LEN = 16384
PAGE_SIZE = 128

A_W = 512
CONV_A_K = 3
DN_HEADS = 4
DN_DK = 128
DN_DV = 128
DN_QK = DN_HEADS * DN_DK
DN_V = DN_HEADS * DN_DV
DN_CONV_CH = 2 * DN_QK + DN_V
DN_CONV_K = 4
DN_CHUNK = 64
MEM_TOKENS = 256
XA_HEADS = 4
XA_DH = 128
XA_W = XA_HEADS * XA_DH
N_BRANCH = 3
MIX_WIDTH = A_W + DN_V + XA_W
IN_WIDTH = 3 * A_W + DN_CONV_CH + DN_V + 2 * DN_HEADS + XA_W + N_BRANCH * D_MODEL
D_FF = 256 * (-(-(8 * D_MODEL) // (3 * 256)))
EPS = 1e-6

kernel_name = 'hybrid_gated_conv_deltanet_memxattn_step'


def rmsnorm(x, g):
    xf = x.astype(jnp.float32)
    y = xf * lax.rsqrt(jnp.mean(xf * xf, axis=-1, keepdims=True) + EPS)
    return (y * g.astype(jnp.float32)).astype(x.dtype)


def l2norm(x):
    xf = x.astype(jnp.float32)
    return xf * lax.rsqrt(jnp.sum(xf * xf, axis=-1, keepdims=True) + EPS)


def causal_conv(x, buf, w):
    width = w.shape[0]
    length = x.shape[1]
    xp = jnp.concatenate([buf.astype(x.dtype), x], axis=1)
    y = sum(xp[:, i:i + length] * w[i].astype(x.dtype) for i in range(width))
    return y, xp[:, length:]


def gated_delta_chunked(q, k, v, g, beta, s0):
    bsz, length = q.shape[0], q.shape[1]
    csz = min(DN_CHUNK, length)
    n = -(-length // csz)
    pad = n * csz - length

    def prep(t):
        t = jnp.pad(t, [(0, 0), (0, pad)] + [(0, 0)] * (t.ndim - 2))
        t = t.reshape((bsz, n, csz) + t.shape[2:])
        return jnp.swapaxes(jnp.moveaxis(t, 1, 0), 2, 3)

    qc, kc, vc, gc, bc = prep(q), prep(k), prep(v), prep(g), prep(beta)
    d = jnp.cumsum(gc, axis=-1)
    idx = jnp.arange(csz)
    causal = idx[:, None] >= idx[None, :]
    strict = idx[:, None] > idx[None, :]
    diff = d[..., :, None] - d[..., None, :]
    gamma = jnp.where(causal, jnp.exp(jnp.where(causal, diff, 0.0)), 0.0)
    kk = jnp.einsum('nbhid,nbhjd->nbhij', kc, kc)
    a_mat = jnp.where(strict, bc[..., :, None] * kk * gamma, 0.0)
    eye = jnp.eye(csz, dtype=a_mat.dtype)
    rhs = jnp.concatenate([vc * bc[..., None], kc * (bc * jnp.exp(d))[..., None]], axis=-1)
    sol = lax.linalg.triangular_solve(a_mat + eye, rhs, left_side=True, lower=True, unit_diagonal=True)
    u, w = sol[..., :DN_DV], sol[..., DN_DV:]
    qk = jnp.where(causal, jnp.einsum('nbhid,nbhjd->nbhij', qc, kc) * gamma, 0.0)

    def step(s, inp):
        q_i, k_i, u_i, w_i, qk_i, d_i = inp
        v_new = u_i - jnp.einsum('bhcd,bhde->bhce', w_i, s)
        o = (jnp.einsum('bhcd,bhde->bhce', q_i * jnp.exp(d_i)[..., None], s)
             + jnp.einsum('bhij,bhje->bhie', qk_i, v_new))
        d_last = d_i[..., -1:]
        s = (s * jnp.exp(d_last)[..., None]
             + jnp.einsum('bhcd,bhce->bhde', k_i * jnp.exp(d_last - d_i)[..., None], v_new))
        return s, o

    s_fin, o = lax.scan(step, s0.astype(jnp.float32), (qc, kc, u, w, qk, d))
    o = jnp.moveaxis(jnp.swapaxes(o, 2, 3), 0, 1).reshape(bsz, n * csz, DN_HEADS, DN_DV)[:, :length]
    return o, s_fin.astype(s0.dtype)


def mem_kv(mem, norm_mem_l, w_mem_kv_l):
    kv = rmsnorm(mem, norm_mem_l) @ w_mem_kv_l
    k, v = jnp.split(kv, 2, axis=-1)
    shp = mem.shape[:2] + (XA_HEADS, XA_DH)
    return k.reshape(shp), v.reshape(shp)


def mem_attention(q, mk, mv):
    s = jnp.einsum('blhd,bmhd->bhlm', q, mk.astype(q.dtype)).astype(jnp.float32) * (XA_DH ** -0.5)
    p = jax.nn.softmax(s, axis=-1)
    o = jnp.einsum('bhlm,bmhd->blhd', p.astype(q.dtype), mv.astype(q.dtype))
    return o.reshape(q.shape[0], q.shape[1], XA_W)


def trunk_layer(x, conv_a_buf, dn_conv_buf, dn_s, mk, mv, norm_mix_l, w_in_l, conv_a_w_l,
                dn_conv_w_l, a_log_l, dt_bias_l, dn_norm_l, w_branch_l, w_o_l, norm_ffn_l,
                w_up_l, w_down_l):
    bsz, length = x.shape[0], x.shape[1]
    xn = rmsnorm(x, norm_mix_l)
    proj = xn @ w_in_l
    sizes = (A_W, A_W, A_W, DN_CONV_CH, DN_V, DN_HEADS, DN_HEADS, XA_W, N_BRANCH * D_MODEL)
    offs = [int(o) for o in np.cumsum(sizes)[:-1]]
    b_gate, c_gate, h_in, qkv, z, a_raw, b_raw, xq, gates = jnp.split(proj, offs, axis=-1)
    g_a, g_dn, g_m = jnp.split(gates, N_BRANCH, axis=-1)

    conv_out, conv_a_new = causal_conv(c_gate * h_in, conv_a_buf, conv_a_w_l)
    y_a = b_gate * conv_out

    qkv_c, dn_conv_new = causal_conv(qkv, dn_conv_buf, dn_conv_w_l)
    qkv_c = jax.nn.silu(qkv_c)
    q, k, v = jnp.split(qkv_c, [DN_QK, 2 * DN_QK], axis=-1)
    q = l2norm(q.reshape(bsz, length, DN_HEADS, DN_DK)) * (DN_DK ** -0.5)
    k = l2norm(k.reshape(bsz, length, DN_HEADS, DN_DK))
    v = v.reshape(bsz, length, DN_HEADS, DN_DV).astype(jnp.float32)
    beta = jax.nn.sigmoid(b_raw.astype(jnp.float32))
    g = -jnp.exp(a_log_l.astype(jnp.float32)) * jax.nn.softplus(
        a_raw.astype(jnp.float32) + dt_bias_l.astype(jnp.float32))
    o_dn, dn_s_new = gated_delta_chunked(q, k, v, g, beta, dn_s)
    o_dn = rmsnorm(o_dn, dn_norm_l).astype(x.dtype).reshape(bsz, length, DN_V)
    y_dn = o_dn * jax.nn.silu(z)

    y_m = mem_attention(xq.reshape(bsz, length, XA_HEADS, XA_DH), mk, mv)

    merged = (jax.nn.sigmoid(g_a) * (y_a @ w_branch_l[:A_W])
              + jax.nn.sigmoid(g_dn) * (y_dn @ w_branch_l[A_W:A_W + DN_V])
              + jax.nn.sigmoid(g_m) * (y_m @ w_branch_l[A_W + DN_V:]))
    x = x + merged @ w_o_l

    gate, up = jnp.split(rmsnorm(x, norm_ffn_l) @ w_up_l, 2, axis=-1)
    x = x + (jax.nn.silu(gate) * up) @ w_down_l
    return x, conv_a_new, dn_conv_new, dn_s_new


def setup_inputs(seed: int = 0) -> dict:
    key = jax.random.key(seed)
    ks = jax.random.split(key, 24)
    f32 = jnp.float32

    def nrm(k, shape, scale):
        return jax.random.normal(k, shape, f32) * scale

    def gain(k, shape):
        return 1.0 + 0.05 * jax.random.normal(k, shape, f32)

    dt = jnp.exp(jax.random.uniform(ks[13], (DEPTH, DN_HEADS), f32, np.log(1e-3), np.log(1e-1)))
    return {
        'x_prompt': nrm(ks[0], (BATCH, SEQ, D_MODEL), 1.0),
        'x_sample': nrm(ks[1], (DEC_BATCH, DEC_SEQ, D_MODEL), 1.0),
        'mem_prompt': nrm(ks[2], (BATCH, MEM_TOKENS, D_MODEL), 1.0),
        'state_conv_a': nrm(ks[3], (DEPTH, DEC_BATCH, CONV_A_K - 1, A_W), 1.0),
        'state_dn_conv': nrm(ks[4], (DEPTH, DEC_BATCH, DN_CONV_K - 1, DN_CONV_CH), 1.0),
        'state_dn': nrm(ks[5], (DEPTH, DEC_BATCH, DN_HEADS, DN_DK, DN_DV), 0.1),
        'cache_mem_k': nrm(ks[6], (DEPTH, DEC_BATCH, MEM_TOKENS, XA_HEADS, XA_DH), 1.0),
        'cache_mem_v': nrm(ks[7], (DEPTH, DEC_BATCH, MEM_TOKENS, XA_HEADS, XA_DH), 1.0),
        'norm_mix': gain(ks[8], (DEPTH, D_MODEL)),
        'w_in': nrm(ks[9], (DEPTH, D_MODEL, IN_WIDTH), D_MODEL ** -0.5),
        'conv_a_w': nrm(ks[10], (DEPTH, CONV_A_K, A_W), CONV_A_K ** -0.5),
        'dn_conv_w': nrm(ks[11], (DEPTH, DN_CONV_K, DN_CONV_CH), DN_CONV_K ** -0.5),
        'dn_a_log': jnp.log(jax.random.uniform(ks[12], (DEPTH, DN_HEADS), f32, 1.0, 16.0)),
        'dn_dt_bias': dt + jnp.log(-jnp.expm1(-dt)),
        'dn_norm': gain(ks[14], (DEPTH, DN_DV)),
        'norm_mem': gain(ks[15], (DEPTH, D_MODEL)),
        'w_mem_kv': nrm(ks[16], (DEPTH, D_MODEL, 2 * XA_W), D_MODEL ** -0.5),
        'w_branch': nrm(ks[17], (DEPTH, MIX_WIDTH, D_MODEL), A_W ** -0.5),
        'w_o': nrm(ks[18], (DEPTH, D_MODEL, D_MODEL), D_MODEL ** -0.5),
        'norm_ffn': gain(ks[19], (DEPTH, D_MODEL)),
        'w_ffn_up': nrm(ks[20], (DEPTH, D_MODEL, 2 * D_FF), D_MODEL ** -0.5),
        'w_ffn_down': nrm(ks[21], (DEPTH, D_FF, D_MODEL), D_FF ** -0.5),
        'norm_final': gain(ks[22], (D_MODEL,)),
    }


def reference(x_prompt, x_sample, mem_prompt, state_conv_a, state_dn_conv, state_dn, cache_mem_k,
              cache_mem_v, norm_mix, w_in, conv_a_w, dn_conv_w, dn_a_log, dn_dt_bias, dn_norm,
              norm_mem, w_mem_kv, w_branch, w_o, norm_ffn, w_ffn_up, w_ffn_down, norm_final):
    bp = x_prompt.shape[0]
    hp, hs = x_prompt, x_sample
    ca_p, dc_p, s_p, mk_p, mv_p = [], [], [], [], []
    ca_s, dc_s, s_s = [], [], []
    for l in range(DEPTH):
        layer_w = (norm_mix[l], w_in[l], conv_a_w[l], dn_conv_w[l], dn_a_log[l], dn_dt_bias[l],
                   dn_norm[l], w_branch[l], w_o[l], norm_ffn[l], w_ffn_up[l], w_ffn_down[l])
        mk, mv = mem_kv(mem_prompt, norm_mem[l], w_mem_kv[l])
        zero_ca = jnp.zeros((bp, CONV_A_K - 1, A_W), x_prompt.dtype)
        zero_dc = jnp.zeros((bp, DN_CONV_K - 1, DN_CONV_CH), x_prompt.dtype)
        zero_s = jnp.zeros((bp, DN_HEADS, DN_DK, DN_DV), state_dn.dtype)
        hp, a_new, d_new, st_new = trunk_layer(hp, zero_ca, zero_dc, zero_s, mk, mv, *layer_w)
        ca_p.append(a_new); dc_p.append(d_new); s_p.append(st_new); mk_p.append(mk); mv_p.append(mv)
        hs, a_new, d_new, st_new = trunk_layer(hs, state_conv_a[l], state_dn_conv[l], state_dn[l],
                                               cache_mem_k[l], cache_mem_v[l], *layer_w)
        ca_s.append(a_new); dc_s.append(d_new); s_s.append(st_new)
    y_prompt = rmsnorm(hp, norm_final)
    y_sample = rmsnorm(hs, norm_final)
    return (y_prompt, y_sample, jnp.stack(ca_p), jnp.stack(dc_p), jnp.stack(s_p), jnp.stack(mk_p),
            jnp.stack(mv_p), jnp.stack(ca_s), jnp.stack(dc_s), jnp.stack(s_s))
```

```python
import functools

import jax
import jax.numpy as jnp
from jax import lax
from jax.experimental import pallas as pl
from jax.experimental.pallas import tpu as pltpu

F32 = jnp.float32
BF16 = jnp.bfloat16

D_MODEL = 1024
A_W = 512
CONV_A_K = 3
DN_HEADS = 4
DN_DK = 128
DN_DV = 128
DN_QK = DN_HEADS * DN_DK
DN_V = DN_HEADS * DN_DV
DN_CONV_CH = 2 * DN_QK + DN_V
DN_CONV_K = 4
MEM_TOKENS = 256
XA_HEADS = 4
XA_DH = 128
XA_W = XA_HEADS * XA_DH
D_FF = 2816
EPS = 1e-6

LANE = 128
CHUNK = 128
TAIL = 8

OFF_A = 0
OFF_QKV = 3 * A_W
OFF_Z = OFF_QKV + DN_CONV_CH
OFF_XQ = OFF_Z + DN_V
OFF_G = OFF_XQ + XA_W
OFF_AB = OFF_G + 3 * D_MODEL
NW = OFF_AB + LANE
AB_SRC = 3 * A_W + DN_CONV_CH + DN_V

VMEM_LIMIT = 56 * 1024 * 1024


def _cparams(sem):
    return pltpu.CompilerParams(dimension_semantics=sem, vmem_limit_bytes=VMEM_LIMIT)


def _mm(a, b):
    return jnp.dot(a.astype(BF16), b.astype(BF16), preferred_element_type=F32)


def _mm_nt(a, b):
    return lax.dot_general(a.astype(BF16), b.astype(BF16), (((1,), (1,)), ((), ())),
                           preferred_element_type=F32)


def _mm_tn(a, b):
    return lax.dot_general(a.astype(BF16), b.astype(BF16), (((0,), (0,)), ((), ())),
                           preferred_element_type=F32)


def _split3(x):
    hi = x.astype(BF16)
    r1 = x - hi.astype(F32)
    mid = r1.astype(BF16)
    lo = (r1 - mid.astype(F32)).astype(BF16)
    return jnp.concatenate([hi, mid, lo], axis=1)


def _mm_exact01(m01, x):
    n = x.shape[1]
    r = jnp.dot(m01.astype(BF16), _split3(x), preferred_element_type=F32)
    return (r[:, :n] + r[:, n:2 * n]) + r[:, 2 * n:]


def _rms(x, g):
    return x * lax.rsqrt(jnp.mean(x * x, axis=-1, keepdims=True) + EPS) * g


def _silu(x):
    return x * jax.nn.sigmoid(x)


def _softplus(x):
    return jnp.maximum(x, 0.0) + jnp.log1p(jnp.exp(-jnp.abs(x)))


def _proj_kernel(x_ref, g_ref, w_ref, o_ref):
    xn = _rms(x_ref[...], g_ref[...]).astype(BF16)
    o_ref[...] = jnp.dot(xn, w_ref[...], preferred_element_type=F32)


def _proj(x2d, gain, w, tm, tn):
    t, d = x2d.shape
    n = w.shape[1]
    return pl.pallas_call(
        _proj_kernel,
        grid=(n // tn, t // tm),
        in_specs=[pl.BlockSpec((tm, d), lambda j, i: (i, 0)),
                  pl.BlockSpec((1, d), lambda j, i: (0, 0)),
                  pl.BlockSpec((d, tn), lambda j, i: (0, j))],
        out_specs=pl.BlockSpec((tm, tn), lambda j, i: (i, j)),
        out_shape=jax.ShapeDtypeStruct((t, n), F32),
        compiler_params=_cparams(("arbitrary", "arbitrary")),
        name="proj",
    )(x2d, gain, w)


def _memkv_kernel(x_ref, g_ref, w_ref, k_ref, v_ref, kb_ref, vb_ref):
    xn = _rms(x_ref[...], g_ref[...]).astype(BF16)
    kv = jnp.dot(xn, w_ref[...], preferred_element_type=F32)
    k = kv[:, :XA_W]
    v = kv[:, XA_W:]
    k_ref[...] = k
    v_ref[...] = v
    kb_ref[...] = k.astype(BF16)
    vb_ref[...] = v.astype(BF16)


def _memkv(mem2d, gain, w, tm):
    t, d = mem2d.shape
    blk = pl.BlockSpec((tm, XA_W), lambda i: (i, 0))
    return pl.pallas_call(
        _memkv_kernel,
        grid=(t // tm,),
        in_specs=[pl.BlockSpec((tm, d), lambda i: (i, 0)),
                  pl.BlockSpec((1, d), lambda i: (0, 0)),
                  pl.BlockSpec((d, 2 * XA_W), lambda i: (0, 0))],
        out_specs=[blk, blk, blk, blk],
        out_shape=[jax.ShapeDtypeStruct((t, XA_W), F32), jax.ShapeDtypeStruct((t, XA_W), F32),
                   jax.ShapeDtypeStruct((t, XA_W), BF16), jax.ShapeDtypeStruct((t, XA_W), BF16)],
        compiler_params=_cparams(("arbitrary",)),
        name="memkv",
    )(mem2d, gain, w)


def _log2(n):
    return n.bit_length() - 1


def _dn_masks(seg):
    r = lax.broadcasted_iota(jnp.int32, (CHUNK, CHUNK), 0)
    c = lax.broadcasted_iota(jnp.int32, (CHUNK, CHUNK), 1)
    ls = _log2(seg)
    same = (r >> ls) == (c >> ls)
    base = min(8, seg)
    lb = _log2(base)
    m = {
        "same": same.astype(F32),
        "causal": (same & (r >= c)).astype(F32),
        "strict": (same & (r > c)).astype(F32),
        "eye": (r == c).astype(F32),
        "diag": ((r >> lb) == (c >> lb)).astype(F32),
        "off": {},
        "base": base,
    }
    s = base
    while s < seg:
        l1, l2 = _log2(s), _log2(2 * s)
        m["off"][s] = (((r >> l2) == (c >> l2)) & ((r >> l1) != (c >> l1))).astype(F32)
        s *= 2
    return m


def _tri_inv(a, m, seg):
    b = -(a * m["diag"])
    p = m["eye"] + b
    b2 = _mm(b, b)
    p = p + _mm(p, b2)
    if m["base"] == 8:
        b4 = _mm(b2, b2)
        p = p + _mm(p, b4)
    s = m["base"]
    while s < seg:
        x = _mm(a * m["off"][s], p)
        p = p - _mm(p, x)
        s *= 2
    return p


def _dn_gates(ab, alog_row, dtb_row, m):
    g = -jnp.exp(alog_row) * _softplus(ab + dtb_row)
    beta = jax.nn.sigmoid(ab)
    d = _mm_exact01(m["causal"], g)
    dl = _mm_exact01(m["same"], g)
    return d, dl, beta


def _dn_intra(q, k, v, d_col, d_row, beta_col, m, seg):
    q = q * lax.rsqrt(jnp.sum(q * q, axis=-1, keepdims=True) + EPS) * (DN_DK ** -0.5)
    k = k * lax.rsqrt(jnp.sum(k * k, axis=-1, keepdims=True) + EPS)
    diff = (d_col - d_row) * m["causal"]
    gamma = jnp.exp(diff) * m["causal"]
    a = (beta_col * _mm_nt(k, k)) * gamma * m["strict"]
    t = _tri_inv(a, m, seg)
    rhs = jnp.concatenate([v * beta_col, k * (beta_col * jnp.exp(d_col))], axis=1)
    sol = _mm(t, rhs)
    u = sol[:, :DN_DV]
    w = sol[:, DN_DV:]
    qk = _mm_nt(q, k) * gamma
    return q, k, u, w, qk


def _dn_out(o, z, dnorm):
    return _rms(o, dnorm) * _silu(z)


def _branch_prompt_kernel(pa_ref, pq_ref, pzx_ref, pab_ref, mk_ref, mv_ref, caw_ref, dcw_ref,
                          alog_ref, dtb_ref, dnorm_ref,
                          yad_ref, ym_ref, ca_ref, s_ref, ubuf, qbuf, *, nb):
    c = CHUNK
    t_idx = pl.program_id(1)

    @pl.when(t_idx == 0)
    def _():
        ubuf[:, 0:TAIL, :] = jnp.zeros((nb, TAIL, A_W), F32)
        qbuf[:, 0:TAIL, :] = jnp.zeros((nb, TAIL, DN_CONV_CH), F32)
        s_ref[...] = jnp.zeros(s_ref.shape, F32)

    m = _dn_masks(c)
    caw = caw_ref[...]
    dcw = dcw_ref[...]
    dnorm = dnorm_ref[...]
    for b in range(nb):
        pa = pa_ref[b]
        u_in = pa[:, A_W:2 * A_W] * pa[:, 2 * A_W:3 * A_W]
        ubuf[b, TAIL:TAIL + c, :] = u_in
        conv = (caw[0:1] * ubuf[b, TAIL - 2:TAIL - 2 + c, :] + caw[1:2] * ubuf[b, TAIL - 1:TAIL - 1 + c, :]
                + caw[2:3] * u_in)
        yad_ref[b, :, 0:A_W] = (pa[:, 0:A_W] * conv).astype(BF16)
        ca_ref[b] = ubuf[b, TAIL + c - 2:TAIL + c, :]
        ubuf[b, 0:TAIL, :] = ubuf[b, c:c + TAIL, :]

        qkv_in = pq_ref[b]
        qbuf[b, TAIL:TAIL + c, :] = qkv_in
        qkv = _silu(dcw[0:1] * qbuf[b, TAIL - 3:TAIL - 3 + c, :] + dcw[1:2] * qbuf[b, TAIL - 2:TAIL - 2 + c, :]
                    + dcw[2:3] * qbuf[b, TAIL - 1:TAIL - 1 + c, :] + dcw[3:4] * qkv_in)
        qbuf[b, 0:TAIL, :] = qbuf[b, c:c + TAIL, :]

        d, dl, beta = _dn_gates(pab_ref[b], alog_ref[...], dtb_ref[...], m)
        d_t = d.T
        pzx = pzx_ref[b]
        for h in range(DN_HEADS):
            sl = slice(h * LANE, (h + 1) * LANE)
            d_col = d[:, h:h + 1]
            dl_col = dl[:, h:h + 1]
            q, k, u, w, qk = _dn_intra(qkv[:, sl], qkv[:, DN_QK + h * LANE:DN_QK + (h + 1) * LANE],
                                       qkv[:, 2 * DN_QK + h * LANE:2 * DN_QK + (h + 1) * LANE],
                                       d_col, d_t[h:h + 1, :], beta[:, DN_HEADS + h:DN_HEADS + h + 1], m, c)
            s_old = s_ref[b, h]
            ws = _mm(jnp.concatenate([w, q * jnp.exp(d_col)], axis=0), s_old)
            vn = u - ws[:c]
            o = ws[c:] + _mm(qk, vn)
            s_ref[b, h] = s_old * jnp.exp(dl_col[0:1, :]) + _mm_tn(k * jnp.exp(dl_col - d_col), vn)
            yad_ref[b, :, A_W + h * LANE:A_W + (h + 1) * LANE] = _dn_out(o, pzx[:, sl], dnorm).astype(BF16)

        for h in range(XA_HEADS):
            sl = slice(h * LANE, (h + 1) * LANE)
            s = _mm_nt(pzx[:, DN_V + h * LANE:DN_V + (h + 1) * LANE], mk_ref[b, :, sl]) * (XA_DH ** -0.5)
            e = jnp.exp(s - jnp.max(s, axis=-1, keepdims=True))
            p = e / jnp.sum(e, axis=-1, keepdims=True)
            ym_ref[b, :, sl] = _mm(p, mv_ref[b, :, sl]).astype(BF16)


def _branch_prompt(proj3, mkb, mvb, caw, dcw, alog_row, dtb_row, dnorm, nb):
    bsz, length, _ = proj3.shape
    c = CHUNK
    full = lambda shape: pl.BlockSpec(shape, lambda g, t: (0,) * len(shape))
    return pl.pallas_call(
        functools.partial(_branch_prompt_kernel, nb=nb),
        grid=(bsz // nb, length // c),
        in_specs=[pl.BlockSpec((nb, c, 3 * A_W), lambda g, t: (g, t, OFF_A // (3 * A_W))),
                  pl.BlockSpec((nb, c, DN_CONV_CH), lambda g, t: (g, t, OFF_QKV // DN_CONV_CH)),
                  pl.BlockSpec((nb, c, DN_V + XA_W), lambda g, t: (g, t, OFF_Z // (DN_V + XA_W))),
                  pl.BlockSpec((nb, c, LANE), lambda g, t: (g, t, OFF_AB // LANE)),
                  pl.BlockSpec((nb, MEM_TOKENS, XA_W), lambda g, t: (g, 0, 0)),
                  pl.BlockSpec((nb, MEM_TOKENS, XA_W), lambda g, t: (g, 0, 0)),
                  full((CONV_A_K, A_W)), full((DN_CONV_K, DN_CONV_CH)),
                  full((1, LANE)), full((1, LANE)), full((1, LANE))],
        out_specs=[pl.BlockSpec((nb, c, A_W + DN_V), lambda g, t: (g, t, 0)),
                   pl.BlockSpec((nb, c, XA_W), lambda g, t: (g, t, 0)),
                   pl.BlockSpec((nb, CONV_A_K - 1, A_W), lambda g, t: (g, 0, 0)),
                   pl.BlockSpec((nb, DN_HEADS, DN_DK, DN_DV), lambda g, t: (g, 0, 0, 0))],
        out_shape=[jax.ShapeDtypeStruct((bsz, length, A_W + DN_V), BF16),
                   jax.ShapeDtypeStruct((bsz, length, XA_W), BF16),
                   jax.ShapeDtypeStruct((bsz, CONV_A_K - 1, A_W), F32),
                   jax.ShapeDtypeStruct((bsz, DN_HEADS, DN_DK, DN_DV), F32)],
        scratch_shapes=[pltpu.VMEM((nb, c + TAIL, A_W), F32), pltpu.VMEM((nb, c + TAIL, DN_CONV_CH), F32)],
        compiler_params=_cparams(("arbitrary", "arbitrary")),
        name="branch_prompt",
    )(proj3, proj3, proj3, proj3, mkb, mvb, caw, dcw, alog_row, dtb_row, dnorm)


SEQ_S = 4
NB_S = CHUNK // SEQ_S


def _seg_conv(x, e, wts, width, tmod):
    rows = x.shape[0]
    acc = None
    for i in range(width):
        s = width - 1 - i
        term = x if s == 0 else jnp.where(tmod >= s, pltpu.roll(x, s, 0), 0.0)
        if i < width - 1:
            hist = e if i == 0 else pltpu.roll(e, rows - i, 0)
            term = term + jnp.where(tmod < SEQ_S - i, hist, 0.0)
        term = wts[i:i + 1] * term
        acc = term if acc is None else acc + term
    return acc


def _branch_sample_kernel(pa_ref, pq_ref, pzx_ref, pab_ref, ea_ref, eq_ref, s0_ref, caw_ref, dcw_ref,
                          alog_ref, dtb_ref, dnorm_ref, yad_ref, u_ref, s_ref):
    c = CHUNK
    m = _dn_masks(SEQ_S)
    tmod = lax.broadcasted_iota(jnp.int32, (c, 1), 0) & (SEQ_S - 1)

    pa = pa_ref[...]
    u_in = pa[:, A_W:2 * A_W] * pa[:, 2 * A_W:3 * A_W]
    u_ref[...] = u_in
    conv = _seg_conv(u_in, ea_ref[...], caw_ref[...], CONV_A_K, tmod)
    yad_ref[:, 0:A_W] = (pa[:, 0:A_W] * conv).astype(BF16)

    qkv = _silu(_seg_conv(pq_ref[...], eq_ref[...], dcw_ref[...], DN_CONV_K, tmod))
    d, dl, beta = _dn_gates(pab_ref[...], alog_ref[...], dtb_ref[...], m)
    d_t = d.T
    dec_t = jnp.exp(dl).T
    pzx = pzx_ref[...]
    dnorm = dnorm_ref[...]

    wide = NB_S * DN_DK
    er = lax.broadcasted_iota(jnp.int32, (c, wide), 0)
    ec = lax.broadcasted_iota(jnp.int32, (c, wide), 1)
    mexp = ((er >> 2) == (ec >> 7)).astype(F32)
    mexp2 = jnp.concatenate([mexp, mexp], axis=0)
    tr = lax.broadcasted_iota(jnp.int32, (wide, c), 0)
    tc = lax.broadcasted_iota(jnp.int32, (wide, c), 1)
    mexp_t = ((tr >> 7) == (tc >> 2)).astype(F32)

    for h in range(DN_HEADS):
        sl = slice(h * LANE, (h + 1) * LANE)
        d_col = d[:, h:h + 1]
        dl_col = dl[:, h:h + 1]
        q, k, u, w, qk = _dn_intra(qkv[:, sl], qkv[:, DN_QK + h * LANE:DN_QK + (h + 1) * LANE],
                                   qkv[:, 2 * DN_QK + h * LANE:2 * DN_QK + (h + 1) * LANE],
                                   d_col, d_t[h:h + 1, :], beta[:, DN_HEADS + h:DN_HEADS + h + 1], m, SEQ_S)
        s_old = s0_ref[:, h].reshape(wide, DN_DV)
        x = jnp.concatenate([w, q * jnp.exp(d_col)], axis=0)
        x_exp = jnp.concatenate([x] * NB_S, axis=1) * mexp2
        ws = _mm(x_exp, s_old)
        vn = u - ws[:c]
        o = ws[c:] + _mm(qk, vn)
        khat_t = (k * jnp.exp(dl_col - d_col)).T
        k_exp = jnp.concatenate([khat_t] * NB_S, axis=0) * mexp_t
        dec_row = dec_t[h:h + 1, :]
        dec = jnp.concatenate([jnp.broadcast_to(dec_row[:, SEQ_S * b:SEQ_S * b + 1], (DN_DK, DN_DV))
                               for b in range(NB_S)], axis=0)
        s_new = s_old * dec + _mm(k_exp, vn)
        s_ref[:, h] = s_new.reshape(NB_S, DN_DK, DN_DV)
        yad_ref[:, A_W + h * LANE:A_W + (h + 1) * LANE] = _dn_out(o, pzx[:, sl], dnorm).astype(BF16)


def _branch_sample(proj2, ea, eq, state, caw, dcw, alog_row, dtb_row, dnorm):
    t = proj2.shape[0]
    c = CHUNK
    full = lambda shape: pl.BlockSpec(shape, lambda i: (0,) * len(shape))
    return pl.pallas_call(
        _branch_sample_kernel,
        grid=(t // c,),
        in_specs=[pl.BlockSpec((c, 3 * A_W), lambda i: (i, OFF_A // (3 * A_W))),
                  pl.BlockSpec((c, DN_CONV_CH), lambda i: (i, OFF_QKV // DN_CONV_CH)),
                  pl.BlockSpec((c, DN_V + XA_W), lambda i: (i, OFF_Z // (DN_V + XA_W))),
                  pl.BlockSpec((c, LANE), lambda i: (i, OFF_AB // LANE)),
                  pl.BlockSpec((c, A_W), lambda i: (i, 0)),
                  pl.BlockSpec((c, DN_CONV_CH), lambda i: (i, 0)),
                  pl.BlockSpec((NB_S, DN_HEADS, DN_DK, DN_DV), lambda i: (i, 0, 0, 0)),
                  full((CONV_A_K, A_W)), full((DN_CONV_K, DN_CONV_CH)),
                  full((1, LANE)), full((1, LANE)), full((1, LANE))],
        out_specs=[pl.BlockSpec((c, A_W + DN_V), lambda i: (i, 0)),
                   pl.BlockSpec((c, A_W), lambda i: (i, 0)),
                   pl.BlockSpec((NB_S, DN_HEADS, DN_DK, DN_DV), lambda i: (i, 0, 0, 0))],
        out_shape=[jax.ShapeDtypeStruct((t, A_W + DN_V), BF16),
                   jax.ShapeDtypeStruct((t, A_W), F32),
                   jax.ShapeDtypeStruct(state.shape, F32)],
        compiler_params=_cparams(("arbitrary",)),
        name="branch_sample",
    )(proj2, proj2, proj2, proj2, ea, eq, state, caw, dcw, alog_row, dtb_row, dnorm)


def _attn_sample_kernel(q_ref, k_ref, v_ref, o_ref):
    for h in range(XA_HEADS):
        sl = slice(h * LANE, (h + 1) * LANE)
        q = q_ref[:, :, sl].astype(BF16)
        s = jnp.einsum("bqd,bkd->bqk", q, k_ref[:, :, sl].astype(BF16),
                       preferred_element_type=F32) * (XA_DH ** -0.5)
        e = jnp.exp(s - jnp.max(s, axis=-1, keepdims=True))
        p = e / jnp.sum(e, axis=-1, keepdims=True)
        o_ref[:, :, sl] = jnp.einsum("bqk,bkd->bqd", p.astype(BF16), v_ref[:, :, sl].astype(BF16),
                                     preferred_element_type=F32).astype(BF16)


def _attn_sample(proj3, ck, cv, nb):
    bsz, length, _ = proj3.shape
    return pl.pallas_call(
        _attn_sample_kernel,
        grid=(bsz // nb,),
        in_specs=[pl.BlockSpec((nb, length, XA_W), lambda i: (i, 0, OFF_XQ // XA_W)),
                  pl.BlockSpec((nb, MEM_TOKENS, XA_W), lambda i: (i, 0, 0)),
                  pl.BlockSpec((nb, MEM_TOKENS, XA_W), lambda i: (i, 0, 0))],
        out_specs=pl.BlockSpec((nb, length, XA_W), lambda i: (i, 0, 0)),
        out_shape=jax.ShapeDtypeStruct((bsz, length, XA_W), BF16),
        compiler_params=_cparams(("arbitrary",)),
        name="attn_sample",
    )(proj3, ck, cv)


def _merge_kernel(yad_ref, ym_ref, ga_ref, gd_ref, gm_ref, x_ref, wb_ref, wo_ref, o_ref):
    yad = yad_ref[...]
    merged = (jax.nn.sigmoid(ga_ref[...]) * jnp.dot(yad[:, :A_W], wb_ref[0:A_W, :], preferred_element_type=F32)
              + jax.nn.sigmoid(gd_ref[...]) * jnp.dot(yad[:, A_W:], wb_ref[A_W:A_W + DN_V, :],
                                                      preferred_element_type=F32)
              + jax.nn.sigmoid(gm_ref[...]) * jnp.dot(ym_ref[...], wb_ref[A_W + DN_V:, :],
                                                      preferred_element_type=F32))
    o_ref[...] = x_ref[...] + jnp.dot(merged.astype(BF16), wo_ref[...], preferred_element_type=F32)


def _merge(yad, ym, proj2, x2d, wb, wo, tm):
    t, d = x2d.shape
    gate = lambda k: pl.BlockSpec((tm, d), lambda i: (i, OFF_G // d + k))
    return pl.pallas_call(
        _merge_kernel,
        grid=(t // tm,),
        in_specs=[pl.BlockSpec((tm, A_W + DN_V), lambda i: (i, 0)),
                  pl.BlockSpec((tm, XA_W), lambda i: (i, 0)),
                  gate(0), gate(1), gate(2),
                  pl.BlockSpec((tm, d), lambda i: (i, 0)),
                  pl.BlockSpec(wb.shape, lambda i: (0, 0)),
                  pl.BlockSpec(wo.shape, lambda i: (0, 0))],
        out_specs=pl.BlockSpec((tm, d), lambda i: (i, 0)),
        out_shape=jax.ShapeDtypeStruct((t, d), F32),
        compiler_params=_cparams(("arbitrary",)),
        name="merge",
    )(yad, ym, proj2, proj2, proj2, x2d, wb, wo)


FF_SPLIT = 2
FF_BLK = D_FF // FF_SPLIT


def _ffn_kernel(x_ref, gf_ref, wu_ref, wd_ref, gl_ref, o_ref):
    x = x_ref[...]
    xn = _rms(x, gf_ref[...]).astype(BF16)
    acc = x
    for j in range(FF_SPLIT):
        gate = jnp.dot(xn, wu_ref[:, j * FF_BLK:(j + 1) * FF_BLK], preferred_element_type=F32)
        up = jnp.dot(xn, wu_ref[:, D_FF + j * FF_BLK:D_FF + (j + 1) * FF_BLK], preferred_element_type=F32)
        hid = (_silu(gate) * up).astype(BF16)
        acc = acc + jnp.dot(hid, wd_ref[j * FF_BLK:(j + 1) * FF_BLK, :], preferred_element_type=F32)
    o_ref[...] = _rms(acc, gl_ref[...])


def _ffn(x2d, gf, wu, wd, gl, tm):
    t, d = x2d.shape
    return pl.pallas_call(
        _ffn_kernel,
        grid=(t // tm,),
        in_specs=[pl.BlockSpec((tm, d), lambda i: (i, 0)),
                  pl.BlockSpec((1, d), lambda i: (0, 0)),
                  pl.BlockSpec(wu.shape, lambda i: (0, 0)),
                  pl.BlockSpec(wd.shape, lambda i: (0, 0)),
                  pl.BlockSpec((1, d), lambda i: (0, 0))],
        out_specs=pl.BlockSpec((tm, d), lambda i: (i, 0)),
        out_shape=jax.ShapeDtypeStruct((t, d), F32),
        compiler_params=_cparams(("arbitrary",)),
        name="ffn",
    )(x2d, gf, wu, wd, gl)


def _pad_lanes(v):
    return jnp.zeros((1, LANE), F32).at[0, :v.shape[0]].set(v.astype(F32))


def kernel(x_prompt, x_sample, mem_prompt, state_conv_a, state_dn_conv, state_dn, cache_mem_k, cache_mem_v,
           norm_mix, w_in, conv_a_w, dn_conv_w, dn_a_log, dn_dt_bias, dn_norm, norm_mem, w_mem_kv, w_branch,
           w_o, norm_ffn, w_ffn_up, w_ffn_down, norm_final):
    bp, lp, d = x_prompt.shape
    bs, ls, _ = x_sample.shape
    assert norm_mix.shape[0] == 1 and ls == SEQ_S and lp % CHUNK == 0 and (bs * ls) % CHUNK == 0

    w = w_in[0]
    w_r = jnp.concatenate([w[:, :AB_SRC], w[:, AB_SRC + 2 * DN_HEADS:], w[:, AB_SRC:AB_SRC + 2 * DN_HEADS],
                           jnp.zeros((d, LANE - 2 * DN_HEADS), w.dtype)], axis=1).astype(BF16)
    wb = w_branch[0].astype(BF16)
    wo = w_o[0].astype(BF16)
    wu = w_ffn_up[0].astype(BF16)
    wd = w_ffn_down[0].astype(BF16)
    wkv = w_mem_kv[0].astype(BF16)
    g_mix = norm_mix[0][None, :]
    g_ffn = norm_ffn[0][None, :]
    g_fin = norm_final[None, :]
    g_mem = norm_mem[0][None, :]
    caw = conv_a_w[0]
    dcw = dn_conv_w[0]
    alog_row = _pad_lanes(dn_a_log[0])
    dtb_row = _pad_lanes(dn_dt_bias[0])
    dnorm = dn_norm[0][None, :]

    tp = bp * lp
    xp2 = x_prompt.reshape(tp, d)
    mk, mv, mkb, mvb = _memkv(mem_prompt.reshape(bp * MEM_TOKENS, d), g_mem, wkv, 512)
    proj_p = _proj(xp2, g_mix, w_r, 512, NW // 3)
    proj_p3 = proj_p.reshape(bp, lp, NW)
    yad_p, ym_p, ca_p, s_p = _branch_prompt(proj_p3, mkb.reshape(bp, MEM_TOKENS, XA_W),
                                            mvb.reshape(bp, MEM_TOKENS, XA_W), caw, dcw, alog_row, dtb_row,
                                            dnorm, 1)
    x1_p = _merge(yad_p.reshape(tp, A_W + DN_V), ym_p.reshape(tp, XA_W), proj_p, xp2, wb, wo, 512)
    y_p = _ffn(x1_p, g_ffn, wu, wd, g_fin, 512).reshape(bp, lp, d)
    dc_p = proj_p3[:, lp - (DN_CONV_K - 1):, OFF_QKV:OFF_QKV + DN_CONV_CH]

    ts = bs * ls
    xs2 = x_sample.reshape(ts, d)
    proj_s = _proj(xs2, g_mix, w_r, ts, NW // 3)
    proj_s3 = proj_s.reshape(bs, ls, NW)
    ea = jnp.pad(state_conv_a[0], ((0, 0), (0, ls - (CONV_A_K - 1)), (0, 0))).reshape(ts, A_W)
    eq = jnp.pad(state_dn_conv[0], ((0, 0), (0, ls - (DN_CONV_K - 1)), (0, 0))).reshape(ts, DN_CONV_CH)
    yad_s, u_s, s_s = _branch_sample(proj_s, ea, eq, state_dn[0], caw, dcw, alog_row, dtb_row, dnorm)
    ym_s = _attn_sample(proj_s3, cache_mem_k[0].reshape(bs, MEM_TOKENS, XA_W),
                        cache_mem_v[0].reshape(bs, MEM_TOKENS, XA_W), 8)
    x1_s = _merge(yad_s, ym_s.reshape(ts, XA_W), proj_s, xs2, wb, wo, ts)
    y_s = _ffn(x1_s, g_ffn, wu, wd, g_fin, ts).reshape(bs, ls, d)
    ca_s = u_s.reshape(bs, ls, A_W)[:, ls - (CONV_A_K - 1):]
    dc_s = proj_s3[:, ls - (DN_CONV_K - 1):, OFF_QKV:OFF_QKV + DN_CONV_CH]

    return (y_p, y_s, ca_p[None], dc_p[None], s_p[None],
            mk.reshape(1, bp, MEM_TOKENS, XA_HEADS, XA_DH), mv.reshape(1, bp, MEM_TOKENS, XA_HEADS, XA_DH),
            ca_s[None], dc_s[None], s_s[None])
```

```python
import functools

import jax
import jax.numpy as jnp
from jax import lax
from jax.experimental import pallas as pl
from jax.experimental.pallas import tpu as pltpu

F32 = jnp.float32
BF16 = jnp.bfloat16

D_MODEL = 1024
A_W = 512
CONV_A_K = 3
DN_HEADS = 4
DN_DK = 128
DN_DV = 128
DN_QK = DN_HEADS * DN_DK
DN_V = DN_HEADS * DN_DV
DN_CONV_CH = 2 * DN_QK + DN_V
DN_CONV_K = 4
MEM_TOKENS = 256
XA_HEADS = 4
XA_DH = 128
XA_W = XA_HEADS * XA_DH
D_FF = 2816
EPS = 1e-6

LANE = 128
CHUNK = 128
TAIL = 8
NB_P = 4
TM = 512

OFF_A = 0
OFF_QKV = 3 * A_W
OFF_Z = OFF_QKV + DN_CONV_CH
OFF_XQ = OFF_Z + DN_V
OFF_G = OFF_XQ + XA_W
OFF_AB = OFF_G + 3 * D_MODEL
NW = OFF_AB + LANE
AB_SRC = 3 * A_W + DN_CONV_CH + DN_V

VMEM_LIMIT = 56 * 1024 * 1024


def _cparams(sem):
    return pltpu.CompilerParams(dimension_semantics=sem, vmem_limit_bytes=VMEM_LIMIT)


def _mm(a, b):
    return jnp.dot(a.astype(BF16), b.astype(BF16), preferred_element_type=F32)


def _mm_nt(a, b):
    return lax.dot_general(a.astype(BF16), b.astype(BF16), (((1,), (1,)), ((), ())),
                           preferred_element_type=F32)


def _mm_tn(a, b):
    return lax.dot_general(a.astype(BF16), b.astype(BF16), (((0,), (0,)), ((), ())),
                           preferred_element_type=F32)


def _split3(x):
    hi = x.astype(BF16)
    r1 = x - hi.astype(F32)
    mid = r1.astype(BF16)
    lo = (r1 - mid.astype(F32)).astype(BF16)
    return jnp.concatenate([hi, mid, lo], axis=1)


def _mm_exact01(m01, x):
    n = x.shape[1]
    r = jnp.dot(m01.astype(BF16), _split3(x), preferred_element_type=F32)
    return (r[:, :n] + r[:, n:2 * n]) + r[:, 2 * n:]


def _rms(x, g):
    return x * lax.rsqrt(jnp.mean(x * x, axis=-1, keepdims=True) + EPS) * g


def _silu(x):
    return x * jax.nn.sigmoid(x)


def _softplus(x):
    return jnp.maximum(x, 0.0) + jnp.log1p(jnp.exp(-jnp.abs(x)))


def _proj_kernel(x_ref, g_ref, w_ref, o_ref):
    xn = _rms(x_ref[...], g_ref[...]).astype(BF16)
    o_ref[...] = jnp.dot(xn, w_ref[...], preferred_element_type=F32)


def _proj(x2d, gain, w, tm, tn):
    t, d = x2d.shape
    n = w.shape[1]
    return pl.pallas_call(
        _proj_kernel,
        grid=(n // tn, t // tm),
        in_specs=[pl.BlockSpec((tm, d), lambda j, i: (i, 0)),
                  pl.BlockSpec((1, d), lambda j, i: (0, 0)),
                  pl.BlockSpec((d, tn), lambda j, i: (0, j))],
        out_specs=pl.BlockSpec((tm, tn), lambda j, i: (i, j)),
        out_shape=jax.ShapeDtypeStruct((t, n), F32),
        compiler_params=_cparams(("arbitrary", "arbitrary")),
        name="proj",
    )(x2d, gain, w)


def _memkv_kernel(x_ref, g_ref, w_ref, k_ref, v_ref, kb_ref, vb_ref):
    xn = _rms(x_ref[...], g_ref[...]).astype(BF16)
    kv = jnp.dot(xn, w_ref[...], preferred_element_type=F32)
    k = kv[:, :XA_W]
    v = kv[:, XA_W:]
    k_ref[...] = k
    v_ref[...] = v
    kb_ref[...] = k.astype(BF16)
    vb_ref[...] = v.astype(BF16)


def _memkv(mem2d, gain, w, tm):
    t, d = mem2d.shape
    blk = pl.BlockSpec((tm, XA_W), lambda i: (i, 0))
    return pl.pallas_call(
        _memkv_kernel,
        grid=(t // tm,),
        in_specs=[pl.BlockSpec((tm, d), lambda i: (i, 0)),
                  pl.BlockSpec((1, d), lambda i: (0, 0)),
                  pl.BlockSpec((d, 2 * XA_W), lambda i: (0, 0))],
        out_specs=[blk, blk, blk, blk],
        out_shape=[jax.ShapeDtypeStruct((t, XA_W), F32), jax.ShapeDtypeStruct((t, XA_W), F32),
                   jax.ShapeDtypeStruct((t, XA_W), BF16), jax.ShapeDtypeStruct((t, XA_W), BF16)],
        compiler_params=_cparams(("arbitrary",)),
        name="memkv",
    )(mem2d, gain, w)


def _log2(n):
    return n.bit_length() - 1


def _dn_masks(seg):
    r = lax.broadcasted_iota(jnp.int32, (CHUNK, CHUNK), 0)
    c = lax.broadcasted_iota(jnp.int32, (CHUNK, CHUNK), 1)
    ls = _log2(seg)
    same = (r >> ls) == (c >> ls)
    base = min(8, seg)
    lb = _log2(base)
    m = {
        "same": same.astype(F32),
        "causal": (same & (r >= c)).astype(F32),
        "strict": (same & (r > c)).astype(F32),
        "eye": (r == c).astype(F32),
        "diag": ((r >> lb) == (c >> lb)).astype(F32),
        "off": {},
        "base": base,
    }
    s = base
    while s < seg:
        l1, l2 = _log2(s), _log2(2 * s)
        m["off"][s] = (((r >> l2) == (c >> l2)) & ((r >> l1) != (c >> l1))).astype(F32)
        s *= 2
    return m


def _each(f, *lists):
    return [f(*args) for args in zip(*lists)]


def _tri_inv(a_list, m, seg):
    b = _each(lambda a: -(a * m["diag"]), a_list)
    p = _each(lambda x: m["eye"] + x, b)
    b2 = _each(_mm, b, b)
    p = _each(lambda x, y: x + _mm(x, y), p, b2)
    if m["base"] == 8:
        b4 = _each(_mm, b2, b2)
        p = _each(lambda x, y: x + _mm(x, y), p, b4)
    s = m["base"]
    while s < seg:
        x = _each(lambda a, t, s=s: _mm(a * m["off"][s], t), a_list, p)
        p = _each(lambda t, y: t - _mm(t, y), p, x)
        s *= 2
    return p


def _dn_gates(ab, alog_row, dtb_row, m):
    g = -jnp.exp(alog_row) * _softplus(ab + dtb_row)
    beta = jax.nn.sigmoid(ab)
    d = _mm_exact01(m["causal"], g)
    dl = _mm_exact01(m["same"], g)
    return d, dl, beta


def _l2n(x):
    return x * lax.rsqrt(jnp.sum(x * x, axis=-1, keepdims=True) + EPS)


def _dn_intra(q, k, v, d_col, d_row, beta_col, m, seg):
    q = _each(lambda x: _l2n(x) * (DN_DK ** -0.5), q)
    k = _each(_l2n, k)
    gamma = _each(lambda dc, dr: jnp.exp((dc - dr) * m["causal"]) * m["causal"], d_col, d_row)
    kk = _each(_mm_nt, k, k)
    a = _each(lambda bc, x, g: (bc * x) * g * m["strict"], beta_col, kk, gamma)
    t = _tri_inv(a, m, seg)
    rhs = _each(lambda vv, kx, bc, dc: jnp.concatenate([vv * bc, kx * (bc * jnp.exp(dc))], axis=1),
                v, k, beta_col, d_col)
    sol = _each(_mm, t, rhs)
    u = [x[:, :DN_DV] for x in sol]
    w = [x[:, DN_DV:] for x in sol]
    qk = _each(lambda x, y, g: _mm_nt(x, y) * g, q, k, gamma)
    return q, k, u, w, qk


def _dn_out(o, z, dnorm):
    return _rms(o, dnorm) * _silu(z)


def _head_lists(qkv, pzx, d, dl, beta):
    d_t = d.T
    out = [[] for _ in range(8)]
    for h in range(DN_HEADS):
        vals = (qkv[:, h * LANE:(h + 1) * LANE],
                qkv[:, DN_QK + h * LANE:DN_QK + (h + 1) * LANE],
                qkv[:, 2 * DN_QK + h * LANE:2 * DN_QK + (h + 1) * LANE],
                pzx[:, h * LANE:(h + 1) * LANE],
                d[:, h:h + 1], d_t[h:h + 1, :], dl[:, h:h + 1], beta[:, DN_HEADS + h:DN_HEADS + h + 1])
        for lst, val in zip(out, vals):
            lst.append(val)
    return out


def _branch_prompt_kernel(pa_ref, pq_ref, pzx_ref, pab_ref, mk_ref, mv_ref, caw_ref, dcw_ref,
                          alog_ref, dtb_ref, dnorm_ref,
                          yad_ref, ym_ref, ca_ref, s_ref, ubuf, qbuf, *, nb):
    c = CHUNK
    t_idx = pl.program_id(1)

    @pl.when(t_idx == 0)
    def _():
        ubuf[:, 0:TAIL, :] = jnp.zeros((nb, TAIL, A_W), F32)
        qbuf[:, 0:TAIL, :] = jnp.zeros((nb, TAIL, DN_CONV_CH), F32)
        s_ref[...] = jnp.zeros(s_ref.shape, F32)

    m = _dn_masks(c)
    caw = caw_ref[...]
    dcw = dcw_ref[...]
    dnorm = dnorm_ref[...]
    lists = [[] for _ in range(8)]
    xq = []
    for b in range(nb):
        pa = pa_ref[b]
        u_in = pa[:, A_W:2 * A_W] * pa[:, 2 * A_W:3 * A_W]
        ubuf[b, TAIL:TAIL + c, :] = u_in
        conv = (caw[0:1] * ubuf[b, TAIL - 2:TAIL - 2 + c, :] + caw[1:2] * ubuf[b, TAIL - 1:TAIL - 1 + c, :]
                + caw[2:3] * u_in)
        yad_ref[b, :, 0:A_W] = (pa[:, 0:A_W] * conv).astype(BF16)
        ca_ref[b] = ubuf[b, TAIL + c - 2:TAIL + c, :]
        ubuf[b, 0:TAIL, :] = ubuf[b, c:c + TAIL, :]

        qkv_in = pq_ref[b]
        qbuf[b, TAIL:TAIL + c, :] = qkv_in
        qkv = _silu(dcw[0:1] * qbuf[b, TAIL - 3:TAIL - 3 + c, :] + dcw[1:2] * qbuf[b, TAIL - 2:TAIL - 2 + c, :]
                    + dcw[2:3] * qbuf[b, TAIL - 1:TAIL - 1 + c, :] + dcw[3:4] * qkv_in)
        qbuf[b, 0:TAIL, :] = qbuf[b, c:c + TAIL, :]
        d, dl, beta = _dn_gates(pab_ref[b], alog_ref[...], dtb_ref[...], m)
        pzx = pzx_ref[b]
        for lst, val in zip(lists, _head_lists(qkv, pzx, d, dl, beta)):
            lst.extend(val)
        xq.extend(pzx[:, DN_V + h * LANE:DN_V + (h + 1) * LANE] for h in range(XA_HEADS))

    q, k, v, z, d_col, d_row, dl_col, beta_col = lists
    idx = [(b, h) for b in range(nb) for h in range(DN_HEADS)]
    q, k, u, w, qk = _dn_intra(q, k, v, d_col, d_row, beta_col, m, c)
    s_old = [s_ref[b, h] for b, h in idx]
    ws = _each(lambda wx, qx, dc, s: _mm(jnp.concatenate([wx, qx * jnp.exp(dc)], axis=0), s), w, q, d_col, s_old)
    vn = _each(lambda ux, x: ux - x[:c], u, ws)
    o = _each(lambda x, y, vx: x[c:] + _mm(y, vx), ws, qk, vn)
    s_new = _each(lambda s, dlc, kx, dc, vx: s * jnp.exp(dlc[0:1, :]) + _mm_tn(kx * jnp.exp(dlc - dc), vx),
                  s_old, dl_col, k, d_col, vn)
    for (b, h), sx, ox, zx in zip(idx, s_new, o, z):
        s_ref[b, h] = sx
        yad_ref[b, :, A_W + h * LANE:A_W + (h + 1) * LANE] = _dn_out(ox, zx, dnorm).astype(BF16)

    sc = _each(lambda x, bh: _mm_nt(x, mk_ref[bh[0], :, bh[1] * LANE:(bh[1] + 1) * LANE]) * (XA_DH ** -0.5), xq, idx)
    e = _each(lambda x: jnp.exp(x - jnp.max(x, axis=-1, keepdims=True)), sc)
    p = _each(lambda x: x / jnp.sum(x, axis=-1, keepdims=True), e)
    for (b, h), px in zip(idx, p):
        ym_ref[b, :, h * LANE:(h + 1) * LANE] = _mm(px, mv_ref[b, :, h * LANE:(h + 1) * LANE]).astype(BF16)


def _branch_prompt(proj3, mkb, mvb, caw, dcw, alog_row, dtb_row, dnorm, nb):
    bsz, length, _ = proj3.shape
    c = CHUNK
    full = lambda shape: pl.BlockSpec(shape, lambda g, t: (0,) * len(shape))
    return pl.pallas_call(
        functools.partial(_branch_prompt_kernel, nb=nb),
        grid=(bsz // nb, length // c),
        in_specs=[pl.BlockSpec((nb, c, 3 * A_W), lambda g, t: (g, t, OFF_A // (3 * A_W))),
                  pl.BlockSpec((nb, c, DN_CONV_CH), lambda g, t: (g, t, OFF_QKV // DN_CONV_CH)),
                  pl.BlockSpec((nb, c, DN_V + XA_W), lambda g, t: (g, t, OFF_Z // (DN_V + XA_W))),
                  pl.BlockSpec((nb, c, LANE), lambda g, t: (g, t, OFF_AB // LANE)),
                  pl.BlockSpec((nb, MEM_TOKENS, XA_W), lambda g, t: (g, 0, 0)),
                  pl.BlockSpec((nb, MEM_TOKENS, XA_W), lambda g, t: (g, 0, 0)),
                  full((CONV_A_K, A_W)), full((DN_CONV_K, DN_CONV_CH)),
                  full((1, LANE)), full((1, LANE)), full((1, LANE))],
        out_specs=[pl.BlockSpec((nb, c, A_W + DN_V), lambda g, t: (g, t, 0)),
                   pl.BlockSpec((nb, c, XA_W), lambda g, t: (g, t, 0)),
                   pl.BlockSpec((nb, CONV_A_K - 1, A_W), lambda g, t: (g, 0, 0)),
                   pl.BlockSpec((nb, DN_HEADS, DN_DK, DN_DV), lambda g, t: (g, 0, 0, 0))],
        out_shape=[jax.ShapeDtypeStruct((bsz, length, A_W + DN_V), BF16),
                   jax.ShapeDtypeStruct((bsz, length, XA_W), BF16),
                   jax.ShapeDtypeStruct((bsz, CONV_A_K - 1, A_W), F32),
                   jax.ShapeDtypeStruct((bsz, DN_HEADS, DN_DK, DN_DV), F32)],
        scratch_shapes=[pltpu.VMEM((nb, c + TAIL, A_W), F32), pltpu.VMEM((nb, c + TAIL, DN_CONV_CH), F32)],
        compiler_params=_cparams(("arbitrary", "arbitrary")),
        name="branch_prompt",
    )(proj3, proj3, proj3, proj3, mkb, mvb, caw, dcw, alog_row, dtb_row, dnorm)


SEQ_S = 4
NB_S = CHUNK // SEQ_S


def _seg_conv(x, e, wts, width, tmod):
    rows = x.shape[0]
    acc = None
    for i in range(width):
        s = width - 1 - i
        term = x if s == 0 else jnp.where(tmod >= s, pltpu.roll(x, s, 0), 0.0)
        if i < width - 1:
            hist = e if i == 0 else pltpu.roll(e, rows - i, 0)
            term = term + jnp.where(tmod < SEQ_S - i, hist, 0.0)
        term = wts[i:i + 1] * term
        acc = term if acc is None else acc + term
    return acc


def _branch_sample_kernel(pa_ref, pq_ref, pzx_ref, pab_ref, ea_ref, eq_ref, s0_ref, caw_ref, dcw_ref,
                          alog_ref, dtb_ref, dnorm_ref, yad_ref, u_ref, s_ref):
    c = CHUNK
    m = _dn_masks(SEQ_S)
    tmod = lax.broadcasted_iota(jnp.int32, (c, 1), 0) & (SEQ_S - 1)

    pa = pa_ref[...]
    u_in = pa[:, A_W:2 * A_W] * pa[:, 2 * A_W:3 * A_W]
    u_ref[...] = u_in
    conv = _seg_conv(u_in, ea_ref[...], caw_ref[...], CONV_A_K, tmod)
    yad_ref[:, 0:A_W] = (pa[:, 0:A_W] * conv).astype(BF16)

    qkv = _silu(_seg_conv(pq_ref[...], eq_ref[...], dcw_ref[...], DN_CONV_K, tmod))
    d, dl, beta = _dn_gates(pab_ref[...], alog_ref[...], dtb_ref[...], m)
    dec_t = jnp.exp(dl).T
    pzx = pzx_ref[...]
    dnorm = dnorm_ref[...]

    wide = NB_S * DN_DK
    er = lax.broadcasted_iota(jnp.int32, (c, wide), 0)
    ec = lax.broadcasted_iota(jnp.int32, (c, wide), 1)
    mexp = ((er >> 2) == (ec >> 7)).astype(F32)
    mexp2 = jnp.concatenate([mexp, mexp], axis=0)
    tr = lax.broadcasted_iota(jnp.int32, (wide, c), 0)
    tc = lax.broadcasted_iota(jnp.int32, (wide, c), 1)
    mexp_t = ((tr >> 7) == (tc >> 2)).astype(F32)

    q, k, v, z, d_col, d_row, dl_col, beta_col = _head_lists(qkv, pzx, d, dl, beta)
    q, k, u, w, qk = _dn_intra(q, k, v, d_col, d_row, beta_col, m, SEQ_S)
    heads = list(range(DN_HEADS))
    s_old = [s0_ref[:, h].reshape(wide, DN_DV) for h in heads]
    x_exp = _each(lambda wx, qx, dc: jnp.concatenate([jnp.concatenate([wx, qx * jnp.exp(dc)], axis=0)] * NB_S,
                                                     axis=1) * mexp2, w, q, d_col)
    ws = _each(_mm, x_exp, s_old)
    vn = _each(lambda ux, x: ux - x[:c], u, ws)
    o = _each(lambda x, y, vx: x[c:] + _mm(y, vx), ws, qk, vn)
    k_exp = _each(lambda kx, dlc, dc: jnp.concatenate([(kx * jnp.exp(dlc - dc)).T] * NB_S, axis=0) * mexp_t,
                  k, dl_col, d_col)
    dec = [jnp.concatenate([jnp.broadcast_to(dec_t[h:h + 1, SEQ_S * b:SEQ_S * b + 1], (DN_DK, DN_DV))
                            for b in range(NB_S)], axis=0) for h in heads]
    s_new = _each(lambda s, dx, kx, vx: s * dx + _mm(kx, vx), s_old, dec, k_exp, vn)
    for h, sx, ox, zx in zip(heads, s_new, o, z):
        s_ref[:, h] = sx.reshape(NB_S, DN_DK, DN_DV)
        yad_ref[:, A_W + h * LANE:A_W + (h + 1) * LANE] = _dn_out(ox, zx, dnorm).astype(BF16)


def _branch_sample(proj2, ea, eq, state, caw, dcw, alog_row, dtb_row, dnorm):
    t = proj2.shape[0]
    c = CHUNK
    full = lambda shape: pl.BlockSpec(shape, lambda i: (0,) * len(shape))
    return pl.pallas_call(
        _branch_sample_kernel,
        grid=(t // c,),
        in_specs=[pl.BlockSpec((c, 3 * A_W), lambda i: (i, OFF_A // (3 * A_W))),
                  pl.BlockSpec((c, DN_CONV_CH), lambda i: (i, OFF_QKV // DN_CONV_CH)),
                  pl.BlockSpec((c, DN_V + XA_W), lambda i: (i, OFF_Z // (DN_V + XA_W))),
                  pl.BlockSpec((c, LANE), lambda i: (i, OFF_AB // LANE)),
                  pl.BlockSpec((c, A_W), lambda i: (i, 0)),
                  pl.BlockSpec((c, DN_CONV_CH), lambda i: (i, 0)),
                  pl.BlockSpec((NB_S, DN_HEADS, DN_DK, DN_DV), lambda i: (i, 0, 0, 0)),
                  full((CONV_A_K, A_W)), full((DN_CONV_K, DN_CONV_CH)),
                  full((1, LANE)), full((1, LANE)), full((1, LANE))],
        out_specs=[pl.BlockSpec((c, A_W + DN_V), lambda i: (i, 0)),
                   pl.BlockSpec((c, A_W), lambda i: (i, 0)),
                   pl.BlockSpec((NB_S, DN_HEADS, DN_DK, DN_DV), lambda i: (i, 0, 0, 0))],
        out_shape=[jax.ShapeDtypeStruct((t, A_W + DN_V), BF16),
                   jax.ShapeDtypeStruct((t, A_W), F32),
                   jax.ShapeDtypeStruct(state.shape, F32)],
        compiler_params=_cparams(("arbitrary",)),
        name="branch_sample",
    )(proj2, proj2, proj2, proj2, ea, eq, state, caw, dcw, alog_row, dtb_row, dnorm)


def _attn_sample_kernel(q_ref, k_ref, v_ref, o_ref):
    for h in range(XA_HEADS):
        sl = slice(h * LANE, (h + 1) * LANE)
        rows = pl.ds(h, MEM_TOKENS, stride=XA_HEADS)
        q = q_ref[:, :, sl].astype(BF16)
        s = jnp.einsum("bqd,bkd->bqk", q, k_ref[:, rows, :].astype(BF16),
                       preferred_element_type=F32) * (XA_DH ** -0.5)
        e = jnp.exp(s - jnp.max(s, axis=-1, keepdims=True))
        p = e / jnp.sum(e, axis=-1, keepdims=True)
        o_ref[:, :, sl] = jnp.einsum("bqk,bkd->bqd", p.astype(BF16), v_ref[:, rows, :].astype(BF16),
                                     preferred_element_type=F32).astype(BF16)


def _attn_sample(proj3, ck, cv, nb):
    bsz, length, _ = proj3.shape
    return pl.pallas_call(
        _attn_sample_kernel,
        grid=(bsz // nb,),
        in_specs=[pl.BlockSpec((nb, length, XA_W), lambda i: (i, 0, OFF_XQ // XA_W)),
                  pl.BlockSpec((nb, MEM_TOKENS * XA_HEADS, XA_DH), lambda i: (i, 0, 0)),
                  pl.BlockSpec((nb, MEM_TOKENS * XA_HEADS, XA_DH), lambda i: (i, 0, 0))],
        out_specs=pl.BlockSpec((nb, length, XA_W), lambda i: (i, 0, 0)),
        out_shape=jax.ShapeDtypeStruct((bsz, length, XA_W), BF16),
        compiler_params=_cparams(("arbitrary",)),
        name="attn_sample",
    )(proj3, ck, cv)


def _merge_kernel(yad_ref, ym_ref, ga_ref, gd_ref, gm_ref, x_ref, wb_ref, wo_ref, o_ref):
    yad = yad_ref[...]
    merged = (jax.nn.sigmoid(ga_ref[...]) * jnp.dot(yad[:, :A_W], wb_ref[0:A_W, :], preferred_element_type=F32)
              + jax.nn.sigmoid(gd_ref[...]) * jnp.dot(yad[:, A_W:], wb_ref[A_W:A_W + DN_V, :],
                                                      preferred_element_type=F32)
              + jax.nn.sigmoid(gm_ref[...]) * jnp.dot(ym_ref[...], wb_ref[A_W + DN_V:, :],
                                                      preferred_element_type=F32))
    o_ref[...] = x_ref[...] + jnp.dot(merged.astype(BF16), wo_ref[...], preferred_element_type=F32)


def _merge(yad, ym, proj2, x2d, wb, wo, tm):
    t, d = x2d.shape
    gate = lambda k: pl.BlockSpec((tm, d), lambda i: (i, OFF_G // d + k))
    return pl.pallas_call(
        _merge_kernel,
        grid=(t // tm,),
        in_specs=[pl.BlockSpec((tm, A_W + DN_V), lambda i: (i, 0)),
                  pl.BlockSpec((tm, XA_W), lambda i: (i, 0)),
                  gate(0), gate(1), gate(2),
                  pl.BlockSpec((tm, d), lambda i: (i, 0)),
                  pl.BlockSpec(wb.shape, lambda i: (0, 0)),
                  pl.BlockSpec(wo.shape, lambda i: (0, 0))],
        out_specs=pl.BlockSpec((tm, d), lambda i: (i, 0)),
        out_shape=jax.ShapeDtypeStruct((t, d), F32),
        compiler_params=_cparams(("arbitrary",)),
        name="merge",
    )(yad, ym, proj2, proj2, proj2, x2d, wb, wo)


FF_SPLIT = 2
FF_BLK = D_FF // FF_SPLIT


def _ffn_kernel(x_ref, gf_ref, wu_ref, wd_ref, gl_ref, o_ref):
    x = x_ref[...]
    xn = _rms(x, gf_ref[...]).astype(BF16)
    acc = x
    for j in range(FF_SPLIT):
        gate = jnp.dot(xn, wu_ref[:, j * FF_BLK:(j + 1) * FF_BLK], preferred_element_type=F32)
        up = jnp.dot(xn, wu_ref[:, D_FF + j * FF_BLK:D_FF + (j + 1) * FF_BLK], preferred_element_type=F32)
        hid = (_silu(gate) * up).astype(BF16)
        acc = acc + jnp.dot(hid, wd_ref[j * FF_BLK:(j + 1) * FF_BLK, :], preferred_element_type=F32)
    o_ref[...] = _rms(acc, gl_ref[...])


def _ffn(x2d, gf, wu, wd, gl, tm):
    t, d = x2d.shape
    return pl.pallas_call(
        _ffn_kernel,
        grid=(t // tm,),
        in_specs=[pl.BlockSpec((tm, d), lambda i: (i, 0)),
                  pl.BlockSpec((1, d), lambda i: (0, 0)),
                  pl.BlockSpec(wu.shape, lambda i: (0, 0)),
                  pl.BlockSpec(wd.shape, lambda i: (0, 0)),
                  pl.BlockSpec((1, d), lambda i: (0, 0))],
        out_specs=pl.BlockSpec((tm, d), lambda i: (i, 0)),
        out_shape=jax.ShapeDtypeStruct((t, d), F32),
        compiler_params=_cparams(("arbitrary",)),
        name="ffn",
    )(x2d, gf, wu, wd, gl)


def _pad_lanes(v):
    return jnp.zeros((1, LANE), F32).at[0, :v.shape[0]].set(v.astype(F32))


def kernel(x_prompt, x_sample, mem_prompt, state_conv_a, state_dn_conv, state_dn, cache_mem_k, cache_mem_v,
           norm_mix, w_in, conv_a_w, dn_conv_w, dn_a_log, dn_dt_bias, dn_norm, norm_mem, w_mem_kv, w_branch,
           w_o, norm_ffn, w_ffn_up, w_ffn_down, norm_final):
    bp, lp, d = x_prompt.shape
    bs, ls, _ = x_sample.shape
    assert norm_mix.shape[0] == 1 and ls == SEQ_S and lp % CHUNK == 0 and (bs * ls) % CHUNK == 0

    w = w_in[0]
    w_r = jnp.concatenate([w[:, :AB_SRC], w[:, AB_SRC + 2 * DN_HEADS:], w[:, AB_SRC:AB_SRC + 2 * DN_HEADS],
                           jnp.zeros((d, LANE - 2 * DN_HEADS), w.dtype)], axis=1).astype(BF16)
    wb = w_branch[0].astype(BF16)
    wo = w_o[0].astype(BF16)
    wu = w_ffn_up[0].astype(BF16)
    wd = w_ffn_down[0].astype(BF16)
    wkv = w_mem_kv[0].astype(BF16)
    g_mix = norm_mix[0][None, :]
    g_ffn = norm_ffn[0][None, :]
    g_fin = norm_final[None, :]
    g_mem = norm_mem[0][None, :]
    caw = conv_a_w[0]
    dcw = dn_conv_w[0]
    alog_row = _pad_lanes(dn_a_log[0])
    dtb_row = _pad_lanes(dn_dt_bias[0])
    dnorm = dn_norm[0][None, :]

    tp = bp * lp
    xp2 = x_prompt.reshape(tp, d)
    mk, mv, mkb, mvb = _memkv(mem_prompt.reshape(bp * MEM_TOKENS, d), g_mem, wkv, TM)
    proj_p = _proj(xp2, g_mix, w_r, TM, NW // 3)
    proj_p3 = proj_p.reshape(bp, lp, NW)
    yad_p, ym_p, ca_p, s_p = _branch_prompt(proj_p3, mkb.reshape(bp, MEM_TOKENS, XA_W),
                                            mvb.reshape(bp, MEM_TOKENS, XA_W), caw, dcw, alog_row, dtb_row,
                                            dnorm, NB_P)
    x1_p = _merge(yad_p.reshape(tp, A_W + DN_V), ym_p.reshape(tp, XA_W), proj_p, xp2, wb, wo, TM)
    y_p = _ffn(x1_p, g_ffn, wu, wd, g_fin, TM).reshape(bp, lp, d)
    dc_p = proj_p3[:, lp - (DN_CONV_K - 1):, OFF_QKV:OFF_QKV + DN_CONV_CH]

    ts = bs * ls
    xs2 = x_sample.reshape(ts, d)
    proj_s = _proj(xs2, g_mix, w_r, ts, NW // 3)
    proj_s3 = proj_s.reshape(bs, ls, NW)
    ea = jnp.pad(state_conv_a[0], ((0, 0), (0, ls - (CONV_A_K - 1)), (0, 0))).reshape(ts, A_W)
    eq = jnp.pad(state_dn_conv[0], ((0, 0), (0, ls - (DN_CONV_K - 1)), (0, 0))).reshape(ts, DN_CONV_CH)
    yad_s, u_s, s_s = _branch_sample(proj_s, ea, eq, state_dn[0], caw, dcw, alog_row, dtb_row, dnorm)
    ym_s = _attn_sample(proj_s3, cache_mem_k.reshape(bs, MEM_TOKENS * XA_HEADS, XA_DH),
                        cache_mem_v.reshape(bs, MEM_TOKENS * XA_HEADS, XA_DH), 8)
    x1_s = _merge(yad_s, ym_s.reshape(ts, XA_W), proj_s, xs2, wb, wo, ts)
    y_s = _ffn(x1_s, g_ffn, wu, wd, g_fin, ts).reshape(bs, ls, d)
    ca_s = u_s.reshape(bs, ls, A_W)[:, ls - (CONV_A_K - 1):]
    dc_s = proj_s3[:, ls - (DN_CONV_K - 1):, OFF_QKV:OFF_QKV + DN_CONV_CH]

    return (y_p, y_s, ca_p[None], dc_p[None], s_p[None],
            mk.reshape(1, bp, MEM_TOKENS, XA_HEADS, XA_DH), mv.reshape(1, bp, MEM_TOKENS, XA_HEADS, XA_DH),
            ca_s[None], dc_s[None], s_s[None])
```

```python
import functools

import jax
import jax.numpy as jnp
from jax import lax
from jax.experimental import pallas as pl
from jax.experimental.pallas import tpu as pltpu

F32 = jnp.float32
BF16 = jnp.bfloat16

D_MODEL = 1024
A_W = 512
CONV_A_K = 3
DN_HEADS = 4
DN_DK = 128
DN_DV = 128
DN_QK = DN_HEADS * DN_DK
DN_V = DN_HEADS * DN_DV
DN_CONV_CH = 2 * DN_QK + DN_V
DN_CONV_K = 4
MEM_TOKENS = 256
XA_HEADS = 4
XA_DH = 128
XA_W = XA_HEADS * XA_DH
D_FF = 2816
EPS = 1e-6

LANE = 128
CHUNK = 128
TAIL = 8
NB_P = 4
TM = 512

W1 = 3 * A_W + DN_CONV_CH + DN_V
OFF_QKV = 3 * A_W
OFF_Z = OFF_QKV + DN_CONV_CH
W2 = XA_W + 3 * D_MODEL
OFF_G = XA_W
N_AB = 2 * DN_HEADS

VMEM_LIMIT = 56 * 1024 * 1024


def _cparams(sem):
    return pltpu.CompilerParams(dimension_semantics=sem, vmem_limit_bytes=VMEM_LIMIT)


def _resident(shape):
    return pl.BlockSpec(shape, lambda *_: (0,) * len(shape), pipeline_mode=pl.Buffered(1))


def _mm(a, b):
    return jnp.dot(a.astype(BF16), b.astype(BF16), preferred_element_type=F32)


def _mm_nt(a, b):
    return lax.dot_general(a.astype(BF16), b.astype(BF16), (((1,), (1,)), ((), ())),
                           preferred_element_type=F32)


def _mm_tn(a, b):
    return lax.dot_general(a.astype(BF16), b.astype(BF16), (((0,), (0,)), ((), ())),
                           preferred_element_type=F32)


def _split3(x):
    hi = x.astype(BF16)
    r1 = x - hi.astype(F32)
    mid = r1.astype(BF16)
    lo = (r1 - mid.astype(F32)).astype(BF16)
    return jnp.concatenate([hi, mid, lo], axis=1)


def _mm_exact01(m01, x):
    n = x.shape[1]
    r = jnp.dot(m01.astype(BF16), _split3(x), preferred_element_type=F32)
    return (r[:, :n] + r[:, n:2 * n]) + r[:, 2 * n:]


def _rms(x, g):
    return x * lax.rsqrt(jnp.mean(x * x, axis=-1, keepdims=True) + EPS) * g


def _silu(x):
    return x * jax.nn.sigmoid(x)


def _softplus(x):
    return jnp.maximum(x, 0.0) + jnp.log1p(jnp.exp(-jnp.abs(x)))


def _memkv_kernel(x_ref, g_ref, w_ref, k_ref, v_ref, kb_ref, vb_ref):
    xn = _rms(x_ref[...], g_ref[...]).astype(BF16)
    kv = jnp.dot(xn, w_ref[...], preferred_element_type=F32)
    k = kv[:, :XA_W]
    v = kv[:, XA_W:]
    k_ref[...] = k
    v_ref[...] = v
    kb_ref[...] = k.astype(BF16)
    vb_ref[...] = v.astype(BF16)


def _memkv(mem2d, gain, w, tm):
    t, d = mem2d.shape
    blk = pl.BlockSpec((tm, XA_W), lambda i: (i, 0))
    return pl.pallas_call(
        _memkv_kernel,
        grid=(t // tm,),
        in_specs=[pl.BlockSpec((tm, d), lambda i: (i, 0)),
                  pl.BlockSpec((1, d), lambda i: (0, 0)),
                  pl.BlockSpec((d, 2 * XA_W), lambda i: (0, 0))],
        out_specs=[blk, blk, blk, blk],
        out_shape=[jax.ShapeDtypeStruct((t, XA_W), F32), jax.ShapeDtypeStruct((t, XA_W), F32),
                   jax.ShapeDtypeStruct((t, XA_W), BF16), jax.ShapeDtypeStruct((t, XA_W), BF16)],
        compiler_params=_cparams(("arbitrary",)),
        name="memkv",
    )(mem2d, gain, w)


def _log2(n):
    return n.bit_length() - 1


def _dn_masks(seg):
    r = lax.broadcasted_iota(jnp.int32, (CHUNK, CHUNK), 0)
    c = lax.broadcasted_iota(jnp.int32, (CHUNK, CHUNK), 1)
    ls = _log2(seg)
    same = (r >> ls) == (c >> ls)
    base = min(8, seg)
    lb = _log2(base)
    m = {
        "same": same.astype(F32),
        "causal": (same & (r >= c)).astype(F32),
        "strict": (same & (r > c)).astype(F32),
        "eye": (r == c).astype(F32),
        "diag": ((r >> lb) == (c >> lb)).astype(F32),
        "off": {},
        "base": base,
    }
    s = base
    while s < seg:
        l1, l2 = _log2(s), _log2(2 * s)
        m["off"][s] = (((r >> l2) == (c >> l2)) & ((r >> l1) != (c >> l1))).astype(F32)
        s *= 2
    return m


def _each(f, *lists):
    return [f(*args) for args in zip(*lists)]


def _tri_inv(a_list, m, seg):
    b = _each(lambda a: -(a * m["diag"]), a_list)
    p = _each(lambda x: m["eye"] + x, b)
    b2 = _each(_mm, b, b)
    p = _each(lambda x, y: x + _mm(x, y), p, b2)
    if m["base"] == 8:
        b4 = _each(_mm, b2, b2)
        p = _each(lambda x, y: x + _mm(x, y), p, b4)
    s = m["base"]
    while s < seg:
        x = _each(lambda a, t, s=s: _mm(a * m["off"][s], t), a_list, p)
        p = _each(lambda t, y: t - _mm(t, y), p, x)
        s *= 2
    return p


def _dn_gates(ab, alog_row, dtb_row, m):
    g = -jnp.exp(alog_row) * _softplus(ab + dtb_row)
    beta = jax.nn.sigmoid(ab)
    d = _mm_exact01(m["causal"], g)
    dl = _mm_exact01(m["same"], g)
    return d, dl, beta


def _l2n(x):
    return x * lax.rsqrt(jnp.sum(x * x, axis=-1, keepdims=True) + EPS)


def _dn_intra(q, k, v, d_col, d_row, beta_col, m, seg):
    q = _each(lambda x: _l2n(x) * (DN_DK ** -0.5), q)
    k = _each(_l2n, k)
    gamma = _each(lambda dc, dr: jnp.exp((dc - dr) * m["causal"]) * m["causal"], d_col, d_row)
    kk = _each(_mm_nt, k, k)
    a = _each(lambda bc, x, g: (bc * x) * g * m["strict"], beta_col, kk, gamma)
    t = _tri_inv(a, m, seg)
    rhs = _each(lambda vv, kx, bc, dc: jnp.concatenate([vv * bc, kx * (bc * jnp.exp(dc))], axis=1),
                v, k, beta_col, d_col)
    sol = _each(_mm, t, rhs)
    u = [x[:, :DN_DV] for x in sol]
    w = [x[:, DN_DV:] for x in sol]
    qk = _each(lambda x, y, g: _mm_nt(x, y) * g, q, k, gamma)
    return q, k, u, w, qk


def _dn_out(o, z, dnorm):
    return _rms(o, dnorm) * _silu(z)


def _head_lists(qkv, z, d, dl, beta):
    d_t = d.T
    out = [[] for _ in range(8)]
    for h in range(DN_HEADS):
        vals = (qkv[:, h * LANE:(h + 1) * LANE],
                qkv[:, DN_QK + h * LANE:DN_QK + (h + 1) * LANE],
                qkv[:, 2 * DN_QK + h * LANE:2 * DN_QK + (h + 1) * LANE],
                z[:, h * LANE:(h + 1) * LANE],
                d[:, h:h + 1], d_t[h:h + 1, :], dl[:, h:h + 1], beta[:, DN_HEADS + h:DN_HEADS + h + 1])
        for lst, val in zip(out, vals):
            lst.append(val)
    return out


def _front_prompt_kernel(x_ref, gmix_ref, w1_ref, w2_ref, wab_ref, mk_ref, mv_ref, caw_ref, dcw_ref,
                         alog_ref, dtb_ref, dnorm_ref, wb_ref, wo_ref,
                         x1_ref, ca_ref, dc_ref, s_ref, ubuf, qbuf, ybuf, *, nb):
    c = CHUNK
    rows = nb * c
    t_idx = pl.program_id(1)

    @pl.when(t_idx == 0)
    def _():
        ubuf[:, 0:TAIL, :] = jnp.zeros((nb, TAIL, A_W), F32)
        qbuf[:, 0:TAIL, :] = jnp.zeros((nb, TAIL, DN_CONV_CH), F32)
        s_ref[...] = jnp.zeros(s_ref.shape, F32)

    x = x_ref[...].reshape(rows, D_MODEL)
    xn = _rms(x, gmix_ref[...]).astype(BF16)
    proj = lambda w_ref, lo, hi: jnp.dot(xn, w_ref[:, lo:hi], preferred_element_type=F32)

    m = _dn_masks(c)
    caw = caw_ref[...]
    dcw = dcw_ref[...]
    dnorm = dnorm_ref[...]
    pa_all = proj(w1_ref, 0, OFF_QKV)
    pq_all = proj(w1_ref, OFF_QKV, OFF_Z)
    pz_all = proj(w1_ref, OFF_Z, W1)
    pxq_all = proj(w2_ref, 0, OFF_G)
    pab_all = jnp.dot(xn, wab_ref[...], preferred_element_type=F32)
    lists = [[] for _ in range(8)]
    xq = []
    for b in range(nb):
        rb = slice(b * c, (b + 1) * c)
        pa = pa_all[rb]
        u_in = pa[:, A_W:2 * A_W] * pa[:, 2 * A_W:3 * A_W]
        ubuf[b, TAIL:TAIL + c, :] = u_in
        conv = (caw[0:1] * ubuf[b, TAIL - 2:TAIL - 2 + c, :] + caw[1:2] * ubuf[b, TAIL - 1:TAIL - 1 + c, :]
                + caw[2:3] * u_in)
        ybuf[rb, 0:A_W] = (pa[:, 0:A_W] * conv).astype(BF16)
        ca_ref[b] = ubuf[b, TAIL + c - 2:TAIL + c, :]
        ubuf[b, 0:TAIL, :] = ubuf[b, c:c + TAIL, :]

        qkv_in = pq_all[rb]
        qbuf[b, TAIL:TAIL + c, :] = qkv_in
        qkv = _silu(dcw[0:1] * qbuf[b, TAIL - 3:TAIL - 3 + c, :] + dcw[1:2] * qbuf[b, TAIL - 2:TAIL - 2 + c, :]
                    + dcw[2:3] * qbuf[b, TAIL - 1:TAIL - 1 + c, :] + dcw[3:4] * qkv_in)
        dc_ref[b] = qbuf[b, TAIL + c - 3:TAIL + c, :]
        qbuf[b, 0:TAIL, :] = qbuf[b, c:c + TAIL, :]
        d, dl, beta = _dn_gates(pab_all[rb], alog_ref[...], dtb_ref[...], m)
        for lst, val in zip(lists, _head_lists(qkv, pz_all[rb], d, dl, beta)):
            lst.extend(val)
        xq.extend(pxq_all[rb, h * LANE:(h + 1) * LANE] for h in range(XA_HEADS))

    q, k, v, z, d_col, d_row, dl_col, beta_col = lists
    idx = [(b, h) for b in range(nb) for h in range(DN_HEADS)]
    q, k, u, w, qk = _dn_intra(q, k, v, d_col, d_row, beta_col, m, c)
    s_old = [s_ref[b, h] for b, h in idx]
    ws = _each(lambda wx, qx, dc, s: _mm(jnp.concatenate([wx, qx * jnp.exp(dc)], axis=0), s), w, q, d_col, s_old)
    vn = _each(lambda ux, x_: ux - x_[:c], u, ws)
    o = _each(lambda x_, y, vx: x_[c:] + _mm(y, vx), ws, qk, vn)
    s_new = _each(lambda s, dlc, kx, dc, vx: s * jnp.exp(dlc[0:1, :]) + _mm_tn(kx * jnp.exp(dlc - dc), vx),
                  s_old, dl_col, k, d_col, vn)
    for (b, h), sx, ox, zx in zip(idx, s_new, o, z):
        s_ref[b, h] = sx
        ybuf[b * c:(b + 1) * c, A_W + h * LANE:A_W + (h + 1) * LANE] = _dn_out(ox, zx, dnorm).astype(BF16)

    sc = _each(lambda x_, bh: _mm_nt(x_, mk_ref[bh[0], :, bh[1] * LANE:(bh[1] + 1) * LANE]) * (XA_DH ** -0.5),
               xq, idx)
    e = _each(lambda x_: jnp.exp(x_ - jnp.max(x_, axis=-1, keepdims=True)), sc)
    p = _each(lambda x_: x_ / jnp.sum(x_, axis=-1, keepdims=True), e)
    for (b, h), px in zip(idx, p):
        ybuf[b * c:(b + 1) * c, A_W + DN_V + h * LANE:A_W + DN_V + (h + 1) * LANE] = (
            _mm(px, mv_ref[b, :, h * LANE:(h + 1) * LANE]).astype(BF16))

    merged = None
    for j, (lo, hi) in enumerate(((0, A_W), (A_W, A_W + DN_V), (A_W + DN_V, A_W + DN_V + XA_W))):
        gate = jax.nn.sigmoid(proj(w2_ref, OFF_G + j * D_MODEL, OFF_G + (j + 1) * D_MODEL))
        term = gate * jnp.dot(ybuf[:, lo:hi], wb_ref[lo:hi, :], preferred_element_type=F32)
        merged = term if merged is None else merged + term
    x1 = x + jnp.dot(merged.astype(BF16), wo_ref[...], preferred_element_type=F32)
    x1_ref[...] = x1.reshape(nb, c, D_MODEL)


def _front_prompt(x, gmix, w1, w2, wab, mkb, mvb, caw, dcw, alog_row, dtb_row, dnorm, wb, wo, nb):
    bsz, length, d = x.shape
    c = CHUNK
    return pl.pallas_call(
        functools.partial(_front_prompt_kernel, nb=nb),
        grid=(bsz // nb, length // c),
        in_specs=[pl.BlockSpec((nb, c, d), lambda g, t: (g, t, 0)),
                  _resident((1, d)), _resident(w1.shape), _resident(w2.shape), _resident(wab.shape),
                  pl.BlockSpec((nb, MEM_TOKENS, XA_W), lambda g, t: (g, 0, 0)),
                  pl.BlockSpec((nb, MEM_TOKENS, XA_W), lambda g, t: (g, 0, 0)),
                  _resident((CONV_A_K, A_W)), _resident((DN_CONV_K, DN_CONV_CH)),
                  _resident((1, LANE)), _resident((1, LANE)), _resident((1, LANE)),
                  _resident(wb.shape), _resident(wo.shape)],
        out_specs=[pl.BlockSpec((nb, c, d), lambda g, t: (g, t, 0)),
                   pl.BlockSpec((nb, CONV_A_K - 1, A_W), lambda g, t: (g, 0, 0)),
                   pl.BlockSpec((nb, DN_CONV_K - 1, DN_CONV_CH), lambda g, t: (g, 0, 0)),
                   pl.BlockSpec((nb, DN_HEADS, DN_DK, DN_DV), lambda g, t: (g, 0, 0, 0))],
        out_shape=[jax.ShapeDtypeStruct((bsz, length, d), F32),
                   jax.ShapeDtypeStruct((bsz, CONV_A_K - 1, A_W), F32),
                   jax.ShapeDtypeStruct((bsz, DN_CONV_K - 1, DN_CONV_CH), F32),
                   jax.ShapeDtypeStruct((bsz, DN_HEADS, DN_DK, DN_DV), F32)],
        scratch_shapes=[pltpu.VMEM((nb, c + TAIL, A_W), F32), pltpu.VMEM((nb, c + TAIL, DN_CONV_CH), F32),
                        pltpu.VMEM((nb * c, A_W + DN_V + XA_W), BF16)],
        compiler_params=_cparams(("arbitrary", "arbitrary")),
        name="front_prompt",
    )(x, gmix, w1, w2, wab, mkb, mvb, caw, dcw, alog_row, dtb_row, dnorm, wb, wo)


def _proj_kernel(x_ref, g_ref, w1_ref, w2_ref, wab_ref, p1_ref, p2_ref, pab_ref):
    xn = _rms(x_ref[...], g_ref[...]).astype(BF16)
    p1_ref[...] = jnp.dot(xn, w1_ref[...], preferred_element_type=F32)
    p2_ref[...] = jnp.dot(xn, w2_ref[...], preferred_element_type=F32)
    pab_ref[...] = jnp.dot(xn, wab_ref[...], preferred_element_type=F32)


def _proj(x2d, gain, w1, w2, wab, tm):
    t, d = x2d.shape
    row = lambda n: pl.BlockSpec((tm, n), lambda i: (i, 0))
    return pl.pallas_call(
        _proj_kernel,
        grid=(t // tm,),
        in_specs=[row(d), _resident((1, d)), _resident(w1.shape), _resident(w2.shape), _resident(wab.shape)],
        out_specs=[row(W1), row(W2), row(LANE)],
        out_shape=[jax.ShapeDtypeStruct((t, W1), F32), jax.ShapeDtypeStruct((t, W2), F32),
                   jax.ShapeDtypeStruct((t, LANE), F32)],
        compiler_params=_cparams(("arbitrary",)),
        name="proj",
    )(x2d, gain, w1, w2, wab)


SEQ_S = 4
NB_S = CHUNK // SEQ_S


def _seg_conv(x, e, wts, width, tmod):
    rows = x.shape[0]
    acc = None
    for i in range(width):
        s = width - 1 - i
        term = x if s == 0 else jnp.where(tmod >= s, pltpu.roll(x, s, 0), 0.0)
        if i < width - 1:
            hist = e if i == 0 else pltpu.roll(e, rows - i, 0)
            term = term + jnp.where(tmod < SEQ_S - i, hist, 0.0)
        term = wts[i:i + 1] * term
        acc = term if acc is None else acc + term
    return acc


def _branch_sample_kernel(pa_ref, pq_ref, pz_ref, pab_ref, ea_ref, eq_ref, s0_ref, caw_ref, dcw_ref,
                          alog_ref, dtb_ref, dnorm_ref, yad_ref, u_ref, s_ref):
    c = CHUNK
    m = _dn_masks(SEQ_S)
    tmod = lax.broadcasted_iota(jnp.int32, (c, 1), 0) & (SEQ_S - 1)

    pa = pa_ref[...]
    u_in = pa[:, A_W:2 * A_W] * pa[:, 2 * A_W:3 * A_W]
    u_ref[...] = u_in
    conv = _seg_conv(u_in, ea_ref[...], caw_ref[...], CONV_A_K, tmod)
    yad_ref[:, 0:A_W] = (pa[:, 0:A_W] * conv).astype(BF16)

    qkv = _silu(_seg_conv(pq_ref[...], eq_ref[...], dcw_ref[...], DN_CONV_K, tmod))
    d, dl, beta = _dn_gates(pab_ref[...], alog_ref[...], dtb_ref[...], m)
    dec_t = jnp.exp(dl).T
    dnorm = dnorm_ref[...]

    wide = NB_S * DN_DK
    er = lax.broadcasted_iota(jnp.int32, (c, wide), 0)
    ec = lax.broadcasted_iota(jnp.int32, (c, wide), 1)
    mexp = ((er >> 2) == (ec >> 7)).astype(F32)
    mexp2 = jnp.concatenate([mexp, mexp], axis=0)
    tr = lax.broadcasted_iota(jnp.int32, (wide, c), 0)
    tc = lax.broadcasted_iota(jnp.int32, (wide, c), 1)
    mexp_t = ((tr >> 7) == (tc >> 2)).astype(F32)

    q, k, v, z, d_col, d_row, dl_col, beta_col = _head_lists(qkv, pz_ref[...], d, dl, beta)
    q, k, u, w, qk = _dn_intra(q, k, v, d_col, d_row, beta_col, m, SEQ_S)
    heads = list(range(DN_HEADS))
    s_old = [s0_ref[:, h].reshape(wide, DN_DV) for h in heads]
    x_exp = _each(lambda wx, qx, dc: jnp.concatenate([jnp.concatenate([wx, qx * jnp.exp(dc)], axis=0)] * NB_S,
                                                     axis=1) * mexp2, w, q, d_col)
    ws = _each(_mm, x_exp, s_old)
    vn = _each(lambda ux, x: ux - x[:c], u, ws)
    o = _each(lambda x, y, vx: x[c:] + _mm(y, vx), ws, qk, vn)
    k_exp = _each(lambda kx, dlc, dc: jnp.concatenate([(kx * jnp.exp(dlc - dc)).T] * NB_S, axis=0) * mexp_t,
                  k, dl_col, d_col)
    dec = [jnp.concatenate([jnp.broadcast_to(dec_t[h:h + 1, SEQ_S * b:SEQ_S * b + 1], (DN_DK, DN_DV))
                            for b in range(NB_S)], axis=0) for h in heads]
    s_new = _each(lambda s, dx, kx, vx: s * dx + _mm(kx, vx), s_old, dec, k_exp, vn)
    for h, sx, ox, zx in zip(heads, s_new, o, z):
        s_ref[:, h] = sx.reshape(NB_S, DN_DK, DN_DV)
        yad_ref[:, A_W + h * LANE:A_W + (h + 1) * LANE] = _dn_out(ox, zx, dnorm).astype(BF16)


def _branch_sample(p1, pab, ea, eq, state, caw, dcw, alog_row, dtb_row, dnorm):
    t = p1.shape[0]
    c = CHUNK
    full = lambda shape: pl.BlockSpec(shape, lambda i: (0,) * len(shape))
    return pl.pallas_call(
        _branch_sample_kernel,
        grid=(t // c,),
        in_specs=[pl.BlockSpec((c, 3 * A_W), lambda i: (i, 0)),
                  pl.BlockSpec((c, DN_CONV_CH), lambda i: (i, OFF_QKV // DN_CONV_CH)),
                  pl.BlockSpec((c, DN_V), lambda i: (i, OFF_Z // DN_V)),
                  pl.BlockSpec((c, LANE), lambda i: (i, 0)),
                  pl.BlockSpec((c, A_W), lambda i: (i, 0)),
                  pl.BlockSpec((c, DN_CONV_CH), lambda i: (i, 0)),
                  pl.BlockSpec((NB_S, DN_HEADS, DN_DK, DN_DV), lambda i: (i, 0, 0, 0)),
                  full((CONV_A_K, A_W)), full((DN_CONV_K, DN_CONV_CH)),
                  full((1, LANE)), full((1, LANE)), full((1, LANE))],
        out_specs=[pl.BlockSpec((c, A_W + DN_V), lambda i: (i, 0)),
                   pl.BlockSpec((c, A_W), lambda i: (i, 0)),
                   pl.BlockSpec((NB_S, DN_HEADS, DN_DK, DN_DV), lambda i: (i, 0, 0, 0))],
        out_shape=[jax.ShapeDtypeStruct((t, A_W + DN_V), BF16),
                   jax.ShapeDtypeStruct((t, A_W), F32),
                   jax.ShapeDtypeStruct(state.shape, F32)],
        compiler_params=_cparams(("arbitrary",)),
        name="branch_sample",
    )(p1, p1, p1, pab, ea, eq, state, caw, dcw, alog_row, dtb_row, dnorm)


def _attn_sample_kernel(q_ref, k_ref, v_ref, o_ref):
    for h in range(XA_HEADS):
        sl = slice(h * LANE, (h + 1) * LANE)
        rows = pl.ds(h, MEM_TOKENS, stride=XA_HEADS)
        q = q_ref[:, :, sl].astype(BF16)
        s = jnp.einsum("bqd,bkd->bqk", q, k_ref[:, rows, :].astype(BF16),
                       preferred_element_type=F32) * (XA_DH ** -0.5)
        e = jnp.exp(s - jnp.max(s, axis=-1, keepdims=True))
        p = e / jnp.sum(e, axis=-1, keepdims=True)
        o_ref[:, :, sl] = jnp.einsum("bqk,bkd->bqd", p.astype(BF16), v_ref[:, rows, :].astype(BF16),
                                     preferred_element_type=F32).astype(BF16)


def _attn_sample(q3, ck, cv, nb):
    bsz, length, _ = q3.shape
    return pl.pallas_call(
        _attn_sample_kernel,
        grid=(bsz // nb,),
        in_specs=[pl.BlockSpec((nb, length, XA_W), lambda i: (i, 0, 0)),
                  pl.BlockSpec((nb, MEM_TOKENS * XA_HEADS, XA_DH), lambda i: (i, 0, 0)),
                  pl.BlockSpec((nb, MEM_TOKENS * XA_HEADS, XA_DH), lambda i: (i, 0, 0))],
        out_specs=pl.BlockSpec((nb, length, XA_W), lambda i: (i, 0, 0)),
        out_shape=jax.ShapeDtypeStruct((bsz, length, XA_W), BF16),
        compiler_params=_cparams(("arbitrary",)),
        name="attn_sample",
    )(q3, ck, cv)


def _merge_kernel(yad_ref, ym_ref, p2_ref, x_ref, wb_ref, wo_ref, o_ref):
    yad = yad_ref[...]
    gate = lambda j: jax.nn.sigmoid(p2_ref[:, OFF_G + j * D_MODEL:OFF_G + (j + 1) * D_MODEL])
    merged = (gate(0) * jnp.dot(yad[:, :A_W], wb_ref[0:A_W, :], preferred_element_type=F32)
              + gate(1) * jnp.dot(yad[:, A_W:], wb_ref[A_W:A_W + DN_V, :], preferred_element_type=F32)
              + gate(2) * jnp.dot(ym_ref[...], wb_ref[A_W + DN_V:, :], preferred_element_type=F32))
    o_ref[...] = x_ref[...] + jnp.dot(merged.astype(BF16), wo_ref[...], preferred_element_type=F32)


def _merge(yad, ym, p2, x2d, wb, wo, tm):
    t, d = x2d.shape
    row = lambda n: pl.BlockSpec((tm, n), lambda i: (i, 0))
    return pl.pallas_call(
        _merge_kernel,
        grid=(t // tm,),
        in_specs=[row(A_W + DN_V), row(XA_W), row(W2), row(d), _resident(wb.shape), _resident(wo.shape)],
        out_specs=row(d),
        out_shape=jax.ShapeDtypeStruct((t, d), F32),
        compiler_params=_cparams(("arbitrary",)),
        name="merge",
    )(yad, ym, p2, x2d, wb, wo)


FF_SPLIT = 2
FF_BLK = D_FF // FF_SPLIT


def _ffn_kernel(x_ref, gf_ref, wu_ref, wd_ref, gl_ref, o_ref):
    x = x_ref[...]
    xn = _rms(x, gf_ref[...]).astype(BF16)
    acc = x
    for j in range(FF_SPLIT):
        gate = jnp.dot(xn, wu_ref[:, j * FF_BLK:(j + 1) * FF_BLK], preferred_element_type=F32)
        up = jnp.dot(xn, wu_ref[:, D_FF + j * FF_BLK:D_FF + (j + 1) * FF_BLK], preferred_element_type=F32)
        hid = (_silu(gate) * up).astype(BF16)
        acc = acc + jnp.dot(hid, wd_ref[j * FF_BLK:(j + 1) * FF_BLK, :], preferred_element_type=F32)
    o_ref[...] = _rms(acc, gl_ref[...])


def _ffn(x2d, gf, wu, wd, gl, tm):
    t, d = x2d.shape
    return pl.pallas_call(
        _ffn_kernel,
        grid=(t // tm,),
        in_specs=[pl.BlockSpec((tm, d), lambda i: (i, 0)),
                  _resident((1, d)), _resident(wu.shape), _resident(wd.shape), _resident((1, d))],
        out_specs=pl.BlockSpec((tm, d), lambda i: (i, 0)),
        out_shape=jax.ShapeDtypeStruct((t, d), F32),
        compiler_params=_cparams(("arbitrary",)),
        name="ffn",
    )(x2d, gf, wu, wd, gl)


def _pad_lanes(v):
    return jnp.zeros((1, LANE), F32).at[0, :v.shape[0]].set(v.astype(F32))


def kernel(x_prompt, x_sample, mem_prompt, state_conv_a, state_dn_conv, state_dn, cache_mem_k, cache_mem_v,
           norm_mix, w_in, conv_a_w, dn_conv_w, dn_a_log, dn_dt_bias, dn_norm, norm_mem, w_mem_kv, w_branch,
           w_o, norm_ffn, w_ffn_up, w_ffn_down, norm_final):
    bp, lp, d = x_prompt.shape
    bs, ls, _ = x_sample.shape
    assert norm_mix.shape[0] == 1 and ls == SEQ_S and lp % CHUNK == 0 and (bs * ls) % CHUNK == 0

    w = w_in[0]
    w1 = w[:, :W1].astype(BF16)
    w2 = w[:, W1 + N_AB:].astype(BF16)
    wab = jnp.pad(w[:, W1:W1 + N_AB], ((0, 0), (0, LANE - N_AB))).astype(BF16)
    wb = w_branch[0].astype(BF16)
    wo = w_o[0].astype(BF16)
    wu = w_ffn_up[0].astype(BF16)
    wd = w_ffn_down[0].astype(BF16)
    wkv = w_mem_kv[0].astype(BF16)
    g_mix = norm_mix[0][None, :]
    g_ffn = norm_ffn[0][None, :]
    g_fin = norm_final[None, :]
    g_mem = norm_mem[0][None, :]
    caw = conv_a_w[0]
    dcw = dn_conv_w[0]
    alog_row = _pad_lanes(dn_a_log[0])
    dtb_row = _pad_lanes(dn_dt_bias[0])
    dnorm = dn_norm[0][None, :]

    tp = bp * lp
    mk, mv, mkb, mvb = _memkv(mem_prompt.reshape(bp * MEM_TOKENS, d), g_mem, wkv, TM)
    x1_p, ca_p, dc_p, s_p = _front_prompt(x_prompt, g_mix, w1, w2, wab, mkb.reshape(bp, MEM_TOKENS, XA_W),
                                          mvb.reshape(bp, MEM_TOKENS, XA_W), caw, dcw, alog_row, dtb_row, dnorm,
                                          wb, wo, NB_P)
    y_p = _ffn(x1_p.reshape(tp, d), g_ffn, wu, wd, g_fin, TM).reshape(bp, lp, d)

    ts = bs * ls
    xs2 = x_sample.reshape(ts, d)
    p1_s, p2_s, pab_s = _proj(xs2, g_mix, w1, w2, wab, CHUNK)
    ea = jnp.pad(state_conv_a[0], ((0, 0), (0, ls - (CONV_A_K - 1)), (0, 0))).reshape(ts, A_W)
    eq = jnp.pad(state_dn_conv[0], ((0, 0), (0, ls - (DN_CONV_K - 1)), (0, 0))).reshape(ts, DN_CONV_CH)
    yad_s, u_s, s_s = _branch_sample(p1_s, pab_s, ea, eq, state_dn[0], caw, dcw, alog_row, dtb_row, dnorm)
    ym_s = _attn_sample(p2_s[:, :XA_W].reshape(bs, ls, XA_W),
                        cache_mem_k.reshape(bs, MEM_TOKENS * XA_HEADS, XA_DH),
                        cache_mem_v.reshape(bs, MEM_TOKENS * XA_HEADS, XA_DH), 8)
    x1_s = _merge(yad_s, ym_s.reshape(ts, XA_W), p2_s, xs2, wb, wo, ts)
    y_s = _ffn(x1_s, g_ffn, wu, wd, g_fin, ts).reshape(bs, ls, d)
    ca_s = u_s.reshape(bs, ls, A_W)[:, ls - (CONV_A_K - 1):]
    dc_s = p1_s.reshape(bs, ls, W1)[:, ls - (DN_CONV_K - 1):, OFF_QKV:OFF_Z]

    return (y_p, y_s, ca_p[None], dc_p[None], s_p[None],
            mk.reshape(1, bp, MEM_TOKENS, XA_HEADS, XA_DH), mv.reshape(1, bp, MEM_TOKENS, XA_HEADS, XA_DH),
            ca_s[None], dc_s[None], s_s[None])
```

```python
import functools

import jax
import jax.numpy as jnp
from jax import lax
from jax.experimental import pallas as pl
from jax.experimental.pallas import tpu as pltpu

F32 = jnp.float32
BF16 = jnp.bfloat16

D_MODEL = 1024
A_W = 512
CONV_A_K = 3
DN_HEADS = 4
DN_DK = 128
DN_DV = 128
DN_QK = DN_HEADS * DN_DK
DN_V = DN_HEADS * DN_DV
DN_CONV_CH = 2 * DN_QK + DN_V
DN_CONV_K = 4
MEM_TOKENS = 256
XA_HEADS = 4
XA_DH = 128
XA_W = XA_HEADS * XA_DH
D_FF = 2816
EPS = 1e-6

LANE = 128
CHUNK = 128
TAIL = 8
NB_P = 4
TM = 512
GATE_BLK = 256
PROJ_BLK = 512

W1 = 3 * A_W + DN_CONV_CH + DN_V
OFF_QKV = 3 * A_W
OFF_Z = OFF_QKV + DN_CONV_CH
W2 = XA_W + 3 * D_MODEL
OFF_G = XA_W
N_AB = 2 * DN_HEADS
P_Z = 3 * A_W
P_XQ = P_Z + DN_V
P_W = P_XQ + XA_W

VMEM_LIMIT = 60 * 1024 * 1024


def _cparams(sem):
    return pltpu.CompilerParams(dimension_semantics=sem, vmem_limit_bytes=VMEM_LIMIT)


def _resident(shape):
    return pl.BlockSpec(shape, lambda *_: (0,) * len(shape), pipeline_mode=pl.Buffered(1))


def _mm(a, b):
    return jnp.dot(a.astype(BF16), b.astype(BF16), preferred_element_type=F32)


def _mm_nt(a, b):
    return lax.dot_general(a.astype(BF16), b.astype(BF16), (((1,), (1,)), ((), ())),
                           preferred_element_type=F32)


def _mm_tn(a, b):
    return lax.dot_general(a.astype(BF16), b.astype(BF16), (((0,), (0,)), ((), ())),
                           preferred_element_type=F32)


def _split3(x):
    hi = x.astype(BF16)
    r1 = x - hi.astype(F32)
    mid = r1.astype(BF16)
    lo = (r1 - mid.astype(F32)).astype(BF16)
    return jnp.concatenate([hi, mid, lo], axis=1)


def _mm_exact01(m01, x):
    n = x.shape[1]
    r = jnp.dot(m01.astype(BF16), _split3(x), preferred_element_type=F32)
    return (r[:, :n] + r[:, n:2 * n]) + r[:, 2 * n:]


def _rms(x, g):
    return x * lax.rsqrt(jnp.mean(x * x, axis=-1, keepdims=True) + EPS) * g


def _silu(x):
    return x * jax.nn.sigmoid(x)


def _softplus(x):
    return jnp.maximum(x, 0.0) + jnp.log1p(jnp.exp(-jnp.abs(x)))


def _memkv_kernel(x_ref, g_ref, w_ref, k_ref, v_ref, kb_ref, vb_ref):
    xn = _rms(x_ref[...], g_ref[...]).astype(BF16)
    kv = jnp.dot(xn, w_ref[...], preferred_element_type=F32)
    k = kv[:, :XA_W]
    v = kv[:, XA_W:]
    k_ref[...] = k
    v_ref[...] = v
    kb_ref[...] = k.astype(BF16)
    vb_ref[...] = v.astype(BF16)


def _memkv(mem2d, gain, w, tm):
    t, d = mem2d.shape
    blk = pl.BlockSpec((tm, XA_W), lambda i: (i, 0))
    return pl.pallas_call(
        _memkv_kernel,
        grid=(t // tm,),
        in_specs=[pl.BlockSpec((tm, d), lambda i: (i, 0)),
                  pl.BlockSpec((1, d), lambda i: (0, 0)),
                  pl.BlockSpec((d, 2 * XA_W), lambda i: (0, 0))],
        out_specs=[blk, blk, blk, blk],
        out_shape=[jax.ShapeDtypeStruct((t, XA_W), F32), jax.ShapeDtypeStruct((t, XA_W), F32),
                   jax.ShapeDtypeStruct((t, XA_W), BF16), jax.ShapeDtypeStruct((t, XA_W), BF16)],
        compiler_params=_cparams(("arbitrary",)),
        name="memkv",
    )(mem2d, gain, w)


def _log2(n):
    return n.bit_length() - 1


def _dn_masks(seg):
    r = lax.broadcasted_iota(jnp.int32, (CHUNK, CHUNK), 0)
    c = lax.broadcasted_iota(jnp.int32, (CHUNK, CHUNK), 1)
    ls = _log2(seg)
    same = (r >> ls) == (c >> ls)
    base = min(8, seg)
    lb = _log2(base)
    m = {
        "same": same.astype(F32),
        "causal": (same & (r >= c)).astype(F32),
        "strict": (same & (r > c)).astype(F32),
        "eye": (r == c).astype(F32),
        "diag": ((r >> lb) == (c >> lb)).astype(F32),
        "off": {},
        "base": base,
    }
    s = base
    while s < seg:
        l1, l2 = _log2(s), _log2(2 * s)
        m["off"][s] = (((r >> l2) == (c >> l2)) & ((r >> l1) != (c >> l1))).astype(F32)
        s *= 2
    return m


def _each(f, *lists):
    return [f(*args) for args in zip(*lists)]


def _tri_inv(a_list, m, seg, tick):
    add = lambda x, y: x + y
    b = _each(lambda a: -(a * m["diag"]), a_list)
    p = _each(lambda x: m["eye"] + x, b)
    b2 = _each(_mm, b, b)
    tick()
    p = _each(add, p, _each(_mm, p, b2))
    tick()
    if m["base"] == 8:
        b4 = _each(_mm, b2, b2)
        tick()
        p = _each(add, p, _each(_mm, p, b4))
        tick()
    s = m["base"]
    while s < seg:
        x = _each(_mm, _each(lambda a, s=s: a * m["off"][s], a_list), p)
        tick()
        p = _each(lambda t, y: t - y, p, _each(_mm, p, x))
        tick()
        s *= 2
    return p


def _dn_gates(ab, alog_row, dtb_row, m):
    g = -jnp.exp(alog_row) * _softplus(ab + dtb_row)
    beta = jax.nn.sigmoid(ab)
    d = _mm_exact01(m["causal"], g)
    dl = _mm_exact01(m["same"], g)
    return d, dl, beta


def _l2n(x):
    return x * lax.rsqrt(jnp.sum(x * x, axis=-1, keepdims=True) + EPS)


def _dn_intra(q, k, v, d_col, d_row, beta_col, m, seg, tick=lambda: None):
    q = _each(lambda x: _l2n(x) * (DN_DK ** -0.5), q)
    k = _each(_l2n, k)
    gamma = _each(lambda dc, dr: jnp.exp((dc - dr) * m["causal"]) * m["causal"], d_col, d_row)
    kk = _each(_mm_nt, k, k)
    a = _each(lambda bc, x, g: (bc * x) * g * m["strict"], beta_col, kk, gamma)
    t = _tri_inv(a, m, seg, tick)
    rhs = _each(lambda vv, kx, bc, dc: jnp.concatenate([vv * bc, kx * (bc * jnp.exp(dc))], axis=1),
                v, k, beta_col, d_col)
    sol = _each(_mm, t, rhs)
    u = [x[:, :DN_DV] for x in sol]
    w = [x[:, DN_DV:] for x in sol]
    qk = _each(lambda x, g: x * g, _each(_mm_nt, q, k), gamma)
    return q, k, u, w, qk


def _dn_out(o, z, dnorm):
    return _rms(o, dnorm) * _silu(z)


def _head_lists(qkv, z, d, dl, beta):
    d_t = d.T
    out = [[] for _ in range(8)]
    for h in range(DN_HEADS):
        vals = (qkv[:, h * LANE:(h + 1) * LANE],
                qkv[:, DN_QK + h * LANE:DN_QK + (h + 1) * LANE],
                qkv[:, 2 * DN_QK + h * LANE:2 * DN_QK + (h + 1) * LANE],
                None if z is None else z[:, h * LANE:(h + 1) * LANE],
                d[:, h:h + 1], d_t[h:h + 1, :], dl[:, h:h + 1], beta[:, DN_HEADS + h:DN_HEADS + h + 1])
        for lst, val in zip(out, vals):
            lst.append(val)
    return out


def _causal_conv(x, tail_ref, b, wts, width):
    c, w = x.shape
    tiles = jnp.concatenate([tail_ref[b][None], x.reshape(c // TAIL, TAIL, w)], axis=0)
    sub = lax.broadcasted_iota(jnp.int32, (1, TAIL, 1), 1)
    acc = None
    for i in range(width):
        s = width - 1 - i
        if s == 0:
            y = tiles[1:]
        else:
            r = pltpu.roll(tiles, s, 1)
            y = jnp.where(sub >= s, r[1:], r[:-1])
        term = wts[i:i + 1][None] * y
        acc = term if acc is None else acc + term
    tail_ref[b] = tiles[c // TAIL]
    return acc.reshape(c, w)


def _front_prompt_kernel(x_ref, gmix_ref, w1_ref, w2_ref, wab_ref, mk_ref, mv_ref, caw_ref, dcw_ref,
                         alog_ref, dtb_ref, dnorm_ref, wb_ref, wo_ref,
                         x1_ref, ca_ref, dc_ref, s_ref, utail, qtail, ybuf, pbuf, gbuf, *, nb):
    c = CHUNK
    rows = nb * c
    t_idx = pl.program_id(1)

    @pl.when(t_idx == 0)
    def _():
        utail[...] = jnp.zeros(utail.shape, F32)
        qtail[...] = jnp.zeros(qtail.shape, F32)
        s_ref[...] = jnp.zeros(s_ref.shape, F32)

    x = x_ref[...].reshape(rows, D_MODEL)
    xn = _rms(x, gmix_ref[...]).astype(BF16)
    proj = lambda w_ref, lo, hi: jnp.dot(xn, w_ref[:, lo:hi], preferred_element_type=F32)

    queue = []

    def enqueue(dst, w_ref, src, dst_lo, width, act=None):
        def run():
            r = proj(w_ref, src, src + width)
            dst[:, dst_lo:dst_lo + width] = r if act is None else act(r)
        queue.append(run)

    for lo in range(0, 3 * A_W, PROJ_BLK):
        enqueue(pbuf, w1_ref, lo, lo, PROJ_BLK)
    enqueue(pbuf, w1_ref, OFF_Z, P_Z, DN_V)
    enqueue(pbuf, w2_ref, 0, P_XQ, XA_W)
    for lo in range(0, 3 * D_MODEL, GATE_BLK):
        enqueue(gbuf, w2_ref, OFF_G + lo, lo, GATE_BLK, jax.nn.sigmoid)
    queue.reverse()

    def tick():
        if queue:
            queue.pop()()

    m = _dn_masks(c)
    caw = caw_ref[...]
    dcw = dcw_ref[...]
    dnorm = dnorm_ref[...]
    pq_all = proj(w1_ref, OFF_QKV, OFF_Z)
    pab_all = jnp.dot(xn, wab_ref[...], preferred_element_type=F32)

    qkv_l, gates_l = [], []
    for b in range(nb):
        rb = slice(b * c, (b + 1) * c)
        qkv_in = pq_all[rb]
        dc_ref[b] = qkv_in[c - (DN_CONV_K - 1):]
        qkv_l.append(_silu(_causal_conv(qkv_in, qtail, b, dcw, DN_CONV_K)))
        gates_l.append(_dn_gates(pab_all[rb], alog_ref[...], dtb_ref[...], m))
        tick()

    for b in range(nb):
        rb = slice(b * c, (b + 1) * c)
        u_in = pbuf[rb, A_W:2 * A_W] * pbuf[rb, 2 * A_W:3 * A_W]
        ca_ref[b] = u_in[c - (CONV_A_K - 1):]
        ybuf[rb, 0:A_W] = (pbuf[rb, 0:A_W] * _causal_conv(u_in, utail, b, caw, CONV_A_K)).astype(BF16)
        tick()

    lists = [[] for _ in range(8)]
    for b in range(nb):
        d, dl, beta = gates_l[b]
        for lst, val in zip(lists, _head_lists(qkv_l[b], None, d, dl, beta)):
            lst.extend(val)
    q, k, v, _, d_col, d_row, dl_col, beta_col = lists
    idx = [(b, h) for b in range(nb) for h in range(DN_HEADS)]
    q, k, u, w, qk = _dn_intra(q, k, v, d_col, d_row, beta_col, m, c, tick)
    s_old = [s_ref[b, h] for b, h in idx]
    ws = _each(lambda wx, qx, dc, s: _mm(jnp.concatenate([wx, qx * jnp.exp(dc)], axis=0), s), w, q, d_col, s_old)
    tick()
    vn = _each(lambda ux, x_: ux - x_[:c], u, ws)
    o = _each(lambda x_, y, vx: x_[c:] + _mm(y, vx), ws, qk, vn)
    tick()
    s_new = _each(lambda s, dlc, kx, dc, vx: s * jnp.exp(dlc[0:1, :]) + _mm_tn(kx * jnp.exp(dlc - dc), vx),
                  s_old, dl_col, k, d_col, vn)
    while queue:
        tick()
    for (b, h), sx, ox in zip(idx, s_new, o):
        s_ref[b, h] = sx
        zx = pbuf[b * c:(b + 1) * c, P_Z + h * LANE:P_Z + (h + 1) * LANE]
        ybuf[b * c:(b + 1) * c, A_W + h * LANE:A_W + (h + 1) * LANE] = _dn_out(ox, zx, dnorm).astype(BF16)

    xq = [pbuf[b * c:(b + 1) * c, P_XQ + h * LANE:P_XQ + (h + 1) * LANE] for b, h in idx]
    sc = _each(lambda x_, bh: _mm_nt(x_, mk_ref[bh[0], :, bh[1] * LANE:(bh[1] + 1) * LANE]) * (XA_DH ** -0.5),
               xq, idx)
    e = _each(lambda x_: jnp.exp(x_ - jnp.max(x_, axis=-1, keepdims=True)), sc)
    p = _each(lambda x_: x_ / jnp.sum(x_, axis=-1, keepdims=True), e)
    for (b, h), px in zip(idx, p):
        ybuf[b * c:(b + 1) * c, A_W + DN_V + h * LANE:A_W + DN_V + (h + 1) * LANE] = (
            _mm(px, mv_ref[b, :, h * LANE:(h + 1) * LANE]).astype(BF16))

    merged = None
    for j, (lo, hi) in enumerate(((0, A_W), (A_W, A_W + DN_V), (A_W + DN_V, A_W + DN_V + XA_W))):
        term = gbuf[:, j * D_MODEL:(j + 1) * D_MODEL] * jnp.dot(ybuf[:, lo:hi], wb_ref[lo:hi, :],
                                                                preferred_element_type=F32)
        merged = term if merged is None else merged + term
    x1 = x + jnp.dot(merged.astype(BF16), wo_ref[...], preferred_element_type=F32)
    x1_ref[...] = x1.reshape(nb, c, D_MODEL)


def _front_prompt(x, gmix, w1, w2, wab, mkb, mvb, caw, dcw, alog_row, dtb_row, dnorm, wb, wo, nb):
    bsz, length, d = x.shape
    c = CHUNK
    return pl.pallas_call(
        functools.partial(_front_prompt_kernel, nb=nb),
        grid=(bsz // nb, length // c),
        in_specs=[pl.BlockSpec((nb, c, d), lambda g, t: (g, t, 0)),
                  _resident((1, d)), _resident(w1.shape), _resident(w2.shape), _resident(wab.shape),
                  pl.BlockSpec((nb, MEM_TOKENS, XA_W), lambda g, t: (g, 0, 0), pipeline_mode=pl.Buffered(1)),
                  pl.BlockSpec((nb, MEM_TOKENS, XA_W), lambda g, t: (g, 0, 0), pipeline_mode=pl.Buffered(1)),
                  _resident((CONV_A_K, A_W)), _resident((DN_CONV_K, DN_CONV_CH)),
                  _resident((1, LANE)), _resident((1, LANE)), _resident((1, LANE)),
                  _resident(wb.shape), _resident(wo.shape)],
        out_specs=[pl.BlockSpec((nb, c, d), lambda g, t: (g, t, 0)),
                   pl.BlockSpec((nb, CONV_A_K - 1, A_W), lambda g, t: (g, 0, 0)),
                   pl.BlockSpec((nb, DN_CONV_K - 1, DN_CONV_CH), lambda g, t: (g, 0, 0)),
                   pl.BlockSpec((nb, DN_HEADS, DN_DK, DN_DV), lambda g, t: (g, 0, 0, 0))],
        out_shape=[jax.ShapeDtypeStruct((bsz, length, d), F32),
                   jax.ShapeDtypeStruct((bsz, CONV_A_K - 1, A_W), F32),
                   jax.ShapeDtypeStruct((bsz, DN_CONV_K - 1, DN_CONV_CH), F32),
                   jax.ShapeDtypeStruct((bsz, DN_HEADS, DN_DK, DN_DV), F32)],
        scratch_shapes=[pltpu.VMEM((nb, TAIL, A_W), F32), pltpu.VMEM((nb, TAIL, DN_CONV_CH), F32),
                        pltpu.VMEM((nb * c, A_W + DN_V + XA_W), BF16), pltpu.VMEM((nb * c, P_W), F32),
                        pltpu.VMEM((nb * c, 3 * d), F32)],
        compiler_params=_cparams(("arbitrary", "arbitrary")),
        name="front_prompt",
    )(x, gmix, w1, w2, wab, mkb, mvb, caw, dcw, alog_row, dtb_row, dnorm, wb, wo)


def _proj_kernel(x_ref, g_ref, w1_ref, w2_ref, wab_ref, p1_ref, p2_ref, pab_ref):
    xn = _rms(x_ref[...], g_ref[...]).astype(BF16)
    p1_ref[...] = jnp.dot(xn, w1_ref[...], preferred_element_type=F32)
    p2_ref[...] = jnp.dot(xn, w2_ref[...], preferred_element_type=F32)
    pab_ref[...] = jnp.dot(xn, wab_ref[...], preferred_element_type=F32)


def _proj(x2d, gain, w1, w2, wab, tm):
    t, d = x2d.shape
    row = lambda n: pl.BlockSpec((tm, n), lambda i: (i, 0))
    return pl.pallas_call(
        _proj_kernel,
        grid=(t // tm,),
        in_specs=[row(d), _resident((1, d)), _resident(w1.shape), _resident(w2.shape), _resident(wab.shape)],
        out_specs=[row(W1), row(W2), row(LANE)],
        out_shape=[jax.ShapeDtypeStruct((t, W1), F32), jax.ShapeDtypeStruct((t, W2), F32),
                   jax.ShapeDtypeStruct((t, LANE), F32)],
        compiler_params=_cparams(("arbitrary",)),
        name="proj",
    )(x2d, gain, w1, w2, wab)


SEQ_S = 4
NB_S = CHUNK // SEQ_S


def _seg_conv(x, e, wts, width, tmod):
    rows = x.shape[0]
    acc = None
    for i in range(width):
        s = width - 1 - i
        term = x if s == 0 else jnp.where(tmod >= s, pltpu.roll(x, s, 0), 0.0)
        if i < width - 1:
            hist = e if i == 0 else pltpu.roll(e, rows - i, 0)
            term = term + jnp.where(tmod < SEQ_S - i, hist, 0.0)
        term = wts[i:i + 1] * term
        acc = term if acc is None else acc + term
    return acc


def _branch_sample_kernel(pa_ref, pq_ref, pz_ref, pab_ref, ea_ref, eq_ref, s0_ref, caw_ref, dcw_ref,
                          alog_ref, dtb_ref, dnorm_ref, yad_ref, u_ref, s_ref):
    c = CHUNK
    m = _dn_masks(SEQ_S)
    tmod = lax.broadcasted_iota(jnp.int32, (c, 1), 0) & (SEQ_S - 1)

    pa = pa_ref[...]
    u_in = pa[:, A_W:2 * A_W] * pa[:, 2 * A_W:3 * A_W]
    u_ref[...] = u_in
    conv = _seg_conv(u_in, ea_ref[...], caw_ref[...], CONV_A_K, tmod)
    yad_ref[:, 0:A_W] = (pa[:, 0:A_W] * conv).astype(BF16)

    qkv = _silu(_seg_conv(pq_ref[...], eq_ref[...], dcw_ref[...], DN_CONV_K, tmod))
    d, dl, beta = _dn_gates(pab_ref[...], alog_ref[...], dtb_ref[...], m)
    dec_t = jnp.exp(dl).T
    dnorm = dnorm_ref[...]

    wide = NB_S * DN_DK
    er = lax.broadcasted_iota(jnp.int32, (c, wide), 0)
    ec = lax.broadcasted_iota(jnp.int32, (c, wide), 1)
    mexp = ((er >> 2) == (ec >> 7)).astype(F32)
    mexp2 = jnp.concatenate([mexp, mexp], axis=0)
    tr = lax.broadcasted_iota(jnp.int32, (wide, c), 0)
    tc = lax.broadcasted_iota(jnp.int32, (wide, c), 1)
    mexp_t = ((tr >> 7) == (tc >> 2)).astype(F32)

    q, k, v, z, d_col, d_row, dl_col, beta_col = _head_lists(qkv, pz_ref[...], d, dl, beta)
    q, k, u, w, qk = _dn_intra(q, k, v, d_col, d_row, beta_col, m, SEQ_S)
    heads = list(range(DN_HEADS))
    s_old = [s0_ref[:, h].reshape(wide, DN_DV) for h in heads]
    x_exp = _each(lambda wx, qx, dc: jnp.concatenate([jnp.concatenate([wx, qx * jnp.exp(dc)], axis=0)] * NB_S,
                                                     axis=1) * mexp2, w, q, d_col)
    ws = _each(_mm, x_exp, s_old)
    vn = _each(lambda ux, x: ux - x[:c], u, ws)
    o = _each(lambda x, y, vx: x[c:] + _mm(y, vx), ws, qk, vn)
    k_exp = _each(lambda kx, dlc, dc: jnp.concatenate([(kx * jnp.exp(dlc - dc)).T] * NB_S, axis=0) * mexp_t,
                  k, dl_col, d_col)
    dec = [jnp.concatenate([jnp.broadcast_to(dec_t[h:h + 1, SEQ_S * b:SEQ_S * b + 1], (DN_DK, DN_DV))
                            for b in range(NB_S)], axis=0) for h in heads]
    s_new = _each(lambda s, dx, kx, vx: s * dx + _mm(kx, vx), s_old, dec, k_exp, vn)
    for h, sx, ox, zx in zip(heads, s_new, o, z):
        s_ref[:, h] = sx.reshape(NB_S, DN_DK, DN_DV)
        yad_ref[:, A_W + h * LANE:A_W + (h + 1) * LANE] = _dn_out(ox, zx, dnorm).astype(BF16)


def _branch_sample(p1, pab, ea, eq, state, caw, dcw, alog_row, dtb_row, dnorm):
    t = p1.shape[0]
    c = CHUNK
    full = lambda shape: pl.BlockSpec(shape, lambda i: (0,) * len(shape))
    return pl.pallas_call(
        _branch_sample_kernel,
        grid=(t // c,),
        in_specs=[pl.BlockSpec((c, 3 * A_W), lambda i: (i, 0)),
                  pl.BlockSpec((c, DN_CONV_CH), lambda i: (i, OFF_QKV // DN_CONV_CH)),
                  pl.BlockSpec((c, DN_V), lambda i: (i, OFF_Z // DN_V)),
                  pl.BlockSpec((c, LANE), lambda i: (i, 0)),
                  pl.BlockSpec((c, A_W), lambda i: (i, 0)),
                  pl.BlockSpec((c, DN_CONV_CH), lambda i: (i, 0)),
                  pl.BlockSpec((NB_S, DN_HEADS, DN_DK, DN_DV), lambda i: (i, 0, 0, 0)),
                  full((CONV_A_K, A_W)), full((DN_CONV_K, DN_CONV_CH)),
                  full((1, LANE)), full((1, LANE)), full((1, LANE))],
        out_specs=[pl.BlockSpec((c, A_W + DN_V), lambda i: (i, 0)),
                   pl.BlockSpec((c, A_W), lambda i: (i, 0)),
                   pl.BlockSpec((NB_S, DN_HEADS, DN_DK, DN_DV), lambda i: (i, 0, 0, 0))],
        out_shape=[jax.ShapeDtypeStruct((t, A_W + DN_V), BF16),
                   jax.ShapeDtypeStruct((t, A_W), F32),
                   jax.ShapeDtypeStruct(state.shape, F32)],
        compiler_params=_cparams(("arbitrary",)),
        name="branch_sample",
    )(p1, p1, p1, pab, ea, eq, state, caw, dcw, alog_row, dtb_row, dnorm)


def _attn_sample_kernel(q_ref, k_ref, v_ref, o_ref):
    for h in range(XA_HEADS):
        sl = slice(h * LANE, (h + 1) * LANE)
        rows = pl.ds(h, MEM_TOKENS, stride=XA_HEADS)
        q = q_ref[:, :, sl].astype(BF16)
        s = jnp.einsum("bqd,bkd->bqk", q, k_ref[:, rows, :].astype(BF16),
                       preferred_element_type=F32) * (XA_DH ** -0.5)
        e = jnp.exp(s - jnp.max(s, axis=-1, keepdims=True))
        p = e / jnp.sum(e, axis=-1, keepdims=True)
        o_ref[:, :, sl] = jnp.einsum("bqk,bkd->bqd", p.astype(BF16), v_ref[:, rows, :].astype(BF16),
                                     preferred_element_type=F32).astype(BF16)


def _attn_sample(q3, ck, cv, nb):
    bsz, length, _ = q3.shape
    return pl.pallas_call(
        _attn_sample_kernel,
        grid=(bsz // nb,),
        in_specs=[pl.BlockSpec((nb, length, XA_W), lambda i: (i, 0, 0)),
                  pl.BlockSpec((nb, MEM_TOKENS * XA_HEADS, XA_DH), lambda i: (i, 0, 0)),
                  pl.BlockSpec((nb, MEM_TOKENS * XA_HEADS, XA_DH), lambda i: (i, 0, 0))],
        out_specs=pl.BlockSpec((nb, length, XA_W), lambda i: (i, 0, 0)),
        out_shape=jax.ShapeDtypeStruct((bsz, length, XA_W), BF16),
        compiler_params=_cparams(("arbitrary",)),
        name="attn_sample",
    )(q3, ck, cv)


def _merge_kernel(yad_ref, ym_ref, p2_ref, x_ref, wb_ref, wo_ref, o_ref):
    yad = yad_ref[...]
    gate = lambda j: jax.nn.sigmoid(p2_ref[:, OFF_G + j * D_MODEL:OFF_G + (j + 1) * D_MODEL])
    merged = (gate(0) * jnp.dot(yad[:, :A_W], wb_ref[0:A_W, :], preferred_element_type=F32)
              + gate(1) * jnp.dot(yad[:, A_W:], wb_ref[A_W:A_W + DN_V, :], preferred_element_type=F32)
              + gate(2) * jnp.dot(ym_ref[...], wb_ref[A_W + DN_V:, :], preferred_element_type=F32))
    o_ref[...] = x_ref[...] + jnp.dot(merged.astype(BF16), wo_ref[...], preferred_element_type=F32)


def _merge(yad, ym, p2, x2d, wb, wo, tm):
    t, d = x2d.shape
    row = lambda n: pl.BlockSpec((tm, n), lambda i: (i, 0))
    return pl.pallas_call(
        _merge_kernel,
        grid=(t // tm,),
        in_specs=[row(A_W + DN_V), row(XA_W), row(W2), row(d), _resident(wb.shape), _resident(wo.shape)],
        out_specs=row(d),
        out_shape=jax.ShapeDtypeStruct((t, d), F32),
        compiler_params=_cparams(("arbitrary",)),
        name="merge",
    )(yad, ym, p2, x2d, wb, wo)


FF_SPLIT = 2
FF_BLK = D_FF // FF_SPLIT


def _ffn_kernel(x_ref, gf_ref, wu_ref, wd_ref, gl_ref, o_ref):
    x = x_ref[...]
    xn = _rms(x, gf_ref[...]).astype(BF16)
    acc = x
    for j in range(FF_SPLIT):
        gate = jnp.dot(xn, wu_ref[:, j * FF_BLK:(j + 1) * FF_BLK], preferred_element_type=F32)
        up = jnp.dot(xn, wu_ref[:, D_FF + j * FF_BLK:D_FF + (j + 1) * FF_BLK], preferred_element_type=F32)
        hid = (_silu(gate) * up).astype(BF16)
        acc = acc + jnp.dot(hid, wd_ref[j * FF_BLK:(j + 1) * FF_BLK, :], preferred_element_type=F32)
    o_ref[...] = _rms(acc, gl_ref[...])


def _ffn(x2d, gf, wu, wd, gl, tm):
    t, d = x2d.shape
    return pl.pallas_call(
        _ffn_kernel,
        grid=(t // tm,),
        in_specs=[pl.BlockSpec((tm, d), lambda i: (i, 0)),
                  _resident((1, d)), _resident(wu.shape), _resident(wd.shape), _resident((1, d))],
        out_specs=pl.BlockSpec((tm, d), lambda i: (i, 0)),
        out_shape=jax.ShapeDtypeStruct((t, d), F32),
        compiler_params=_cparams(("arbitrary",)),
        name="ffn",
    )(x2d, gf, wu, wd, gl)


def _pad_lanes(v):
    return jnp.zeros((1, LANE), F32).at[0, :v.shape[0]].set(v.astype(F32))


def kernel(x_prompt, x_sample, mem_prompt, state_conv_a, state_dn_conv, state_dn, cache_mem_k, cache_mem_v,
           norm_mix, w_in, conv_a_w, dn_conv_w, dn_a_log, dn_dt_bias, dn_norm, norm_mem, w_mem_kv, w_branch,
           w_o, norm_ffn, w_ffn_up, w_ffn_down, norm_final):
    bp, lp, d = x_prompt.shape
    bs, ls, _ = x_sample.shape
    assert norm_mix.shape[0] == 1 and ls == SEQ_S and lp % CHUNK == 0 and (bs * ls) % CHUNK == 0

    w = w_in[0]
    w1 = w[:, :W1].astype(BF16)
    w2 = w[:, W1 + N_AB:].astype(BF16)
    wab = jnp.pad(w[:, W1:W1 + N_AB], ((0, 0), (0, LANE - N_AB))).astype(BF16)
    wb = w_branch[0].astype(BF16)
    wo = w_o[0].astype(BF16)
    wu = w_ffn_up[0].astype(BF16)
    wd = w_ffn_down[0].astype(BF16)
    wkv = w_mem_kv[0].astype(BF16)
    g_mix = norm_mix[0][None, :]
    g_ffn = norm_ffn[0][None, :]
    g_fin = norm_final[None, :]
    g_mem = norm_mem[0][None, :]
    caw = conv_a_w[0]
    dcw = dn_conv_w[0]
    alog_row = _pad_lanes(dn_a_log[0])
    dtb_row = _pad_lanes(dn_dt_bias[0])
    dnorm = dn_norm[0][None, :]

    tp = bp * lp
    mk, mv, mkb, mvb = _memkv(mem_prompt.reshape(bp * MEM_TOKENS, d), g_mem, wkv, TM)
    x1_p, ca_p, dc_p, s_p = _front_prompt(x_prompt, g_mix, w1, w2, wab, mkb.reshape(bp, MEM_TOKENS, XA_W),
                                          mvb.reshape(bp, MEM_TOKENS, XA_W), caw, dcw, alog_row, dtb_row, dnorm,
                                          wb, wo, NB_P)
    y_p = _ffn(x1_p.reshape(tp, d), g_ffn, wu, wd, g_fin, TM).reshape(bp, lp, d)

    ts = bs * ls
    xs2 = x_sample.reshape(ts, d)
    p1_s, p2_s, pab_s = _proj(xs2, g_mix, w1, w2, wab, CHUNK)
    ea = jnp.pad(state_conv_a[0], ((0, 0), (0, ls - (CONV_A_K - 1)), (0, 0))).reshape(ts, A_W)
    eq = jnp.pad(state_dn_conv[0], ((0, 0), (0, ls - (DN_CONV_K - 1)), (0, 0))).reshape(ts, DN_CONV_CH)
    yad_s, u_s, s_s = _branch_sample(p1_s, pab_s, ea, eq, state_dn[0], caw, dcw, alog_row, dtb_row, dnorm)
    ym_s = _attn_sample(p2_s[:, :XA_W].reshape(bs, ls, XA_W),
                        cache_mem_k.reshape(bs, MEM_TOKENS * XA_HEADS, XA_DH),
                        cache_mem_v.reshape(bs, MEM_TOKENS * XA_HEADS, XA_DH), 8)
    x1_s = _merge(yad_s, ym_s.reshape(ts, XA_W), p2_s, xs2, wb, wo, ts)
    y_s = _ffn(x1_s, g_ffn, wu, wd, g_fin, ts).reshape(bs, ls, d)
    ca_s = u_s.reshape(bs, ls, A_W)[:, ls - (CONV_A_K - 1):]
    dc_s = p1_s.reshape(bs, ls, W1)[:, ls - (DN_CONV_K - 1):, OFF_QKV:OFF_Z]

    return (y_p, y_s, ca_p[None], dc_p[None], s_p[None],
            mk.reshape(1, bp, MEM_TOKENS, XA_HEADS, XA_DH), mv.reshape(1, bp, MEM_TOKENS, XA_HEADS, XA_DH),
            ca_s[None], dc_s[None], s_s[None])
```

```python
import functools

import jax
import jax.numpy as jnp
from jax import lax
from jax.experimental import pallas as pl
from jax.experimental.pallas import tpu as pltpu

F32 = jnp.float32
BF16 = jnp.bfloat16

D_MODEL = 1024
A_W = 512
CONV_A_K = 3
DN_HEADS = 4
DN_DK = 128
DN_DV = 128
DN_QK = DN_HEADS * DN_DK
DN_V = DN_HEADS * DN_DV
DN_CONV_CH = 2 * DN_QK + DN_V
DN_CONV_K = 4
MEM_TOKENS = 256
XA_HEADS = 4
XA_DH = 128
XA_W = XA_HEADS * XA_DH
D_FF = 2816
EPS = 1e-6

LANE = 128
CHUNK = 128
TAIL = 8
NB_P = 4
TM = 512
GATE_BLK = 256
PROJ_BLK = 512

W1 = 3 * A_W + DN_CONV_CH + DN_V
OFF_QKV = 3 * A_W
OFF_Z = OFF_QKV + DN_CONV_CH
W2 = XA_W + 3 * D_MODEL
OFF_G = XA_W
N_AB = 2 * DN_HEADS
P_Z = 3 * A_W
P_XQ = P_Z + DN_V
P_W = P_XQ + XA_W

VMEM_LIMIT = 60 * 1024 * 1024


def _cparams(sem):
    return pltpu.CompilerParams(dimension_semantics=sem, vmem_limit_bytes=VMEM_LIMIT)


def _resident(shape):
    return pl.BlockSpec(shape, lambda *_: (0,) * len(shape), pipeline_mode=pl.Buffered(1))


def _mm(a, b):
    return jnp.dot(a.astype(BF16), b.astype(BF16), preferred_element_type=F32)


def _mm_nt(a, b):
    return lax.dot_general(a.astype(BF16), b.astype(BF16), (((1,), (1,)), ((), ())),
                           preferred_element_type=F32)


def _mm_tn(a, b):
    return lax.dot_general(a.astype(BF16), b.astype(BF16), (((0,), (0,)), ((), ())),
                           preferred_element_type=F32)


def _split3(x):
    hi = x.astype(BF16)
    r1 = x - hi.astype(F32)
    mid = r1.astype(BF16)
    lo = (r1 - mid.astype(F32)).astype(BF16)
    return jnp.concatenate([hi, mid, lo], axis=1)


def _mm_exact01(m01, x):
    n = x.shape[1]
    r = jnp.dot(m01.astype(BF16), _split3(x), preferred_element_type=F32)
    return (r[:, :n] + r[:, n:2 * n]) + r[:, 2 * n:]


def _rms(x, g):
    return x * lax.rsqrt(jnp.mean(x * x, axis=-1, keepdims=True) + EPS) * g


def _silu(x):
    return x * jax.nn.sigmoid(x)


def _softplus(x):
    return jnp.maximum(x, 0.0) + jnp.log1p(jnp.exp(-jnp.abs(x)))


def _memkv_kernel(x_ref, g_ref, w_ref, k_ref, v_ref, kb_ref, vb_ref):
    xn = _rms(x_ref[...], g_ref[...]).astype(BF16)
    kv = jnp.dot(xn, w_ref[...], preferred_element_type=F32)
    k = kv[:, :XA_W]
    v = kv[:, XA_W:]
    k_ref[...] = k
    v_ref[...] = v
    kb_ref[...] = k.astype(BF16)
    vb_ref[...] = v.astype(BF16)


def _memkv(mem2d, gain, w, tm):
    t, d = mem2d.shape
    blk = pl.BlockSpec((tm, XA_W), lambda i: (i, 0))
    return pl.pallas_call(
        _memkv_kernel,
        grid=(t // tm,),
        in_specs=[pl.BlockSpec((tm, d), lambda i: (i, 0)),
                  pl.BlockSpec((1, d), lambda i: (0, 0)),
                  pl.BlockSpec((d, 2 * XA_W), lambda i: (0, 0))],
        out_specs=[blk, blk, blk, blk],
        out_shape=[jax.ShapeDtypeStruct((t, XA_W), F32), jax.ShapeDtypeStruct((t, XA_W), F32),
                   jax.ShapeDtypeStruct((t, XA_W), BF16), jax.ShapeDtypeStruct((t, XA_W), BF16)],
        compiler_params=_cparams(("arbitrary",)),
        name="memkv",
    )(mem2d, gain, w)


def _log2(n):
    return n.bit_length() - 1


def _dn_masks(seg):
    r = lax.broadcasted_iota(jnp.int32, (CHUNK, CHUNK), 0)
    c = lax.broadcasted_iota(jnp.int32, (CHUNK, CHUNK), 1)
    ls = _log2(seg)
    same = (r >> ls) == (c >> ls)
    base = min(8, seg)
    lb = _log2(base)
    m = {
        "same": same.astype(F32),
        "causal": (same & (r >= c)).astype(F32),
        "strict": (same & (r > c)).astype(F32),
        "eye": (r == c).astype(F32),
        "diag": ((r >> lb) == (c >> lb)).astype(F32),
        "off": {},
        "base": base,
    }
    s = base
    while s < seg:
        l1, l2 = _log2(s), _log2(2 * s)
        m["off"][s] = (((r >> l2) == (c >> l2)) & ((r >> l1) != (c >> l1))).astype(F32)
        s *= 2
    return m


def _each(f, *lists):
    return [f(*args) for args in zip(*lists)]


def _low_rows(x, s):
    return jnp.concatenate([x[i + s:i + 2 * s] for i in range(0, x.shape[0], 2 * s)], axis=0)


def _merge_low(x, low, s):
    parts = []
    for j, i in enumerate(range(0, x.shape[0], 2 * s)):
        parts += [x[i:i + s], low[j * s:(j + 1) * s]]
    return jnp.concatenate(parts, axis=0)


def _spread_low(low, s):
    return _merge_low(jnp.zeros((2 * low.shape[0], low.shape[1]), low.dtype), low, s)


def _tri_inv(a_list, m, seg, tick):
    add = lambda x, y: x + y
    b = _each(lambda a: -(a * m["diag"]), a_list)
    p = _each(lambda x: m["eye"] + x, b)
    b2 = _each(_mm, b, b)
    tick()
    p = _each(add, p, _each(_mm, p, b2))
    tick()
    if m["base"] == 8:
        b4 = _each(_mm, b2, b2)
        tick()
        p = _each(add, p, _each(_mm, p, b4))
        tick()
    s = m["base"]
    while s < seg:
        low = lambda t, s=s: _low_rows(t, s)
        x = _each(_mm, _each(lambda a, s=s: low(a) * low(m["off"][s]), a_list), p)
        tick()
        r = _each(_mm, _each(low, p), _each(lambda y, s=s: _spread_low(y, s), x))
        p = _each(lambda t, y, s=s: _merge_low(t, low(t) - y, s), p, r)
        tick()
        s *= 2
    return p


def _dn_gates(ab, alog_row, dtb_row, m):
    g = -jnp.exp(alog_row) * _softplus(ab + dtb_row)
    beta = jax.nn.sigmoid(ab)
    d = _mm_exact01(m["causal"], g)
    dl = _mm_exact01(m["same"], g)
    return d, dl, beta


def _l2n(x):
    return x * lax.rsqrt(jnp.sum(x * x, axis=-1, keepdims=True) + EPS)


def _dn_intra(q, k, v, d_col, d_row, beta_col, m, seg, tick=lambda: None):
    q = _each(lambda x: _l2n(x) * (DN_DK ** -0.5), q)
    k = _each(_l2n, k)
    gamma = _each(lambda dc, dr: jnp.exp((dc - dr) * m["causal"]) * m["causal"], d_col, d_row)
    kk = _each(_mm_nt, k, k)
    a = _each(lambda bc, x, g: (bc * x) * g * m["strict"], beta_col, kk, gamma)
    t = _tri_inv(a, m, seg, tick)
    rhs = _each(lambda vv, kx, bc, dc: jnp.concatenate([vv * bc, kx * (bc * jnp.exp(dc))], axis=1),
                v, k, beta_col, d_col)
    sol = _each(_mm, t, rhs)
    u = [x[:, :DN_DV] for x in sol]
    w = [x[:, DN_DV:] for x in sol]
    qk = _each(lambda x, g: x * g, _each(_mm_nt, q, k), gamma)
    return q, k, u, w, qk


def _dn_out(o, z, dnorm):
    return _rms(o, dnorm) * _silu(z)


def _head_lists(qkv, z, d, dl, beta):
    d_t = d.T
    out = [[] for _ in range(8)]
    for h in range(DN_HEADS):
        vals = (qkv[:, h * LANE:(h + 1) * LANE],
                qkv[:, DN_QK + h * LANE:DN_QK + (h + 1) * LANE],
                qkv[:, 2 * DN_QK + h * LANE:2 * DN_QK + (h + 1) * LANE],
                None if z is None else z[:, h * LANE:(h + 1) * LANE],
                d[:, h:h + 1], d_t[h:h + 1, :], dl[:, h:h + 1], beta[:, DN_HEADS + h:DN_HEADS + h + 1])
        for lst, val in zip(out, vals):
            lst.append(val)
    return out


def _causal_conv(x, tail_ref, b, wts, width):
    c, w = x.shape
    tiles = jnp.concatenate([tail_ref[b][None], x.reshape(c // TAIL, TAIL, w)], axis=0)
    sub = lax.broadcasted_iota(jnp.int32, (1, TAIL, 1), 1)
    acc = None
    for i in range(width):
        s = width - 1 - i
        if s == 0:
            y = tiles[1:]
        else:
            r = pltpu.roll(tiles, s, 1)
            y = jnp.where(sub >= s, r[1:], r[:-1])
        term = wts[i:i + 1][None] * y
        acc = term if acc is None else acc + term
    tail_ref[b] = tiles[c // TAIL]
    return acc.reshape(c, w)


def _front_prompt_kernel(x_ref, gmix_ref, w1_ref, w2_ref, wab_ref, mk_ref, mv_ref, caw_ref, dcw_ref,
                         alog_ref, dtb_ref, dnorm_ref, wb_ref, wo_ref,
                         x1_ref, ca_ref, dc_ref, s_ref, utail, qtail, ybuf, pbuf, gbuf, *, nb):
    c = CHUNK
    rows = nb * c
    t_idx = pl.program_id(1)

    @pl.when(t_idx == 0)
    def _():
        utail[...] = jnp.zeros(utail.shape, F32)
        qtail[...] = jnp.zeros(qtail.shape, F32)
        s_ref[...] = jnp.zeros(s_ref.shape, F32)

    x = x_ref[...].reshape(rows, D_MODEL)
    xn = _rms(x, gmix_ref[...]).astype(BF16)
    proj = lambda w_ref, lo, hi: jnp.dot(xn, w_ref[:, lo:hi], preferred_element_type=F32)

    queue = []

    def enqueue(dst, w_ref, src, dst_lo, width, act=None):
        def run():
            r = proj(w_ref, src, src + width)
            dst[:, dst_lo:dst_lo + width] = r if act is None else act(r)
        queue.append(run)

    for lo in range(0, 3 * A_W, PROJ_BLK):
        enqueue(pbuf, w1_ref, lo, lo, PROJ_BLK)
    enqueue(pbuf, w1_ref, OFF_Z, P_Z, DN_V)
    enqueue(pbuf, w2_ref, 0, P_XQ, XA_W)
    for lo in range(0, 3 * D_MODEL, GATE_BLK):
        enqueue(gbuf, w2_ref, OFF_G + lo, lo, GATE_BLK, jax.nn.sigmoid)
    queue.reverse()

    def tick():
        if queue:
            queue.pop()()

    m = _dn_masks(c)
    caw = caw_ref[...]
    dcw = dcw_ref[...]
    dnorm = dnorm_ref[...]
    pq_all = proj(w1_ref, OFF_QKV, OFF_Z)
    pab_all = jnp.dot(xn, wab_ref[...], preferred_element_type=F32)

    qkv_l, gates_l = [], []
    for b in range(nb):
        rb = slice(b * c, (b + 1) * c)
        qkv_in = pq_all[rb]
        dc_ref[b] = qkv_in[c - (DN_CONV_K - 1):]
        qkv_l.append(_silu(_causal_conv(qkv_in, qtail, b, dcw, DN_CONV_K)))
        gates_l.append(_dn_gates(pab_all[rb], alog_ref[...], dtb_ref[...], m))
        tick()

    for b in range(nb):
        rb = slice(b * c, (b + 1) * c)
        u_in = pbuf[rb, A_W:2 * A_W] * pbuf[rb, 2 * A_W:3 * A_W]
        ca_ref[b] = u_in[c - (CONV_A_K - 1):]
        ybuf[rb, 0:A_W] = (pbuf[rb, 0:A_W] * _causal_conv(u_in, utail, b, caw, CONV_A_K)).astype(BF16)
        tick()

    lists = [[] for _ in range(8)]
    for b in range(nb):
        d, dl, beta = gates_l[b]
        for lst, val in zip(lists, _head_lists(qkv_l[b], None, d, dl, beta)):
            lst.extend(val)
    q, k, v, _, d_col, d_row, dl_col, beta_col = lists
    idx = [(b, h) for b in range(nb) for h in range(DN_HEADS)]
    q, k, u, w, qk = _dn_intra(q, k, v, d_col, d_row, beta_col, m, c, tick)
    s_old = [s_ref[b, h] for b, h in idx]
    vn = _each(lambda ux, wx, s: ux - _mm(wx, s), u, w, s_old)
    tick()
    o = _each(lambda qx, dc, y, s, vx: _mm(jnp.concatenate([qx * jnp.exp(dc), y], axis=1),
                                           jnp.concatenate([s, vx], axis=0)), q, d_col, qk, s_old, vn)
    tick()
    s_new = _each(lambda s, dlc, kx, dc, vx: s * jnp.exp(dlc[0:1, :]) + _mm_tn(kx * jnp.exp(dlc - dc), vx),
                  s_old, dl_col, k, d_col, vn)
    while queue:
        tick()
    for (b, h), sx, ox in zip(idx, s_new, o):
        s_ref[b, h] = sx
        zx = pbuf[b * c:(b + 1) * c, P_Z + h * LANE:P_Z + (h + 1) * LANE]
        ybuf[b * c:(b + 1) * c, A_W + h * LANE:A_W + (h + 1) * LANE] = _dn_out(ox, zx, dnorm).astype(BF16)

    xq = [pbuf[b * c:(b + 1) * c, P_XQ + h * LANE:P_XQ + (h + 1) * LANE] for b, h in idx]
    sc = _each(lambda x_, bh: _mm_nt(x_, mk_ref[bh[0], :, bh[1] * LANE:(bh[1] + 1) * LANE]) * (XA_DH ** -0.5),
               xq, idx)
    e = _each(lambda x_: jnp.exp(x_ - jnp.max(x_, axis=-1, keepdims=True)), sc)
    p = _each(lambda x_: x_ / jnp.sum(x_, axis=-1, keepdims=True), e)
    for (b, h), px in zip(idx, p):
        ybuf[b * c:(b + 1) * c, A_W + DN_V + h * LANE:A_W + DN_V + (h + 1) * LANE] = (
            _mm(px, mv_ref[b, :, h * LANE:(h + 1) * LANE]).astype(BF16))

    merged = None
    for j, (lo, hi) in enumerate(((0, A_W), (A_W, A_W + DN_V), (A_W + DN_V, A_W + DN_V + XA_W))):
        term = gbuf[:, j * D_MODEL:(j + 1) * D_MODEL] * jnp.dot(ybuf[:, lo:hi], wb_ref[lo:hi, :],
                                                                preferred_element_type=F32)
        merged = term if merged is None else merged + term
    x1 = x + jnp.dot(merged.astype(BF16), wo_ref[...], preferred_element_type=F32)
    x1_ref[...] = x1.reshape(nb, c, D_MODEL)


def _front_prompt(x, gmix, w1, w2, wab, mkb, mvb, caw, dcw, alog_row, dtb_row, dnorm, wb, wo, nb):
    bsz, length, d = x.shape
    c = CHUNK
    return pl.pallas_call(
        functools.partial(_front_prompt_kernel, nb=nb),
        grid=(bsz // nb, length // c),
        in_specs=[pl.BlockSpec((nb, c, d), lambda g, t: (g, t, 0)),
                  _resident((1, d)), _resident(w1.shape), _resident(w2.shape), _resident(wab.shape),
                  pl.BlockSpec((nb, MEM_TOKENS, XA_W), lambda g, t: (g, 0, 0), pipeline_mode=pl.Buffered(1)),
                  pl.BlockSpec((nb, MEM_TOKENS, XA_W), lambda g, t: (g, 0, 0), pipeline_mode=pl.Buffered(1)),
                  _resident((CONV_A_K, A_W)), _resident((DN_CONV_K, DN_CONV_CH)),
                  _resident((1, LANE)), _resident((1, LANE)), _resident((1, LANE)),
                  _resident(wb.shape), _resident(wo.shape)],
        out_specs=[pl.BlockSpec((nb, c, d), lambda g, t: (g, t, 0)),
                   pl.BlockSpec((nb, CONV_A_K - 1, A_W), lambda g, t: (g, 0, 0)),
                   pl.BlockSpec((nb, DN_CONV_K - 1, DN_CONV_CH), lambda g, t: (g, 0, 0)),
                   pl.BlockSpec((nb, DN_HEADS, DN_DK, DN_DV), lambda g, t: (g, 0, 0, 0))],
        out_shape=[jax.ShapeDtypeStruct((bsz, length, d), F32),
                   jax.ShapeDtypeStruct((bsz, CONV_A_K - 1, A_W), F32),
                   jax.ShapeDtypeStruct((bsz, DN_CONV_K - 1, DN_CONV_CH), F32),
                   jax.ShapeDtypeStruct((bsz, DN_HEADS, DN_DK, DN_DV), F32)],
        scratch_shapes=[pltpu.VMEM((nb, TAIL, A_W), F32), pltpu.VMEM((nb, TAIL, DN_CONV_CH), F32),
                        pltpu.VMEM((nb * c, A_W + DN_V + XA_W), BF16), pltpu.VMEM((nb * c, P_W), F32),
                        pltpu.VMEM((nb * c, 3 * d), F32)],
        compiler_params=_cparams(("arbitrary", "arbitrary")),
        name="front_prompt",
    )(x, gmix, w1, w2, wab, mkb, mvb, caw, dcw, alog_row, dtb_row, dnorm, wb, wo)


def _proj_kernel(x_ref, g_ref, w1_ref, w2_ref, wab_ref, p1_ref, p2_ref, pab_ref):
    xn = _rms(x_ref[...], g_ref[...]).astype(BF16)
    p1_ref[...] = jnp.dot(xn, w1_ref[...], preferred_element_type=F32)
    p2_ref[...] = jnp.dot(xn, w2_ref[...], preferred_element_type=F32)
    pab_ref[...] = jnp.dot(xn, wab_ref[...], preferred_element_type=F32)


def _proj(x2d, gain, w1, w2, wab, tm):
    t, d = x2d.shape
    row = lambda n: pl.BlockSpec((tm, n), lambda i: (i, 0))
    return pl.pallas_call(
        _proj_kernel,
        grid=(t // tm,),
        in_specs=[row(d), _resident((1, d)), _resident(w1.shape), _resident(w2.shape), _resident(wab.shape)],
        out_specs=[row(W1), row(W2), row(LANE)],
        out_shape=[jax.ShapeDtypeStruct((t, W1), F32), jax.ShapeDtypeStruct((t, W2), F32),
                   jax.ShapeDtypeStruct((t, LANE), F32)],
        compiler_params=_cparams(("arbitrary",)),
        name="proj",
    )(x2d, gain, w1, w2, wab)


SEQ_S = 4
NB_S = CHUNK // SEQ_S


def _seg_conv(x, e, wts, width, tmod):
    rows = x.shape[0]
    acc = None
    for i in range(width):
        s = width - 1 - i
        term = x if s == 0 else jnp.where(tmod >= s, pltpu.roll(x, s, 0), 0.0)
        if i < width - 1:
            hist = e if i == 0 else pltpu.roll(e, rows - i, 0)
            term = term + jnp.where(tmod < SEQ_S - i, hist, 0.0)
        term = wts[i:i + 1] * term
        acc = term if acc is None else acc + term
    return acc


def _branch_sample_kernel(pa_ref, pq_ref, pz_ref, pab_ref, ea_ref, eq_ref, s0_ref, caw_ref, dcw_ref,
                          alog_ref, dtb_ref, dnorm_ref, yad_ref, u_ref, s_ref):
    c = CHUNK
    m = _dn_masks(SEQ_S)
    tmod = lax.broadcasted_iota(jnp.int32, (c, 1), 0) & (SEQ_S - 1)

    pa = pa_ref[...]
    u_in = pa[:, A_W:2 * A_W] * pa[:, 2 * A_W:3 * A_W]
    u_ref[...] = u_in
    conv = _seg_conv(u_in, ea_ref[...], caw_ref[...], CONV_A_K, tmod)
    yad_ref[:, 0:A_W] = (pa[:, 0:A_W] * conv).astype(BF16)

    qkv = _silu(_seg_conv(pq_ref[...], eq_ref[...], dcw_ref[...], DN_CONV_K, tmod))
    d, dl, beta = _dn_gates(pab_ref[...], alog_ref[...], dtb_ref[...], m)
    dec_t = jnp.exp(dl).T
    dnorm = dnorm_ref[...]

    wide = NB_S * DN_DK
    er = lax.broadcasted_iota(jnp.int32, (c, wide), 0)
    ec = lax.broadcasted_iota(jnp.int32, (c, wide), 1)
    mexp = ((er >> 2) == (ec >> 7)).astype(F32)
    mexp2 = jnp.concatenate([mexp, mexp], axis=0)
    tr = lax.broadcasted_iota(jnp.int32, (wide, c), 0)
    tc = lax.broadcasted_iota(jnp.int32, (wide, c), 1)
    mexp_t = ((tr >> 7) == (tc >> 2)).astype(F32)

    q, k, v, z, d_col, d_row, dl_col, beta_col = _head_lists(qkv, pz_ref[...], d, dl, beta)
    q, k, u, w, qk = _dn_intra(q, k, v, d_col, d_row, beta_col, m, SEQ_S)
    heads = list(range(DN_HEADS))
    s_old = [s0_ref[:, h].reshape(wide, DN_DV) for h in heads]
    x_exp = _each(lambda wx, qx, dc: jnp.concatenate([jnp.concatenate([wx, qx * jnp.exp(dc)], axis=0)] * NB_S,
                                                     axis=1) * mexp2, w, q, d_col)
    ws = _each(_mm, x_exp, s_old)
    vn = _each(lambda ux, x: ux - x[:c], u, ws)
    o = _each(lambda x, y, vx: x[c:] + _mm(y, vx), ws, qk, vn)
    k_exp = _each(lambda kx, dlc, dc: jnp.concatenate([(kx * jnp.exp(dlc - dc)).T] * NB_S, axis=0) * mexp_t,
                  k, dl_col, d_col)
    dec = [jnp.concatenate([jnp.broadcast_to(dec_t[h:h + 1, SEQ_S * b:SEQ_S * b + 1], (DN_DK, DN_DV))
                            for b in range(NB_S)], axis=0) for h in heads]
    s_new = _each(lambda s, dx, kx, vx: s * dx + _mm(kx, vx), s_old, dec, k_exp, vn)
    for h, sx, ox, zx in zip(heads, s_new, o, z):
        s_ref[:, h] = sx.reshape(NB_S, DN_DK, DN_DV)
        yad_ref[:, A_W + h * LANE:A_W + (h + 1) * LANE] = _dn_out(ox, zx, dnorm).astype(BF16)


def _branch_sample(p1, pab, ea, eq, state, caw, dcw, alog_row, dtb_row, dnorm):
    t = p1.shape[0]
    c = CHUNK
    full = lambda shape: pl.BlockSpec(shape, lambda i: (0,) * len(shape))
    return pl.pallas_call(
        _branch_sample_kernel,
        grid=(t // c,),
        in_specs=[pl.BlockSpec((c, 3 * A_W), lambda i: (i, 0)),
                  pl.BlockSpec((c, DN_CONV_CH), lambda i: (i, OFF_QKV // DN_CONV_CH)),
                  pl.BlockSpec((c, DN_V), lambda i: (i, OFF_Z // DN_V)),
                  pl.BlockSpec((c, LANE), lambda i: (i, 0)),
                  pl.BlockSpec((c, A_W), lambda i: (i, 0)),
                  pl.BlockSpec((c, DN_CONV_CH), lambda i: (i, 0)),
                  pl.BlockSpec((NB_S, DN_HEADS, DN_DK, DN_DV), lambda i: (i, 0, 0, 0)),
                  full((CONV_A_K, A_W)), full((DN_CONV_K, DN_CONV_CH)),
                  full((1, LANE)), full((1, LANE)), full((1, LANE))],
        out_specs=[pl.BlockSpec((c, A_W + DN_V), lambda i: (i, 0)),
                   pl.BlockSpec((c, A_W), lambda i: (i, 0)),
                   pl.BlockSpec((NB_S, DN_HEADS, DN_DK, DN_DV), lambda i: (i, 0, 0, 0))],
        out_shape=[jax.ShapeDtypeStruct((t, A_W + DN_V), BF16),
                   jax.ShapeDtypeStruct((t, A_W), F32),
                   jax.ShapeDtypeStruct(state.shape, F32)],
        compiler_params=_cparams(("arbitrary",)),
        name="branch_sample",
    )(p1, p1, p1, pab, ea, eq, state, caw, dcw, alog_row, dtb_row, dnorm)


def _attn_sample_kernel(q_ref, k_ref, v_ref, o_ref):
    for h in range(XA_HEADS):
        sl = slice(h * LANE, (h + 1) * LANE)
        rows = pl.ds(h, MEM_TOKENS, stride=XA_HEADS)
        q = q_ref[:, :, sl].astype(BF16)
        s = jnp.einsum("bqd,bkd->bqk", q, k_ref[:, rows, :].astype(BF16),
                       preferred_element_type=F32) * (XA_DH ** -0.5)
        e = jnp.exp(s - jnp.max(s, axis=-1, keepdims=True))
        p = e / jnp.sum(e, axis=-1, keepdims=True)
        o_ref[:, :, sl] = jnp.einsum("bqk,bkd->bqd", p.astype(BF16), v_ref[:, rows, :].astype(BF16),
                                     preferred_element_type=F32).astype(BF16)


def _attn_sample(q3, ck, cv, nb):
    bsz, length, _ = q3.shape
    return pl.pallas_call(
        _attn_sample_kernel,
        grid=(bsz // nb,),
        in_specs=[pl.BlockSpec((nb, length, XA_W), lambda i: (i, 0, 0)),
                  pl.BlockSpec((nb, MEM_TOKENS * XA_HEADS, XA_DH), lambda i: (i, 0, 0)),
                  pl.BlockSpec((nb, MEM_TOKENS * XA_HEADS, XA_DH), lambda i: (i, 0, 0))],
        out_specs=pl.BlockSpec((nb, length, XA_W), lambda i: (i, 0, 0)),
        out_shape=jax.ShapeDtypeStruct((bsz, length, XA_W), BF16),
        compiler_params=_cparams(("arbitrary",)),
        name="attn_sample",
    )(q3, ck, cv)


def _merge_kernel(yad_ref, ym_ref, p2_ref, x_ref, wb_ref, wo_ref, o_ref):
    yad = yad_ref[...]
    gate = lambda j: jax.nn.sigmoid(p2_ref[:, OFF_G + j * D_MODEL:OFF_G + (j + 1) * D_MODEL])
    merged = (gate(0) * jnp.dot(yad[:, :A_W], wb_ref[0:A_W, :], preferred_element_type=F32)
              + gate(1) * jnp.dot(yad[:, A_W:], wb_ref[A_W:A_W + DN_V, :], preferred_element_type=F32)
              + gate(2) * jnp.dot(ym_ref[...], wb_ref[A_W + DN_V:, :], preferred_element_type=F32))
    o_ref[...] = x_ref[...] + jnp.dot(merged.astype(BF16), wo_ref[...], preferred_element_type=F32)


def _merge(yad, ym, p2, x2d, wb, wo, tm):
    t, d = x2d.shape
    row = lambda n: pl.BlockSpec((tm, n), lambda i: (i, 0))
    return pl.pallas_call(
        _merge_kernel,
        grid=(t // tm,),
        in_specs=[row(A_W + DN_V), row(XA_W), row(W2), row(d), _resident(wb.shape), _resident(wo.shape)],
        out_specs=row(d),
        out_shape=jax.ShapeDtypeStruct((t, d), F32),
        compiler_params=_cparams(("arbitrary",)),
        name="merge",
    )(yad, ym, p2, x2d, wb, wo)


FF_SPLIT = 2
FF_BLK = D_FF // FF_SPLIT
FF_SUB = 512
TM_FF = 2 * FF_SUB


def _ffn_kernel(x_ref, gf_ref, wu_ref, wd_ref, gl_ref, o_ref):
    n_sub = x_ref.shape[0] // FF_SUB

    def block(xn, acc, j):
        gate = jnp.dot(xn, wu_ref[:, j * FF_BLK:(j + 1) * FF_BLK], preferred_element_type=F32)
        up = jnp.dot(xn, wu_ref[:, D_FF + j * FF_BLK:D_FF + (j + 1) * FF_BLK], preferred_element_type=F32)
        hid = (_silu(gate) * up).astype(BF16)
        return acc + jnp.dot(hid, wd_ref[j * FF_BLK:(j + 1) * FF_BLK, :], preferred_element_type=F32)

    state = [None] * n_sub
    for i in range(n_sub + 1):
        if i < n_sub:
            x = x_ref[i * FF_SUB:(i + 1) * FF_SUB, :]
            xn = _rms(x, gf_ref[...]).astype(BF16)
            state[i] = (xn, block(xn, x, 0))
        if i > 0:
            xn, acc = state[i - 1]
            for j in range(1, FF_SPLIT):
                acc = block(xn, acc, j)
            o_ref[(i - 1) * FF_SUB:i * FF_SUB, :] = _rms(acc, gl_ref[...])


def _ffn(x2d, gf, wu, wd, gl, tm):
    t, d = x2d.shape
    return pl.pallas_call(
        _ffn_kernel,
        grid=(t // tm,),
        in_specs=[pl.BlockSpec((tm, d), lambda i: (i, 0)),
                  _resident((1, d)), _resident(wu.shape), _resident(wd.shape), _resident((1, d))],
        out_specs=pl.BlockSpec((tm, d), lambda i: (i, 0)),
        out_shape=jax.ShapeDtypeStruct((t, d), F32),
        compiler_params=_cparams(("arbitrary",)),
        name="ffn",
    )(x2d, gf, wu, wd, gl)


def _pad_lanes(v):
    return jnp.zeros((1, LANE), F32).at[0, :v.shape[0]].set(v.astype(F32))


def kernel(x_prompt, x_sample, mem_prompt, state_conv_a, state_dn_conv, state_dn, cache_mem_k, cache_mem_v,
           norm_mix, w_in, conv_a_w, dn_conv_w, dn_a_log, dn_dt_bias, dn_norm, norm_mem, w_mem_kv, w_branch,
           w_o, norm_ffn, w_ffn_up, w_ffn_down, norm_final):
    bp, lp, d = x_prompt.shape
    bs, ls, _ = x_sample.shape
    assert norm_mix.shape[0] == 1 and ls == SEQ_S and lp % CHUNK == 0 and (bs * ls) % CHUNK == 0

    w = w_in[0]
    w1 = w[:, :W1].astype(BF16)
    w2 = w[:, W1 + N_AB:].astype(BF16)
    wab = jnp.pad(w[:, W1:W1 + N_AB], ((0, 0), (0, LANE - N_AB))).astype(BF16)
    wb = w_branch[0].astype(BF16)
    wo = w_o[0].astype(BF16)
    wu = w_ffn_up[0].astype(BF16)
    wd = w_ffn_down[0].astype(BF16)
    wkv = w_mem_kv[0].astype(BF16)
    g_mix = norm_mix[0][None, :]
    g_ffn = norm_ffn[0][None, :]
    g_fin = norm_final[None, :]
    g_mem = norm_mem[0][None, :]
    caw = conv_a_w[0]
    dcw = dn_conv_w[0]
    alog_row = _pad_lanes(dn_a_log[0])
    dtb_row = _pad_lanes(dn_dt_bias[0])
    dnorm = dn_norm[0][None, :]

    tp = bp * lp
    mk, mv, mkb, mvb = _memkv(mem_prompt.reshape(bp * MEM_TOKENS, d), g_mem, wkv, TM)
    x1_p, ca_p, dc_p, s_p = _front_prompt(x_prompt, g_mix, w1, w2, wab, mkb.reshape(bp, MEM_TOKENS, XA_W),
                                          mvb.reshape(bp, MEM_TOKENS, XA_W), caw, dcw, alog_row, dtb_row, dnorm,
                                          wb, wo, NB_P)
    y_p = _ffn(x1_p.reshape(tp, d), g_ffn, wu, wd, g_fin, TM_FF).reshape(bp, lp, d)

    ts = bs * ls
    xs2 = x_sample.reshape(ts, d)
    p1_s, p2_s, pab_s = _proj(xs2, g_mix, w1, w2, wab, CHUNK)
    ea = jnp.pad(state_conv_a[0], ((0, 0), (0, ls - (CONV_A_K - 1)), (0, 0))).reshape(ts, A_W)
    eq = jnp.pad(state_dn_conv[0], ((0, 0), (0, ls - (DN_CONV_K - 1)), (0, 0))).reshape(ts, DN_CONV_CH)
    yad_s, u_s, s_s = _branch_sample(p1_s, pab_s, ea, eq, state_dn[0], caw, dcw, alog_row, dtb_row, dnorm)
    ym_s = _attn_sample(p2_s[:, :XA_W].reshape(bs, ls, XA_W),
                        cache_mem_k.reshape(bs, MEM_TOKENS * XA_HEADS, XA_DH),
                        cache_mem_v.reshape(bs, MEM_TOKENS * XA_HEADS, XA_DH), 8)
    x1_s = _merge(yad_s, ym_s.reshape(ts, XA_W), p2_s, xs2, wb, wo, ts)
    y_s = _ffn(x1_s, g_ffn, wu, wd, g_fin, ts).reshape(bs, ls, d)
    ca_s = u_s.reshape(bs, ls, A_W)[:, ls - (CONV_A_K - 1):]
    dc_s = p1_s.reshape(bs, ls, W1)[:, ls - (DN_CONV_K - 1):, OFF_QKV:OFF_Z]

    return (y_p, y_s, ca_p[None], dc_p[None], s_p[None],
            mk.reshape(1, bp, MEM_TOKENS, XA_HEADS, XA_DH), mv.reshape(1, bp, MEM_TOKENS, XA_HEADS, XA_DH),
            ca_s[None], dc_s[None], s_s[None])
```

```python
import functools

import jax
import jax.numpy as jnp
from jax import lax
from jax.experimental import pallas as pl
from jax.experimental.pallas import tpu as pltpu

F32 = jnp.float32
BF16 = jnp.bfloat16

D_MODEL = 1024
A_W = 512
CONV_A_K = 3
DN_HEADS = 4
DN_DK = 128
DN_DV = 128
DN_QK = DN_HEADS * DN_DK
DN_V = DN_HEADS * DN_DV
DN_CONV_CH = 2 * DN_QK + DN_V
DN_CONV_K = 4
MEM_TOKENS = 256
XA_HEADS = 4
XA_DH = 128
XA_W = XA_HEADS * XA_DH
D_FF = 2816
EPS = 1e-6

LANE = 128
CHUNK = 128
TAIL = 8
NB_P = 4
TM = 512
GATE_BLK = 256
PROJ_BLK = 512
DN_GROUP = 16

W1 = 3 * A_W + DN_CONV_CH + DN_V
OFF_QKV = 3 * A_W
OFF_Z = OFF_QKV + DN_CONV_CH
W2 = XA_W + 3 * D_MODEL
OFF_G = XA_W
N_AB = 2 * DN_HEADS
P_Z = 3 * A_W
P_XQ = P_Z + DN_V
P_W = P_XQ + XA_W

VMEM_LIMIT = 60 * 1024 * 1024


def _cparams(sem):
    return pltpu.CompilerParams(dimension_semantics=sem, vmem_limit_bytes=VMEM_LIMIT)


def _resident(shape):
    return pl.BlockSpec(shape, lambda *_: (0,) * len(shape), pipeline_mode=pl.Buffered(1))


def _mm(a, b):
    return jnp.dot(a.astype(BF16), b.astype(BF16), preferred_element_type=F32)


def _mm_nt(a, b):
    return lax.dot_general(a.astype(BF16), b.astype(BF16), (((1,), (1,)), ((), ())),
                           preferred_element_type=F32)


def _mm_tn(a, b):
    return lax.dot_general(a.astype(BF16), b.astype(BF16), (((0,), (0,)), ((), ())),
                           preferred_element_type=F32)


def _split3(x):
    hi = x.astype(BF16)
    r1 = x - hi.astype(F32)
    mid = r1.astype(BF16)
    lo = (r1 - mid.astype(F32)).astype(BF16)
    return jnp.concatenate([hi, mid, lo], axis=1)


def _mm_exact01(m01, x):
    n = x.shape[1]
    r = jnp.dot(m01.astype(BF16), _split3(x), preferred_element_type=F32)
    return (r[:, :n] + r[:, n:2 * n]) + r[:, 2 * n:]


def _rms(x, g):
    return x * lax.rsqrt(jnp.mean(x * x, axis=-1, keepdims=True) + EPS) * g


def _silu(x):
    return x * jax.nn.sigmoid(x)


def _softplus(x):
    return jnp.maximum(x, 0.0) + jnp.log1p(jnp.exp(-jnp.abs(x)))


def _memkv_kernel(x_ref, g_ref, w_ref, k_ref, v_ref, kb_ref, vb_ref):
    xn = _rms(x_ref[...], g_ref[...]).astype(BF16)
    kv = jnp.dot(xn, w_ref[...], preferred_element_type=F32)
    k = kv[:, :XA_W]
    v = kv[:, XA_W:]
    k_ref[...] = k
    v_ref[...] = v
    kb_ref[...] = k.astype(BF16)
    vb_ref[...] = v.astype(BF16)


def _memkv(mem2d, gain, w, tm):
    t, d = mem2d.shape
    blk = pl.BlockSpec((tm, XA_W), lambda i: (i, 0))
    return pl.pallas_call(
        _memkv_kernel,
        grid=(t // tm,),
        in_specs=[pl.BlockSpec((tm, d), lambda i: (i, 0)),
                  pl.BlockSpec((1, d), lambda i: (0, 0)),
                  pl.BlockSpec((d, 2 * XA_W), lambda i: (0, 0))],
        out_specs=[blk, blk, blk, blk],
        out_shape=[jax.ShapeDtypeStruct((t, XA_W), F32), jax.ShapeDtypeStruct((t, XA_W), F32),
                   jax.ShapeDtypeStruct((t, XA_W), BF16), jax.ShapeDtypeStruct((t, XA_W), BF16)],
        compiler_params=_cparams(("arbitrary",)),
        name="memkv",
    )(mem2d, gain, w)


def _log2(n):
    return n.bit_length() - 1


def _dn_masks(seg):
    r = lax.broadcasted_iota(jnp.int32, (CHUNK, CHUNK), 0)
    c = lax.broadcasted_iota(jnp.int32, (CHUNK, CHUNK), 1)
    ls = _log2(seg)
    same = (r >> ls) == (c >> ls)
    base = min(8, seg)
    lb = _log2(base)
    m = {
        "same": same.astype(F32),
        "causal": (same & (r >= c)).astype(F32),
        "strict": (same & (r > c)).astype(F32),
        "eye": (r == c).astype(F32),
        "diag": ((r >> lb) == (c >> lb)).astype(F32),
        "off": {},
        "base": base,
    }
    s = base
    while s < seg:
        l1, l2 = _log2(s), _log2(2 * s)
        m["off"][s] = (((r >> l2) == (c >> l2)) & ((r >> l1) != (c >> l1))).astype(F32)
        s *= 2
    return m


def _each(f, *lists):
    return [f(*args) for args in zip(*lists)]


def _low_rows(x, s):
    return jnp.concatenate([x[i + s:i + 2 * s] for i in range(0, x.shape[0], 2 * s)], axis=0)


def _merge_low(x, low, s):
    parts = []
    for j, i in enumerate(range(0, x.shape[0], 2 * s)):
        parts += [x[i:i + s], low[j * s:(j + 1) * s]]
    return jnp.concatenate(parts, axis=0)


def _spread_low(low, s):
    return _merge_low(jnp.zeros((2 * low.shape[0], low.shape[1]), low.dtype), low, s)


def _tri_inv(a_list, m, seg, tick):
    add = lambda x, y: x + y
    b = _each(lambda a: -(a * m["diag"]), a_list)
    p = _each(lambda x: m["eye"] + x, b)
    b2 = _each(_mm, b, b)
    tick()
    p = _each(add, p, _each(_mm, p, b2))
    tick()
    if m["base"] == 8:
        b4 = _each(_mm, b2, b2)
        tick()
        p = _each(add, p, _each(_mm, p, b4))
        tick()
    s = m["base"]
    while s < seg:
        low = lambda t, s=s: _low_rows(t, s)
        x = _each(_mm, _each(lambda a, s=s: low(a) * low(m["off"][s]), a_list), p)
        tick()
        r = _each(_mm, _each(low, p), _each(lambda y, s=s: _spread_low(y, s), x))
        p = _each(lambda t, y, s=s: _merge_low(t, low(t) - y, s), p, r)
        tick()
        s *= 2
    return p


def _dn_gates(ab, alog_row, dtb_row, m):
    g = -jnp.exp(alog_row) * _softplus(ab + dtb_row)
    beta = jax.nn.sigmoid(ab)
    d = _mm_exact01(m["causal"], g)
    dl = _mm_exact01(m["same"], g)
    return d, dl, beta


def _l2n(x):
    return x * lax.rsqrt(jnp.sum(x * x, axis=-1, keepdims=True) + EPS)


def _dn_intra(q, k, v, d_col, d_row, beta_col, m, seg, tick=lambda: None):
    q = _each(lambda x: _l2n(x) * (DN_DK ** -0.5), q)
    k = _each(_l2n, k)
    gamma = _each(lambda dc, dr: jnp.exp((dc - dr) * m["causal"]) * m["causal"], d_col, d_row)
    kk = _each(_mm_nt, k, k)
    a = _each(lambda bc, x, g: (bc * x) * g * m["strict"], beta_col, kk, gamma)
    t = _tri_inv(a, m, seg, tick)
    rhs = _each(lambda vv, kx, bc, dc: jnp.concatenate([vv * bc, kx * (bc * jnp.exp(dc))], axis=1),
                v, k, beta_col, d_col)
    sol = _each(_mm, t, rhs)
    u = [x[:, :DN_DV] for x in sol]
    w = [x[:, DN_DV:] for x in sol]
    qk = _each(lambda x, g: x * g, _each(_mm_nt, q, k), gamma)
    return q, k, u, w, qk


def _dn_out(o, z, dnorm):
    return _rms(o, dnorm) * _silu(z)


def _head_lists(qkv, z, d, dl, beta):
    d_t = d.T
    out = [[] for _ in range(8)]
    for h in range(DN_HEADS):
        vals = (qkv[:, h * LANE:(h + 1) * LANE],
                qkv[:, DN_QK + h * LANE:DN_QK + (h + 1) * LANE],
                qkv[:, 2 * DN_QK + h * LANE:2 * DN_QK + (h + 1) * LANE],
                None if z is None else z[:, h * LANE:(h + 1) * LANE],
                d[:, h:h + 1], d_t[h:h + 1, :], dl[:, h:h + 1], beta[:, DN_HEADS + h:DN_HEADS + h + 1])
        for lst, val in zip(out, vals):
            lst.append(val)
    return out


def _causal_conv(x, tail_ref, b, wts, width):
    c, w = x.shape
    tiles = jnp.concatenate([tail_ref[b][None], x.reshape(c // TAIL, TAIL, w)], axis=0)
    sub = lax.broadcasted_iota(jnp.int32, (1, TAIL, 1), 1)
    acc = None
    for i in range(width):
        s = width - 1 - i
        if s == 0:
            y = tiles[1:]
        else:
            r = pltpu.roll(tiles, s, 1)
            y = jnp.where(sub >= s, r[1:], r[:-1])
        term = wts[i:i + 1][None] * y
        acc = term if acc is None else acc + term
    tail_ref[b] = tiles[c // TAIL]
    return acc.reshape(c, w)


def _front_prompt_kernel(x_ref, gmix_ref, w1_ref, w2_ref, wab_ref, mk_ref, mv_ref, caw_ref, dcw_ref,
                         alog_ref, dtb_ref, dnorm_ref, wb_ref, wo_ref,
                         x1_ref, ca_ref, dc_ref, s_ref, utail, qtail, ybuf, pbuf, gbuf, *, nb):
    c = CHUNK
    rows = nb * c
    t_idx = pl.program_id(1)

    @pl.when(t_idx == 0)
    def _():
        utail[...] = jnp.zeros(utail.shape, F32)
        qtail[...] = jnp.zeros(qtail.shape, F32)
        s_ref[...] = jnp.zeros(s_ref.shape, F32)

    x = x_ref[...].reshape(rows, D_MODEL)
    xn = _rms(x, gmix_ref[...]).astype(BF16)
    proj = lambda w_ref, lo, hi: jnp.dot(xn, w_ref[:, lo:hi], preferred_element_type=F32)

    queue = []

    def enqueue(dst, w_ref, src, dst_lo, width, act=None):
        def run():
            r = proj(w_ref, src, src + width)
            dst[:, dst_lo:dst_lo + width] = r if act is None else act(r)
        queue.append(run)

    for lo in range(0, 3 * A_W, PROJ_BLK):
        enqueue(pbuf, w1_ref, lo, lo, PROJ_BLK)
    enqueue(pbuf, w1_ref, OFF_Z, P_Z, DN_V)
    enqueue(pbuf, w2_ref, 0, P_XQ, XA_W)
    for lo in range(0, 3 * D_MODEL, GATE_BLK):
        enqueue(gbuf, w2_ref, OFF_G + lo, lo, GATE_BLK, jax.nn.sigmoid)
    queue.reverse()

    def tick():
        if queue:
            queue.pop()()

    m = _dn_masks(c)
    caw = caw_ref[...]
    dcw = dcw_ref[...]
    dnorm = dnorm_ref[...]
    pq_all = proj(w1_ref, OFF_QKV, OFF_Z)
    pab_all = jnp.dot(xn, wab_ref[...], preferred_element_type=F32)

    qkv_l, gates_l = [], []
    for b in range(nb):
        rb = slice(b * c, (b + 1) * c)
        qkv_in = pq_all[rb]
        dc_ref[b] = qkv_in[c - (DN_CONV_K - 1):]
        qkv_l.append(_silu(_causal_conv(qkv_in, qtail, b, dcw, DN_CONV_K)))
        gates_l.append(_dn_gates(pab_all[rb], alog_ref[...], dtb_ref[...], m))
        tick()

    for b in range(nb):
        rb = slice(b * c, (b + 1) * c)
        u_in = pbuf[rb, A_W:2 * A_W] * pbuf[rb, 2 * A_W:3 * A_W]
        ca_ref[b] = u_in[c - (CONV_A_K - 1):]
        ybuf[rb, 0:A_W] = (pbuf[rb, 0:A_W] * _causal_conv(u_in, utail, b, caw, CONV_A_K)).astype(BF16)
        tick()

    lists = [[] for _ in range(8)]
    for b in range(nb):
        d, dl, beta = gates_l[b]
        for lst, val in zip(lists, _head_lists(qkv_l[b], None, d, dl, beta)):
            lst.extend(val)
    idx = [(b, h) for b in range(nb) for h in range(DN_HEADS)]
    for g0 in range(0, len(idx), DN_GROUP):
        grp = idx[g0:g0 + DN_GROUP]
        q, k, v, _, d_col, d_row, dl_col, beta_col = [lst[g0:g0 + DN_GROUP] for lst in lists]
        q, k, u, w, qk = _dn_intra(q, k, v, d_col, d_row, beta_col, m, c, tick)
        s_old = [s_ref[b, h] for b, h in grp]
        vn = _each(lambda ux, wx, s: ux - _mm(wx, s), u, w, s_old)
        tick()
        o = _each(lambda qx, dc, y, s, vx: _mm(jnp.concatenate([qx * jnp.exp(dc), y], axis=1),
                                               jnp.concatenate([s, vx], axis=0)), q, d_col, qk, s_old, vn)
        tick()
        s_new = _each(lambda s, dlc, kx, dc, vx: s * jnp.exp(dlc[0:1, :]) + _mm_tn(kx * jnp.exp(dlc - dc), vx),
                      s_old, dl_col, k, d_col, vn)
        for (b, h), sx, ox in zip(grp, s_new, o):
            s_ref[b, h] = sx
            zx = pbuf[b * c:(b + 1) * c, P_Z + h * LANE:P_Z + (h + 1) * LANE]
            ybuf[b * c:(b + 1) * c, A_W + h * LANE:A_W + (h + 1) * LANE] = _dn_out(ox, zx, dnorm).astype(BF16)
    while queue:
        tick()

    xq = [pbuf[b * c:(b + 1) * c, P_XQ + h * LANE:P_XQ + (h + 1) * LANE] for b, h in idx]
    sc = _each(lambda x_, bh: _mm_nt(x_, mk_ref[bh[0], :, bh[1] * LANE:(bh[1] + 1) * LANE]) * (XA_DH ** -0.5),
               xq, idx)
    e = _each(lambda x_: jnp.exp(x_ - jnp.max(x_, axis=-1, keepdims=True)), sc)
    p = _each(lambda x_: x_ / jnp.sum(x_, axis=-1, keepdims=True), e)
    for (b, h), px in zip(idx, p):
        ybuf[b * c:(b + 1) * c, A_W + DN_V + h * LANE:A_W + DN_V + (h + 1) * LANE] = (
            _mm(px, mv_ref[b, :, h * LANE:(h + 1) * LANE]).astype(BF16))

    merged = None
    for j, (lo, hi) in enumerate(((0, A_W), (A_W, A_W + DN_V), (A_W + DN_V, A_W + DN_V + XA_W))):
        term = gbuf[:, j * D_MODEL:(j + 1) * D_MODEL] * jnp.dot(ybuf[:, lo:hi], wb_ref[lo:hi, :],
                                                                preferred_element_type=F32)
        merged = term if merged is None else merged + term
    x1 = x + jnp.dot(merged.astype(BF16), wo_ref[...], preferred_element_type=F32)
    x1_ref[...] = x1.reshape(nb, c, D_MODEL)


def _front_prompt(x, gmix, w1, w2, wab, mkb, mvb, caw, dcw, alog_row, dtb_row, dnorm, wb, wo, nb):
    bsz, length, d = x.shape
    c = CHUNK
    return pl.pallas_call(
        functools.partial(_front_prompt_kernel, nb=nb),
        grid=(bsz // nb, length // c),
        in_specs=[pl.BlockSpec((nb, c, d), lambda g, t: (g, t, 0)),
                  _resident((1, d)), _resident(w1.shape), _resident(w2.shape), _resident(wab.shape),
                  pl.BlockSpec((nb, MEM_TOKENS, XA_W), lambda g, t: (g, 0, 0), pipeline_mode=pl.Buffered(1)),
                  pl.BlockSpec((nb, MEM_TOKENS, XA_W), lambda g, t: (g, 0, 0), pipeline_mode=pl.Buffered(1)),
                  _resident((CONV_A_K, A_W)), _resident((DN_CONV_K, DN_CONV_CH)),
                  _resident((1, LANE)), _resident((1, LANE)), _resident((1, LANE)),
                  _resident(wb.shape), _resident(wo.shape)],
        out_specs=[pl.BlockSpec((nb, c, d), lambda g, t: (g, t, 0)),
                   pl.BlockSpec((nb, CONV_A_K - 1, A_W), lambda g, t: (g, 0, 0)),
                   pl.BlockSpec((nb, DN_CONV_K - 1, DN_CONV_CH), lambda g, t: (g, 0, 0)),
                   pl.BlockSpec((nb, DN_HEADS, DN_DK, DN_DV), lambda g, t: (g, 0, 0, 0))],
        out_shape=[jax.ShapeDtypeStruct((bsz, length, d), F32),
                   jax.ShapeDtypeStruct((bsz, CONV_A_K - 1, A_W), F32),
                   jax.ShapeDtypeStruct((bsz, DN_CONV_K - 1, DN_CONV_CH), F32),
                   jax.ShapeDtypeStruct((bsz, DN_HEADS, DN_DK, DN_DV), F32)],
        scratch_shapes=[pltpu.VMEM((nb, TAIL, A_W), F32), pltpu.VMEM((nb, TAIL, DN_CONV_CH), F32),
                        pltpu.VMEM((nb * c, A_W + DN_V + XA_W), BF16), pltpu.VMEM((nb * c, P_W), F32),
                        pltpu.VMEM((nb * c, 3 * d), F32)],
        compiler_params=_cparams(("arbitrary", "arbitrary")),
        name="front_prompt",
    )(x, gmix, w1, w2, wab, mkb, mvb, caw, dcw, alog_row, dtb_row, dnorm, wb, wo)


def _proj_kernel(x_ref, g_ref, w1_ref, w2_ref, wab_ref, p1_ref, p2_ref, pab_ref):
    xn = _rms(x_ref[...], g_ref[...]).astype(BF16)
    p1_ref[...] = jnp.dot(xn, w1_ref[...], preferred_element_type=F32)
    p2_ref[...] = jnp.dot(xn, w2_ref[...], preferred_element_type=F32)
    pab_ref[...] = jnp.dot(xn, wab_ref[...], preferred_element_type=F32)


def _proj(x2d, gain, w1, w2, wab, tm):
    t, d = x2d.shape
    row = lambda n: pl.BlockSpec((tm, n), lambda i: (i, 0))
    return pl.pallas_call(
        _proj_kernel,
        grid=(t // tm,),
        in_specs=[row(d), _resident((1, d)), _resident(w1.shape), _resident(w2.shape), _resident(wab.shape)],
        out_specs=[row(W1), row(W2), row(LANE)],
        out_shape=[jax.ShapeDtypeStruct((t, W1), F32), jax.ShapeDtypeStruct((t, W2), F32),
                   jax.ShapeDtypeStruct((t, LANE), F32)],
        compiler_params=_cparams(("arbitrary",)),
        name="proj",
    )(x2d, gain, w1, w2, wab)


SEQ_S = 4
NB_S = CHUNK // SEQ_S


def _seg_conv(x, e, wts, width, tmod):
    rows = x.shape[0]
    acc = None
    for i in range(width):
        s = width - 1 - i
        term = x if s == 0 else jnp.where(tmod >= s, pltpu.roll(x, s, 0), 0.0)
        if i < width - 1:
            hist = e if i == 0 else pltpu.roll(e, rows - i, 0)
            term = term + jnp.where(tmod < SEQ_S - i, hist, 0.0)
        term = wts[i:i + 1] * term
        acc = term if acc is None else acc + term
    return acc


def _branch_sample_kernel(pa_ref, pq_ref, pz_ref, pab_ref, ea_ref, eq_ref, s0_ref, caw_ref, dcw_ref,
                          alog_ref, dtb_ref, dnorm_ref, yad_ref, u_ref, s_ref):
    c = CHUNK
    m = _dn_masks(SEQ_S)
    tmod = lax.broadcasted_iota(jnp.int32, (c, 1), 0) & (SEQ_S - 1)

    pa = pa_ref[...]
    u_in = pa[:, A_W:2 * A_W] * pa[:, 2 * A_W:3 * A_W]
    u_ref[...] = u_in
    conv = _seg_conv(u_in, ea_ref[...], caw_ref[...], CONV_A_K, tmod)
    yad_ref[:, 0:A_W] = (pa[:, 0:A_W] * conv).astype(BF16)

    qkv = _silu(_seg_conv(pq_ref[...], eq_ref[...], dcw_ref[...], DN_CONV_K, tmod))
    d, dl, beta = _dn_gates(pab_ref[...], alog_ref[...], dtb_ref[...], m)
    dec_t = jnp.exp(dl).T
    dnorm = dnorm_ref[...]

    wide = NB_S * DN_DK
    er = lax.broadcasted_iota(jnp.int32, (c, wide), 0)
    ec = lax.broadcasted_iota(jnp.int32, (c, wide), 1)
    mexp = ((er >> 2) == (ec >> 7)).astype(F32)
    mexp2 = jnp.concatenate([mexp, mexp], axis=0)
    tr = lax.broadcasted_iota(jnp.int32, (wide, c), 0)
    tc = lax.broadcasted_iota(jnp.int32, (wide, c), 1)
    mexp_t = ((tr >> 7) == (tc >> 2)).astype(F32)

    q, k, v, z, d_col, d_row, dl_col, beta_col = _head_lists(qkv, pz_ref[...], d, dl, beta)
    q, k, u, w, qk = _dn_intra(q, k, v, d_col, d_row, beta_col, m, SEQ_S)
    heads = list(range(DN_HEADS))
    s_old = [s0_ref[:, h].reshape(wide, DN_DV) for h in heads]
    x_exp = _each(lambda wx, qx, dc: jnp.concatenate([jnp.concatenate([wx, qx * jnp.exp(dc)], axis=0)] * NB_S,
                                                     axis=1) * mexp2, w, q, d_col)
    ws = _each(_mm, x_exp, s_old)
    vn = _each(lambda ux, x: ux - x[:c], u, ws)
    o = _each(lambda x, y, vx: x[c:] + _mm(y, vx), ws, qk, vn)
    k_exp = _each(lambda kx, dlc, dc: jnp.concatenate([(kx * jnp.exp(dlc - dc)).T] * NB_S, axis=0) * mexp_t,
                  k, dl_col, d_col)
    dec = [jnp.concatenate([jnp.broadcast_to(dec_t[h:h + 1, SEQ_S * b:SEQ_S * b + 1], (DN_DK, DN_DV))
                            for b in range(NB_S)], axis=0) for h in heads]
    s_new = _each(lambda s, dx, kx, vx: s * dx + _mm(kx, vx), s_old, dec, k_exp, vn)
    for h, sx, ox, zx in zip(heads, s_new, o, z):
        s_ref[:, h] = sx.reshape(NB_S, DN_DK, DN_DV)
        yad_ref[:, A_W + h * LANE:A_W + (h + 1) * LANE] = _dn_out(ox, zx, dnorm).astype(BF16)


def _branch_sample(p1, pab, ea, eq, state, caw, dcw, alog_row, dtb_row, dnorm):
    t = p1.shape[0]
    c = CHUNK
    full = lambda shape: pl.BlockSpec(shape, lambda i: (0,) * len(shape))
    return pl.pallas_call(
        _branch_sample_kernel,
        grid=(t // c,),
        in_specs=[pl.BlockSpec((c, 3 * A_W), lambda i: (i, 0)),
                  pl.BlockSpec((c, DN_CONV_CH), lambda i: (i, OFF_QKV // DN_CONV_CH)),
                  pl.BlockSpec((c, DN_V), lambda i: (i, OFF_Z // DN_V)),
                  pl.BlockSpec((c, LANE), lambda i: (i, 0)),
                  pl.BlockSpec((c, A_W), lambda i: (i, 0)),
                  pl.BlockSpec((c, DN_CONV_CH), lambda i: (i, 0)),
                  pl.BlockSpec((NB_S, DN_HEADS, DN_DK, DN_DV), lambda i: (i, 0, 0, 0)),
                  full((CONV_A_K, A_W)), full((DN_CONV_K, DN_CONV_CH)),
                  full((1, LANE)), full((1, LANE)), full((1, LANE))],
        out_specs=[pl.BlockSpec((c, A_W + DN_V), lambda i: (i, 0)),
                   pl.BlockSpec((c, A_W), lambda i: (i, 0)),
                   pl.BlockSpec((NB_S, DN_HEADS, DN_DK, DN_DV), lambda i: (i, 0, 0, 0))],
        out_shape=[jax.ShapeDtypeStruct((t, A_W + DN_V), BF16),
                   jax.ShapeDtypeStruct((t, A_W), F32),
                   jax.ShapeDtypeStruct(state.shape, F32)],
        compiler_params=_cparams(("arbitrary",)),
        name="branch_sample",
    )(p1, p1, p1, pab, ea, eq, state, caw, dcw, alog_row, dtb_row, dnorm)


def _attn_sample_kernel(q_ref, k_ref, v_ref, o_ref):
    for h in range(XA_HEADS):
        sl = slice(h * LANE, (h + 1) * LANE)
        rows = pl.ds(h, MEM_TOKENS, stride=XA_HEADS)
        q = q_ref[:, :, sl].astype(BF16)
        s = jnp.einsum("bqd,bkd->bqk", q, k_ref[:, rows, :].astype(BF16),
                       preferred_element_type=F32) * (XA_DH ** -0.5)
        e = jnp.exp(s - jnp.max(s, axis=-1, keepdims=True))
        p = e / jnp.sum(e, axis=-1, keepdims=True)
        o_ref[:, :, sl] = jnp.einsum("bqk,bkd->bqd", p.astype(BF16), v_ref[:, rows, :].astype(BF16),
                                     preferred_element_type=F32).astype(BF16)


def _attn_sample(q3, ck, cv, nb):
    bsz, length, _ = q3.shape
    return pl.pallas_call(
        _attn_sample_kernel,
        grid=(bsz // nb,),
        in_specs=[pl.BlockSpec((nb, length, XA_W), lambda i: (i, 0, 0)),
                  pl.BlockSpec((nb, MEM_TOKENS * XA_HEADS, XA_DH), lambda i: (i, 0, 0)),
                  pl.BlockSpec((nb, MEM_TOKENS * XA_HEADS, XA_DH), lambda i: (i, 0, 0))],
        out_specs=pl.BlockSpec((nb, length, XA_W), lambda i: (i, 0, 0)),
        out_shape=jax.ShapeDtypeStruct((bsz, length, XA_W), BF16),
        compiler_params=_cparams(("arbitrary",)),
        name="attn_sample",
    )(q3, ck, cv)


def _merge_kernel(yad_ref, ym_ref, p2_ref, x_ref, wb_ref, wo_ref, o_ref):
    yad = yad_ref[...]
    gate = lambda j: jax.nn.sigmoid(p2_ref[:, OFF_G + j * D_MODEL:OFF_G + (j + 1) * D_MODEL])
    merged = (gate(0) * jnp.dot(yad[:, :A_W], wb_ref[0:A_W, :], preferred_element_type=F32)
              + gate(1) * jnp.dot(yad[:, A_W:], wb_ref[A_W:A_W + DN_V, :], preferred_element_type=F32)
              + gate(2) * jnp.dot(ym_ref[...], wb_ref[A_W + DN_V:, :], preferred_element_type=F32))
    o_ref[...] = x_ref[...] + jnp.dot(merged.astype(BF16), wo_ref[...], preferred_element_type=F32)


def _merge(yad, ym, p2, x2d, wb, wo, tm):
    t, d = x2d.shape
    row = lambda n: pl.BlockSpec((tm, n), lambda i: (i, 0))
    return pl.pallas_call(
        _merge_kernel,
        grid=(t // tm,),
        in_specs=[row(A_W + DN_V), row(XA_W), row(W2), row(d), _resident(wb.shape), _resident(wo.shape)],
        out_specs=row(d),
        out_shape=jax.ShapeDtypeStruct((t, d), F32),
        compiler_params=_cparams(("arbitrary",)),
        name="merge",
    )(yad, ym, p2, x2d, wb, wo)


MXU_K = 256
FF_EDGES = (0, 6 * MXU_K, D_FF)
FF_SPLIT = len(FF_EDGES) - 1
FF_SUB = 512
TM_FF = 2 * FF_SUB


def _ffn_kernel(x_ref, gf_ref, wu_ref, wd_ref, gl_ref, o_ref):
    n_sub = x_ref.shape[0] // FF_SUB

    def block(xn, acc, j):
        lo, hi = FF_EDGES[j], FF_EDGES[j + 1]
        gate = jnp.dot(xn, wu_ref[:, lo:hi], preferred_element_type=F32)
        up = jnp.dot(xn, wu_ref[:, D_FF + lo:D_FF + hi], preferred_element_type=F32)
        hid = (_silu(gate) * up).astype(BF16)
        return acc + jnp.dot(hid, wd_ref[lo:hi, :], preferred_element_type=F32)

    state = [None] * n_sub
    for i in range(n_sub + 1):
        if i < n_sub:
            x = x_ref[i * FF_SUB:(i + 1) * FF_SUB, :]
            xn = _rms(x, gf_ref[...]).astype(BF16)
            state[i] = (xn, block(xn, x, 0))
        if i > 0:
            xn, acc = state[i - 1]
            for j in range(1, FF_SPLIT):
                acc = block(xn, acc, j)
            o_ref[(i - 1) * FF_SUB:i * FF_SUB, :] = _rms(acc, gl_ref[...])


def _ffn(x2d, gf, wu, wd, gl, tm):
    t, d = x2d.shape
    return pl.pallas_call(
        _ffn_kernel,
        grid=(t // tm,),
        in_specs=[pl.BlockSpec((tm, d), lambda i: (i, 0)),
                  _resident((1, d)), _resident(wu.shape), _resident(wd.shape), _resident((1, d))],
        out_specs=pl.BlockSpec((tm, d), lambda i: (i, 0)),
        out_shape=jax.ShapeDtypeStruct((t, d), F32),
        compiler_params=_cparams(("arbitrary",)),
        name="ffn",
    )(x2d, gf, wu, wd, gl)


def _pad_lanes(v):
    return jnp.zeros((1, LANE), F32).at[0, :v.shape[0]].set(v.astype(F32))


def kernel(x_prompt, x_sample, mem_prompt, state_conv_a, state_dn_conv, state_dn, cache_mem_k, cache_mem_v,
           norm_mix, w_in, conv_a_w, dn_conv_w, dn_a_log, dn_dt_bias, dn_norm, norm_mem, w_mem_kv, w_branch,
           w_o, norm_ffn, w_ffn_up, w_ffn_down, norm_final):
    bp, lp, d = x_prompt.shape
    bs, ls, _ = x_sample.shape
    assert norm_mix.shape[0] == 1 and ls == SEQ_S and lp % CHUNK == 0 and (bs * ls) % CHUNK == 0

    w = w_in[0]
    w1 = w[:, :W1].astype(BF16)
    w2 = w[:, W1 + N_AB:].astype(BF16)
    wab = jnp.pad(w[:, W1:W1 + N_AB], ((0, 0), (0, LANE - N_AB))).astype(BF16)
    wb = w_branch[0].astype(BF16)
    wo = w_o[0].astype(BF16)
    wu = w_ffn_up[0].astype(BF16)
    wd = w_ffn_down[0].astype(BF16)
    wkv = w_mem_kv[0].astype(BF16)
    g_mix = norm_mix[0][None, :]
    g_ffn = norm_ffn[0][None, :]
    g_fin = norm_final[None, :]
    g_mem = norm_mem[0][None, :]
    caw = conv_a_w[0]
    dcw = dn_conv_w[0]
    alog_row = _pad_lanes(dn_a_log[0])
    dtb_row = _pad_lanes(dn_dt_bias[0])
    dnorm = dn_norm[0][None, :]

    tp = bp * lp
    mk, mv, mkb, mvb = _memkv(mem_prompt.reshape(bp * MEM_TOKENS, d), g_mem, wkv, TM)
    x1_p, ca_p, dc_p, s_p = _front_prompt(x_prompt, g_mix, w1, w2, wab, mkb.reshape(bp, MEM_TOKENS, XA_W),
                                          mvb.reshape(bp, MEM_TOKENS, XA_W), caw, dcw, alog_row, dtb_row, dnorm,
                                          wb, wo, NB_P)
    y_p = _ffn(x1_p.reshape(tp, d), g_ffn, wu, wd, g_fin, TM_FF).reshape(bp, lp, d)

    ts = bs * ls
    xs2 = x_sample.reshape(ts, d)
    p1_s, p2_s, pab_s = _proj(xs2, g_mix, w1, w2, wab, CHUNK)
    ea = jnp.pad(state_conv_a[0], ((0, 0), (0, ls - (CONV_A_K - 1)), (0, 0))).reshape(ts, A_W)
    eq = jnp.pad(state_dn_conv[0], ((0, 0), (0, ls - (DN_CONV_K - 1)), (0, 0))).reshape(ts, DN_CONV_CH)
    yad_s, u_s, s_s = _branch_sample(p1_s, pab_s, ea, eq, state_dn[0], caw, dcw, alog_row, dtb_row, dnorm)
    ym_s = _attn_sample(p2_s[:, :XA_W].reshape(bs, ls, XA_W),
                        cache_mem_k.reshape(bs, MEM_TOKENS * XA_HEADS, XA_DH),
                        cache_mem_v.reshape(bs, MEM_TOKENS * XA_HEADS, XA_DH), 8)
    x1_s = _merge(yad_s, ym_s.reshape(ts, XA_W), p2_s, xs2, wb, wo, ts)
    y_s = _ffn(x1_s, g_ffn, wu, wd, g_fin, ts).reshape(bs, ls, d)
    ca_s = u_s.reshape(bs, ls, A_W)[:, ls - (CONV_A_K - 1):]
    dc_s = p1_s.reshape(bs, ls, W1)[:, ls - (DN_CONV_K - 1):, OFF_QKV:OFF_Z]

    return (y_p, y_s, ca_p[None], dc_p[None], s_p[None],
            mk.reshape(1, bp, MEM_TOKENS, XA_HEADS, XA_DH), mv.reshape(1, bp, MEM_TOKENS, XA_HEADS, XA_DH),
            ca_s[None], dc_s[None], s_s[None])
```

```python
import functools

import jax
import jax.numpy as jnp
from jax import lax
from jax.experimental import pallas as pl
from jax.experimental.pallas import tpu as pltpu

F32 = jnp.float32
BF16 = jnp.bfloat16

D_MODEL = 1024
A_W = 512
CONV_A_K = 3
DN_HEADS = 4
DN_DK = 128
DN_DV = 128
DN_QK = DN_HEADS * DN_DK
DN_V = DN_HEADS * DN_DV
DN_CONV_CH = 2 * DN_QK + DN_V
DN_CONV_K = 4
MEM_TOKENS = 256
XA_HEADS = 4
XA_DH = 128
XA_W = XA_HEADS * XA_DH
D_FF = 2816
EPS = 1e-6

LANE = 128
CHUNK = 128
TAIL = 8
NB_P = 4
TM = 512
GATE_BLK = 256
PROJ_BLK = 512
DN_GROUP = 16
ATTN_RESERVE = 4

W1 = 3 * A_W + DN_CONV_CH + DN_V
OFF_QKV = 3 * A_W
OFF_Z = OFF_QKV + DN_CONV_CH
W2 = XA_W + 3 * D_MODEL
OFF_G = XA_W
N_AB = 2 * DN_HEADS
P_Z = 3 * A_W
P_XQ = P_Z + DN_V
P_W = P_XQ + XA_W

VMEM_LIMIT = 60 * 1024 * 1024


def _cparams(sem):
    return pltpu.CompilerParams(dimension_semantics=sem, vmem_limit_bytes=VMEM_LIMIT)


def _resident(shape):
    return pl.BlockSpec(shape, lambda *_: (0,) * len(shape), pipeline_mode=pl.Buffered(1))


def _mm(a, b):
    return jnp.dot(a.astype(BF16), b.astype(BF16), preferred_element_type=F32)


def _mm_nt(a, b):
    return lax.dot_general(a.astype(BF16), b.astype(BF16), (((1,), (1,)), ((), ())),
                           preferred_element_type=F32)


def _mm_tn(a, b):
    return lax.dot_general(a.astype(BF16), b.astype(BF16), (((0,), (0,)), ((), ())),
                           preferred_element_type=F32)


def _split3(x):
    hi = x.astype(BF16)
    r1 = x - hi.astype(F32)
    mid = r1.astype(BF16)
    lo = (r1 - mid.astype(F32)).astype(BF16)
    return jnp.concatenate([hi, mid, lo], axis=1)


def _mm_exact01(m01, x):
    n = x.shape[1]
    r = jnp.dot(m01.astype(BF16), _split3(x), preferred_element_type=F32)
    return (r[:, :n] + r[:, n:2 * n]) + r[:, 2 * n:]


def _rms(x, g):
    return x * lax.rsqrt(jnp.mean(x * x, axis=-1, keepdims=True) + EPS) * g


def _silu(x):
    return x * jax.nn.sigmoid(x)


def _softplus(x):
    return jnp.maximum(x, 0.0) + jnp.log1p(jnp.exp(-jnp.abs(x)))


def _memkv_kernel(x_ref, g_ref, w_ref, k_ref, v_ref, kb_ref, vb_ref):
    xn = _rms(x_ref[...], g_ref[...]).astype(BF16)
    kv = jnp.dot(xn, w_ref[...], preferred_element_type=F32)
    k = kv[:, :XA_W]
    v = kv[:, XA_W:]
    tm = k.shape[0]
    for h in range(XA_HEADS):
        rows = pl.ds(h, tm, stride=XA_HEADS)
        k_ref[rows, :] = k[:, h * LANE:(h + 1) * LANE]
        v_ref[rows, :] = v[:, h * LANE:(h + 1) * LANE]
    kb_ref[...] = k.astype(BF16)
    vb_ref[...] = v.astype(BF16)


def _memkv(mem2d, gain, w, tm):
    t, d = mem2d.shape
    blk = pl.BlockSpec((tm, XA_W), lambda i: (i, 0))
    blk_rows = pl.BlockSpec((tm * XA_HEADS, XA_DH), lambda i: (i, 0))
    return pl.pallas_call(
        _memkv_kernel,
        grid=(t // tm,),
        in_specs=[pl.BlockSpec((tm, d), lambda i: (i, 0)),
                  pl.BlockSpec((1, d), lambda i: (0, 0)),
                  pl.BlockSpec((d, 2 * XA_W), lambda i: (0, 0))],
        out_specs=[blk_rows, blk_rows, blk, blk],
        out_shape=[jax.ShapeDtypeStruct((t * XA_HEADS, XA_DH), F32),
                   jax.ShapeDtypeStruct((t * XA_HEADS, XA_DH), F32),
                   jax.ShapeDtypeStruct((t, XA_W), BF16), jax.ShapeDtypeStruct((t, XA_W), BF16)],
        compiler_params=_cparams(("arbitrary",)),
        name="memkv",
    )(mem2d, gain, w)


def _log2(n):
    return n.bit_length() - 1


def _dn_masks(seg):
    r = lax.broadcasted_iota(jnp.int32, (CHUNK, CHUNK), 0)
    c = lax.broadcasted_iota(jnp.int32, (CHUNK, CHUNK), 1)
    ls = _log2(seg)
    same = (r >> ls) == (c >> ls)
    base = min(8, seg)
    lb = _log2(base)
    m = {
        "same": same.astype(F32),
        "causal": (same & (r >= c)).astype(F32),
        "strict": (same & (r > c)).astype(F32),
        "eye": (r == c).astype(F32),
        "diag": ((r >> lb) == (c >> lb)).astype(F32),
        "off": {},
        "base": base,
    }
    s = base
    while s < seg:
        l1, l2 = _log2(s), _log2(2 * s)
        m["off"][s] = (((r >> l2) == (c >> l2)) & ((r >> l1) != (c >> l1))).astype(F32)
        s *= 2
    return m


def _each(f, *lists):
    return [f(*args) for args in zip(*lists)]


def _low_rows(x, s):
    return jnp.concatenate([x[i + s:i + 2 * s] for i in range(0, x.shape[0], 2 * s)], axis=0)


def _merge_low(x, low, s):
    parts = []
    for j, i in enumerate(range(0, x.shape[0], 2 * s)):
        parts += [x[i:i + s], low[j * s:(j + 1) * s]]
    return jnp.concatenate(parts, axis=0)


def _spread_low(low, s):
    return _merge_low(jnp.zeros((2 * low.shape[0], low.shape[1]), low.dtype), low, s)


def _tri_inv(a_list, m, seg, tick):
    add = lambda x, y: x + y
    b = _each(lambda a: -(a * m["diag"]), a_list)
    p = _each(lambda x: m["eye"] + x, b)
    b2 = _each(_mm, b, b)
    tick()
    p = _each(add, p, _each(_mm, p, b2))
    tick()
    if m["base"] == 8:
        b4 = _each(_mm, b2, b2)
        tick()
        p = _each(add, p, _each(_mm, p, b4))
        tick()
    s = m["base"]
    while s < seg:
        low = lambda t, s=s: _low_rows(t, s)
        x = _each(_mm, _each(lambda a, s=s: low(a) * low(m["off"][s]), a_list), p)
        tick()
        r = _each(_mm, _each(low, p), _each(lambda y, s=s: _spread_low(y, s), x))
        p = _each(lambda t, y, s=s: _merge_low(t, low(t) - y, s), p, r)
        tick()
        s *= 2
    return p


def _dn_gates(ab, alog_row, dtb_row, m):
    g = -jnp.exp(alog_row) * _softplus(ab + dtb_row)
    beta = jax.nn.sigmoid(ab)
    d = _mm_exact01(m["causal"], g)
    dl = _mm_exact01(m["same"], g)
    return d, dl, beta


def _l2n(x):
    return x * lax.rsqrt(jnp.sum(x * x, axis=-1, keepdims=True) + EPS)


def _dn_intra(q, k, v, d_col, d_row, beta_col, m, seg, tick=lambda: None):
    q = _each(lambda x: _l2n(x) * (DN_DK ** -0.5), q)
    k = _each(_l2n, k)
    gamma = _each(lambda dc, dr: jnp.exp((dc - dr) * m["causal"]) * m["causal"], d_col, d_row)
    kk = _each(_mm_nt, k, k)
    a = _each(lambda bc, x, g: (bc * x) * g * m["strict"], beta_col, kk, gamma)
    t = _tri_inv(a, m, seg, tick)
    rhs = _each(lambda vv, kx, bc, dc: jnp.concatenate([vv * bc, kx * (bc * jnp.exp(dc))], axis=1),
                v, k, beta_col, d_col)
    sol = _each(_mm, t, rhs)
    u = [x[:, :DN_DV] for x in sol]
    w = [x[:, DN_DV:] for x in sol]
    qk = _each(lambda x, g: x * g, _each(_mm_nt, q, k), gamma)
    return q, k, u, w, qk


def _dn_out(o, z, dnorm):
    return _rms(o, dnorm) * _silu(z)


def _head_lists(qkv, z, d, dl, beta):
    d_t = d.T
    out = [[] for _ in range(8)]
    for h in range(DN_HEADS):
        vals = (qkv[:, h * LANE:(h + 1) * LANE],
                qkv[:, DN_QK + h * LANE:DN_QK + (h + 1) * LANE],
                qkv[:, 2 * DN_QK + h * LANE:2 * DN_QK + (h + 1) * LANE],
                None if z is None else z[:, h * LANE:(h + 1) * LANE],
                d[:, h:h + 1], d_t[h:h + 1, :], dl[:, h:h + 1], beta[:, DN_HEADS + h:DN_HEADS + h + 1])
        for lst, val in zip(out, vals):
            lst.append(val)
    return out


def _causal_conv(x, tail_ref, b, wts, width):
    c, w = x.shape
    tiles = jnp.concatenate([tail_ref[b][None], x.reshape(c // TAIL, TAIL, w)], axis=0)
    sub = lax.broadcasted_iota(jnp.int32, (1, TAIL, 1), 1)
    acc = None
    for i in range(width):
        s = width - 1 - i
        if s == 0:
            y = tiles[1:]
        else:
            r = pltpu.roll(tiles, s, 1)
            y = jnp.where(sub >= s, r[1:], r[:-1])
        term = wts[i:i + 1][None] * y
        acc = term if acc is None else acc + term
    tail_ref[b] = tiles[c // TAIL]
    return acc.reshape(c, w)


def _front_prompt_kernel(x_ref, gmix_ref, w1_ref, w2_ref, wab_ref, mk_ref, mv_ref, caw_ref, dcw_ref,
                         alog_ref, dtb_ref, dnorm_ref, wb_ref, wo_ref,
                         x1_ref, ca_ref, dc_ref, s_ref, utail, qtail, ybuf, pbuf, gbuf, *, nb):
    c = CHUNK
    rows = nb * c
    t_idx = pl.program_id(1)

    @pl.when(t_idx == 0)
    def _():
        utail[...] = jnp.zeros(utail.shape, F32)
        qtail[...] = jnp.zeros(qtail.shape, F32)
        s_ref[...] = jnp.zeros(s_ref.shape, F32)

    x = x_ref[...].reshape(rows, D_MODEL)
    xn = _rms(x, gmix_ref[...]).astype(BF16)
    proj = lambda w_ref, lo, hi: jnp.dot(xn, w_ref[:, lo:hi], preferred_element_type=F32)

    queue = []

    def enqueue(dst, w_ref, src, dst_lo, width, act=None):
        def run():
            r = proj(w_ref, src, src + width)
            dst[:, dst_lo:dst_lo + width] = r if act is None else act(r)
        queue.append(run)

    for lo in range(0, 3 * A_W, PROJ_BLK):
        enqueue(pbuf, w1_ref, lo, lo, PROJ_BLK)
    enqueue(pbuf, w1_ref, OFF_Z, P_Z, DN_V)
    enqueue(pbuf, w2_ref, 0, P_XQ, XA_W)
    for lo in range(0, 3 * D_MODEL, GATE_BLK):
        enqueue(gbuf, w2_ref, OFF_G + lo, lo, GATE_BLK, jax.nn.sigmoid)
    queue.reverse()

    def tick(keep=ATTN_RESERVE):
        if len(queue) > keep:
            queue.pop()()

    m = _dn_masks(c)
    caw = caw_ref[...]
    dcw = dcw_ref[...]
    dnorm = dnorm_ref[...]
    pq_all = proj(w1_ref, OFF_QKV, OFF_Z)
    pab_all = jnp.dot(xn, wab_ref[...], preferred_element_type=F32)

    qkv_l, gates_l = [], []
    for b in range(nb):
        rb = slice(b * c, (b + 1) * c)
        tick()
        qkv_in = pq_all[rb]
        dc_ref[b] = qkv_in[c - (DN_CONV_K - 1):]
        qkv_l.append(_silu(_causal_conv(qkv_in, qtail, b, dcw, DN_CONV_K)))
        gates_l.append(_dn_gates(pab_all[rb], alog_ref[...], dtb_ref[...], m))

    for b in range(nb):
        rb = slice(b * c, (b + 1) * c)
        tick()
        u_in = pbuf[rb, A_W:2 * A_W] * pbuf[rb, 2 * A_W:3 * A_W]
        ca_ref[b] = u_in[c - (CONV_A_K - 1):]
        ybuf[rb, 0:A_W] = (pbuf[rb, 0:A_W] * _causal_conv(u_in, utail, b, caw, CONV_A_K)).astype(BF16)

    lists = [[] for _ in range(8)]
    for b in range(nb):
        d, dl, beta = gates_l[b]
        for lst, val in zip(lists, _head_lists(qkv_l[b], None, d, dl, beta)):
            lst.extend(val)
    idx = [(b, h) for b in range(nb) for h in range(DN_HEADS)]
    for g0 in range(0, len(idx), DN_GROUP):
        grp = idx[g0:g0 + DN_GROUP]
        q, k, v, _, d_col, d_row, dl_col, beta_col = [lst[g0:g0 + DN_GROUP] for lst in lists]
        q, k, u, w, qk = _dn_intra(q, k, v, d_col, d_row, beta_col, m, c, tick)
        s_old = [s_ref[b, h] for b, h in grp]
        vn = _each(lambda ux, wx, s: ux - _mm(wx, s), u, w, s_old)
        tick()
        o = _each(lambda qx, dc, y, s, vx: _mm(jnp.concatenate([qx * jnp.exp(dc), y], axis=1),
                                               jnp.concatenate([s, vx], axis=0)), q, d_col, qk, s_old, vn)
        tick()
        s_new = _each(lambda s, dlc, kx, dc, vx: s * jnp.exp(dlc[0:1, :]) + _mm_tn(kx * jnp.exp(dlc - dc), vx),
                      s_old, dl_col, k, d_col, vn)
        for (b, h), sx, ox in zip(grp, s_new, o):
            s_ref[b, h] = sx
            zx = pbuf[b * c:(b + 1) * c, P_Z + h * LANE:P_Z + (h + 1) * LANE]
            ybuf[b * c:(b + 1) * c, A_W + h * LANE:A_W + (h + 1) * LANE] = _dn_out(ox, zx, dnorm).astype(BF16)

    for b in range(nb):
        tick(0)
        heads = range(XA_HEADS)
        sc = [_mm_nt(pbuf[b * c:(b + 1) * c, P_XQ + h * LANE:P_XQ + (h + 1) * LANE],
                     mk_ref[b, :, h * LANE:(h + 1) * LANE]) * (XA_DH ** -0.5) for h in heads]
        e = _each(lambda x_: jnp.exp(x_ - jnp.max(x_, axis=-1, keepdims=True)), sc)
        inv = _each(lambda x_: 1.0 / jnp.sum(x_, axis=-1, keepdims=True), e)
        for h, ex, ix in zip(heads, e, inv):
            ybuf[b * c:(b + 1) * c, A_W + DN_V + h * LANE:A_W + DN_V + (h + 1) * LANE] = (
                _mm(ex, mv_ref[b, :, h * LANE:(h + 1) * LANE]) * ix).astype(BF16)
    while queue:
        tick(0)

    merged = None
    for j, (lo, hi) in enumerate(((0, A_W), (A_W, A_W + DN_V), (A_W + DN_V, A_W + DN_V + XA_W))):
        term = gbuf[:, j * D_MODEL:(j + 1) * D_MODEL] * jnp.dot(ybuf[:, lo:hi], wb_ref[lo:hi, :],
                                                                preferred_element_type=F32)
        merged = term if merged is None else merged + term
    x1 = x + jnp.dot(merged.astype(BF16), wo_ref[...], preferred_element_type=F32)
    x1_ref[...] = x1.reshape(nb, c, D_MODEL)


def _front_prompt(x, gmix, w1, w2, wab, mkb, mvb, caw, dcw, alog_row, dtb_row, dnorm, wb, wo, nb):
    bsz, length, d = x.shape
    c = CHUNK
    return pl.pallas_call(
        functools.partial(_front_prompt_kernel, nb=nb),
        grid=(bsz // nb, length // c),
        in_specs=[pl.BlockSpec((nb, c, d), lambda g, t: (g, t, 0)),
                  _resident((1, d)), _resident(w1.shape), _resident(w2.shape), _resident(wab.shape),
                  pl.BlockSpec((nb, MEM_TOKENS, XA_W), lambda g, t: (g, 0, 0), pipeline_mode=pl.Buffered(1)),
                  pl.BlockSpec((nb, MEM_TOKENS, XA_W), lambda g, t: (g, 0, 0), pipeline_mode=pl.Buffered(1)),
                  _resident((CONV_A_K, A_W)), _resident((DN_CONV_K, DN_CONV_CH)),
                  _resident((1, LANE)), _resident((1, LANE)), _resident((1, LANE)),
                  _resident(wb.shape), _resident(wo.shape)],
        out_specs=[pl.BlockSpec((nb, c, d), lambda g, t: (g, t, 0)),
                   pl.BlockSpec((nb, CONV_A_K - 1, A_W), lambda g, t: (g, 0, 0)),
                   pl.BlockSpec((nb, DN_CONV_K - 1, DN_CONV_CH), lambda g, t: (g, 0, 0)),
                   pl.BlockSpec((nb, DN_HEADS, DN_DK, DN_DV), lambda g, t: (g, 0, 0, 0))],
        out_shape=[jax.ShapeDtypeStruct((bsz, length, d), F32),
                   jax.ShapeDtypeStruct((bsz, CONV_A_K - 1, A_W), F32),
                   jax.ShapeDtypeStruct((bsz, DN_CONV_K - 1, DN_CONV_CH), F32),
                   jax.ShapeDtypeStruct((bsz, DN_HEADS, DN_DK, DN_DV), F32)],
        scratch_shapes=[pltpu.VMEM((nb, TAIL, A_W), F32), pltpu.VMEM((nb, TAIL, DN_CONV_CH), F32),
                        pltpu.VMEM((nb * c, A_W + DN_V + XA_W), BF16), pltpu.VMEM((nb * c, P_W), F32),
                        pltpu.VMEM((nb * c, 3 * d), F32)],
        compiler_params=_cparams(("arbitrary", "arbitrary")),
        name="front_prompt",
    )(x, gmix, w1, w2, wab, mkb, mvb, caw, dcw, alog_row, dtb_row, dnorm, wb, wo)


def _proj_kernel(x_ref, g_ref, w1_ref, w2_ref, wab_ref, p1_ref, p2_ref, pab_ref):
    xn = _rms(x_ref[...], g_ref[...]).astype(BF16)
    p1_ref[...] = jnp.dot(xn, w1_ref[...], preferred_element_type=F32)
    p2_ref[...] = jnp.dot(xn, w2_ref[...], preferred_element_type=F32)
    pab_ref[...] = jnp.dot(xn, wab_ref[...], preferred_element_type=F32)


def _proj(x2d, gain, w1, w2, wab, tm):
    t, d = x2d.shape
    row = lambda n: pl.BlockSpec((tm, n), lambda i: (i, 0))
    return pl.pallas_call(
        _proj_kernel,
        grid=(t // tm,),
        in_specs=[row(d), _resident((1, d)), _resident(w1.shape), _resident(w2.shape), _resident(wab.shape)],
        out_specs=[row(W1), row(W2), row(LANE)],
        out_shape=[jax.ShapeDtypeStruct((t, W1), F32), jax.ShapeDtypeStruct((t, W2), F32),
                   jax.ShapeDtypeStruct((t, LANE), F32)],
        compiler_params=_cparams(("arbitrary",)),
        name="proj",
    )(x2d, gain, w1, w2, wab)


SEQ_S = 4
NB_S = CHUNK // SEQ_S


def _seg_conv(x, e, wts, width, tmod):
    rows = x.shape[0]
    acc = None
    for i in range(width):
        s = width - 1 - i
        term = x if s == 0 else jnp.where(tmod >= s, pltpu.roll(x, s, 0), 0.0)
        if i < width - 1:
            hist = e if i == 0 else pltpu.roll(e, rows - i, 0)
            term = term + jnp.where(tmod < SEQ_S - i, hist, 0.0)
        term = wts[i:i + 1] * term
        acc = term if acc is None else acc + term
    return acc


def _branch_sample_kernel(pa_ref, pq_ref, pz_ref, pab_ref, ea_ref, eq_ref, s0_ref, caw_ref, dcw_ref,
                          alog_ref, dtb_ref, dnorm_ref, yad_ref, u_ref, s_ref):
    c = CHUNK
    m = _dn_masks(SEQ_S)
    tmod = lax.broadcasted_iota(jnp.int32, (c, 1), 0) & (SEQ_S - 1)

    pa = pa_ref[...]
    u_in = pa[:, A_W:2 * A_W] * pa[:, 2 * A_W:3 * A_W]
    u_ref[...] = u_in
    conv = _seg_conv(u_in, ea_ref[...], caw_ref[...], CONV_A_K, tmod)
    yad_ref[:, 0:A_W] = (pa[:, 0:A_W] * conv).astype(BF16)

    qkv = _silu(_seg_conv(pq_ref[...], eq_ref[...], dcw_ref[...], DN_CONV_K, tmod))
    d, dl, beta = _dn_gates(pab_ref[...], alog_ref[...], dtb_ref[...], m)
    dec_t = jnp.exp(dl).T
    dnorm = dnorm_ref[...]

    wide = NB_S * DN_DK
    er = lax.broadcasted_iota(jnp.int32, (c, wide), 0)
    ec = lax.broadcasted_iota(jnp.int32, (c, wide), 1)
    mexp = ((er >> 2) == (ec >> 7)).astype(F32)
    mexp2 = jnp.concatenate([mexp, mexp], axis=0)
    tr = lax.broadcasted_iota(jnp.int32, (wide, c), 0)
    tc = lax.broadcasted_iota(jnp.int32, (wide, c), 1)
    mexp_t = ((tr >> 7) == (tc >> 2)).astype(F32)

    q, k, v, z, d_col, d_row, dl_col, beta_col = _head_lists(qkv, pz_ref[...], d, dl, beta)
    q, k, u, w, qk = _dn_intra(q, k, v, d_col, d_row, beta_col, m, SEQ_S)
    heads = list(range(DN_HEADS))
    s_old = [s0_ref[:, h].reshape(wide, DN_DV) for h in heads]
    x_exp = _each(lambda wx, qx, dc: jnp.concatenate([jnp.concatenate([wx, qx * jnp.exp(dc)], axis=0)] * NB_S,
                                                     axis=1) * mexp2, w, q, d_col)
    ws = _each(_mm, x_exp, s_old)
    vn = _each(lambda ux, x: ux - x[:c], u, ws)
    o = _each(lambda x, y, vx: x[c:] + _mm(y, vx), ws, qk, vn)
    k_exp = _each(lambda kx, dlc, dc: jnp.concatenate([(kx * jnp.exp(dlc - dc)).T] * NB_S, axis=0) * mexp_t,
                  k, dl_col, d_col)
    dec = [jnp.concatenate([jnp.broadcast_to(dec_t[h:h + 1, SEQ_S * b:SEQ_S * b + 1], (DN_DK, DN_DV))
                            for b in range(NB_S)], axis=0) for h in heads]
    s_new = _each(lambda s, dx, kx, vx: s * dx + _mm(kx, vx), s_old, dec, k_exp, vn)
    for h, sx, ox, zx in zip(heads, s_new, o, z):
        s_ref[:, h] = sx.reshape(NB_S, DN_DK, DN_DV)
        yad_ref[:, A_W + h * LANE:A_W + (h + 1) * LANE] = _dn_out(ox, zx, dnorm).astype(BF16)


def _branch_sample(p1, pab, ea, eq, state, caw, dcw, alog_row, dtb_row, dnorm):
    t = p1.shape[0]
    c = CHUNK
    full = lambda shape: pl.BlockSpec(shape, lambda i: (0,) * len(shape))
    return pl.pallas_call(
        _branch_sample_kernel,
        grid=(t // c,),
        in_specs=[pl.BlockSpec((c, 3 * A_W), lambda i: (i, 0)),
                  pl.BlockSpec((c, DN_CONV_CH), lambda i: (i, OFF_QKV // DN_CONV_CH)),
                  pl.BlockSpec((c, DN_V), lambda i: (i, OFF_Z // DN_V)),
                  pl.BlockSpec((c, LANE), lambda i: (i, 0)),
                  pl.BlockSpec((c, A_W), lambda i: (i, 0)),
                  pl.BlockSpec((c, DN_CONV_CH), lambda i: (i, 0)),
                  pl.BlockSpec((NB_S, DN_HEADS, DN_DK, DN_DV), lambda i: (i, 0, 0, 0)),
                  full((CONV_A_K, A_W)), full((DN_CONV_K, DN_CONV_CH)),
                  full((1, LANE)), full((1, LANE)), full((1, LANE))],
        out_specs=[pl.BlockSpec((c, A_W + DN_V), lambda i: (i, 0)),
                   pl.BlockSpec((c, A_W), lambda i: (i, 0)),
                   pl.BlockSpec((NB_S, DN_HEADS, DN_DK, DN_DV), lambda i: (i, 0, 0, 0))],
        out_shape=[jax.ShapeDtypeStruct((t, A_W + DN_V), BF16),
                   jax.ShapeDtypeStruct((t, A_W), F32),
                   jax.ShapeDtypeStruct(state.shape, F32)],
        compiler_params=_cparams(("arbitrary",)),
        name="branch_sample",
    )(p1, p1, p1, pab, ea, eq, state, caw, dcw, alog_row, dtb_row, dnorm)


def _attn_sample_kernel(q_ref, k_ref, v_ref, o_ref):
    for h in range(XA_HEADS):
        sl = slice(h * LANE, (h + 1) * LANE)
        rows = pl.ds(h, MEM_TOKENS, stride=XA_HEADS)
        q = q_ref[:, :, sl].astype(BF16)
        s = jnp.einsum("bqd,bkd->bqk", q, k_ref[:, rows, :].astype(BF16),
                       preferred_element_type=F32) * (XA_DH ** -0.5)
        e = jnp.exp(s - jnp.max(s, axis=-1, keepdims=True))
        p = e / jnp.sum(e, axis=-1, keepdims=True)
        o_ref[:, :, sl] = jnp.einsum("bqk,bkd->bqd", p.astype(BF16), v_ref[:, rows, :].astype(BF16),
                                     preferred_element_type=F32).astype(BF16)


def _attn_sample(q3, ck, cv, nb):
    bsz, length, _ = q3.shape
    return pl.pallas_call(
        _attn_sample_kernel,
        grid=(bsz // nb,),
        in_specs=[pl.BlockSpec((nb, length, XA_W), lambda i: (i, 0, 0)),
                  pl.BlockSpec((nb, MEM_TOKENS * XA_HEADS, XA_DH), lambda i: (i, 0, 0)),
                  pl.BlockSpec((nb, MEM_TOKENS * XA_HEADS, XA_DH), lambda i: (i, 0, 0))],
        out_specs=pl.BlockSpec((nb, length, XA_W), lambda i: (i, 0, 0)),
        out_shape=jax.ShapeDtypeStruct((bsz, length, XA_W), BF16),
        compiler_params=_cparams(("arbitrary",)),
        name="attn_sample",
    )(q3, ck, cv)


def _merge_kernel(yad_ref, ym_ref, p2_ref, x_ref, wb_ref, wo_ref, o_ref):
    yad = yad_ref[...]
    gate = lambda j: jax.nn.sigmoid(p2_ref[:, OFF_G + j * D_MODEL:OFF_G + (j + 1) * D_MODEL])
    merged = (gate(0) * jnp.dot(yad[:, :A_W], wb_ref[0:A_W, :], preferred_element_type=F32)
              + gate(1) * jnp.dot(yad[:, A_W:], wb_ref[A_W:A_W + DN_V, :], preferred_element_type=F32)
              + gate(2) * jnp.dot(ym_ref[...], wb_ref[A_W + DN_V:, :], preferred_element_type=F32))
    o_ref[...] = x_ref[...] + jnp.dot(merged.astype(BF16), wo_ref[...], preferred_element_type=F32)


def _merge(yad, ym, p2, x2d, wb, wo, tm):
    t, d = x2d.shape
    row = lambda n: pl.BlockSpec((tm, n), lambda i: (i, 0))
    return pl.pallas_call(
        _merge_kernel,
        grid=(t // tm,),
        in_specs=[row(A_W + DN_V), row(XA_W), row(W2), row(d), _resident(wb.shape), _resident(wo.shape)],
        out_specs=row(d),
        out_shape=jax.ShapeDtypeStruct((t, d), F32),
        compiler_params=_cparams(("arbitrary",)),
        name="merge",
    )(yad, ym, p2, x2d, wb, wo)


MXU_K = 256
FF_EDGES = (0, 6 * MXU_K, D_FF)
FF_SPLIT = len(FF_EDGES) - 1
FF_SUB = 512
TM_FF = 2 * FF_SUB


def _ffn_kernel(x_ref, gf_ref, wu_ref, wd_ref, gl_ref, o_ref):
    n_sub = x_ref.shape[0] // FF_SUB

    def block(xn, acc, j):
        lo, hi = FF_EDGES[j], FF_EDGES[j + 1]
        gate = jnp.dot(xn, wu_ref[:, lo:hi], preferred_element_type=F32)
        up = jnp.dot(xn, wu_ref[:, D_FF + lo:D_FF + hi], preferred_element_type=F32)
        hid = (_silu(gate) * up).astype(BF16)
        return acc + jnp.dot(hid, wd_ref[lo:hi, :], preferred_element_type=F32)

    state = [None] * n_sub
    for i in range(n_sub + 1):
        if i < n_sub:
            x = x_ref[i * FF_SUB:(i + 1) * FF_SUB, :]
            xn = _rms(x, gf_ref[...]).astype(BF16)
            state[i] = (xn, block(xn, x, 0))
        if i > 0:
            xn, acc = state[i - 1]
            for j in range(1, FF_SPLIT):
                acc = block(xn, acc, j)
            o_ref[(i - 1) * FF_SUB:i * FF_SUB, :] = _rms(acc, gl_ref[...])


def _ffn(x2d, gf, wu, wd, gl, tm):
    t, d = x2d.shape
    return pl.pallas_call(
        _ffn_kernel,
        grid=(t // tm,),
        in_specs=[pl.BlockSpec((tm, d), lambda i: (i, 0)),
                  _resident((1, d)), _resident(wu.shape), _resident(wd.shape), _resident((1, d))],
        out_specs=pl.BlockSpec((tm, d), lambda i: (i, 0)),
        out_shape=jax.ShapeDtypeStruct((t, d), F32),
        compiler_params=_cparams(("arbitrary",)),
        name="ffn",
    )(x2d, gf, wu, wd, gl)


def _pad_lanes(v):
    return jnp.zeros((1, LANE), F32).at[0, :v.shape[0]].set(v.astype(F32))


def kernel(x_prompt, x_sample, mem_prompt, state_conv_a, state_dn_conv, state_dn, cache_mem_k, cache_mem_v,
           norm_mix, w_in, conv_a_w, dn_conv_w, dn_a_log, dn_dt_bias, dn_norm, norm_mem, w_mem_kv, w_branch,
           w_o, norm_ffn, w_ffn_up, w_ffn_down, norm_final):
    bp, lp, d = x_prompt.shape
    bs, ls, _ = x_sample.shape
    assert norm_mix.shape[0] == 1 and ls == SEQ_S and lp % CHUNK == 0 and (bs * ls) % CHUNK == 0

    w = w_in[0]
    w1 = w[:, :W1].astype(BF16)
    w2 = w[:, W1 + N_AB:].astype(BF16)
    wab = jnp.pad(w[:, W1:W1 + N_AB], ((0, 0), (0, LANE - N_AB))).astype(BF16)
    wb = w_branch[0].astype(BF16)
    wo = w_o[0].astype(BF16)
    wu = w_ffn_up[0].astype(BF16)
    wd = w_ffn_down[0].astype(BF16)
    wkv = w_mem_kv[0].astype(BF16)
    g_mix = norm_mix[0][None, :]
    g_ffn = norm_ffn[0][None, :]
    g_fin = norm_final[None, :]
    g_mem = norm_mem[0][None, :]
    caw = conv_a_w[0]
    dcw = dn_conv_w[0]
    alog_row = _pad_lanes(dn_a_log[0])
    dtb_row = _pad_lanes(dn_dt_bias[0])
    dnorm = dn_norm[0][None, :]

    tp = bp * lp
    mk, mv, mkb, mvb = _memkv(mem_prompt.reshape(bp * MEM_TOKENS, d), g_mem, wkv, TM)
    x1_p, ca_p, dc_p, s_p = _front_prompt(x_prompt, g_mix, w1, w2, wab, mkb.reshape(bp, MEM_TOKENS, XA_W),
                                          mvb.reshape(bp, MEM_TOKENS, XA_W), caw, dcw, alog_row, dtb_row, dnorm,
                                          wb, wo, NB_P)
    y_p = _ffn(x1_p.reshape(tp, d), g_ffn, wu, wd, g_fin, TM_FF).reshape(bp, lp, d)

    ts = bs * ls
    xs2 = x_sample.reshape(ts, d)
    p1_s, p2_s, pab_s = _proj(xs2, g_mix, w1, w2, wab, CHUNK)
    ea = jnp.pad(state_conv_a[0], ((0, 0), (0, ls - (CONV_A_K - 1)), (0, 0))).reshape(ts, A_W)
    eq = jnp.pad(state_dn_conv[0], ((0, 0), (0, ls - (DN_CONV_K - 1)), (0, 0))).reshape(ts, DN_CONV_CH)
    yad_s, u_s, s_s = _branch_sample(p1_s, pab_s, ea, eq, state_dn[0], caw, dcw, alog_row, dtb_row, dnorm)
    ym_s = _attn_sample(p2_s[:, :XA_W].reshape(bs, ls, XA_W),
                        cache_mem_k.reshape(bs, MEM_TOKENS * XA_HEADS, XA_DH),
                        cache_mem_v.reshape(bs, MEM_TOKENS * XA_HEADS, XA_DH), 8)
    x1_s = _merge(yad_s, ym_s.reshape(ts, XA_W), p2_s, xs2, wb, wo, ts)
    y_s = _ffn(x1_s, g_ffn, wu, wd, g_fin, ts).reshape(bs, ls, d)
    ca_s = u_s.reshape(bs, ls, A_W)[:, ls - (CONV_A_K - 1):]
    dc_s = p1_s.reshape(bs, ls, W1)[:, ls - (DN_CONV_K - 1):, OFF_QKV:OFF_Z]

    return (y_p, y_s, ca_p[None], dc_p[None], s_p[None],
            mk.reshape(1, bp, MEM_TOKENS, XA_HEADS, XA_DH), mv.reshape(1, bp, MEM_TOKENS, XA_HEADS, XA_DH),
            ca_s[None], dc_s[None], s_s[None])
```

```python
import functools

import jax
import jax.numpy as jnp
from jax import lax
from jax.experimental import pallas as pl
from jax.experimental.pallas import tpu as pltpu

F32 = jnp.float32
BF16 = jnp.bfloat16

D_MODEL = 1024
A_W = 512
CONV_A_K = 3
DN_HEADS = 4
DN_DK = 128
DN_DV = 128
DN_QK = DN_HEADS * DN_DK
DN_V = DN_HEADS * DN_DV
DN_CONV_CH = 2 * DN_QK + DN_V
DN_CONV_K = 4
MEM_TOKENS = 256
XA_HEADS = 4
XA_DH = 128
XA_W = XA_HEADS * XA_DH
D_FF = 2816
EPS = 1e-6

LANE = 128
CHUNK = 128
TAIL = 8
NB_P = 4
TM = 512
GATE_BLK = 256
PROJ_BLK = 512
ATTN_RESERVE = 4

W1 = 3 * A_W + DN_CONV_CH + DN_V
OFF_QKV = 3 * A_W
OFF_Z = OFF_QKV + DN_CONV_CH
W2 = XA_W + 3 * D_MODEL
OFF_G = XA_W
N_AB = 2 * DN_HEADS
P_Z = 3 * A_W
P_XQ = P_Z + DN_V
P_W = P_XQ + XA_W

VMEM_LIMIT = 60 * 1024 * 1024


def _cparams(sem):
    return pltpu.CompilerParams(dimension_semantics=sem, vmem_limit_bytes=VMEM_LIMIT)


def _resident(shape):
    return pl.BlockSpec(shape, lambda *_: (0,) * len(shape), pipeline_mode=pl.Buffered(1))


def _mm(a, b):
    return jnp.dot(a.astype(BF16), b.astype(BF16), preferred_element_type=F32)


def _mm_nt(a, b):
    return lax.dot_general(a.astype(BF16), b.astype(BF16), (((1,), (1,)), ((), ())),
                           preferred_element_type=F32)


def _mm_tn(a, b):
    return lax.dot_general(a.astype(BF16), b.astype(BF16), (((0,), (0,)), ((), ())),
                           preferred_element_type=F32)


def _split3(x):
    hi = x.astype(BF16)
    r1 = x - hi.astype(F32)
    mid = r1.astype(BF16)
    lo = (r1 - mid.astype(F32)).astype(BF16)
    return jnp.concatenate([hi, mid, lo], axis=1)


def _mm_exact01(m01, x):
    n = x.shape[1]
    r = jnp.dot(m01.astype(BF16), _split3(x), preferred_element_type=F32)
    return (r[:, :n] + r[:, n:2 * n]) + r[:, 2 * n:]


def _rms(x, g):
    return x * lax.rsqrt(jnp.mean(x * x, axis=-1, keepdims=True) + EPS) * g


def _silu(x):
    return x * jax.nn.sigmoid(x)


def _softplus(x):
    return jnp.maximum(x, 0.0) + jnp.log1p(jnp.exp(-jnp.abs(x)))


def _memkv_kernel(x_ref, g_ref, w_ref, k_ref, v_ref, kb_ref, vb_ref):
    xn = _rms(x_ref[...], g_ref[...]).astype(BF16)
    kv = jnp.dot(xn, w_ref[...], preferred_element_type=F32)
    k = kv[:, :XA_W]
    v = kv[:, XA_W:]
    tm = k.shape[0]
    for h in range(XA_HEADS):
        rows = pl.ds(h, tm, stride=XA_HEADS)
        k_ref[rows, :] = k[:, h * LANE:(h + 1) * LANE]
        v_ref[rows, :] = v[:, h * LANE:(h + 1) * LANE]
    kb_ref[...] = k.astype(BF16)
    vb_ref[...] = v.astype(BF16)


def _memkv(mem2d, gain, w, tm):
    t, d = mem2d.shape
    blk = pl.BlockSpec((tm, XA_W), lambda i: (i, 0))
    blk_rows = pl.BlockSpec((tm * XA_HEADS, XA_DH), lambda i: (i, 0))
    return pl.pallas_call(
        _memkv_kernel,
        grid=(t // tm,),
        in_specs=[pl.BlockSpec((tm, d), lambda i: (i, 0)),
                  pl.BlockSpec((1, d), lambda i: (0, 0)),
                  pl.BlockSpec((d, 2 * XA_W), lambda i: (0, 0))],
        out_specs=[blk_rows, blk_rows, blk, blk],
        out_shape=[jax.ShapeDtypeStruct((t * XA_HEADS, XA_DH), F32),
                   jax.ShapeDtypeStruct((t * XA_HEADS, XA_DH), F32),
                   jax.ShapeDtypeStruct((t, XA_W), BF16), jax.ShapeDtypeStruct((t, XA_W), BF16)],
        compiler_params=_cparams(("arbitrary",)),
        name="memkv",
    )(mem2d, gain, w)


def _log2(n):
    return n.bit_length() - 1


def _dn_masks(seg):
    r = lax.broadcasted_iota(jnp.int32, (CHUNK, CHUNK), 0)
    c = lax.broadcasted_iota(jnp.int32, (CHUNK, CHUNK), 1)
    ls = _log2(seg)
    same = (r >> ls) == (c >> ls)
    base = min(8, seg)
    lb = _log2(base)
    m = {
        "same": same.astype(F32),
        "causal": (same & (r >= c)).astype(F32),
        "strict": (same & (r > c)).astype(F32),
        "eye": (r == c).astype(F32),
        "diag": ((r >> lb) == (c >> lb)).astype(F32),
        "off": {},
        "base": base,
    }
    s = base
    while s < seg:
        l1, l2 = _log2(s), _log2(2 * s)
        m["off"][s] = (((r >> l2) == (c >> l2)) & ((r >> l1) != (c >> l1))).astype(F32)
        s *= 2
    return m


def _each(f, *lists):
    return [f(*args) for args in zip(*lists)]


def _low_rows(x, s):
    return jnp.concatenate([x[i + s:i + 2 * s] for i in range(0, x.shape[0], 2 * s)], axis=0)


def _merge_low(x, low, s):
    parts = []
    for j, i in enumerate(range(0, x.shape[0], 2 * s)):
        parts += [x[i:i + s], low[j * s:(j + 1) * s]]
    return jnp.concatenate(parts, axis=0)


def _spread_low(low, s):
    return _merge_low(jnp.zeros((2 * low.shape[0], low.shape[1]), low.dtype), low, s)


def _tri_inv(a_list, m, seg, tick):
    add = lambda x, y: x + y
    b = _each(lambda a: -(a * m["diag"]), a_list)
    p = _each(lambda x: m["eye"] + x, b)
    b2 = _each(_mm, b, b)
    tick()
    p = _each(add, p, _each(_mm, p, b2))
    tick()
    if m["base"] == 8:
        b4 = _each(_mm, b2, b2)
        tick()
        p = _each(add, p, _each(_mm, p, b4))
        tick()
    s = m["base"]
    while s < seg:
        low = lambda t, s=s: _low_rows(t, s)
        x = _each(_mm, _each(lambda a, s=s: low(a) * low(m["off"][s]), a_list), p)
        tick()
        r = _each(_mm, _each(low, p), _each(lambda y, s=s: _spread_low(y, s), x))
        p = _each(lambda t, y, s=s: _merge_low(t, low(t) - y, s), p, r)
        tick()
        s *= 2
    return p


def _dn_gates(ab, alog_row, dtb_row, m):
    g = -jnp.exp(alog_row) * _softplus(ab + dtb_row)
    beta = jax.nn.sigmoid(ab)
    d = _mm_exact01(m["causal"], g)
    dl = _mm_exact01(m["same"], g)
    return d, dl, beta


def _l2n(x):
    return x * lax.rsqrt(jnp.sum(x * x, axis=-1, keepdims=True) + EPS)


def _dn_intra(q, k, v, d_col, d_row, beta_col, m, seg, tick=lambda: None):
    q = _each(lambda x: _l2n(x) * (DN_DK ** -0.5), q)
    k = _each(_l2n, k)
    gamma = _each(lambda dc, dr: jnp.exp((dc - dr) * m["causal"]) * m["causal"], d_col, d_row)
    kk = _each(_mm_nt, k, k)
    a = _each(lambda bc, x, g: (bc * x) * g * m["strict"], beta_col, kk, gamma)
    t = _tri_inv(a, m, seg, tick)
    rhs = _each(lambda vv, kx, bc, dc: jnp.concatenate([vv * bc, kx * (bc * jnp.exp(dc))], axis=1),
                v, k, beta_col, d_col)
    sol = _each(_mm, t, rhs)
    u = [x[:, :DN_DV] for x in sol]
    w = [x[:, DN_DV:] for x in sol]
    qk = _each(lambda x, g: x * g, _each(_mm_nt, q, k), gamma)
    return q, k, u, w, qk


def _dn_out(o, z, dnorm):
    return _rms(o, dnorm) * _silu(z)


def _head_lists(qkv, z, d, dl, beta):
    d_t = d.T
    out = [[] for _ in range(8)]
    for h in range(DN_HEADS):
        vals = (qkv[:, h * LANE:(h + 1) * LANE],
                qkv[:, DN_QK + h * LANE:DN_QK + (h + 1) * LANE],
                qkv[:, 2 * DN_QK + h * LANE:2 * DN_QK + (h + 1) * LANE],
                None if z is None else z[:, h * LANE:(h + 1) * LANE],
                d[:, h:h + 1], d_t[h:h + 1, :], dl[:, h:h + 1], beta[:, DN_HEADS + h:DN_HEADS + h + 1])
        for lst, val in zip(out, vals):
            lst.append(val)
    return out


def _causal_conv(x, tail_ref, b, wts, width):
    c, w = x.shape
    tiles = jnp.concatenate([tail_ref[b][None], x.reshape(c // TAIL, TAIL, w)], axis=0)
    sub = lax.broadcasted_iota(jnp.int32, (1, TAIL, 1), 1)
    acc = None
    for i in range(width):
        s = width - 1 - i
        if s == 0:
            y = tiles[1:]
        else:
            r = pltpu.roll(tiles, s, 1)
            y = jnp.where(sub >= s, r[1:], r[:-1])
        term = wts[i:i + 1][None] * y
        acc = term if acc is None else acc + term
    tail_ref[b] = tiles[c // TAIL]
    return acc.reshape(c, w)


def _front_prompt_kernel(x_ref, gmix_ref, w1_ref, w2_ref, wab_ref, mk_ref, mv_ref, caw_ref, dcw_ref,
                         alog_ref, dtb_ref, dnorm_ref, wb_ref, wo_ref,
                         x1_ref, ca_ref, dc_ref, s_ref, utail, qtail, ybuf, pbuf, gbuf, *, nb):
    c = CHUNK
    rows = nb * c
    t_idx = pl.program_id(1)

    @pl.when(t_idx == 0)
    def _():
        utail[...] = jnp.zeros(utail.shape, F32)
        qtail[...] = jnp.zeros(qtail.shape, F32)
        s_ref[...] = jnp.zeros(s_ref.shape, F32)

    x = x_ref[...].reshape(rows, D_MODEL)
    xn = _rms(x, gmix_ref[...]).astype(BF16)
    proj = lambda w_ref, lo, hi: jnp.dot(xn, w_ref[:, lo:hi], preferred_element_type=F32)

    queue = []

    def enqueue(dst, w_ref, src, dst_lo, width, act=None):
        def run():
            r = proj(w_ref, src, src + width)
            dst[:, dst_lo:dst_lo + width] = r if act is None else act(r)
        queue.append(run)

    for lo in range(0, 3 * A_W, PROJ_BLK):
        enqueue(pbuf, w1_ref, lo, lo, PROJ_BLK)
    enqueue(pbuf, w1_ref, OFF_Z, P_Z, DN_V)
    enqueue(pbuf, w2_ref, 0, P_XQ, XA_W)
    for lo in range(0, 3 * D_MODEL, GATE_BLK):
        enqueue(gbuf, w2_ref, OFF_G + lo, lo, GATE_BLK, jax.nn.sigmoid)
    queue.reverse()

    def tick(keep=ATTN_RESERVE):
        if len(queue) > keep:
            queue.pop()()

    m = _dn_masks(c)
    caw = caw_ref[...]
    dcw = dcw_ref[...]
    dnorm = dnorm_ref[...]
    pq_all = proj(w1_ref, OFF_QKV, OFF_Z)
    pab_all = jnp.dot(xn, wab_ref[...], preferred_element_type=F32)

    qkv_l, gates_l = [], []
    for b in range(nb):
        rb = slice(b * c, (b + 1) * c)
        tick()
        qkv_in = pq_all[rb]
        dc_ref[b] = qkv_in[c - (DN_CONV_K - 1):]
        qkv_l.append(_silu(_causal_conv(qkv_in, qtail, b, dcw, DN_CONV_K)))
        gates_l.append(_dn_gates(pab_all[rb], alog_ref[...], dtb_ref[...], m))

    for b in range(nb):
        rb = slice(b * c, (b + 1) * c)
        tick()
        u_in = pbuf[rb, A_W:2 * A_W] * pbuf[rb, 2 * A_W:3 * A_W]
        ca_ref[b] = u_in[c - (CONV_A_K - 1):]
        ybuf[rb, 0:A_W] = (pbuf[rb, 0:A_W] * _causal_conv(u_in, utail, b, caw, CONV_A_K)).astype(BF16)

    lists = [[] for _ in range(8)]
    for b in range(nb):
        d, dl, beta = gates_l[b]
        for lst, val in zip(lists, _head_lists(qkv_l[b], None, d, dl, beta)):
            lst.extend(val)
    idx = [(b, h) for b in range(nb) for h in range(DN_HEADS)]
    q, k, v, _, d_col, d_row, dl_col, beta_col = lists
    q, k, u, w, qk = _dn_intra(q, k, v, d_col, d_row, beta_col, m, c, tick)
    s_old = [s_ref[b, h] for b, h in idx]
    vn = _each(lambda ux, wx, s: ux - _mm(wx, s), u, w, s_old)
    tick()
    o = _each(lambda qx, dc, y, s, vx: _mm(jnp.concatenate([qx * jnp.exp(dc), y], axis=1),
                                           jnp.concatenate([s, vx], axis=0)), q, d_col, qk, s_old, vn)
    tick()
    s_new = _each(lambda s, dlc, kx, dc, vx: s * jnp.exp(dlc[0:1, :]) + _mm_tn(kx * jnp.exp(dlc - dc), vx),
                  s_old, dl_col, k, d_col, vn)
    for (b, h), sx, ox in zip(idx, s_new, o):
        s_ref[b, h] = sx
        zx = pbuf[b * c:(b + 1) * c, P_Z + h * LANE:P_Z + (h + 1) * LANE]
        ybuf[b * c:(b + 1) * c, A_W + h * LANE:A_W + (h + 1) * LANE] = _dn_out(ox, zx, dnorm).astype(BF16)

    for b in range(nb):
        tick(0)
        heads = range(XA_HEADS)
        sc = [_mm_nt(pbuf[b * c:(b + 1) * c, P_XQ + h * LANE:P_XQ + (h + 1) * LANE],
                     mk_ref[b, :, h * LANE:(h + 1) * LANE]) * (XA_DH ** -0.5) for h in heads]
        e = _each(lambda x_: jnp.exp(x_ - jnp.max(x_, axis=-1, keepdims=True)), sc)
        inv = _each(lambda x_: 1.0 / jnp.sum(x_, axis=-1, keepdims=True), e)
        for h, ex, ix in zip(heads, e, inv):
            ybuf[b * c:(b + 1) * c, A_W + DN_V + h * LANE:A_W + DN_V + (h + 1) * LANE] = (
                _mm(ex, mv_ref[b, :, h * LANE:(h + 1) * LANE]) * ix).astype(BF16)
    while queue:
        tick(0)

    merged = None
    for j, (lo, hi) in enumerate(((0, A_W), (A_W, A_W + DN_V), (A_W + DN_V, A_W + DN_V + XA_W))):
        term = gbuf[:, j * D_MODEL:(j + 1) * D_MODEL] * jnp.dot(ybuf[:, lo:hi], wb_ref[lo:hi, :],
                                                                preferred_element_type=F32)
        merged = term if merged is None else merged + term
    x1 = x + jnp.dot(merged.astype(BF16), wo_ref[...], preferred_element_type=F32)
    x1_ref[...] = x1.reshape(nb, c, D_MODEL)


def _front_prompt(x, gmix, w1, w2, wab, mkb, mvb, caw, dcw, alog_row, dtb_row, dnorm, wb, wo, nb):
    bsz, length, d = x.shape
    c = CHUNK
    return pl.pallas_call(
        functools.partial(_front_prompt_kernel, nb=nb),
        grid=(bsz // nb, length // c),
        in_specs=[pl.BlockSpec((nb, c, d), lambda g, t: (g, t, 0)),
                  _resident((1, d)), _resident(w1.shape), _resident(w2.shape), _resident(wab.shape),
                  pl.BlockSpec((nb, MEM_TOKENS, XA_W), lambda g, t: (g, 0, 0), pipeline_mode=pl.Buffered(1)),
                  pl.BlockSpec((nb, MEM_TOKENS, XA_W), lambda g, t: (g, 0, 0), pipeline_mode=pl.Buffered(1)),
                  _resident((CONV_A_K, A_W)), _resident((DN_CONV_K, DN_CONV_CH)),
                  _resident((1, LANE)), _resident((1, LANE)), _resident((1, LANE)),
                  _resident(wb.shape), _resident(wo.shape)],
        out_specs=[pl.BlockSpec((nb, c, d), lambda g, t: (g, t, 0)),
                   pl.BlockSpec((nb, CONV_A_K - 1, A_W), lambda g, t: (g, 0, 0)),
                   pl.BlockSpec((nb, DN_CONV_K - 1, DN_CONV_CH), lambda g, t: (g, 0, 0)),
                   pl.BlockSpec((nb, DN_HEADS, DN_DK, DN_DV), lambda g, t: (g, 0, 0, 0))],
        out_shape=[jax.ShapeDtypeStruct((bsz, length, d), F32),
                   jax.ShapeDtypeStruct((bsz, CONV_A_K - 1, A_W), F32),
                   jax.ShapeDtypeStruct((bsz, DN_CONV_K - 1, DN_CONV_CH), F32),
                   jax.ShapeDtypeStruct((bsz, DN_HEADS, DN_DK, DN_DV), F32)],
        scratch_shapes=[pltpu.VMEM((nb, TAIL, A_W), F32), pltpu.VMEM((nb, TAIL, DN_CONV_CH), F32),
                        pltpu.VMEM((nb * c, A_W + DN_V + XA_W), BF16), pltpu.VMEM((nb * c, P_W), F32),
                        pltpu.VMEM((nb * c, 3 * d), F32)],
        compiler_params=_cparams(("arbitrary", "arbitrary")),
        name="front_prompt",
    )(x, gmix, w1, w2, wab, mkb, mvb, caw, dcw, alog_row, dtb_row, dnorm, wb, wo)


def _win_prep_kernel(a_ref, b_ref, nxt_ref, tail_ref, w1_ref, w2_ref, wab_ref):
    j = pl.program_id(0)
    b = b_ref[...]
    w1_ref[...] = a_ref[...].astype(BF16)
    nxt = jnp.where(j == pl.num_programs(0) - 1, tail_ref[...], nxt_ref[...])
    w2_ref[...] = jnp.concatenate([b, nxt], axis=1)[:, N_AB:N_AB + PROJ_BLK].astype(BF16)

    @pl.when(j == 0)
    def _():
        lane = lax.broadcasted_iota(jnp.int32, (b.shape[0], LANE), 1)
        wab_ref[...] = jnp.where(lane < N_AB, b[:, :LANE], 0.0).astype(BF16)


def _win_prep(w, tail):
    d = w.shape[0]
    n = W1 // PROJ_BLK
    per = PROJ_BLK // LANE
    last_tile = (W1 + W2) // LANE - 1
    return pl.pallas_call(
        _win_prep_kernel,
        grid=(n,),
        in_specs=[pl.BlockSpec((d, PROJ_BLK), lambda j: (0, j)),
                  pl.BlockSpec((d, PROJ_BLK), lambda j: (0, n + j)),
                  pl.BlockSpec((d, LANE), lambda j: (0, jnp.minimum((n + j + 1) * per, last_tile))),
                  _resident((d, LANE))],
        out_specs=[pl.BlockSpec((d, PROJ_BLK), lambda j: (0, j)),
                   pl.BlockSpec((d, PROJ_BLK), lambda j: (0, j)),
                   pl.BlockSpec((d, LANE), lambda j: (0, 0))],
        out_shape=[jax.ShapeDtypeStruct((d, W1), BF16), jax.ShapeDtypeStruct((d, W2), BF16),
                   jax.ShapeDtypeStruct((d, LANE), BF16)],
        compiler_params=_cparams(("arbitrary",)),
        name="win_prep",
    )(w, w, w, tail)


def _proj_kernel(x_ref, g_ref, w1_ref, w2_ref, wab_ref, p1_ref, p2_ref, pab_ref, xn_ref):
    @pl.when(pl.program_id(0) == 0)
    def _():
        xn_ref[...] = _rms(x_ref[...], g_ref[...]).astype(BF16)
        pab_ref[...] = jnp.dot(xn_ref[...], wab_ref[...], preferred_element_type=F32)

    xn = xn_ref[...]
    p1_ref[...] = jnp.dot(xn, w1_ref[...], preferred_element_type=F32)
    p2_ref[...] = jnp.dot(xn, w2_ref[...], preferred_element_type=F32)


def _proj(x2d, gain, w1, w2, wab):
    t, d = x2d.shape
    col = lambda: pl.BlockSpec((d, PROJ_BLK), lambda j: (0, j))
    out = lambda: pl.BlockSpec((t, PROJ_BLK), lambda j: (0, j))
    return pl.pallas_call(
        _proj_kernel,
        grid=(W1 // PROJ_BLK,),
        in_specs=[_resident((t, d)), _resident((1, d)), col(), col(), _resident(wab.shape)],
        out_specs=[out(), out(), pl.BlockSpec((t, LANE), lambda j: (0, 0))],
        out_shape=[jax.ShapeDtypeStruct((t, W1), F32), jax.ShapeDtypeStruct((t, W2), F32),
                   jax.ShapeDtypeStruct((t, LANE), F32)],
        scratch_shapes=[pltpu.VMEM((t, d), BF16)],
        compiler_params=_cparams(("arbitrary",)),
        name="proj",
    )(x2d, gain, w1, w2, wab)


SEQ_S = 4
NB_S = CHUNK // SEQ_S


def _seg_conv(x, e, wts, width, tmod):
    rows = x.shape[0]
    acc = None
    for i in range(width):
        s = width - 1 - i
        term = x if s == 0 else jnp.where(tmod >= s, pltpu.roll(x, s, 0), 0.0)
        if i < width - 1:
            hist = e if i == 0 else pltpu.roll(e, rows - i, 0)
            term = term + jnp.where(tmod < SEQ_S - i, hist, 0.0)
        term = wts[i:i + 1] * term
        acc = term if acc is None else acc + term
    return acc


def _branch_sample_kernel(pa_ref, pq_ref, pz_ref, pab_ref, ea_ref, eq_ref, s0_ref, caw_ref, dcw_ref,
                          alog_ref, dtb_ref, dnorm_ref, yad_ref, u_ref, s_ref):
    c = CHUNK
    m = _dn_masks(SEQ_S)
    tmod = lax.broadcasted_iota(jnp.int32, (c, 1), 0) & (SEQ_S - 1)

    pa = pa_ref[...]
    u_in = pa[:, A_W:2 * A_W] * pa[:, 2 * A_W:3 * A_W]
    u_ref[...] = u_in
    conv = _seg_conv(u_in, ea_ref[...], caw_ref[...], CONV_A_K, tmod)
    yad_ref[:, 0:A_W] = (pa[:, 0:A_W] * conv).astype(BF16)

    qkv = _silu(_seg_conv(pq_ref[...], eq_ref[...], dcw_ref[...], DN_CONV_K, tmod))
    d, dl, beta = _dn_gates(pab_ref[...], alog_ref[...], dtb_ref[...], m)
    dec_t = jnp.exp(dl).T
    dnorm = dnorm_ref[...]

    wide = NB_S * DN_DK
    er = lax.broadcasted_iota(jnp.int32, (c, wide), 0)
    ec = lax.broadcasted_iota(jnp.int32, (c, wide), 1)
    mexp = ((er >> 2) == (ec >> 7)).astype(F32)
    mexp2 = jnp.concatenate([mexp, mexp], axis=0)
    tr = lax.broadcasted_iota(jnp.int32, (wide, c), 0)
    tc = lax.broadcasted_iota(jnp.int32, (wide, c), 1)
    mexp_t = ((tr >> 7) == (tc >> 2)).astype(F32)

    q, k, v, z, d_col, d_row, dl_col, beta_col = _head_lists(qkv, pz_ref[...], d, dl, beta)
    q, k, u, w, qk = _dn_intra(q, k, v, d_col, d_row, beta_col, m, SEQ_S)
    heads = list(range(DN_HEADS))
    s_old = [s0_ref[:, h].reshape(wide, DN_DV) for h in heads]
    x_exp = _each(lambda wx, qx, dc: jnp.concatenate([jnp.concatenate([wx, qx * jnp.exp(dc)], axis=0)] * NB_S,
                                                     axis=1) * mexp2, w, q, d_col)
    ws = _each(_mm, x_exp, s_old)
    vn = _each(lambda ux, x: ux - x[:c], u, ws)
    o = _each(lambda x, y, vx: x[c:] + _mm(y, vx), ws, qk, vn)
    k_exp = _each(lambda kx, dlc, dc: jnp.concatenate([(kx * jnp.exp(dlc - dc)).T] * NB_S, axis=0) * mexp_t,
                  k, dl_col, d_col)
    dec = [jnp.concatenate([jnp.broadcast_to(dec_t[h:h + 1, SEQ_S * b:SEQ_S * b + 1], (DN_DK, DN_DV))
                            for b in range(NB_S)], axis=0) for h in heads]
    s_new = _each(lambda s, dx, kx, vx: s * dx + _mm(kx, vx), s_old, dec, k_exp, vn)
    for h, sx, ox, zx in zip(heads, s_new, o, z):
        s_ref[:, h] = sx.reshape(NB_S, DN_DK, DN_DV)
        yad_ref[:, A_W + h * LANE:A_W + (h + 1) * LANE] = _dn_out(ox, zx, dnorm).astype(BF16)


def _branch_sample(p1, pab, ea, eq, state, caw, dcw, alog_row, dtb_row, dnorm):
    t = p1.shape[0]
    c = CHUNK
    full = lambda shape: pl.BlockSpec(shape, lambda i: (0,) * len(shape))
    return pl.pallas_call(
        _branch_sample_kernel,
        grid=(t // c,),
        in_specs=[pl.BlockSpec((c, 3 * A_W), lambda i: (i, 0)),
                  pl.BlockSpec((c, DN_CONV_CH), lambda i: (i, OFF_QKV // DN_CONV_CH)),
                  pl.BlockSpec((c, DN_V), lambda i: (i, OFF_Z // DN_V)),
                  pl.BlockSpec((c, LANE), lambda i: (i, 0)),
                  pl.BlockSpec((c, A_W), lambda i: (i, 0)),
                  pl.BlockSpec((c, DN_CONV_CH), lambda i: (i, 0)),
                  pl.BlockSpec((NB_S, DN_HEADS, DN_DK, DN_DV), lambda i: (i, 0, 0, 0)),
                  full((CONV_A_K, A_W)), full((DN_CONV_K, DN_CONV_CH)),
                  full((1, LANE)), full((1, LANE)), full((1, LANE))],
        out_specs=[pl.BlockSpec((c, A_W + DN_V), lambda i: (i, 0)),
                   pl.BlockSpec((c, A_W), lambda i: (i, 0)),
                   pl.BlockSpec((NB_S, DN_HEADS, DN_DK, DN_DV), lambda i: (i, 0, 0, 0))],
        out_shape=[jax.ShapeDtypeStruct((t, A_W + DN_V), BF16),
                   jax.ShapeDtypeStruct((t, A_W), F32),
                   jax.ShapeDtypeStruct(state.shape, F32)],
        compiler_params=_cparams(("arbitrary",)),
        name="branch_sample",
    )(p1, p1, p1, pab, ea, eq, state, caw, dcw, alog_row, dtb_row, dnorm)


def _attn_sample_kernel(q_ref, k_ref, v_ref, o_ref):
    for h in range(XA_HEADS):
        sl = slice(h * LANE, (h + 1) * LANE)
        rows = pl.ds(h, MEM_TOKENS, stride=XA_HEADS)
        q = q_ref[:, :, sl].astype(BF16)
        s = jnp.einsum("bqd,bkd->bqk", q, k_ref[:, rows, :].astype(BF16),
                       preferred_element_type=F32) * (XA_DH ** -0.5)
        e = jnp.exp(s - jnp.max(s, axis=-1, keepdims=True))
        p = e / jnp.sum(e, axis=-1, keepdims=True)
        o_ref[:, :, sl] = jnp.einsum("bqk,bkd->bqd", p.astype(BF16), v_ref[:, rows, :].astype(BF16),
                                     preferred_element_type=F32).astype(BF16)


def _attn_sample(q3, ck, cv, nb):
    bsz, length, _ = q3.shape
    return pl.pallas_call(
        _attn_sample_kernel,
        grid=(bsz // nb,),
        in_specs=[pl.BlockSpec((nb, length, XA_W), lambda i: (i, 0, 0)),
                  pl.BlockSpec((nb, MEM_TOKENS * XA_HEADS, XA_DH), lambda i: (i, 0, 0)),
                  pl.BlockSpec((nb, MEM_TOKENS * XA_HEADS, XA_DH), lambda i: (i, 0, 0))],
        out_specs=pl.BlockSpec((nb, length, XA_W), lambda i: (i, 0, 0)),
        out_shape=jax.ShapeDtypeStruct((bsz, length, XA_W), BF16),
        compiler_params=_cparams(("arbitrary",)),
        name="attn_sample",
    )(q3, ck, cv)


def _merge_kernel(yad_ref, ym_ref, p2_ref, x_ref, wb_ref, wo_ref, o_ref):
    yad = yad_ref[...]
    gate = lambda j: jax.nn.sigmoid(p2_ref[:, OFF_G + j * D_MODEL:OFF_G + (j + 1) * D_MODEL])
    merged = (gate(0) * jnp.dot(yad[:, :A_W], wb_ref[0:A_W, :], preferred_element_type=F32)
              + gate(1) * jnp.dot(yad[:, A_W:], wb_ref[A_W:A_W + DN_V, :], preferred_element_type=F32)
              + gate(2) * jnp.dot(ym_ref[...], wb_ref[A_W + DN_V:, :], preferred_element_type=F32))
    o_ref[...] = x_ref[...] + jnp.dot(merged.astype(BF16), wo_ref[...], preferred_element_type=F32)


def _merge(yad, ym, p2, x2d, wb, wo, tm):
    t, d = x2d.shape
    row = lambda n: pl.BlockSpec((tm, n), lambda i: (i, 0))
    return pl.pallas_call(
        _merge_kernel,
        grid=(t // tm,),
        in_specs=[row(A_W + DN_V), row(XA_W), row(W2), row(d), _resident(wb.shape), _resident(wo.shape)],
        out_specs=row(d),
        out_shape=jax.ShapeDtypeStruct((t, d), F32),
        compiler_params=_cparams(("arbitrary",)),
        name="merge",
    )(yad, ym, p2, x2d, wb, wo)


MXU_K = 256
FF_EDGES = (0, 6 * MXU_K, D_FF)
FF_SPLIT = len(FF_EDGES) - 1
FF_SUB = 512
TM_FF = 2 * FF_SUB


def _ffn_kernel(xp_ref, xs_ref, gf_ref, wu_ref, wd_ref, gl_ref, yp_ref, ys_ref, *, n_p):
    def block(xn, acc, j):
        lo, hi = FF_EDGES[j], FF_EDGES[j + 1]
        gate = jnp.dot(xn, wu_ref[:, lo:hi], preferred_element_type=F32)
        up = jnp.dot(xn, wu_ref[:, D_FF + lo:D_FF + hi], preferred_element_type=F32)
        hid = (_silu(gate) * up).astype(BF16)
        return acc + jnp.dot(hid, wd_ref[lo:hi, :], preferred_element_type=F32)

    def run(x_ref, o_ref):
        n_sub = x_ref.shape[0] // FF_SUB
        state = [None] * n_sub
        for i in range(n_sub + 1):
            if i < n_sub:
                x = x_ref[i * FF_SUB:(i + 1) * FF_SUB, :]
                xn = _rms(x, gf_ref[...]).astype(BF16)
                state[i] = (xn, block(xn, x, 0))
            if i > 0:
                xn, acc = state[i - 1]
                for j in range(1, FF_SPLIT):
                    acc = block(xn, acc, j)
                o_ref[(i - 1) * FF_SUB:i * FF_SUB, :] = _rms(acc, gl_ref[...])

    step = pl.program_id(0)
    pl.when(step < n_p)(lambda: run(xp_ref, yp_ref))
    pl.when(step == n_p)(lambda: run(xs_ref, ys_ref))


def _ffn(xp, xs, gf, wu, wd, gl, tm):
    tp, d = xp.shape
    ts = xs.shape[0]
    n_p = tp // tm
    prompt = lambda: pl.BlockSpec((tm, d), lambda i: (jnp.minimum(i, n_p - 1), 0))
    return pl.pallas_call(
        functools.partial(_ffn_kernel, n_p=n_p),
        grid=(n_p + 1,),
        in_specs=[prompt(), _resident((ts, d)),
                  _resident((1, d)), _resident(wu.shape), _resident(wd.shape), _resident((1, d))],
        out_specs=[prompt(), pl.BlockSpec((ts, d), lambda i: (0, 0))],
        out_shape=[jax.ShapeDtypeStruct((tp, d), F32), jax.ShapeDtypeStruct((ts, d), F32)],
        compiler_params=_cparams(("arbitrary",)),
        name="ffn",
    )(xp, xs, gf, wu, wd, gl)


def _pad_lanes(v):
    return jnp.zeros((1, LANE), F32).at[0, :v.shape[0]].set(v.astype(F32))


def kernel(x_prompt, x_sample, mem_prompt, state_conv_a, state_dn_conv, state_dn, cache_mem_k, cache_mem_v,
           norm_mix, w_in, conv_a_w, dn_conv_w, dn_a_log, dn_dt_bias, dn_norm, norm_mem, w_mem_kv, w_branch,
           w_o, norm_ffn, w_ffn_up, w_ffn_down, norm_final):
    bp, lp, d = x_prompt.shape
    bs, ls, _ = x_sample.shape
    assert norm_mix.shape[0] == 1 and ls == SEQ_S and lp % CHUNK == 0 and (bs * ls) % CHUNK == 0
    assert w_in.shape[2] == W1 + N_AB + W2 and bs * ls == FF_SUB

    w = w_in[0]
    w_tail = jnp.pad(w[:, W1 + W2:], ((0, 0), (0, LANE - N_AB)))
    w1, w2, wab = _win_prep(w, w_tail)
    wb = w_branch[0].astype(BF16)
    wo = w_o[0].astype(BF16)
    wu = w_ffn_up[0].astype(BF16)
    wd = w_ffn_down[0].astype(BF16)
    wkv = w_mem_kv[0].astype(BF16)
    g_mix = norm_mix[0][None, :]
    g_ffn = norm_ffn[0][None, :]
    g_fin = norm_final[None, :]
    g_mem = norm_mem[0][None, :]
    caw = conv_a_w[0]
    dcw = dn_conv_w[0]
    alog_row = _pad_lanes(dn_a_log[0])
    dtb_row = _pad_lanes(dn_dt_bias[0])
    dnorm = dn_norm[0][None, :]

    tp = bp * lp
    mk, mv, mkb, mvb = _memkv(mem_prompt.reshape(bp * MEM_TOKENS, d), g_mem, wkv, TM)
    x1_p, ca_p, dc_p, s_p = _front_prompt(x_prompt, g_mix, w1, w2, wab, mkb.reshape(bp, MEM_TOKENS, XA_W),
                                          mvb.reshape(bp, MEM_TOKENS, XA_W), caw, dcw, alog_row, dtb_row, dnorm,
                                          wb, wo, NB_P)

    ts = bs * ls
    xs2 = x_sample.reshape(ts, d)
    p1_s, p2_s, pab_s = _proj(xs2, g_mix, w1, w2, wab)
    ea = jnp.pad(state_conv_a[0], ((0, 0), (0, ls - (CONV_A_K - 1)), (0, 0))).reshape(ts, A_W)
    eq = jnp.pad(state_dn_conv[0], ((0, 0), (0, ls - (DN_CONV_K - 1)), (0, 0))).reshape(ts, DN_CONV_CH)
    yad_s, u_s, s_s = _branch_sample(p1_s, pab_s, ea, eq, state_dn[0], caw, dcw, alog_row, dtb_row, dnorm)
    ym_s = _attn_sample(p2_s[:, :XA_W].reshape(bs, ls, XA_W),
                        cache_mem_k.reshape(bs, MEM_TOKENS * XA_HEADS, XA_DH),
                        cache_mem_v.reshape(bs, MEM_TOKENS * XA_HEADS, XA_DH), 8)
    x1_s = _merge(yad_s, ym_s.reshape(ts, XA_W), p2_s, xs2, wb, wo, ts)
    ca_s = u_s.reshape(bs, ls, A_W)[:, ls - (CONV_A_K - 1):]
    dc_s = p1_s[:, OFF_QKV:OFF_Z].reshape(bs, ls, DN_CONV_CH)[:, ls - (DN_CONV_K - 1):]

    y_p, y_s = _ffn(x1_p.reshape(tp, d), x1_s, g_ffn, wu, wd, g_fin, TM_FF)

    return (y_p.reshape(bp, lp, d), y_s.reshape(bs, ls, d), ca_p[None], dc_p[None], s_p[None],
            mk.reshape(1, bp, MEM_TOKENS, XA_HEADS, XA_DH), mv.reshape(1, bp, MEM_TOKENS, XA_HEADS, XA_DH),
            ca_s[None], dc_s[None], s_s[None])
```

```python
import functools

import jax
import jax.numpy as jnp
from jax import lax
from jax.experimental import pallas as pl
from jax.experimental.pallas import tpu as pltpu

F32 = jnp.float32
BF16 = jnp.bfloat16

D_MODEL = 1024
A_W = 512
CONV_A_K = 3
DN_HEADS = 4
DN_DK = 128
DN_DV = 128
DN_QK = DN_HEADS * DN_DK
DN_V = DN_HEADS * DN_DV
DN_CONV_CH = 2 * DN_QK + DN_V
DN_CONV_K = 4
MEM_TOKENS = 256
XA_HEADS = 4
XA_DH = 128
XA_W = XA_HEADS * XA_DH
D_FF = 2816
EPS = 1e-6

LANE = 128
CHUNK = 128
TAIL = 8
NB_P = 4
TM = 512
GATE_BLK = 256
PROJ_BLK = 512
ATTN_RESERVE = 4

W1 = 3 * A_W + DN_CONV_CH + DN_V
OFF_QKV = 3 * A_W
OFF_Z = OFF_QKV + DN_CONV_CH
W2 = XA_W + 3 * D_MODEL
OFF_G = XA_W
N_AB = 2 * DN_HEADS
P_Z = 3 * A_W
P_XQ = P_Z + DN_V
P_W = P_XQ + XA_W

VMEM_LIMIT = 60 * 1024 * 1024


def _cparams(sem):
    return pltpu.CompilerParams(dimension_semantics=sem, vmem_limit_bytes=VMEM_LIMIT)


def _resident(shape):
    return pl.BlockSpec(shape, lambda *_: (0,) * len(shape), pipeline_mode=pl.Buffered(1))


def _mm(a, b):
    return jnp.dot(a.astype(BF16), b.astype(BF16), preferred_element_type=F32)


def _mm_nt(a, b):
    return lax.dot_general(a.astype(BF16), b.astype(BF16), (((1,), (1,)), ((), ())),
                           preferred_element_type=F32)


def _mm_tn(a, b):
    return lax.dot_general(a.astype(BF16), b.astype(BF16), (((0,), (0,)), ((), ())),
                           preferred_element_type=F32)


def _split3(x):
    hi = x.astype(BF16)
    r1 = x - hi.astype(F32)
    mid = r1.astype(BF16)
    lo = (r1 - mid.astype(F32)).astype(BF16)
    return jnp.concatenate([hi, mid, lo], axis=1)


def _mm_exact01(m01, x):
    n = x.shape[1]
    r = jnp.dot(m01.astype(BF16), _split3(x), preferred_element_type=F32)
    return (r[:, :n] + r[:, n:2 * n]) + r[:, 2 * n:]


def _rms(x, g):
    return x * lax.rsqrt(jnp.mean(x * x, axis=-1, keepdims=True) + EPS) * g


def _silu(x):
    return x * jax.nn.sigmoid(x)


def _softplus(x):
    return jnp.maximum(x, 0.0) + jnp.log1p(jnp.exp(-jnp.abs(x)))


def _memkv_kernel(x_ref, g_ref, w_ref, k_ref, v_ref, kb_ref, vb_ref):
    xn = _rms(x_ref[...], g_ref[...]).astype(BF16)
    kv = jnp.dot(xn, w_ref[...], preferred_element_type=F32)
    k = kv[:, :XA_W]
    v = kv[:, XA_W:]
    tm = k.shape[0]
    for h in range(XA_HEADS):
        rows = pl.ds(h, tm, stride=XA_HEADS)
        k_ref[rows, :] = k[:, h * LANE:(h + 1) * LANE]
        v_ref[rows, :] = v[:, h * LANE:(h + 1) * LANE]
    kb_ref[...] = k.astype(BF16)
    vb_ref[...] = v.astype(BF16)


def _memkv(mem2d, gain, w, tm):
    t, d = mem2d.shape
    blk = pl.BlockSpec((tm, XA_W), lambda i: (i, 0))
    blk_rows = pl.BlockSpec((tm * XA_HEADS, XA_DH), lambda i: (i, 0))
    return pl.pallas_call(
        _memkv_kernel,
        grid=(t // tm,),
        in_specs=[pl.BlockSpec((tm, d), lambda i: (i, 0)),
                  pl.BlockSpec((1, d), lambda i: (0, 0)),
                  pl.BlockSpec((d, 2 * XA_W), lambda i: (0, 0))],
        out_specs=[blk_rows, blk_rows, blk, blk],
        out_shape=[jax.ShapeDtypeStruct((t * XA_HEADS, XA_DH), F32),
                   jax.ShapeDtypeStruct((t * XA_HEADS, XA_DH), F32),
                   jax.ShapeDtypeStruct((t, XA_W), BF16), jax.ShapeDtypeStruct((t, XA_W), BF16)],
        compiler_params=_cparams(("arbitrary",)),
        name="memkv",
    )(mem2d, gain, w)


def _log2(n):
    return n.bit_length() - 1


def _dn_masks(seg):
    r = lax.broadcasted_iota(jnp.int32, (CHUNK, CHUNK), 0)
    c = lax.broadcasted_iota(jnp.int32, (CHUNK, CHUNK), 1)
    ls = _log2(seg)
    same = (r >> ls) == (c >> ls)
    base = min(8, seg)
    lb = _log2(base)
    m = {
        "same": same.astype(F32),
        "causal": (same & (r >= c)).astype(F32),
        "strict": (same & (r > c)).astype(F32),
        "eye": (r == c).astype(F32),
        "diag": ((r >> lb) == (c >> lb)).astype(F32),
        "off": {},
        "base": base,
        "whole": seg == CHUNK,
    }
    s = base
    while s < seg:
        l1, l2 = _log2(s), _log2(2 * s)
        m["off"][s] = (((r >> l2) == (c >> l2)) & ((r >> l1) != (c >> l1))).astype(F32)
        s *= 2
    return m


def _each(f, *lists):
    return [f(*args) for args in zip(*lists)]


def _low_rows(x, s):
    return jnp.concatenate([x[i + s:i + 2 * s] for i in range(0, x.shape[0], 2 * s)], axis=0)


def _merge_low(x, low, s):
    parts = []
    for j, i in enumerate(range(0, x.shape[0], 2 * s)):
        parts += [x[i:i + s], low[j * s:(j + 1) * s]]
    return jnp.concatenate(parts, axis=0)


def _spread_low(low, s):
    return _merge_low(jnp.zeros((2 * low.shape[0], low.shape[1]), low.dtype), low, s)


def _tri_inv(a_list, m, seg, tick):
    add = lambda x, y: x + y
    b = _each(lambda a: -(a * m["diag"]), a_list)
    p = _each(lambda x: m["eye"] + x, b)
    b2 = _each(_mm, b, b)
    tick()
    p = _each(add, p, _each(_mm, p, b2))
    tick()
    if m["base"] == 8:
        b4 = _each(_mm, b2, b2)
        tick()
        p = _each(add, p, _each(_mm, p, b4))
        tick()
    s = m["base"]
    while s < seg:
        low = lambda t, s=s: _low_rows(t, s)
        x = _each(_mm, _each(lambda a, s=s: low(a) * low(m["off"][s]), a_list), p)
        tick()
        r = _each(_mm, _each(low, p), _each(lambda y, s=s: _spread_low(y, s), x))
        p = _each(lambda t, y, s=s: _merge_low(t, low(t) - y, s), p, r)
        tick()
        s *= 2
    return p


def _dn_gates(ab, alog_row, dtb_row, m):
    g = -jnp.exp(alog_row) * _softplus(ab + dtb_row)
    beta = jax.nn.sigmoid(ab)
    d = _mm_exact01(m["causal"], g)
    if m["whole"]:
        dl = jnp.broadcast_to(d[CHUNK - 1:CHUNK, :], d.shape)
    else:
        dl = _mm_exact01(m["same"], g)
    return d, dl, beta


def _l2n(x):
    return x * lax.rsqrt(jnp.sum(x * x, axis=-1, keepdims=True) + EPS)


def _dn_intra(q, k, v, d_col, d_row, beta_col, m, seg, tick=lambda: None):
    q = _each(lambda x: _l2n(x) * (DN_DK ** -0.5), q)
    k = _each(_l2n, k)
    gamma = _each(lambda dc, dr: jnp.exp((dc - dr) * m["causal"]) * m["causal"], d_col, d_row)
    kk = _each(_mm_nt, k, k)
    a = _each(lambda bc, x, g: (bc * x) * g * m["strict"], beta_col, kk, gamma)
    t = _tri_inv(a, m, seg, tick)
    rhs = _each(lambda vv, kx, bc, dc: jnp.concatenate([vv * bc, kx * (bc * jnp.exp(dc))], axis=1),
                v, k, beta_col, d_col)
    sol = _each(_mm, t, rhs)
    u = [x[:, :DN_DV] for x in sol]
    w = [x[:, DN_DV:] for x in sol]
    qk = _each(lambda x, g: x * g, _each(_mm_nt, q, k), gamma)
    return q, k, u, w, qk


def _dn_out(o, z, dnorm):
    return _rms(o, dnorm) * _silu(z)


def _head_lists(qkv, z, d, dl, beta):
    d_t = d.T
    out = [[] for _ in range(8)]
    for h in range(DN_HEADS):
        vals = (qkv[:, h * LANE:(h + 1) * LANE],
                qkv[:, DN_QK + h * LANE:DN_QK + (h + 1) * LANE],
                qkv[:, 2 * DN_QK + h * LANE:2 * DN_QK + (h + 1) * LANE],
                None if z is None else z[:, h * LANE:(h + 1) * LANE],
                d[:, h:h + 1], d_t[h:h + 1, :], dl[:, h:h + 1], beta[:, DN_HEADS + h:DN_HEADS + h + 1])
        for lst, val in zip(out, vals):
            lst.append(val)
    return out


def _causal_conv(x, tail_ref, b, wts, width):
    c, w = x.shape
    tiles = jnp.concatenate([tail_ref[b][None], x.reshape(c // TAIL, TAIL, w)], axis=0)
    sub = lax.broadcasted_iota(jnp.int32, (1, TAIL, 1), 1)
    acc = None
    for i in range(width):
        s = width - 1 - i
        if s == 0:
            y = tiles[1:]
        else:
            r = pltpu.roll(tiles, s, 1)
            y = jnp.where(sub >= s, r[1:], r[:-1])
        term = wts[i:i + 1][None] * y
        acc = term if acc is None else acc + term
    tail_ref[b] = tiles[c // TAIL]
    return acc.reshape(c, w)


def _front_prompt_kernel(x_ref, gmix_ref, w1_ref, w2_ref, wab_ref, mk_ref, mv_ref, caw_ref, dcw_ref,
                         alog_ref, dtb_ref, dnorm_ref, wb_ref, wo_ref,
                         x1_ref, ca_ref, dc_ref, s_ref, utail, qtail, ybuf, pbuf, gbuf, *, nb):
    c = CHUNK
    rows = nb * c
    t_idx = pl.program_id(1)

    @pl.when(t_idx == 0)
    def _():
        utail[...] = jnp.zeros(utail.shape, F32)
        qtail[...] = jnp.zeros(qtail.shape, F32)
        s_ref[...] = jnp.zeros(s_ref.shape, F32)

    x = x_ref[...].reshape(rows, D_MODEL)
    xn = _rms(x, gmix_ref[...]).astype(BF16)
    proj = lambda w_ref, lo, hi: jnp.dot(xn, w_ref[:, lo:hi], preferred_element_type=F32)

    queue = []

    def enqueue(dst, w_ref, src, dst_lo, width, act=None):
        def run():
            r = proj(w_ref, src, src + width)
            dst[:, dst_lo:dst_lo + width] = r if act is None else act(r)
        queue.append(run)

    for lo in range(0, 3 * A_W, PROJ_BLK):
        enqueue(pbuf, w1_ref, lo, lo, PROJ_BLK)
    enqueue(pbuf, w1_ref, OFF_Z, P_Z, DN_V)
    enqueue(pbuf, w2_ref, 0, P_XQ, XA_W)
    for lo in range(0, 3 * D_MODEL, GATE_BLK):
        enqueue(gbuf, w2_ref, OFF_G + lo, lo, GATE_BLK, jax.nn.sigmoid)
    queue.reverse()

    def tick(keep=ATTN_RESERVE):
        if len(queue) > keep:
            queue.pop()()

    m = _dn_masks(c)
    caw = caw_ref[...]
    dcw = dcw_ref[...]
    dnorm = dnorm_ref[...]
    pq_all = proj(w1_ref, OFF_QKV, OFF_Z)
    pab_all = jnp.dot(xn, wab_ref[...], preferred_element_type=F32)

    qkv_l, gates_l = [], []
    for b in range(nb):
        rb = slice(b * c, (b + 1) * c)
        tick()
        qkv_in = pq_all[rb]
        dc_ref[b] = qkv_in[c - (DN_CONV_K - 1):]
        qkv_l.append(_silu(_causal_conv(qkv_in, qtail, b, dcw, DN_CONV_K)))
        gates_l.append(_dn_gates(pab_all[rb], alog_ref[...], dtb_ref[...], m))

    for b in range(nb):
        rb = slice(b * c, (b + 1) * c)
        tick()
        u_in = pbuf[rb, A_W:2 * A_W] * pbuf[rb, 2 * A_W:3 * A_W]
        ca_ref[b] = u_in[c - (CONV_A_K - 1):]
        ybuf[rb, 0:A_W] = (pbuf[rb, 0:A_W] * _causal_conv(u_in, utail, b, caw, CONV_A_K)).astype(BF16)

    lists = [[] for _ in range(8)]
    for b in range(nb):
        d, dl, beta = gates_l[b]
        for lst, val in zip(lists, _head_lists(qkv_l[b], None, d, dl, beta)):
            lst.extend(val)
    idx = [(b, h) for b in range(nb) for h in range(DN_HEADS)]
    q, k, v, _, d_col, d_row, dl_col, beta_col = lists
    q, k, u, w, qk = _dn_intra(q, k, v, d_col, d_row, beta_col, m, c, tick)
    s_old = [s_ref[b, h] for b, h in idx]
    vn = _each(lambda ux, wx, s: ux - _mm(wx, s), u, w, s_old)
    tick()
    o = _each(lambda qx, dc, y, s, vx: _mm(jnp.concatenate([qx * jnp.exp(dc), y], axis=1),
                                           jnp.concatenate([s, vx], axis=0)), q, d_col, qk, s_old, vn)
    tick()
    s_new = _each(lambda s, dlc, kx, dc, vx: s * jnp.exp(dlc[0:1, :]) + _mm_tn(kx * jnp.exp(dlc - dc), vx),
                  s_old, dl_col, k, d_col, vn)
    for (b, h), sx, ox in zip(idx, s_new, o):
        s_ref[b, h] = sx
        zx = pbuf[b * c:(b + 1) * c, P_Z + h * LANE:P_Z + (h + 1) * LANE]
        ybuf[b * c:(b + 1) * c, A_W + h * LANE:A_W + (h + 1) * LANE] = _dn_out(ox, zx, dnorm).astype(BF16)

    for b in range(nb):
        tick(0)
        heads = range(XA_HEADS)
        sc = [_mm_nt(pbuf[b * c:(b + 1) * c, P_XQ + h * LANE:P_XQ + (h + 1) * LANE],
                     mk_ref[b, :, h * LANE:(h + 1) * LANE]) * (XA_DH ** -0.5) for h in heads]
        e = _each(lambda x_: jnp.exp(x_ - jnp.max(x_, axis=-1, keepdims=True)), sc)
        inv = _each(lambda x_: 1.0 / jnp.sum(x_, axis=-1, keepdims=True), e)
        for h, ex, ix in zip(heads, e, inv):
            ybuf[b * c:(b + 1) * c, A_W + DN_V + h * LANE:A_W + DN_V + (h + 1) * LANE] = (
                _mm(ex, mv_ref[b, :, h * LANE:(h + 1) * LANE]) * ix).astype(BF16)
    while queue:
        tick(0)

    merged = None
    for j, (lo, hi) in enumerate(((0, A_W), (A_W, A_W + DN_V), (A_W + DN_V, A_W + DN_V + XA_W))):
        term = gbuf[:, j * D_MODEL:(j + 1) * D_MODEL] * jnp.dot(ybuf[:, lo:hi], wb_ref[lo:hi, :],
                                                                preferred_element_type=F32)
        merged = term if merged is None else merged + term
    x1 = x + jnp.dot(merged.astype(BF16), wo_ref[...], preferred_element_type=F32)
    x1_ref[...] = x1.reshape(nb, c, D_MODEL)


def _front_prompt(x, gmix, w1, w2, wab, mkb, mvb, caw, dcw, alog_row, dtb_row, dnorm, wb, wo, nb):
    bsz, length, d = x.shape
    c = CHUNK
    return pl.pallas_call(
        functools.partial(_front_prompt_kernel, nb=nb),
        grid=(bsz // nb, length // c),
        in_specs=[pl.BlockSpec((nb, c, d), lambda g, t: (g, t, 0)),
                  _resident((1, d)), _resident(w1.shape), _resident(w2.shape), _resident(wab.shape),
                  pl.BlockSpec((nb, MEM_TOKENS, XA_W), lambda g, t: (g, 0, 0), pipeline_mode=pl.Buffered(1)),
                  pl.BlockSpec((nb, MEM_TOKENS, XA_W), lambda g, t: (g, 0, 0), pipeline_mode=pl.Buffered(1)),
                  _resident((CONV_A_K, A_W)), _resident((DN_CONV_K, DN_CONV_CH)),
                  _resident((1, LANE)), _resident((1, LANE)), _resident((1, LANE)),
                  _resident(wb.shape), _resident(wo.shape)],
        out_specs=[pl.BlockSpec((nb, c, d), lambda g, t: (g, t, 0)),
                   pl.BlockSpec((nb, CONV_A_K - 1, A_W), lambda g, t: (g, 0, 0)),
                   pl.BlockSpec((nb, DN_CONV_K - 1, DN_CONV_CH), lambda g, t: (g, 0, 0)),
                   pl.BlockSpec((nb, DN_HEADS, DN_DK, DN_DV), lambda g, t: (g, 0, 0, 0))],
        out_shape=[jax.ShapeDtypeStruct((bsz, length, d), F32),
                   jax.ShapeDtypeStruct((bsz, CONV_A_K - 1, A_W), F32),
                   jax.ShapeDtypeStruct((bsz, DN_CONV_K - 1, DN_CONV_CH), F32),
                   jax.ShapeDtypeStruct((bsz, DN_HEADS, DN_DK, DN_DV), F32)],
        scratch_shapes=[pltpu.VMEM((nb, TAIL, A_W), F32), pltpu.VMEM((nb, TAIL, DN_CONV_CH), F32),
                        pltpu.VMEM((nb * c, A_W + DN_V + XA_W), BF16), pltpu.VMEM((nb * c, P_W), F32),
                        pltpu.VMEM((nb * c, 3 * d), F32)],
        compiler_params=_cparams(("arbitrary", "arbitrary")),
        name="front_prompt",
    )(x, gmix, w1, w2, wab, mkb, mvb, caw, dcw, alog_row, dtb_row, dnorm, wb, wo)


def _proj_kernel(x_ref, g_ref, w1_ref, w2_ref, wab_ref, p1_ref, p2_ref, pab_ref):
    xn = _rms(x_ref[...], g_ref[...]).astype(BF16)
    p1_ref[...] = jnp.dot(xn, w1_ref[...], preferred_element_type=F32)
    p2_ref[...] = jnp.dot(xn, w2_ref[...], preferred_element_type=F32)
    pab_ref[...] = jnp.dot(xn, wab_ref[...], preferred_element_type=F32)


def _proj(x2d, gain, w1, w2, wab, tm):
    t, d = x2d.shape
    row = lambda n: pl.BlockSpec((tm, n), lambda i: (i, 0))
    return pl.pallas_call(
        _proj_kernel,
        grid=(t // tm,),
        in_specs=[row(d), _resident((1, d)), _resident(w1.shape), _resident(w2.shape), _resident(wab.shape)],
        out_specs=[row(W1), row(W2), row(LANE)],
        out_shape=[jax.ShapeDtypeStruct((t, W1), F32), jax.ShapeDtypeStruct((t, W2), F32),
                   jax.ShapeDtypeStruct((t, LANE), F32)],
        compiler_params=_cparams(("arbitrary",)),
        name="proj",
    )(x2d, gain, w1, w2, wab)


SEQ_S = 4
NB_S = CHUNK // SEQ_S
NB_ATTN_S = 16


def _seg_conv(x, e, wts, width, tmod):
    rows = x.shape[0]
    acc = None
    for i in range(width):
        s = width - 1 - i
        term = x if s == 0 else jnp.where(tmod >= s, pltpu.roll(x, s, 0), 0.0)
        if i < width - 1:
            hist = e if i == 0 else pltpu.roll(e, rows - i, 0)
            term = term + jnp.where(tmod < SEQ_S - i, hist, 0.0)
        term = wts[i:i + 1] * term
        acc = term if acc is None else acc + term
    return acc


def _branch_sample_kernel(pa_ref, pq_ref, pz_ref, pab_ref, ea_ref, eq_ref, s0_ref, caw_ref, dcw_ref,
                          alog_ref, dtb_ref, dnorm_ref, yad_ref, u_ref, s_ref):
    c = CHUNK
    m = _dn_masks(SEQ_S)
    tmod = lax.broadcasted_iota(jnp.int32, (c, 1), 0) & (SEQ_S - 1)

    pa = pa_ref[...]
    u_in = pa[:, A_W:2 * A_W] * pa[:, 2 * A_W:3 * A_W]
    u_ref[...] = u_in
    conv = _seg_conv(u_in, ea_ref[...], caw_ref[...], CONV_A_K, tmod)
    yad_ref[:, 0:A_W] = (pa[:, 0:A_W] * conv).astype(BF16)

    qkv = _silu(_seg_conv(pq_ref[...], eq_ref[...], dcw_ref[...], DN_CONV_K, tmod))
    d, dl, beta = _dn_gates(pab_ref[...], alog_ref[...], dtb_ref[...], m)
    dec_t = jnp.exp(dl).T
    dnorm = dnorm_ref[...]

    wide = NB_S * DN_DK
    er = lax.broadcasted_iota(jnp.int32, (c, wide), 0)
    ec = lax.broadcasted_iota(jnp.int32, (c, wide), 1)
    mexp = ((er >> 2) == (ec >> 7)).astype(F32)
    mexp2 = jnp.concatenate([mexp, mexp], axis=0)
    tr = lax.broadcasted_iota(jnp.int32, (wide, c), 0)
    tc = lax.broadcasted_iota(jnp.int32, (wide, c), 1)
    mexp_t = ((tr >> 7) == (tc >> 2)).astype(F32)

    q, k, v, z, d_col, d_row, dl_col, beta_col = _head_lists(qkv, pz_ref[...], d, dl, beta)
    q, k, u, w, qk = _dn_intra(q, k, v, d_col, d_row, beta_col, m, SEQ_S)
    heads = list(range(DN_HEADS))
    s_old = [s0_ref[:, h].reshape(wide, DN_DV) for h in heads]
    x_exp = _each(lambda wx, qx, dc: jnp.concatenate([jnp.concatenate([wx, qx * jnp.exp(dc)], axis=0)] * NB_S,
                                                     axis=1) * mexp2, w, q, d_col)
    ws = _each(_mm, x_exp, s_old)
    vn = _each(lambda ux, x: ux - x[:c], u, ws)
    o = _each(lambda x, y, vx: x[c:] + _mm(y, vx), ws, qk, vn)
    k_exp = _each(lambda kx, dlc, dc: jnp.concatenate([(kx * jnp.exp(dlc - dc)).T] * NB_S, axis=0) * mexp_t,
                  k, dl_col, d_col)
    dec = [jnp.concatenate([jnp.broadcast_to(dec_t[h:h + 1, SEQ_S * b:SEQ_S * b + 1], (DN_DK, DN_DV))
                            for b in range(NB_S)], axis=0) for h in heads]
    s_new = _each(lambda s, dx, kx, vx: s * dx + _mm(kx, vx), s_old, dec, k_exp, vn)
    for h, sx, ox, zx in zip(heads, s_new, o, z):
        s_ref[:, h] = sx.reshape(NB_S, DN_DK, DN_DV)
        yad_ref[:, A_W + h * LANE:A_W + (h + 1) * LANE] = _dn_out(ox, zx, dnorm).astype(BF16)


def _branch_sample(p1, pab, ea, eq, state, caw, dcw, alog_row, dtb_row, dnorm):
    t = p1.shape[0]
    c = CHUNK
    full = lambda shape: pl.BlockSpec(shape, lambda i: (0,) * len(shape))
    return pl.pallas_call(
        _branch_sample_kernel,
        grid=(t // c,),
        in_specs=[pl.BlockSpec((c, 3 * A_W), lambda i: (i, 0)),
                  pl.BlockSpec((c, DN_CONV_CH), lambda i: (i, OFF_QKV // DN_CONV_CH)),
                  pl.BlockSpec((c, DN_V), lambda i: (i, OFF_Z // DN_V)),
                  pl.BlockSpec((c, LANE), lambda i: (i, 0)),
                  pl.BlockSpec((c, A_W), lambda i: (i, 0)),
                  pl.BlockSpec((c, DN_CONV_CH), lambda i: (i, 0)),
                  pl.BlockSpec((NB_S, DN_HEADS, DN_DK, DN_DV), lambda i: (i, 0, 0, 0)),
                  full((CONV_A_K, A_W)), full((DN_CONV_K, DN_CONV_CH)),
                  full((1, LANE)), full((1, LANE)), full((1, LANE))],
        out_specs=[pl.BlockSpec((c, A_W + DN_V), lambda i: (i, 0)),
                   pl.BlockSpec((c, A_W), lambda i: (i, 0)),
                   pl.BlockSpec((NB_S, DN_HEADS, DN_DK, DN_DV), lambda i: (i, 0, 0, 0))],
        out_shape=[jax.ShapeDtypeStruct((t, A_W + DN_V), BF16),
                   jax.ShapeDtypeStruct((t, A_W), F32),
                   jax.ShapeDtypeStruct(state.shape, F32)],
        compiler_params=_cparams(("arbitrary",)),
        name="branch_sample",
    )(p1, p1, p1, pab, ea, eq, state, caw, dcw, alog_row, dtb_row, dnorm)


def _attn_sample_kernel(q_ref, k_ref, v_ref, o_ref):
    for h in range(XA_HEADS):
        sl = slice(h * LANE, (h + 1) * LANE)
        rows = pl.ds(h, MEM_TOKENS, stride=XA_HEADS)
        q = q_ref[:, :, sl].astype(BF16)
        s = jnp.einsum("bqd,bkd->bqk", q, k_ref[:, rows, :].astype(BF16),
                       preferred_element_type=F32) * (XA_DH ** -0.5)
        e = jnp.exp(s - jnp.max(s, axis=-1, keepdims=True))
        p = e / jnp.sum(e, axis=-1, keepdims=True)
        o_ref[:, :, sl] = jnp.einsum("bqk,bkd->bqd", p.astype(BF16), v_ref[:, rows, :].astype(BF16),
                                     preferred_element_type=F32).astype(BF16)


def _attn_sample(q3, ck, cv, nb):
    bsz, length, _ = q3.shape
    return pl.pallas_call(
        _attn_sample_kernel,
        grid=(bsz // nb,),
        in_specs=[pl.BlockSpec((nb, length, XA_W), lambda i: (i, 0, 0)),
                  pl.BlockSpec((nb, MEM_TOKENS * XA_HEADS, XA_DH), lambda i: (i, 0, 0)),
                  pl.BlockSpec((nb, MEM_TOKENS * XA_HEADS, XA_DH), lambda i: (i, 0, 0))],
        out_specs=pl.BlockSpec((nb, length, XA_W), lambda i: (i, 0, 0)),
        out_shape=jax.ShapeDtypeStruct((bsz, length, XA_W), BF16),
        compiler_params=_cparams(("arbitrary",)),
        name="attn_sample",
    )(q3, ck, cv)


def _merge_kernel(yad_ref, ym_ref, p2_ref, x_ref, wb_ref, wo_ref, o_ref):
    yad = yad_ref[...]
    gate = lambda j: jax.nn.sigmoid(p2_ref[:, OFF_G + j * D_MODEL:OFF_G + (j + 1) * D_MODEL])
    merged = (gate(0) * jnp.dot(yad[:, :A_W], wb_ref[0:A_W, :], preferred_element_type=F32)
              + gate(1) * jnp.dot(yad[:, A_W:], wb_ref[A_W:A_W + DN_V, :], preferred_element_type=F32)
              + gate(2) * jnp.dot(ym_ref[...], wb_ref[A_W + DN_V:, :], preferred_element_type=F32))
    o_ref[...] = x_ref[...] + jnp.dot(merged.astype(BF16), wo_ref[...], preferred_element_type=F32)


def _merge(yad, ym, p2, x2d, wb, wo, tm):
    t, d = x2d.shape
    row = lambda n: pl.BlockSpec((tm, n), lambda i: (i, 0))
    return pl.pallas_call(
        _merge_kernel,
        grid=(t // tm,),
        in_specs=[row(A_W + DN_V), row(XA_W), row(W2), row(d), _resident(wb.shape), _resident(wo.shape)],
        out_specs=row(d),
        out_shape=jax.ShapeDtypeStruct((t, d), F32),
        compiler_params=_cparams(("arbitrary",)),
        name="merge",
    )(yad, ym, p2, x2d, wb, wo)


MXU_K = 256
FF_EDGES = (0, 6 * MXU_K, D_FF)
FF_SPLIT = len(FF_EDGES) - 1
FF_SUB = 512
TM_FF = 2 * FF_SUB


def _ffn_kernel(x_ref, gf_ref, wu_ref, wd_ref, gl_ref, o_ref):
    n_sub = x_ref.shape[0] // FF_SUB

    def block(xn, acc, j):
        lo, hi = FF_EDGES[j], FF_EDGES[j + 1]
        gate = jnp.dot(xn, wu_ref[:, lo:hi], preferred_element_type=F32)
        up = jnp.dot(xn, wu_ref[:, D_FF + lo:D_FF + hi], preferred_element_type=F32)
        hid = (_silu(gate) * up).astype(BF16)
        return acc + jnp.dot(hid, wd_ref[lo:hi, :], preferred_element_type=F32)

    state = [None] * n_sub
    for i in range(n_sub + 1):
        if i < n_sub:
            x = x_ref[i * FF_SUB:(i + 1) * FF_SUB, :]
            xn = _rms(x, gf_ref[...]).astype(BF16)
            state[i] = (xn, block(xn, x, 0))
        if i > 0:
            xn, acc = state[i - 1]
            for j in range(1, FF_SPLIT):
                acc = block(xn, acc, j)
            o_ref[(i - 1) * FF_SUB:i * FF_SUB, :] = _rms(acc, gl_ref[...])


def _ffn(x2d, gf, wu, wd, gl, tm):
    t, d = x2d.shape
    return pl.pallas_call(
        _ffn_kernel,
        grid=(t // tm,),
        in_specs=[pl.BlockSpec((tm, d), lambda i: (i, 0)),
                  _resident((1, d)), _resident(wu.shape), _resident(wd.shape), _resident((1, d))],
        out_specs=pl.BlockSpec((tm, d), lambda i: (i, 0)),
        out_shape=jax.ShapeDtypeStruct((t, d), F32),
        compiler_params=_cparams(("arbitrary",)),
        name="ffn",
    )(x2d, gf, wu, wd, gl)


def _pad_lanes(v):
    return jnp.zeros((1, LANE), F32).at[0, :v.shape[0]].set(v.astype(F32))


def kernel(x_prompt, x_sample, mem_prompt, state_conv_a, state_dn_conv, state_dn, cache_mem_k, cache_mem_v,
           norm_mix, w_in, conv_a_w, dn_conv_w, dn_a_log, dn_dt_bias, dn_norm, norm_mem, w_mem_kv, w_branch,
           w_o, norm_ffn, w_ffn_up, w_ffn_down, norm_final):
    bp, lp, d = x_prompt.shape
    bs, ls, _ = x_sample.shape
    assert norm_mix.shape[0] == 1 and ls == SEQ_S and lp % CHUNK == 0 and (bs * ls) % CHUNK == 0
    assert w_in.shape[2] == W1 + N_AB + W2

    w = w_in[0]
    w1 = w[:, :W1].astype(BF16)
    w2 = w[:, W1 + N_AB:].astype(BF16)
    wab = jnp.pad(w[:, W1:W1 + N_AB], ((0, 0), (0, LANE - N_AB))).astype(BF16)
    wb = w_branch[0].astype(BF16)
    wo = w_o[0].astype(BF16)
    wu = w_ffn_up[0].astype(BF16)
    wd = w_ffn_down[0].astype(BF16)
    wkv = w_mem_kv[0].astype(BF16)
    g_mix = norm_mix[0][None, :]
    g_ffn = norm_ffn[0][None, :]
    g_fin = norm_final[None, :]
    g_mem = norm_mem[0][None, :]
    caw = conv_a_w[0]
    dcw = dn_conv_w[0]
    alog_row = _pad_lanes(dn_a_log[0])
    dtb_row = _pad_lanes(dn_dt_bias[0])
    dnorm = dn_norm[0][None, :]

    tp = bp * lp
    mk, mv, mkb, mvb = _memkv(mem_prompt.reshape(bp * MEM_TOKENS, d), g_mem, wkv, TM)
    x1_p, ca_p, dc_p, s_p = _front_prompt(x_prompt, g_mix, w1, w2, wab, mkb.reshape(bp, MEM_TOKENS, XA_W),
                                          mvb.reshape(bp, MEM_TOKENS, XA_W), caw, dcw, alog_row, dtb_row, dnorm,
                                          wb, wo, NB_P)

    ts = bs * ls
    xs2 = x_sample.reshape(ts, d)
    p1_s, p2_s, pab_s = _proj(xs2, g_mix, w1, w2, wab, CHUNK)
    ea = jnp.pad(state_conv_a[0], ((0, 0), (0, ls - (CONV_A_K - 1)), (0, 0))).reshape(ts, A_W)
    eq = jnp.pad(state_dn_conv[0], ((0, 0), (0, ls - (DN_CONV_K - 1)), (0, 0))).reshape(ts, DN_CONV_CH)
    yad_s, u_s, s_s = _branch_sample(p1_s, pab_s, ea, eq, state_dn[0], caw, dcw, alog_row, dtb_row, dnorm)
    ym_s = _attn_sample(p2_s[:, :XA_W].reshape(bs, ls, XA_W),
                        cache_mem_k.reshape(bs, MEM_TOKENS * XA_HEADS, XA_DH),
                        cache_mem_v.reshape(bs, MEM_TOKENS * XA_HEADS, XA_DH), NB_ATTN_S)
    x1_s = _merge(yad_s, ym_s.reshape(ts, XA_W), p2_s, xs2, wb, wo, ts)
    ca_s = u_s.reshape(bs, ls, A_W)[:, ls - (CONV_A_K - 1):]
    dc_s = p1_s[:, OFF_QKV:OFF_Z].reshape(bs, ls, DN_CONV_CH)[:, ls - (DN_CONV_K - 1):]

    y_p = _ffn(x1_p.reshape(tp, d), g_ffn, wu, wd, g_fin, TM_FF).reshape(bp, lp, d)
    y_s = _ffn(x1_s, g_ffn, wu, wd, g_fin, ts).reshape(bs, ls, d)

    return (y_p, y_s, ca_p[None], dc_p[None], s_p[None],
            mk.reshape(1, bp, MEM_TOKENS, XA_HEADS, XA_DH), mv.reshape(1, bp, MEM_TOKENS, XA_HEADS, XA_DH),
            ca_s[None], dc_s[None], s_s[None])
```

```python
import functools

import jax
import jax.numpy as jnp
from jax import lax
from jax.experimental import pallas as pl
from jax.experimental.pallas import tpu as pltpu

F32 = jnp.float32
BF16 = jnp.bfloat16

D_MODEL = 1024
A_W = 512
CONV_A_K = 3
DN_HEADS = 4
DN_DK = 128
DN_DV = 128
DN_QK = DN_HEADS * DN_DK
DN_V = DN_HEADS * DN_DV
DN_CONV_CH = 2 * DN_QK + DN_V
DN_CONV_K = 4
MEM_TOKENS = 256
XA_HEADS = 4
XA_DH = 128
XA_W = XA_HEADS * XA_DH
D_FF = 2816
EPS = 1e-6

LANE = 128
CHUNK = 128
TAIL = 8
NB_P = 4
TM = 512
GATE_BLK = 256
PROJ_BLK = 512
ATTN_RESERVE = 4

W1 = 3 * A_W + DN_CONV_CH + DN_V
OFF_QKV = 3 * A_W
OFF_Z = OFF_QKV + DN_CONV_CH
W2 = XA_W + 3 * D_MODEL
OFF_G = XA_W
N_AB = 2 * DN_HEADS
P_Z = 3 * A_W
P_XQ = P_Z + DN_V
P_W = P_XQ + XA_W

VMEM_LIMIT = 60 * 1024 * 1024


def _cparams(sem):
    return pltpu.CompilerParams(dimension_semantics=sem, vmem_limit_bytes=VMEM_LIMIT)


def _resident(shape):
    return pl.BlockSpec(shape, lambda *_: (0,) * len(shape), pipeline_mode=pl.Buffered(1))


def _mm(a, b):
    return jnp.dot(a.astype(BF16), b.astype(BF16), preferred_element_type=F32)


def _mm_nt(a, b):
    return lax.dot_general(a.astype(BF16), b.astype(BF16), (((1,), (1,)), ((), ())),
                           preferred_element_type=F32)


def _mm_tn(a, b):
    return lax.dot_general(a.astype(BF16), b.astype(BF16), (((0,), (0,)), ((), ())),
                           preferred_element_type=F32)


def _rms(x, g):
    return x * lax.rsqrt(jnp.mean(x * x, axis=-1, keepdims=True) + EPS) * g


def _silu(x):
    return x * jax.nn.sigmoid(x)


def _softplus(x):
    return jnp.maximum(x, 0.0) + jnp.log1p(jnp.exp(-jnp.abs(x)))


def _memkv_kernel(x_ref, g_ref, w_ref, k_ref, v_ref, kb_ref, vb_ref):
    xn = _rms(x_ref[...], g_ref[...]).astype(BF16)
    kv = jnp.dot(xn, w_ref[...], preferred_element_type=F32)
    k = kv[:, :XA_W]
    v = kv[:, XA_W:]
    tm = k.shape[0]
    for h in range(XA_HEADS):
        rows = pl.ds(h, tm, stride=XA_HEADS)
        k_ref[rows, :] = k[:, h * LANE:(h + 1) * LANE]
        v_ref[rows, :] = v[:, h * LANE:(h + 1) * LANE]
    kb_ref[...] = k.astype(BF16)
    vb_ref[...] = v.astype(BF16)


def _memkv(mem2d, gain, w, tm):
    t, d = mem2d.shape
    blk = pl.BlockSpec((tm, XA_W), lambda i: (i, 0))
    blk_rows = pl.BlockSpec((tm * XA_HEADS, XA_DH), lambda i: (i, 0))
    return pl.pallas_call(
        _memkv_kernel,
        grid=(t // tm,),
        in_specs=[pl.BlockSpec((tm, d), lambda i: (i, 0)),
                  pl.BlockSpec((1, d), lambda i: (0, 0)),
                  pl.BlockSpec((d, 2 * XA_W), lambda i: (0, 0))],
        out_specs=[blk_rows, blk_rows, blk, blk],
        out_shape=[jax.ShapeDtypeStruct((t * XA_HEADS, XA_DH), F32),
                   jax.ShapeDtypeStruct((t * XA_HEADS, XA_DH), F32),
                   jax.ShapeDtypeStruct((t, XA_W), BF16), jax.ShapeDtypeStruct((t, XA_W), BF16)],
        compiler_params=_cparams(("arbitrary",)),
        name="memkv",
    )(mem2d, gain, w)


def _log2(n):
    return n.bit_length() - 1


def _dn_masks(seg):
    r = lax.broadcasted_iota(jnp.int32, (CHUNK, CHUNK), 0)
    c = lax.broadcasted_iota(jnp.int32, (CHUNK, CHUNK), 1)
    ls = _log2(seg)
    same = (r >> ls) == (c >> ls)
    base = min(8, seg)
    lb = _log2(base)
    m = {
        "causal": (same & (r >= c)).astype(F32),
        "strict": (same & (r > c)).astype(F32),
        "eye": (r == c).astype(F32),
        "neg_diag": -((r >> lb) == (c >> lb)).astype(F32),
        "off": {},
        "base": base,
    }
    s = base
    while s < seg:
        l1, l2 = _log2(s), _log2(2 * s)
        m["off"][s] = (((r >> l2) == (c >> l2)) & ((r >> l1) != (c >> l1))).astype(F32)
        s *= 2
    return m


def _each(f, *lists):
    return [f(*args) for args in zip(*lists)]


def _low_rows(x, s):
    return jnp.concatenate([x[i + s:i + 2 * s] for i in range(0, x.shape[0], 2 * s)], axis=0)


def _merge_low(x, low, s):
    parts = []
    for j, i in enumerate(range(0, x.shape[0], 2 * s)):
        parts += [x[i:i + s], low[j * s:(j + 1) * s]]
    return jnp.concatenate(parts, axis=0)


def _spread_low(low, s):
    return _merge_low(jnp.zeros((2 * low.shape[0], low.shape[1]), low.dtype), low, s)


def _tri_inv(a_list, m, seg, tick):
    add = lambda x, y: x + y
    b = _each(lambda a: a * m["neg_diag"], a_list)
    p = _each(lambda x: m["eye"] + x, b)
    b2 = _each(_mm, b, b)
    tick()
    p = _each(add, p, _each(_mm, p, b2))
    tick()
    if m["base"] == 8:
        b4 = _each(_mm, b2, b2)
        tick()
        p = _each(add, p, _each(_mm, p, b4))
        tick()
    s = m["base"]
    while s < seg:
        low = lambda t, s=s: _low_rows(t, s)
        x = _each(_mm, _each(lambda a, s=s: low(a) * low(m["off"][s]), a_list), p)
        tick()
        r = _each(_mm, _each(low, p), _each(lambda y, s=s: _spread_low(y, s), x))
        p = _each(lambda t, y, s=s: _merge_low(t, low(t) - y, s), p, r)
        tick()
        s *= 2
    return p


def _seg_scan(x, seg, reverse):
    n = x.shape[1]
    pos = lax.broadcasted_iota(jnp.int32, x.shape, 1) & (seg - 1)
    s = 1
    while s < seg:
        shifted = pltpu.roll(x, n - s if reverse else s, 1)
        x = x + jnp.where(pos < seg - s if reverse else pos >= s, shifted, 0.0)
        s *= 2
    return x


def _dn_gates(ab, alog8, dtb8, seg):
    abt = ab.T[0:TAIL]
    g = -jnp.exp(alog8) * _softplus(abt + dtb8)
    d = _seg_scan(g, seg, False)
    dl = d + _seg_scan(g, seg, True) - g
    beta = jax.nn.sigmoid(abt)
    pad = jnp.zeros((CHUNK - 3 * TAIL, CHUNK), F32)
    return d, dl, jnp.concatenate([d, dl, beta, pad], axis=0).T


def _l2n(x, scale=1.0):
    return x * (lax.rsqrt(jnp.sum(x * x, axis=-1, keepdims=True) + EPS) * scale)


def _dn_intra(q, k, v, d_col, d_row, beta_col, m, seg, tick=lambda: None):
    q = _each(lambda x: _l2n(x, DN_DK ** -0.5), q)
    k = _each(_l2n, k)
    gamma = _each(lambda dc, dr: jnp.exp((dc - dr) * m["causal"]) * m["causal"], d_col, d_row)
    kk = _each(_mm_nt, k, k)
    a = _each(lambda bc, x, g: (bc * x) * g * m["strict"], beta_col, kk, gamma)
    t = _tri_inv(a, m, seg, tick)
    rhs = _each(lambda vv, kx, bc, dc: jnp.concatenate([vv * bc, kx * (bc * jnp.exp(dc))], axis=1),
                v, k, beta_col, d_col)
    sol = _each(_mm, t, rhs)
    u = [x[:, :DN_DV] for x in sol]
    w = [x[:, DN_DV:] for x in sol]
    qk = _each(lambda x, g: x * g, _each(_mm_nt, q, k), gamma)
    return q, k, u, w, qk


def _dn_out(o, z, dnorm):
    return _rms(o, dnorm) * _silu(z)


def _head_lists(qkv, z, d, cols):
    out = [[] for _ in range(8)]
    for h in range(DN_HEADS):
        beta_lane = 2 * TAIL + DN_HEADS + h
        vals = (qkv[:, h * LANE:(h + 1) * LANE],
                qkv[:, DN_QK + h * LANE:DN_QK + (h + 1) * LANE],
                qkv[:, 2 * DN_QK + h * LANE:2 * DN_QK + (h + 1) * LANE],
                None if z is None else z[:, h * LANE:(h + 1) * LANE],
                cols[:, h:h + 1], d[h:h + 1, :], cols[:, TAIL + h:TAIL + h + 1],
                cols[:, beta_lane:beta_lane + 1])
        for lst, val in zip(out, vals):
            lst.append(val)
    return out


def _causal_conv(x, tail_ref, b, wts, width):
    c, w = x.shape
    tiles = jnp.concatenate([tail_ref[b][None], x.reshape(c // TAIL, TAIL, w)], axis=0)
    sub = lax.broadcasted_iota(jnp.int32, (1, TAIL, 1), 1)
    acc = None
    for i in range(width):
        s = width - 1 - i
        if s == 0:
            y = tiles[1:]
        else:
            r = pltpu.roll(tiles, s, 1)
            y = jnp.where(sub >= s, r[1:], r[:-1])
        term = wts[i:i + 1][None] * y
        acc = term if acc is None else acc + term
    tail_ref[b] = tiles[c // TAIL]
    return acc.reshape(c, w)


def _front_prompt_kernel(x_ref, gmix_ref, w1_ref, w2_ref, wab_ref, mk_ref, mv_ref, caw_ref, dcw_ref,
                         alog_ref, dtb_ref, dnorm_ref, wb_ref, wo_ref,
                         x1_ref, ca_ref, dc_ref, s_ref, utail, qtail, ybuf, pbuf, gbuf, *, nb):
    c = CHUNK
    rows = nb * c
    t_idx = pl.program_id(1)

    @pl.when(t_idx == 0)
    def _():
        utail[...] = jnp.zeros(utail.shape, F32)
        qtail[...] = jnp.zeros(qtail.shape, F32)
        s_ref[...] = jnp.zeros(s_ref.shape, F32)

    x = x_ref[...].reshape(rows, D_MODEL)
    xn = _rms(x, gmix_ref[...]).astype(BF16)
    proj = lambda w_ref, lo, hi: jnp.dot(xn, w_ref[:, lo:hi], preferred_element_type=F32)

    queue = []

    def enqueue(dst, w_ref, src, dst_lo, width, act=None):
        def run():
            r = proj(w_ref, src, src + width)
            dst[:, dst_lo:dst_lo + width] = r if act is None else act(r)
        queue.append(run)

    for lo in range(0, 3 * A_W, PROJ_BLK):
        enqueue(pbuf, w1_ref, lo, lo, PROJ_BLK)
    enqueue(pbuf, w1_ref, OFF_Z, P_Z, DN_V)
    enqueue(pbuf, w2_ref, 0, P_XQ, XA_W)
    for lo in range(0, 3 * D_MODEL, GATE_BLK):
        enqueue(gbuf, w2_ref, OFF_G + lo, lo, GATE_BLK, jax.nn.sigmoid)
    queue.reverse()

    def tick(keep=ATTN_RESERVE):
        if len(queue) > keep:
            queue.pop()()

    m = _dn_masks(c)
    caw = caw_ref[...]
    dcw = dcw_ref[...]
    dnorm = dnorm_ref[...]
    pq_all = proj(w1_ref, OFF_QKV, OFF_Z)
    pab_all = jnp.dot(xn, wab_ref[...], preferred_element_type=F32)

    qkv_l, gates_l = [], []
    for b in range(nb):
        rb = slice(b * c, (b + 1) * c)
        tick()
        qkv_in = pq_all[rb]
        dc_ref[b] = qkv_in[c - (DN_CONV_K - 1):]
        qkv_l.append(_silu(_causal_conv(qkv_in, qtail, b, dcw, DN_CONV_K)))
        gates_l.append(_dn_gates(pab_all[rb], alog_ref[...], dtb_ref[...], c))

    for b in range(nb):
        rb = slice(b * c, (b + 1) * c)
        tick()
        u_in = pbuf[rb, A_W:2 * A_W] * pbuf[rb, 2 * A_W:3 * A_W]
        ca_ref[b] = u_in[c - (CONV_A_K - 1):]
        ybuf[rb, 0:A_W] = (pbuf[rb, 0:A_W] * _causal_conv(u_in, utail, b, caw, CONV_A_K)).astype(BF16)

    lists = [[] for _ in range(8)]
    for b in range(nb):
        d, _, cols = gates_l[b]
        for lst, val in zip(lists, _head_lists(qkv_l[b], None, d, cols)):
            lst.extend(val)
    idx = [(b, h) for b in range(nb) for h in range(DN_HEADS)]
    q, k, v, _, d_col, d_row, dl_col, beta_col = lists
    q, k, u, w, qk = _dn_intra(q, k, v, d_col, d_row, beta_col, m, c, tick)
    s_old = [s_ref[b, h] for b, h in idx]
    vn = _each(lambda ux, wx, s: ux - _mm(wx, s), u, w, s_old)
    tick()
    o = _each(lambda qx, dc, y, s, vx: _mm(jnp.concatenate([qx * jnp.exp(dc), y], axis=1),
                                           jnp.concatenate([s, vx], axis=0)), q, d_col, qk, s_old, vn)
    tick()
    s_new = _each(lambda s, dlc, kx, dc, vx: s * jnp.exp(dlc[0:1, :]) + _mm_tn(kx * jnp.exp(dlc - dc), vx),
                  s_old, dl_col, k, d_col, vn)
    for (b, h), sx, ox in zip(idx, s_new, o):
        s_ref[b, h] = sx
        zx = pbuf[b * c:(b + 1) * c, P_Z + h * LANE:P_Z + (h + 1) * LANE]
        ybuf[b * c:(b + 1) * c, A_W + h * LANE:A_W + (h + 1) * LANE] = _dn_out(ox, zx, dnorm).astype(BF16)

    for b in range(nb):
        tick(0)
        heads = range(XA_HEADS)
        sc = [_mm_nt(pbuf[b * c:(b + 1) * c, P_XQ + h * LANE:P_XQ + (h + 1) * LANE],
                     mk_ref[b, :, h * LANE:(h + 1) * LANE]) * (XA_DH ** -0.5) for h in heads]
        e = _each(lambda x_: jnp.exp(x_ - jnp.max(x_, axis=-1, keepdims=True)), sc)
        inv = _each(lambda x_: 1.0 / jnp.sum(x_, axis=-1, keepdims=True), e)
        for h, ex, ix in zip(heads, e, inv):
            ybuf[b * c:(b + 1) * c, A_W + DN_V + h * LANE:A_W + DN_V + (h + 1) * LANE] = (
                _mm(ex, mv_ref[b, :, h * LANE:(h + 1) * LANE]) * ix).astype(BF16)
    while queue:
        tick(0)

    merged = None
    for j, (lo, hi) in enumerate(((0, A_W), (A_W, A_W + DN_V), (A_W + DN_V, A_W + DN_V + XA_W))):
        term = gbuf[:, j * D_MODEL:(j + 1) * D_MODEL] * jnp.dot(ybuf[:, lo:hi], wb_ref[lo:hi, :],
                                                                preferred_element_type=F32)
        merged = term if merged is None else merged + term
    x1 = x + jnp.dot(merged.astype(BF16), wo_ref[...], preferred_element_type=F32)
    x1_ref[...] = x1.reshape(nb, c, D_MODEL)


def _front_prompt(x, gmix, w1, w2, wab, mkb, mvb, caw, dcw, alog_row, dtb_row, dnorm, wb, wo, nb):
    bsz, length, d = x.shape
    c = CHUNK
    return pl.pallas_call(
        functools.partial(_front_prompt_kernel, nb=nb),
        grid=(bsz // nb, length // c),
        in_specs=[pl.BlockSpec((nb, c, d), lambda g, t: (g, t, 0)),
                  _resident((1, d)), _resident(w1.shape), _resident(w2.shape), _resident(wab.shape),
                  pl.BlockSpec((nb, MEM_TOKENS, XA_W), lambda g, t: (g, 0, 0), pipeline_mode=pl.Buffered(1)),
                  pl.BlockSpec((nb, MEM_TOKENS, XA_W), lambda g, t: (g, 0, 0), pipeline_mode=pl.Buffered(1)),
                  _resident((CONV_A_K, A_W)), _resident((DN_CONV_K, DN_CONV_CH)),
                  _resident((TAIL, LANE)), _resident((TAIL, LANE)), _resident((1, LANE)),
                  _resident(wb.shape), _resident(wo.shape)],
        out_specs=[pl.BlockSpec((nb, c, d), lambda g, t: (g, t, 0)),
                   pl.BlockSpec((nb, CONV_A_K - 1, A_W), lambda g, t: (g, 0, 0)),
                   pl.BlockSpec((nb, DN_CONV_K - 1, DN_CONV_CH), lambda g, t: (g, 0, 0)),
                   pl.BlockSpec((nb, DN_HEADS, DN_DK, DN_DV), lambda g, t: (g, 0, 0, 0))],
        out_shape=[jax.ShapeDtypeStruct((bsz, length, d), F32),
                   jax.ShapeDtypeStruct((bsz, CONV_A_K - 1, A_W), F32),
                   jax.ShapeDtypeStruct((bsz, DN_CONV_K - 1, DN_CONV_CH), F32),
                   jax.ShapeDtypeStruct((bsz, DN_HEADS, DN_DK, DN_DV), F32)],
        scratch_shapes=[pltpu.VMEM((nb, TAIL, A_W), F32), pltpu.VMEM((nb, TAIL, DN_CONV_CH), F32),
                        pltpu.VMEM((nb * c, A_W + DN_V + XA_W), BF16), pltpu.VMEM((nb * c, P_W), F32),
                        pltpu.VMEM((nb * c, 3 * d), F32)],
        compiler_params=_cparams(("arbitrary", "arbitrary")),
        name="front_prompt",
    )(x, gmix, w1, w2, wab, mkb, mvb, caw, dcw, alog_row, dtb_row, dnorm, wb, wo)


def _proj_kernel(x_ref, g_ref, w1_ref, w2_ref, wab_ref, p1_ref, p2_ref, pab_ref):
    xn = _rms(x_ref[...], g_ref[...]).astype(BF16)
    p1_ref[...] = jnp.dot(xn, w1_ref[...], preferred_element_type=F32)
    p2_ref[...] = jnp.dot(xn, w2_ref[...], preferred_element_type=F32)
    pab_ref[...] = jnp.dot(xn, wab_ref[...], preferred_element_type=F32)


def _proj(x2d, gain, w1, w2, wab, tm):
    t, d = x2d.shape
    row = lambda n: pl.BlockSpec((tm, n), lambda i: (i, 0))
    return pl.pallas_call(
        _proj_kernel,
        grid=(t // tm,),
        in_specs=[row(d), _resident((1, d)), _resident(w1.shape), _resident(w2.shape), _resident(wab.shape)],
        out_specs=[row(W1), row(W2), row(LANE)],
        out_shape=[jax.ShapeDtypeStruct((t, W1), F32), jax.ShapeDtypeStruct((t, W2), F32),
                   jax.ShapeDtypeStruct((t, LANE), F32)],
        compiler_params=_cparams(("arbitrary",)),
        name="proj",
    )(x2d, gain, w1, w2, wab)


SEQ_S = 4
NB_S = CHUNK // SEQ_S
NB_ATTN_S = 16


def _seg_conv(x, e, wts, width, tmod):
    rows = x.shape[0]
    acc = None
    for i in range(width):
        s = width - 1 - i
        term = x if s == 0 else jnp.where(tmod >= s, pltpu.roll(x, s, 0), 0.0)
        if i < width - 1:
            hist = e if i == 0 else pltpu.roll(e, rows - i, 0)
            term = term + jnp.where(tmod < SEQ_S - i, hist, 0.0)
        term = wts[i:i + 1] * term
        acc = term if acc is None else acc + term
    return acc


def _branch_sample_kernel(pa_ref, pq_ref, pz_ref, pab_ref, ea_ref, eq_ref, s0_ref, caw_ref, dcw_ref,
                          alog_ref, dtb_ref, dnorm_ref, yad_ref, u_ref, s_ref):
    c = CHUNK
    m = _dn_masks(SEQ_S)
    tmod = lax.broadcasted_iota(jnp.int32, (c, 1), 0) & (SEQ_S - 1)

    pa = pa_ref[...]
    u_in = pa[:, A_W:2 * A_W] * pa[:, 2 * A_W:3 * A_W]
    u_ref[...] = u_in
    conv = _seg_conv(u_in, ea_ref[...], caw_ref[...], CONV_A_K, tmod)
    yad_ref[:, 0:A_W] = (pa[:, 0:A_W] * conv).astype(BF16)

    qkv = _silu(_seg_conv(pq_ref[...], eq_ref[...], dcw_ref[...], DN_CONV_K, tmod))
    d, dl, cols = _dn_gates(pab_ref[...], alog_ref[...], dtb_ref[...], SEQ_S)
    dec_t = jnp.exp(dl)
    dnorm = dnorm_ref[...]

    wide = NB_S * DN_DK
    er = lax.broadcasted_iota(jnp.int32, (c, wide), 0)
    ec = lax.broadcasted_iota(jnp.int32, (c, wide), 1)
    mexp = ((er >> 2) == (ec >> 7)).astype(F32)
    mexp2 = jnp.concatenate([mexp, mexp], axis=0)
    tr = lax.broadcasted_iota(jnp.int32, (wide, c), 0)
    tc = lax.broadcasted_iota(jnp.int32, (wide, c), 1)
    mexp_t = ((tr >> 7) == (tc >> 2)).astype(F32)

    q, k, v, z, d_col, d_row, dl_col, beta_col = _head_lists(qkv, pz_ref[...], d, cols)
    q, k, u, w, qk = _dn_intra(q, k, v, d_col, d_row, beta_col, m, SEQ_S)
    heads = list(range(DN_HEADS))
    s_old = [s0_ref[:, h].reshape(wide, DN_DV) for h in heads]
    x_exp = _each(lambda wx, qx, dc: jnp.concatenate([jnp.concatenate([wx, qx * jnp.exp(dc)], axis=0)] * NB_S,
                                                     axis=1) * mexp2, w, q, d_col)
    ws = _each(_mm, x_exp, s_old)
    vn = _each(lambda ux, x: ux - x[:c], u, ws)
    o = _each(lambda x, y, vx: x[c:] + _mm(y, vx), ws, qk, vn)
    k_exp = _each(lambda kx, dlc, dc: jnp.concatenate([(kx * jnp.exp(dlc - dc)).T] * NB_S, axis=0) * mexp_t,
                  k, dl_col, d_col)
    dec = [jnp.concatenate([jnp.broadcast_to(dec_t[h:h + 1, SEQ_S * b:SEQ_S * b + 1], (DN_DK, DN_DV))
                            for b in range(NB_S)], axis=0) for h in heads]
    s_new = _each(lambda s, dx, kx, vx: s * dx + _mm(kx, vx), s_old, dec, k_exp, vn)
    for h, sx, ox, zx in zip(heads, s_new, o, z):
        s_ref[:, h] = sx.reshape(NB_S, DN_DK, DN_DV)
        yad_ref[:, A_W + h * LANE:A_W + (h + 1) * LANE] = _dn_out(ox, zx, dnorm).astype(BF16)


def _branch_sample(p1, pab, ea, eq, state, caw, dcw, alog_row, dtb_row, dnorm):
    t = p1.shape[0]
    c = CHUNK
    full = lambda shape: pl.BlockSpec(shape, lambda i: (0,) * len(shape))
    return pl.pallas_call(
        _branch_sample_kernel,
        grid=(t // c,),
        in_specs=[pl.BlockSpec((c, 3 * A_W), lambda i: (i, 0)),
                  pl.BlockSpec((c, DN_CONV_CH), lambda i: (i, OFF_QKV // DN_CONV_CH)),
                  pl.BlockSpec((c, DN_V), lambda i: (i, OFF_Z // DN_V)),
                  pl.BlockSpec((c, LANE), lambda i: (i, 0)),
                  pl.BlockSpec((c, A_W), lambda i: (i, 0)),
                  pl.BlockSpec((c, DN_CONV_CH), lambda i: (i, 0)),
                  pl.BlockSpec((NB_S, DN_HEADS, DN_DK, DN_DV), lambda i: (i, 0, 0, 0)),
                  full((CONV_A_K, A_W)), full((DN_CONV_K, DN_CONV_CH)),
                  full((TAIL, LANE)), full((TAIL, LANE)), full((1, LANE))],
        out_specs=[pl.BlockSpec((c, A_W + DN_V), lambda i: (i, 0)),
                   pl.BlockSpec((c, A_W), lambda i: (i, 0)),
                   pl.BlockSpec((NB_S, DN_HEADS, DN_DK, DN_DV), lambda i: (i, 0, 0, 0))],
        out_shape=[jax.ShapeDtypeStruct((t, A_W + DN_V), BF16),
                   jax.ShapeDtypeStruct((t, A_W), F32),
                   jax.ShapeDtypeStruct(state.shape, F32)],
        compiler_params=_cparams(("arbitrary",)),
        name="branch_sample",
    )(p1, p1, p1, pab, ea, eq, state, caw, dcw, alog_row, dtb_row, dnorm)


def _attn_sample_kernel(q_ref, k_ref, v_ref, o_ref):
    for h in range(XA_HEADS):
        sl = slice(h * LANE, (h + 1) * LANE)
        rows = pl.ds(h, MEM_TOKENS, stride=XA_HEADS)
        q = q_ref[:, :, sl].astype(BF16)
        s = jnp.einsum("bqd,bkd->bqk", q, k_ref[:, rows, :].astype(BF16),
                       preferred_element_type=F32) * (XA_DH ** -0.5)
        e = jnp.exp(s - jnp.max(s, axis=-1, keepdims=True))
        p = e / jnp.sum(e, axis=-1, keepdims=True)
        o_ref[:, :, sl] = jnp.einsum("bqk,bkd->bqd", p.astype(BF16), v_ref[:, rows, :].astype(BF16),
                                     preferred_element_type=F32).astype(BF16)


def _attn_sample(q3, ck, cv, nb):
    bsz, length, _ = q3.shape
    return pl.pallas_call(
        _attn_sample_kernel,
        grid=(bsz // nb,),
        in_specs=[pl.BlockSpec((nb, length, XA_W), lambda i: (i, 0, 0)),
                  pl.BlockSpec((nb, MEM_TOKENS * XA_HEADS, XA_DH), lambda i: (i, 0, 0)),
                  pl.BlockSpec((nb, MEM_TOKENS * XA_HEADS, XA_DH), lambda i: (i, 0, 0))],
        out_specs=pl.BlockSpec((nb, length, XA_W), lambda i: (i, 0, 0)),
        out_shape=jax.ShapeDtypeStruct((bsz, length, XA_W), BF16),
        compiler_params=_cparams(("arbitrary",)),
        name="attn_sample",
    )(q3, ck, cv)


def _merge_kernel(yad_ref, ym_ref, p2_ref, x_ref, wb_ref, wo_ref, o_ref):
    yad = yad_ref[...]
    gate = lambda j: jax.nn.sigmoid(p2_ref[:, OFF_G + j * D_MODEL:OFF_G + (j + 1) * D_MODEL])
    merged = (gate(0) * jnp.dot(yad[:, :A_W], wb_ref[0:A_W, :], preferred_element_type=F32)
              + gate(1) * jnp.dot(yad[:, A_W:], wb_ref[A_W:A_W + DN_V, :], preferred_element_type=F32)
              + gate(2) * jnp.dot(ym_ref[...], wb_ref[A_W + DN_V:, :], preferred_element_type=F32))
    o_ref[...] = x_ref[...] + jnp.dot(merged.astype(BF16), wo_ref[...], preferred_element_type=F32)


def _merge(yad, ym, p2, x2d, wb, wo, tm):
    t, d = x2d.shape
    row = lambda n: pl.BlockSpec((tm, n), lambda i: (i, 0))
    return pl.pallas_call(
        _merge_kernel,
        grid=(t // tm,),
        in_specs=[row(A_W + DN_V), row(XA_W), row(W2), row(d), _resident(wb.shape), _resident(wo.shape)],
        out_specs=row(d),
        out_shape=jax.ShapeDtypeStruct((t, d), F32),
        compiler_params=_cparams(("arbitrary",)),
        name="merge",
    )(yad, ym, p2, x2d, wb, wo)


MXU_K = 256
FF_EDGES = (0, 6 * MXU_K, D_FF)
FF_SPLIT = len(FF_EDGES) - 1
FF_SUB = 512
TM_FF = 2 * FF_SUB


def _ffn_kernel(x_ref, gf_ref, wu_ref, wd_ref, gl_ref, o_ref):
    n_sub = x_ref.shape[0] // FF_SUB

    def block(xn, acc, j):
        lo, hi = FF_EDGES[j], FF_EDGES[j + 1]
        gate = jnp.dot(xn, wu_ref[:, lo:hi], preferred_element_type=F32)
        up = jnp.dot(xn, wu_ref[:, D_FF + lo:D_FF + hi], preferred_element_type=F32)
        hid = (_silu(gate) * up).astype(BF16)
        return acc + jnp.dot(hid, wd_ref[lo:hi, :], preferred_element_type=F32)

    state = [None] * n_sub
    for i in range(n_sub + 1):
        if i < n_sub:
            x = x_ref[i * FF_SUB:(i + 1) * FF_SUB, :]
            xn = _rms(x, gf_ref[...]).astype(BF16)
            state[i] = (xn, block(xn, x, 0))
        if i > 0:
            xn, acc = state[i - 1]
            for j in range(1, FF_SPLIT):
                acc = block(xn, acc, j)
            o_ref[(i - 1) * FF_SUB:i * FF_SUB, :] = _rms(acc, gl_ref[...])


def _ffn(x2d, gf, wu, wd, gl, tm):
    t, d = x2d.shape
    return pl.pallas_call(
        _ffn_kernel,
        grid=(t // tm,),
        in_specs=[pl.BlockSpec((tm, d), lambda i: (i, 0)),
                  _resident((1, d)), _resident(wu.shape), _resident(wd.shape), _resident((1, d))],
        out_specs=pl.BlockSpec((tm, d), lambda i: (i, 0)),
        out_shape=jax.ShapeDtypeStruct((t, d), F32),
        compiler_params=_cparams(("arbitrary",)),
        name="ffn",
    )(x2d, gf, wu, wd, gl)


def _head_rows(v):
    col = jnp.zeros((TAIL, 1), F32).at[:v.shape[0], 0].set(v.astype(F32))
    return jnp.broadcast_to(col, (TAIL, LANE))


def kernel(x_prompt, x_sample, mem_prompt, state_conv_a, state_dn_conv, state_dn, cache_mem_k, cache_mem_v,
           norm_mix, w_in, conv_a_w, dn_conv_w, dn_a_log, dn_dt_bias, dn_norm, norm_mem, w_mem_kv, w_branch,
           w_o, norm_ffn, w_ffn_up, w_ffn_down, norm_final):
    bp, lp, d = x_prompt.shape
    bs, ls, _ = x_sample.shape
    assert norm_mix.shape[0] == 1 and ls == SEQ_S and lp % CHUNK == 0 and (bs * ls) % CHUNK == 0
    assert w_in.shape[2] == W1 + N_AB + W2

    w = w_in[0]
    w1 = w[:, :W1].astype(BF16)
    w2 = w[:, W1 + N_AB:].astype(BF16)
    wab = jnp.pad(w[:, W1:W1 + N_AB], ((0, 0), (0, LANE - N_AB))).astype(BF16)
    wb = w_branch[0].astype(BF16)
    wo = w_o[0].astype(BF16)
    wu = w_ffn_up[0].astype(BF16)
    wd = w_ffn_down[0].astype(BF16)
    wkv = w_mem_kv[0].astype(BF16)
    g_mix = norm_mix[0][None, :]
    g_ffn = norm_ffn[0][None, :]
    g_fin = norm_final[None, :]
    g_mem = norm_mem[0][None, :]
    caw = conv_a_w[0]
    dcw = dn_conv_w[0]
    alog_row = _head_rows(dn_a_log[0])
    dtb_row = _head_rows(dn_dt_bias[0])
    dnorm = dn_norm[0][None, :]

    tp = bp * lp
    mk, mv, mkb, mvb = _memkv(mem_prompt.reshape(bp * MEM_TOKENS, d), g_mem, wkv, TM)
    x1_p, ca_p, dc_p, s_p = _front_prompt(x_prompt, g_mix, w1, w2, wab, mkb.reshape(bp, MEM_TOKENS, XA_W),
                                          mvb.reshape(bp, MEM_TOKENS, XA_W), caw, dcw, alog_row, dtb_row, dnorm,
                                          wb, wo, NB_P)

    ts = bs * ls
    xs2 = x_sample.reshape(ts, d)
    p1_s, p2_s, pab_s = _proj(xs2, g_mix, w1, w2, wab, CHUNK)
    ea = jnp.pad(state_conv_a[0], ((0, 0), (0, ls - (CONV_A_K - 1)), (0, 0))).reshape(ts, A_W)
    eq = jnp.pad(state_dn_conv[0], ((0, 0), (0, ls - (DN_CONV_K - 1)), (0, 0))).reshape(ts, DN_CONV_CH)
    yad_s, u_s, s_s = _branch_sample(p1_s, pab_s, ea, eq, state_dn[0], caw, dcw, alog_row, dtb_row, dnorm)
    ym_s = _attn_sample(p2_s[:, :XA_W].reshape(bs, ls, XA_W),
                        cache_mem_k.reshape(bs, MEM_TOKENS * XA_HEADS, XA_DH),
                        cache_mem_v.reshape(bs, MEM_TOKENS * XA_HEADS, XA_DH), NB_ATTN_S)
    x1_s = _merge(yad_s, ym_s.reshape(ts, XA_W), p2_s, xs2, wb, wo, ts)
    ca_s = u_s.reshape(bs, ls, A_W)[:, ls - (CONV_A_K - 1):]
    dc_s = p1_s[:, OFF_QKV:OFF_Z].reshape(bs, ls, DN_CONV_CH)[:, ls - (DN_CONV_K - 1):]

    y_p = _ffn(x1_p.reshape(tp, d), g_ffn, wu, wd, g_fin, TM_FF).reshape(bp, lp, d)
    y_s = _ffn(x1_s, g_ffn, wu, wd, g_fin, ts).reshape(bs, ls, d)

    return (y_p, y_s, ca_p[None], dc_p[None], s_p[None],
            mk.reshape(1, bp, MEM_TOKENS, XA_HEADS, XA_DH), mv.reshape(1, bp, MEM_TOKENS, XA_HEADS, XA_DH),
            ca_s[None], dc_s[None], s_s[None])
```

```python
import functools

import jax
import jax.numpy as jnp
from jax import lax
from jax.experimental import pallas as pl
from jax.experimental.pallas import tpu as pltpu

F32 = jnp.float32
BF16 = jnp.bfloat16

D_MODEL = 1024
A_W = 512
CONV_A_K = 3
DN_HEADS = 4
DN_DK = 128
DN_DV = 128
DN_QK = DN_HEADS * DN_DK
DN_V = DN_HEADS * DN_DV
DN_CONV_CH = 2 * DN_QK + DN_V
DN_CONV_K = 4
MEM_TOKENS = 256
XA_HEADS = 4
XA_DH = 128
XA_W = XA_HEADS * XA_DH
D_FF = 2816
EPS = 1e-6

LANE = 128
CHUNK = 128
TAIL = 8
NB_P = 4
TM = 512
GATE_BLK = 256
PROJ_BLK = 512
ATTN_RESERVE = 4

W1 = 3 * A_W + DN_CONV_CH + DN_V
OFF_QKV = 3 * A_W
OFF_Z = OFF_QKV + DN_CONV_CH
W2 = XA_W + 3 * D_MODEL
OFF_G = XA_W
N_AB = 2 * DN_HEADS
P_Z = 3 * A_W
P_XQ = P_Z + DN_V
P_W = P_XQ + XA_W

VMEM_LIMIT = 60 * 1024 * 1024


def _cparams(sem):
    return pltpu.CompilerParams(dimension_semantics=sem, vmem_limit_bytes=VMEM_LIMIT)


def _resident(shape):
    return pl.BlockSpec(shape, lambda *_: (0,) * len(shape), pipeline_mode=pl.Buffered(1))


def _mm(a, b):
    return jnp.dot(a.astype(BF16), b.astype(BF16), preferred_element_type=F32)


def _mm_nt(a, b):
    return lax.dot_general(a.astype(BF16), b.astype(BF16), (((1,), (1,)), ((), ())),
                           preferred_element_type=F32)


def _mm_tn(a, b):
    return lax.dot_general(a.astype(BF16), b.astype(BF16), (((0,), (0,)), ((), ())),
                           preferred_element_type=F32)


def _rms(x, g):
    return x * lax.rsqrt(jnp.mean(x * x, axis=-1, keepdims=True) + EPS) * g


def _silu(x):
    return x * jax.nn.sigmoid(x)


def _softplus(x):
    return jnp.maximum(x, 0.0) + jnp.log1p(jnp.exp(-jnp.abs(x)))


def _memkv_kernel(x_ref, g_ref, w_ref, k_ref, v_ref, kb_ref, vb_ref):
    xn = _rms(x_ref[...], g_ref[...]).astype(BF16)
    kv = jnp.dot(xn, w_ref[...], preferred_element_type=F32)
    k = kv[:, :XA_W]
    v = kv[:, XA_W:]
    tm = k.shape[0]
    for h in range(XA_HEADS):
        rows = pl.ds(h, tm, stride=XA_HEADS)
        k_ref[rows, :] = k[:, h * LANE:(h + 1) * LANE]
        v_ref[rows, :] = v[:, h * LANE:(h + 1) * LANE]
    kb_ref[...] = k.astype(BF16)
    vb_ref[...] = v.astype(BF16)


def _memkv(mem2d, gain, w, tm):
    t, d = mem2d.shape
    blk = pl.BlockSpec((tm, XA_W), lambda i: (i, 0))
    blk_rows = pl.BlockSpec((tm * XA_HEADS, XA_DH), lambda i: (i, 0))
    return pl.pallas_call(
        _memkv_kernel,
        grid=(t // tm,),
        in_specs=[pl.BlockSpec((tm, d), lambda i: (i, 0)),
                  pl.BlockSpec((1, d), lambda i: (0, 0)),
                  pl.BlockSpec((d, 2 * XA_W), lambda i: (0, 0))],
        out_specs=[blk_rows, blk_rows, blk, blk],
        out_shape=[jax.ShapeDtypeStruct((t * XA_HEADS, XA_DH), F32),
                   jax.ShapeDtypeStruct((t * XA_HEADS, XA_DH), F32),
                   jax.ShapeDtypeStruct((t, XA_W), BF16), jax.ShapeDtypeStruct((t, XA_W), BF16)],
        compiler_params=_cparams(("arbitrary",)),
        name="memkv",
    )(mem2d, gain, w)


def _log2(n):
    return n.bit_length() - 1


def _dn_masks(seg):
    r = lax.broadcasted_iota(jnp.int32, (CHUNK, CHUNK), 0)
    c = lax.broadcasted_iota(jnp.int32, (CHUNK, CHUNK), 1)
    ls = _log2(seg)
    same = (r >> ls) == (c >> ls)
    base = min(8, seg)
    lb = _log2(base)
    m = {
        "causal": (same & (r >= c)).astype(F32),
        "strict": (same & (r > c)).astype(F32),
        "eye": (r == c).astype(F32),
        "neg_diag": -((r >> lb) == (c >> lb)).astype(F32),
        "off": {},
        "base": base,
    }
    s = base
    while s < seg:
        l1, l2 = _log2(s), _log2(2 * s)
        m["off"][s] = (((r >> l2) == (c >> l2)) & ((r >> l1) != (c >> l1))).astype(F32)
        s *= 2
    return m


def _each(f, *lists):
    return [f(*args) for args in zip(*lists)]


def _low_rows(x, s):
    return jnp.concatenate([x[i + s:i + 2 * s] for i in range(0, x.shape[0], 2 * s)], axis=0)


def _merge_low(x, low, s):
    parts = []
    for j, i in enumerate(range(0, x.shape[0], 2 * s)):
        parts += [x[i:i + s], low[j * s:(j + 1) * s]]
    return jnp.concatenate(parts, axis=0)


def _spread_low(low, s):
    return _merge_low(jnp.zeros((2 * low.shape[0], low.shape[1]), low.dtype), low, s)


def _tri_inv(a_list, m, seg, tick):
    add = lambda x, y: x + y
    b = _each(lambda a: a * m["neg_diag"], a_list)
    p = _each(lambda x: m["eye"] + x, b)
    b2 = _each(_mm, b, b)
    tick()
    p = _each(add, p, _each(_mm, p, b2))
    tick()
    if m["base"] == 8:
        b4 = _each(_mm, b2, b2)
        tick()
        p = _each(add, p, _each(_mm, p, b4))
        tick()
    s = m["base"]
    while s < seg:
        low = lambda t, s=s: _low_rows(t, s)
        x = _each(_mm, _each(lambda a, s=s: low(a) * low(m["off"][s]), a_list), p)
        tick()
        r = _each(_mm, _each(low, p), _each(lambda y, s=s: _spread_low(y, s), x))
        p = _each(lambda t, y, s=s: _merge_low(t, low(t) - y, s), p, r)
        tick()
        s *= 2
    return p


def _seg_scan(x, seg, reverse):
    n = x.shape[1]
    pos = lax.broadcasted_iota(jnp.int32, x.shape, 1) & (seg - 1)
    s = 1
    while s < seg:
        shifted = pltpu.roll(x, n - s if reverse else s, 1)
        x = x + jnp.where(pos < seg - s if reverse else pos >= s, shifted, 0.0)
        s *= 2
    return x


def _dn_gates(ab, alog8, dtb8, seg):
    abt = ab.T[0:TAIL]
    g = -jnp.exp(alog8) * _softplus(abt + dtb8)
    d = _seg_scan(g, seg, False)
    dl = d + _seg_scan(g, seg, True) - g
    beta = jax.nn.sigmoid(abt)
    pad = jnp.zeros((CHUNK - 3 * TAIL, CHUNK), F32)
    return d, dl, jnp.concatenate([d, dl, beta, pad], axis=0).T


def _l2n(x, scale=1.0):
    return x * (lax.rsqrt(jnp.sum(x * x, axis=-1, keepdims=True) + EPS) * scale)


def _dn_intra(q, k, v, d_col, d_row, beta_col, m, seg, tick=lambda: None):
    q = _each(lambda x: _l2n(x, DN_DK ** -0.5), q)
    k = _each(_l2n, k)
    gamma = _each(lambda dc, dr: jnp.exp((dc - dr) * m["causal"]) * m["causal"], d_col, d_row)
    kk = _each(_mm_nt, k, k)
    a = _each(lambda bc, x, g: (bc * x) * g * m["strict"], beta_col, kk, gamma)
    t = _tri_inv(a, m, seg, tick)
    rhs = _each(lambda vv, kx, bc, dc: jnp.concatenate([vv * bc, kx * (bc * jnp.exp(dc))], axis=1),
                v, k, beta_col, d_col)
    sol = _each(_mm, t, rhs)
    u = [x[:, :DN_DV] for x in sol]
    w = [x[:, DN_DV:] for x in sol]
    qk = _each(lambda x, g: x * g, _each(_mm_nt, q, k), gamma)
    return q, k, u, w, qk


def _dn_out(o, z, dnorm):
    return _rms(o, dnorm) * _silu(z)


def _head_lists(qkv, z, d, cols):
    out = [[] for _ in range(8)]
    for h in range(DN_HEADS):
        beta_lane = 2 * TAIL + DN_HEADS + h
        vals = (qkv[:, h * LANE:(h + 1) * LANE],
                qkv[:, DN_QK + h * LANE:DN_QK + (h + 1) * LANE],
                qkv[:, 2 * DN_QK + h * LANE:2 * DN_QK + (h + 1) * LANE],
                None if z is None else z[:, h * LANE:(h + 1) * LANE],
                cols[:, h:h + 1], d[h:h + 1, :], cols[:, TAIL + h:TAIL + h + 1],
                cols[:, beta_lane:beta_lane + 1])
        for lst, val in zip(out, vals):
            lst.append(val)
    return out


def _causal_conv(x, tail_ref, b, wts, width):
    c, w = x.shape
    tiles = jnp.concatenate([tail_ref[b][None], x.reshape(c // TAIL, TAIL, w)], axis=0)
    sub = lax.broadcasted_iota(jnp.int32, (1, TAIL, 1), 1)
    acc = None
    for i in range(width):
        s = width - 1 - i
        if s == 0:
            y = tiles[1:]
        else:
            r = pltpu.roll(tiles, s, 1)
            y = jnp.where(sub >= s, r[1:], r[:-1])
        term = wts[i:i + 1][None] * y
        acc = term if acc is None else acc + term
    tail_ref[b] = tiles[c // TAIL]
    return acc.reshape(c, w)


def _front_prompt_kernel(x_ref, gmix_ref, w1_ref, w2_ref, wab_ref, mk_ref, mv_ref, caw_ref, dcw_ref,
                         alog_ref, dtb_ref, dnorm_ref, wb_ref, wo_ref,
                         x1_ref, ca_ref, dc_ref, s_ref, utail, qtail, ybuf, pbuf, gbuf, *, nb):
    c = CHUNK
    rows = nb * c
    t_idx = pl.program_id(1)

    @pl.when(t_idx == 0)
    def _():
        utail[...] = jnp.zeros(utail.shape, F32)
        qtail[...] = jnp.zeros(qtail.shape, F32)
        s_ref[...] = jnp.zeros(s_ref.shape, F32)

    x = x_ref[...].reshape(rows, D_MODEL)
    xn = _rms(x, gmix_ref[...]).astype(BF16)
    proj = lambda w_ref, lo, hi: jnp.dot(xn, w_ref[:, lo:hi], preferred_element_type=F32)

    queue = []

    def enqueue(dst, w_ref, src, dst_lo, width, act=None):
        def run():
            r = proj(w_ref, src, src + width)
            dst[:, dst_lo:dst_lo + width] = r if act is None else act(r)
        queue.append(run)

    for lo in range(0, 3 * A_W, PROJ_BLK):
        enqueue(pbuf, w1_ref, lo, lo, PROJ_BLK)
    enqueue(pbuf, w1_ref, OFF_Z, P_Z, DN_V)
    enqueue(pbuf, w2_ref, 0, P_XQ, XA_W)
    for lo in range(0, 3 * D_MODEL, GATE_BLK):
        enqueue(gbuf, w2_ref, OFF_G + lo, lo, GATE_BLK, jax.nn.sigmoid)
    queue.reverse()

    def tick(keep=ATTN_RESERVE):
        if len(queue) > keep:
            queue.pop()()

    m = _dn_masks(c)
    caw = caw_ref[...]
    dcw = dcw_ref[...]
    dnorm = dnorm_ref[...]
    pq_all = proj(w1_ref, OFF_QKV, OFF_Z)
    pab_all = jnp.dot(xn, wab_ref[...], preferred_element_type=F32)

    qkv_l, gates_l = [], []
    for b in range(nb):
        rb = slice(b * c, (b + 1) * c)
        tick()
        qkv_in = pq_all[rb]
        dc_ref[b] = qkv_in[c - (DN_CONV_K - 1):]
        qkv_l.append(_silu(_causal_conv(qkv_in, qtail, b, dcw, DN_CONV_K)))
        gates_l.append(_dn_gates(pab_all[rb], alog_ref[...], dtb_ref[...], c))

    for b in range(nb):
        rb = slice(b * c, (b + 1) * c)
        tick()
        u_in = pbuf[rb, A_W:2 * A_W] * pbuf[rb, 2 * A_W:3 * A_W]
        ca_ref[b] = u_in[c - (CONV_A_K - 1):]
        ybuf[rb, 0:A_W] = (pbuf[rb, 0:A_W] * _causal_conv(u_in, utail, b, caw, CONV_A_K)).astype(BF16)

    lists = [[] for _ in range(8)]
    for b in range(nb):
        d, _, cols = gates_l[b]
        for lst, val in zip(lists, _head_lists(qkv_l[b], None, d, cols)):
            lst.extend(val)
    idx = [(b, h) for b in range(nb) for h in range(DN_HEADS)]
    q, k, v, _, d_col, d_row, dl_col, beta_col = lists
    q, k, u, w, qk = _dn_intra(q, k, v, d_col, d_row, beta_col, m, c, tick)
    s_old = [s_ref[b, h] for b, h in idx]
    vn = _each(lambda ux, wx, s: ux - _mm(wx, s), u, w, s_old)
    tick()
    o = _each(lambda qx, dc, y, s, vx: _mm(jnp.concatenate([qx * jnp.exp(dc), y], axis=1),
                                           jnp.concatenate([s, vx], axis=0)), q, d_col, qk, s_old, vn)
    tick()
    s_new = _each(lambda s, dlc, kx, dc, vx: s * jnp.exp(dlc[0:1, :]) + _mm_tn(kx * jnp.exp(dlc - dc), vx),
                  s_old, dl_col, k, d_col, vn)
    for (b, h), sx, ox in zip(idx, s_new, o):
        s_ref[b, h] = sx
        zx = pbuf[b * c:(b + 1) * c, P_Z + h * LANE:P_Z + (h + 1) * LANE]
        ybuf[b * c:(b + 1) * c, A_W + h * LANE:A_W + (h + 1) * LANE] = _dn_out(ox, zx, dnorm).astype(BF16)

    for b in range(nb):
        tick(0)
        heads = range(XA_HEADS)
        sc = [_mm_nt(pbuf[b * c:(b + 1) * c, P_XQ + h * LANE:P_XQ + (h + 1) * LANE],
                     mk_ref[b, :, h * LANE:(h + 1) * LANE]) * (XA_DH ** -0.5) for h in heads]
        e = _each(lambda x_: jnp.exp(x_ - jnp.max(x_, axis=-1, keepdims=True)), sc)
        inv = _each(lambda x_: 1.0 / jnp.sum(x_, axis=-1, keepdims=True), e)
        for h, ex, ix in zip(heads, e, inv):
            ybuf[b * c:(b + 1) * c, A_W + DN_V + h * LANE:A_W + DN_V + (h + 1) * LANE] = (
                _mm(ex, mv_ref[b, :, h * LANE:(h + 1) * LANE]) * ix).astype(BF16)
    while queue:
        tick(0)

    merged = None
    for j, (lo, hi) in enumerate(((0, A_W), (A_W, A_W + DN_V), (A_W + DN_V, A_W + DN_V + XA_W))):
        term = gbuf[:, j * D_MODEL:(j + 1) * D_MODEL] * jnp.dot(ybuf[:, lo:hi], wb_ref[lo:hi, :],
                                                                preferred_element_type=F32)
        merged = term if merged is None else merged + term
    x1 = x + jnp.dot(merged.astype(BF16), wo_ref[...], preferred_element_type=F32)
    x1_ref[...] = x1.reshape(nb, c, D_MODEL)


def _front_prompt(x, gmix, w1, w2, wab, mkb, mvb, caw, dcw, alog_row, dtb_row, dnorm, wb, wo, nb):
    bsz, length, d = x.shape
    c = CHUNK
    return pl.pallas_call(
        functools.partial(_front_prompt_kernel, nb=nb),
        grid=(bsz // nb, length // c),
        in_specs=[pl.BlockSpec((nb, c, d), lambda g, t: (g, t, 0)),
                  _resident((1, d)), _resident(w1.shape), _resident(w2.shape), _resident(wab.shape),
                  pl.BlockSpec((nb, MEM_TOKENS, XA_W), lambda g, t: (g, 0, 0), pipeline_mode=pl.Buffered(1)),
                  pl.BlockSpec((nb, MEM_TOKENS, XA_W), lambda g, t: (g, 0, 0), pipeline_mode=pl.Buffered(1)),
                  _resident((CONV_A_K, A_W)), _resident((DN_CONV_K, DN_CONV_CH)),
                  _resident((TAIL, LANE)), _resident((TAIL, LANE)), _resident((1, LANE)),
                  _resident(wb.shape), _resident(wo.shape)],
        out_specs=[pl.BlockSpec((nb, c, d), lambda g, t: (g, t, 0)),
                   pl.BlockSpec((nb, CONV_A_K - 1, A_W), lambda g, t: (g, 0, 0)),
                   pl.BlockSpec((nb, DN_CONV_K - 1, DN_CONV_CH), lambda g, t: (g, 0, 0)),
                   pl.BlockSpec((nb, DN_HEADS, DN_DK, DN_DV), lambda g, t: (g, 0, 0, 0))],
        out_shape=[jax.ShapeDtypeStruct((bsz, length, d), F32),
                   jax.ShapeDtypeStruct((bsz, CONV_A_K - 1, A_W), F32),
                   jax.ShapeDtypeStruct((bsz, DN_CONV_K - 1, DN_CONV_CH), F32),
                   jax.ShapeDtypeStruct((bsz, DN_HEADS, DN_DK, DN_DV), F32)],
        scratch_shapes=[pltpu.VMEM((nb, TAIL, A_W), F32), pltpu.VMEM((nb, TAIL, DN_CONV_CH), F32),
                        pltpu.VMEM((nb * c, A_W + DN_V + XA_W), BF16), pltpu.VMEM((nb * c, P_W), F32),
                        pltpu.VMEM((nb * c, 3 * d), F32)],
        compiler_params=_cparams(("arbitrary", "arbitrary")),
        name="front_prompt",
    )(x, gmix, w1, w2, wab, mkb, mvb, caw, dcw, alog_row, dtb_row, dnorm, wb, wo)


def _proj_kernel(x_ref, g_ref, w1_ref, w2_ref, wab_ref, p1_ref, p2_ref, pab_ref):
    xn = _rms(x_ref[...], g_ref[...]).astype(BF16)
    p1_ref[...] = jnp.dot(xn, w1_ref[...], preferred_element_type=F32)
    p2_ref[...] = jnp.dot(xn, w2_ref[...], preferred_element_type=F32)
    pab_ref[...] = jnp.dot(xn, wab_ref[...], preferred_element_type=F32)


def _proj(x2d, gain, w1, w2, wab, tm):
    t, d = x2d.shape
    row = lambda n: pl.BlockSpec((tm, n), lambda i: (i, 0))
    return pl.pallas_call(
        _proj_kernel,
        grid=(t // tm,),
        in_specs=[row(d), _resident((1, d)), _resident(w1.shape), _resident(w2.shape), _resident(wab.shape)],
        out_specs=[row(W1), row(W2), row(LANE)],
        out_shape=[jax.ShapeDtypeStruct((t, W1), F32), jax.ShapeDtypeStruct((t, W2), F32),
                   jax.ShapeDtypeStruct((t, LANE), F32)],
        compiler_params=_cparams(("arbitrary",)),
        name="proj",
    )(x2d, gain, w1, w2, wab)


SEQ_S = 4
NB_S = CHUNK // SEQ_S
NB_ATTN_S = 16
TM_PROJ_S = 256


def _seg_conv(x, e, wts, width, tmod):
    rows = x.shape[0]
    acc = None
    for i in range(width):
        s = width - 1 - i
        term = x if s == 0 else jnp.where(tmod >= s, pltpu.roll(x, s, 0), 0.0)
        if i < width - 1:
            hist = e if i == 0 else pltpu.roll(e, rows - i, 0)
            term = term + jnp.where(tmod < SEQ_S - i, hist, 0.0)
        term = wts[i:i + 1] * term
        acc = term if acc is None else acc + term
    return acc


def _branch_sample_kernel(pa_ref, pq_ref, pz_ref, pab_ref, ea_ref, eq_ref, s0_ref, caw_ref, dcw_ref,
                          alog_ref, dtb_ref, dnorm_ref, yad_ref, u_ref, s_ref):
    c = CHUNK
    m = _dn_masks(SEQ_S)
    tmod = lax.broadcasted_iota(jnp.int32, (c, 1), 0) & (SEQ_S - 1)

    pa = pa_ref[...]
    u_in = pa[:, A_W:2 * A_W] * pa[:, 2 * A_W:3 * A_W]
    u_ref[...] = u_in
    conv = _seg_conv(u_in, ea_ref[...], caw_ref[...], CONV_A_K, tmod)
    yad_ref[:, 0:A_W] = (pa[:, 0:A_W] * conv).astype(BF16)

    qkv = _silu(_seg_conv(pq_ref[...], eq_ref[...], dcw_ref[...], DN_CONV_K, tmod))
    d, dl, cols = _dn_gates(pab_ref[...], alog_ref[...], dtb_ref[...], SEQ_S)
    dec_t = jnp.exp(dl)
    dnorm = dnorm_ref[...]

    wide = NB_S * DN_DK
    er = lax.broadcasted_iota(jnp.int32, (c, wide), 0)
    ec = lax.broadcasted_iota(jnp.int32, (c, wide), 1)
    mexp = ((er >> 2) == (ec >> 7)).astype(F32)
    mexp2 = jnp.concatenate([mexp, mexp], axis=0)
    tr = lax.broadcasted_iota(jnp.int32, (wide, c), 0)
    tc = lax.broadcasted_iota(jnp.int32, (wide, c), 1)
    mexp_t = ((tr >> 7) == (tc >> 2)).astype(F32)

    q, k, v, z, d_col, d_row, dl_col, beta_col = _head_lists(qkv, pz_ref[...], d, cols)
    q, k, u, w, qk = _dn_intra(q, k, v, d_col, d_row, beta_col, m, SEQ_S)
    heads = list(range(DN_HEADS))
    s_old = [s0_ref[:, h].reshape(wide, DN_DV) for h in heads]
    x_exp = _each(lambda wx, qx, dc: jnp.concatenate([jnp.concatenate([wx, qx * jnp.exp(dc)], axis=0)] * NB_S,
                                                     axis=1) * mexp2, w, q, d_col)
    ws = _each(_mm, x_exp, s_old)
    vn = _each(lambda ux, x: ux - x[:c], u, ws)
    o = _each(lambda x, y, vx: x[c:] + _mm(y, vx), ws, qk, vn)
    k_exp = _each(lambda kx, dlc, dc: jnp.concatenate([(kx * jnp.exp(dlc - dc)).T] * NB_S, axis=0) * mexp_t,
                  k, dl_col, d_col)
    dec = [jnp.concatenate([jnp.broadcast_to(dec_t[h:h + 1, SEQ_S * b:SEQ_S * b + 1], (DN_DK, DN_DV))
                            for b in range(NB_S)], axis=0) for h in heads]
    s_new = _each(lambda s, dx, kx, vx: s * dx + _mm(kx, vx), s_old, dec, k_exp, vn)
    for h, sx, ox, zx in zip(heads, s_new, o, z):
        s_ref[:, h] = sx.reshape(NB_S, DN_DK, DN_DV)
        yad_ref[:, A_W + h * LANE:A_W + (h + 1) * LANE] = _dn_out(ox, zx, dnorm).astype(BF16)


def _branch_sample(p1, pab, ea, eq, state, caw, dcw, alog_row, dtb_row, dnorm):
    t = p1.shape[0]
    c = CHUNK
    full = lambda shape: pl.BlockSpec(shape, lambda i: (0,) * len(shape))
    return pl.pallas_call(
        _branch_sample_kernel,
        grid=(t // c,),
        in_specs=[pl.BlockSpec((c, 3 * A_W), lambda i: (i, 0)),
                  pl.BlockSpec((c, DN_CONV_CH), lambda i: (i, OFF_QKV // DN_CONV_CH)),
                  pl.BlockSpec((c, DN_V), lambda i: (i, OFF_Z // DN_V)),
                  pl.BlockSpec((c, LANE), lambda i: (i, 0)),
                  pl.BlockSpec((c, A_W), lambda i: (i, 0)),
                  pl.BlockSpec((c, DN_CONV_CH), lambda i: (i, 0)),
                  pl.BlockSpec((NB_S, DN_HEADS, DN_DK, DN_DV), lambda i: (i, 0, 0, 0)),
                  full((CONV_A_K, A_W)), full((DN_CONV_K, DN_CONV_CH)),
                  full((TAIL, LANE)), full((TAIL, LANE)), full((1, LANE))],
        out_specs=[pl.BlockSpec((c, A_W + DN_V), lambda i: (i, 0)),
                   pl.BlockSpec((c, A_W), lambda i: (i, 0)),
                   pl.BlockSpec((NB_S, DN_HEADS, DN_DK, DN_DV), lambda i: (i, 0, 0, 0))],
        out_shape=[jax.ShapeDtypeStruct((t, A_W + DN_V), BF16),
                   jax.ShapeDtypeStruct((t, A_W), F32),
                   jax.ShapeDtypeStruct(state.shape, F32)],
        compiler_params=_cparams(("arbitrary",)),
        name="branch_sample",
    )(p1, p1, p1, pab, ea, eq, state, caw, dcw, alog_row, dtb_row, dnorm)


def _attn_sample_kernel(q_ref, k_ref, v_ref, o_ref):
    for h in range(XA_HEADS):
        sl = slice(h * LANE, (h + 1) * LANE)
        rows = pl.ds(h, MEM_TOKENS, stride=XA_HEADS)
        q = q_ref[:, :, sl].astype(BF16)
        s = jnp.einsum("bqd,bkd->bqk", q, k_ref[:, rows, :].astype(BF16),
                       preferred_element_type=F32) * (XA_DH ** -0.5)
        e = jnp.exp(s - jnp.max(s, axis=-1, keepdims=True))
        p = e / jnp.sum(e, axis=-1, keepdims=True)
        o_ref[:, :, sl] = jnp.einsum("bqk,bkd->bqd", p.astype(BF16), v_ref[:, rows, :].astype(BF16),
                                     preferred_element_type=F32).astype(BF16)


def _attn_sample(q3, ck, cv, nb):
    bsz, length, _ = q3.shape
    return pl.pallas_call(
        _attn_sample_kernel,
        grid=(bsz // nb,),
        in_specs=[pl.BlockSpec((nb, length, XA_W), lambda i: (i, 0, 0)),
                  pl.BlockSpec((nb, MEM_TOKENS * XA_HEADS, XA_DH), lambda i: (i, 0, 0)),
                  pl.BlockSpec((nb, MEM_TOKENS * XA_HEADS, XA_DH), lambda i: (i, 0, 0))],
        out_specs=pl.BlockSpec((nb, length, XA_W), lambda i: (i, 0, 0)),
        out_shape=jax.ShapeDtypeStruct((bsz, length, XA_W), BF16),
        compiler_params=_cparams(("arbitrary",)),
        name="attn_sample",
    )(q3, ck, cv)


def _merge_kernel(yad_ref, ym_ref, p2_ref, x_ref, wb_ref, wo_ref, o_ref):
    yad = yad_ref[...]
    gate = lambda j: jax.nn.sigmoid(p2_ref[:, OFF_G + j * D_MODEL:OFF_G + (j + 1) * D_MODEL])
    merged = (gate(0) * jnp.dot(yad[:, :A_W], wb_ref[0:A_W, :], preferred_element_type=F32)
              + gate(1) * jnp.dot(yad[:, A_W:], wb_ref[A_W:A_W + DN_V, :], preferred_element_type=F32)
              + gate(2) * jnp.dot(ym_ref[...], wb_ref[A_W + DN_V:, :], preferred_element_type=F32))
    o_ref[...] = x_ref[...] + jnp.dot(merged.astype(BF16), wo_ref[...], preferred_element_type=F32)


def _merge(yad, ym, p2, x2d, wb, wo, tm):
    t, d = x2d.shape
    row = lambda n: pl.BlockSpec((tm, n), lambda i: (i, 0))
    return pl.pallas_call(
        _merge_kernel,
        grid=(t // tm,),
        in_specs=[row(A_W + DN_V), row(XA_W), row(W2), row(d), _resident(wb.shape), _resident(wo.shape)],
        out_specs=row(d),
        out_shape=jax.ShapeDtypeStruct((t, d), F32),
        compiler_params=_cparams(("arbitrary",)),
        name="merge",
    )(yad, ym, p2, x2d, wb, wo)


MXU_K = 256
FF_EDGES = (0, 6 * MXU_K, D_FF)
FF_SPLIT = len(FF_EDGES) - 1
FF_SUB = 512
TM_FF = 2 * FF_SUB


def _ffn_kernel(x_ref, gf_ref, wu_ref, wd_ref, gl_ref, o_ref):
    n_sub = x_ref.shape[0] // FF_SUB

    def block(xn, acc, j):
        lo, hi = FF_EDGES[j], FF_EDGES[j + 1]
        gate = jnp.dot(xn, wu_ref[:, lo:hi], preferred_element_type=F32)
        up = jnp.dot(xn, wu_ref[:, D_FF + lo:D_FF + hi], preferred_element_type=F32)
        hid = (_silu(gate) * up).astype(BF16)
        return acc + jnp.dot(hid, wd_ref[lo:hi, :], preferred_element_type=F32)

    state = [None] * n_sub
    for i in range(n_sub + 1):
        if i < n_sub:
            x = x_ref[i * FF_SUB:(i + 1) * FF_SUB, :]
            xn = _rms(x, gf_ref[...]).astype(BF16)
            state[i] = (xn, block(xn, x, 0))
        if i > 0:
            xn, acc = state[i - 1]
            for j in range(1, FF_SPLIT):
                acc = block(xn, acc, j)
            o_ref[(i - 1) * FF_SUB:i * FF_SUB, :] = _rms(acc, gl_ref[...])


def _ffn(x2d, gf, wu, wd, gl, tm):
    t, d = x2d.shape
    return pl.pallas_call(
        _ffn_kernel,
        grid=(t // tm,),
        in_specs=[pl.BlockSpec((tm, d), lambda i: (i, 0)),
                  _resident((1, d)), _resident(wu.shape), _resident(wd.shape), _resident((1, d))],
        out_specs=pl.BlockSpec((tm, d), lambda i: (i, 0)),
        out_shape=jax.ShapeDtypeStruct((t, d), F32),
        compiler_params=_cparams(("arbitrary",)),
        name="ffn",
    )(x2d, gf, wu, wd, gl)


def _head_rows(v):
    col = jnp.zeros((TAIL, 1), F32).at[:v.shape[0], 0].set(v.astype(F32))
    return jnp.broadcast_to(col, (TAIL, LANE))


def kernel(x_prompt, x_sample, mem_prompt, state_conv_a, state_dn_conv, state_dn, cache_mem_k, cache_mem_v,
           norm_mix, w_in, conv_a_w, dn_conv_w, dn_a_log, dn_dt_bias, dn_norm, norm_mem, w_mem_kv, w_branch,
           w_o, norm_ffn, w_ffn_up, w_ffn_down, norm_final):
    bp, lp, d = x_prompt.shape
    bs, ls, _ = x_sample.shape
    assert norm_mix.shape[0] == 1 and ls == SEQ_S and lp % CHUNK == 0 and (bs * ls) % CHUNK == 0
    assert w_in.shape[2] == W1 + N_AB + W2

    w = w_in[0]
    w1 = w[:, :W1].astype(BF16)
    w2 = w[:, W1 + N_AB:].astype(BF16)
    wab = jnp.pad(w[:, W1:W1 + N_AB], ((0, 0), (0, LANE - N_AB))).astype(BF16)
    wb = w_branch[0].astype(BF16)
    wo = w_o[0].astype(BF16)
    wu = w_ffn_up[0].astype(BF16)
    wd = w_ffn_down[0].astype(BF16)
    wkv = w_mem_kv[0].astype(BF16)
    g_mix = norm_mix[0][None, :]
    g_ffn = norm_ffn[0][None, :]
    g_fin = norm_final[None, :]
    g_mem = norm_mem[0][None, :]
    caw = conv_a_w[0]
    dcw = dn_conv_w[0]
    alog_row = _head_rows(dn_a_log[0])
    dtb_row = _head_rows(dn_dt_bias[0])
    dnorm = dn_norm[0][None, :]

    tp = bp * lp
    mk, mv, mkb, mvb = _memkv(mem_prompt.reshape(bp * MEM_TOKENS, d), g_mem, wkv, TM)
    x1_p, ca_p, dc_p, s_p = _front_prompt(x_prompt, g_mix, w1, w2, wab, mkb.reshape(bp, MEM_TOKENS, XA_W),
                                          mvb.reshape(bp, MEM_TOKENS, XA_W), caw, dcw, alog_row, dtb_row, dnorm,
                                          wb, wo, NB_P)

    ts = bs * ls
    xs2 = x_sample.reshape(ts, d)
    p1_s, p2_s, pab_s = _proj(xs2, g_mix, w1, w2, wab, TM_PROJ_S)
    ea = jnp.pad(state_conv_a[0], ((0, 0), (0, ls - (CONV_A_K - 1)), (0, 0))).reshape(ts, A_W)
    eq = jnp.pad(state_dn_conv[0], ((0, 0), (0, ls - (DN_CONV_K - 1)), (0, 0))).reshape(ts, DN_CONV_CH)
    yad_s, u_s, s_s = _branch_sample(p1_s, pab_s, ea, eq, state_dn[0], caw, dcw, alog_row, dtb_row, dnorm)
    ym_s = _attn_sample(p2_s[:, :XA_W].reshape(bs, ls, XA_W),
                        cache_mem_k.reshape(bs, MEM_TOKENS * XA_HEADS, XA_DH),
                        cache_mem_v.reshape(bs, MEM_TOKENS * XA_HEADS, XA_DH), NB_ATTN_S)
    x1_s = _merge(yad_s, ym_s.reshape(ts, XA_W), p2_s, xs2, wb, wo, ts)
    ca_s = u_s.reshape(bs, ls, A_W)[:, ls - (CONV_A_K - 1):]
    dc_s = p1_s[:, OFF_QKV:OFF_Z].reshape(bs, ls, DN_CONV_CH)[:, ls - (DN_CONV_K - 1):]

    y_p = _ffn(x1_p.reshape(tp, d), g_ffn, wu, wd, g_fin, TM_FF).reshape(bp, lp, d)
    y_s = _ffn(x1_s, g_ffn, wu, wd, g_fin, ts).reshape(bs, ls, d)

    return (y_p, y_s, ca_p[None], dc_p[None], s_p[None],
            mk.reshape(1, bp, MEM_TOKENS, XA_HEADS, XA_DH), mv.reshape(1, bp, MEM_TOKENS, XA_HEADS, XA_DH),
            ca_s[None], dc_s[None], s_s[None])
```

```python
import functools

import jax
import jax.numpy as jnp
from jax import lax
from jax.experimental import pallas as pl
from jax.experimental.pallas import tpu as pltpu

F32 = jnp.float32
BF16 = jnp.bfloat16

D_MODEL = 1024
A_W = 512
CONV_A_K = 3
DN_HEADS = 4
DN_DK = 128
DN_DV = 128
DN_QK = DN_HEADS * DN_DK
DN_V = DN_HEADS * DN_DV
DN_CONV_CH = 2 * DN_QK + DN_V
DN_CONV_K = 4
MEM_TOKENS = 256
XA_HEADS = 4
XA_DH = 128
XA_W = XA_HEADS * XA_DH
D_FF = 2816
EPS = 1e-6

LANE = 128
CHUNK = 128
TAIL = 8
NB_P = 4
TM = 512
GATE_BLK = 256
PROJ_BLK = 512
ATTN_RESERVE = 4

W1 = 3 * A_W + DN_CONV_CH + DN_V
OFF_QKV = 3 * A_W
OFF_Z = OFF_QKV + DN_CONV_CH
W2 = XA_W + 3 * D_MODEL
OFF_G = XA_W
N_AB = 2 * DN_HEADS
P_Z = 3 * A_W
P_XQ = P_Z + DN_V
P_W = P_XQ + XA_W

VMEM_LIMIT = 60 * 1024 * 1024


def _cparams(sem):
    return pltpu.CompilerParams(dimension_semantics=sem, vmem_limit_bytes=VMEM_LIMIT)


def _resident(shape):
    return pl.BlockSpec(shape, lambda *_: (0,) * len(shape), pipeline_mode=pl.Buffered(1))


def _mm(a, b):
    return jnp.dot(a.astype(BF16), b.astype(BF16), preferred_element_type=F32)


def _mm_nt(a, b):
    return lax.dot_general(a.astype(BF16), b.astype(BF16), (((1,), (1,)), ((), ())),
                           preferred_element_type=F32)


def _mm_tn(a, b):
    return lax.dot_general(a.astype(BF16), b.astype(BF16), (((0,), (0,)), ((), ())),
                           preferred_element_type=F32)


def _rms(x, g):
    return x * lax.rsqrt(jnp.mean(x * x, axis=-1, keepdims=True) + EPS) * g


def _silu(x):
    return x * jax.nn.sigmoid(x)


def _softplus(x):
    return jnp.maximum(x, 0.0) + jnp.log1p(jnp.exp(-jnp.abs(x)))


def _memkv_kernel(x_ref, g_ref, w_ref, k_ref, v_ref, kb_ref, vb_ref):
    xn = _rms(x_ref[...], g_ref[...]).astype(BF16)
    kv = jnp.dot(xn, w_ref[...], preferred_element_type=F32)
    k = kv[:, :XA_W]
    v = kv[:, XA_W:]
    tm = k.shape[0]
    for h in range(XA_HEADS):
        rows = pl.ds(h, tm, stride=XA_HEADS)
        k_ref[rows, :] = k[:, h * LANE:(h + 1) * LANE]
        v_ref[rows, :] = v[:, h * LANE:(h + 1) * LANE]
    kb_ref[...] = k.astype(BF16)
    vb_ref[...] = v.astype(BF16)


def _memkv(mem2d, gain, w, tm):
    t, d = mem2d.shape
    blk = pl.BlockSpec((tm, XA_W), lambda i: (i, 0))
    blk_rows = pl.BlockSpec((tm * XA_HEADS, XA_DH), lambda i: (i, 0))
    return pl.pallas_call(
        _memkv_kernel,
        grid=(t // tm,),
        in_specs=[pl.BlockSpec((tm, d), lambda i: (i, 0)),
                  pl.BlockSpec((1, d), lambda i: (0, 0)),
                  pl.BlockSpec((d, 2 * XA_W), lambda i: (0, 0))],
        out_specs=[blk_rows, blk_rows, blk, blk],
        out_shape=[jax.ShapeDtypeStruct((t * XA_HEADS, XA_DH), F32),
                   jax.ShapeDtypeStruct((t * XA_HEADS, XA_DH), F32),
                   jax.ShapeDtypeStruct((t, XA_W), BF16), jax.ShapeDtypeStruct((t, XA_W), BF16)],
        compiler_params=_cparams(("arbitrary",)),
        name="memkv",
    )(mem2d, gain, w)


def _log2(n):
    return n.bit_length() - 1


def _dn_masks(seg):
    r = lax.broadcasted_iota(jnp.int32, (CHUNK, CHUNK), 0)
    c = lax.broadcasted_iota(jnp.int32, (CHUNK, CHUNK), 1)
    ls = _log2(seg)
    same = (r >> ls) == (c >> ls)
    base = min(8, seg)
    lb = _log2(base)
    m = {
        "causal": (same & (r >= c)).astype(F32),
        "strict": (same & (r > c)).astype(F32),
        "eye": (r == c).astype(F32),
        "neg_diag": -((r >> lb) == (c >> lb)).astype(F32),
        "off": {},
        "base": base,
    }
    s = base
    while s < seg:
        l1, l2 = _log2(s), _log2(2 * s)
        m["off"][s] = (((r >> l2) == (c >> l2)) & ((r >> l1) != (c >> l1))).astype(F32)
        s *= 2
    return m


def _each(f, *lists):
    return [f(*args) for args in zip(*lists)]


def _low_rows(x, s):
    return jnp.concatenate([x[i + s:i + 2 * s] for i in range(0, x.shape[0], 2 * s)], axis=0)


def _merge_low(x, low, s):
    parts = []
    for j, i in enumerate(range(0, x.shape[0], 2 * s)):
        parts += [x[i:i + s], low[j * s:(j + 1) * s]]
    return jnp.concatenate(parts, axis=0)


def _spread_low(low, s):
    return _merge_low(jnp.zeros((2 * low.shape[0], low.shape[1]), low.dtype), low, s)


def _tri_inv(a_list, m, seg, tick):
    add = lambda x, y: x + y
    b = _each(lambda a: a * m["neg_diag"], a_list)
    p = _each(lambda x: m["eye"] + x, b)
    b2 = _each(_mm, b, b)
    tick()
    p = _each(add, p, _each(_mm, p, b2))
    tick()
    if m["base"] == 8:
        b4 = _each(_mm, b2, b2)
        tick()
        p = _each(add, p, _each(_mm, p, b4))
        tick()
    s = m["base"]
    while s < seg:
        low = lambda t, s=s: _low_rows(t, s)
        x = _each(_mm, _each(lambda a, s=s: low(a) * low(m["off"][s]), a_list), p)
        tick()
        r = _each(_mm, _each(low, p), _each(lambda y, s=s: _spread_low(y, s), x))
        p = _each(lambda t, y, s=s: _merge_low(t, low(t) - y, s), p, r)
        tick()
        s *= 2
    return p


def _seg_scan(x, seg, reverse):
    n = x.shape[1]
    pos = lax.broadcasted_iota(jnp.int32, x.shape, 1) & (seg - 1)
    s = 1
    while s < seg:
        shifted = pltpu.roll(x, n - s if reverse else s, 1)
        x = x + jnp.where(pos < seg - s if reverse else pos >= s, shifted, 0.0)
        s *= 2
    return x


def _dn_gates(ab, alog8, dtb8, seg):
    abt = ab.T[0:TAIL]
    g = -jnp.exp(alog8) * _softplus(abt + dtb8)
    d = _seg_scan(g, seg, False)
    dl = d + _seg_scan(g, seg, True) - g
    beta = jax.nn.sigmoid(abt)
    pad = jnp.zeros((CHUNK - 3 * TAIL, CHUNK), F32)
    return d, dl, jnp.concatenate([d, dl, beta, pad], axis=0).T


def _l2n(x, scale=1.0):
    return x * (lax.rsqrt(jnp.sum(x * x, axis=-1, keepdims=True) + EPS) * scale)


def _dn_intra(q, k, v, d_col, d_row, beta_col, m, seg, tick=lambda: None):
    q = _each(lambda x: _l2n(x, DN_DK ** -0.5), q)
    k = _each(_l2n, k)
    gamma = _each(lambda dc, dr: jnp.exp((dc - dr) * m["causal"]) * m["causal"], d_col, d_row)
    kk = _each(_mm_nt, k, k)
    a = _each(lambda bc, x, g: (bc * x) * g * m["strict"], beta_col, kk, gamma)
    t = _tri_inv(a, m, seg, tick)
    rhs = _each(lambda vv, kx, bc, dc: jnp.concatenate([vv * bc, kx * (bc * jnp.exp(dc))], axis=1),
                v, k, beta_col, d_col)
    sol = _each(_mm, t, rhs)
    u = [x[:, :DN_DV] for x in sol]
    w = [x[:, DN_DV:] for x in sol]
    qk = _each(lambda x, g: x * g, _each(_mm_nt, q, k), gamma)
    return q, k, u, w, qk


def _dn_out(o, z, dnorm):
    return _rms(o, dnorm) * _silu(z)


def _head_lists(qkv, z, d, cols):
    out = [[] for _ in range(8)]
    for h in range(DN_HEADS):
        beta_lane = 2 * TAIL + DN_HEADS + h
        vals = (qkv[:, h * LANE:(h + 1) * LANE],
                qkv[:, DN_QK + h * LANE:DN_QK + (h + 1) * LANE],
                qkv[:, 2 * DN_QK + h * LANE:2 * DN_QK + (h + 1) * LANE],
                None if z is None else z[:, h * LANE:(h + 1) * LANE],
                cols[:, h:h + 1], d[h:h + 1, :], cols[:, TAIL + h:TAIL + h + 1],
                cols[:, beta_lane:beta_lane + 1])
        for lst, val in zip(out, vals):
            lst.append(val)
    return out


def _causal_conv(x, tail_ref, b, wts, width):
    c, w = x.shape
    tiles = jnp.concatenate([tail_ref[b][None], x.reshape(c // TAIL, TAIL, w)], axis=0)
    sub = lax.broadcasted_iota(jnp.int32, (1, TAIL, 1), 1)
    acc = None
    for i in range(width):
        s = width - 1 - i
        if s == 0:
            y = tiles[1:]
        else:
            r = pltpu.roll(tiles, s, 1)
            y = jnp.where(sub >= s, r[1:], r[:-1])
        term = wts[i:i + 1][None] * y
        acc = term if acc is None else acc + term
    tail_ref[b] = tiles[c // TAIL]
    return acc.reshape(c, w)


def _front_prompt_kernel(x_ref, gmix_ref, w1_ref, w2_ref, wab_ref, mk_ref, mv_ref, caw_ref, dcw_ref,
                         alog_ref, dtb_ref, dnorm_ref, wb_ref, wo_ref,
                         x1_ref, ca_ref, dc_ref, s_ref, utail, qtail, ybuf, pbuf, gbuf, *, nb):
    c = CHUNK
    rows = nb * c
    t_idx = pl.program_id(1)

    @pl.when(t_idx == 0)
    def _():
        utail[...] = jnp.zeros(utail.shape, F32)
        qtail[...] = jnp.zeros(qtail.shape, F32)
        s_ref[...] = jnp.zeros(s_ref.shape, F32)

    x = x_ref[...].reshape(rows, D_MODEL)
    xn = _rms(x, gmix_ref[...]).astype(BF16)
    proj = lambda w_ref, lo, hi: jnp.dot(xn, w_ref[:, lo:hi], preferred_element_type=F32)

    queue = []

    def enqueue(dst, w_ref, src, dst_lo, width, act=None):
        def run():
            r = proj(w_ref, src, src + width)
            dst[:, dst_lo:dst_lo + width] = r if act is None else act(r)
        queue.append(run)

    for lo in range(0, 3 * A_W, PROJ_BLK):
        enqueue(pbuf, w1_ref, lo, lo, PROJ_BLK)
    enqueue(pbuf, w1_ref, OFF_Z, P_Z, DN_V)
    enqueue(pbuf, w2_ref, 0, P_XQ, XA_W)
    for lo in range(0, 3 * D_MODEL, GATE_BLK):
        enqueue(gbuf, w2_ref, OFF_G + lo, lo, GATE_BLK, jax.nn.sigmoid)
    queue.reverse()

    def tick(keep=ATTN_RESERVE):
        if len(queue) > keep:
            queue.pop()()

    m = _dn_masks(c)
    caw = caw_ref[...]
    dcw = dcw_ref[...]
    dnorm = dnorm_ref[...]
    pq_all = proj(w1_ref, OFF_QKV, OFF_Z)
    pab_all = jnp.dot(xn, wab_ref[...], preferred_element_type=F32)

    qkv_l, gates_l = [], []
    for b in range(nb):
        rb = slice(b * c, (b + 1) * c)
        tick()
        qkv_in = pq_all[rb]
        dc_ref[b] = qkv_in[c - (DN_CONV_K - 1):]
        qkv_l.append(_silu(_causal_conv(qkv_in, qtail, b, dcw, DN_CONV_K)))
        gates_l.append(_dn_gates(pab_all[rb], alog_ref[...], dtb_ref[...], c))

    for b in range(nb):
        rb = slice(b * c, (b + 1) * c)
        tick()
        u_in = pbuf[rb, A_W:2 * A_W] * pbuf[rb, 2 * A_W:3 * A_W]
        ca_ref[b] = u_in[c - (CONV_A_K - 1):]
        ybuf[rb, 0:A_W] = (pbuf[rb, 0:A_W] * _causal_conv(u_in, utail, b, caw, CONV_A_K)).astype(BF16)

    lists = [[] for _ in range(8)]
    for b in range(nb):
        d, _, cols = gates_l[b]
        for lst, val in zip(lists, _head_lists(qkv_l[b], None, d, cols)):
            lst.extend(val)
    idx = [(b, h) for b in range(nb) for h in range(DN_HEADS)]
    q, k, v, _, d_col, d_row, dl_col, beta_col = lists
    q, k, u, w, qk = _dn_intra(q, k, v, d_col, d_row, beta_col, m, c, tick)
    s_old = [s_ref[b, h] for b, h in idx]
    vn = _each(lambda ux, wx, s: ux - _mm(wx, s), u, w, s_old)
    tick()
    o = _each(lambda qx, dc, y, s, vx: _mm(jnp.concatenate([qx * jnp.exp(dc), y], axis=1),
                                           jnp.concatenate([s, vx], axis=0)), q, d_col, qk, s_old, vn)
    tick()
    s_new = _each(lambda s, dlc, kx, dc, vx: s * jnp.exp(dlc[0:1, :]) + _mm_tn(kx * jnp.exp(dlc - dc), vx),
                  s_old, dl_col, k, d_col, vn)
    for (b, h), sx, ox in zip(idx, s_new, o):
        s_ref[b, h] = sx
        zx = pbuf[b * c:(b + 1) * c, P_Z + h * LANE:P_Z + (h + 1) * LANE]
        ybuf[b * c:(b + 1) * c, A_W + h * LANE:A_W + (h + 1) * LANE] = _dn_out(ox, zx, dnorm).astype(BF16)

    for b in range(nb):
        tick(0)
        heads = range(XA_HEADS)
        sc = [_mm_nt(pbuf[b * c:(b + 1) * c, P_XQ + h * LANE:P_XQ + (h + 1) * LANE],
                     mk_ref[b, :, h * LANE:(h + 1) * LANE]) * (XA_DH ** -0.5) for h in heads]
        e = _each(lambda x_: jnp.exp(x_ - jnp.max(x_, axis=-1, keepdims=True)), sc)
        inv = _each(lambda x_: 1.0 / jnp.sum(x_, axis=-1, keepdims=True), e)
        for h, ex, ix in zip(heads, e, inv):
            ybuf[b * c:(b + 1) * c, A_W + DN_V + h * LANE:A_W + DN_V + (h + 1) * LANE] = (
                _mm(ex, mv_ref[b, :, h * LANE:(h + 1) * LANE]) * ix).astype(BF16)
    while queue:
        tick(0)

    merged = None
    for j, (lo, hi) in enumerate(((0, A_W), (A_W, A_W + DN_V), (A_W + DN_V, A_W + DN_V + XA_W))):
        term = gbuf[:, j * D_MODEL:(j + 1) * D_MODEL] * jnp.dot(ybuf[:, lo:hi], wb_ref[lo:hi, :],
                                                                preferred_element_type=F32)
        merged = term if merged is None else merged + term
    x1 = x + jnp.dot(merged.astype(BF16), wo_ref[...], preferred_element_type=F32)
    x1_ref[...] = x1.reshape(nb, c, D_MODEL)


def _front_prompt(x, gmix, w1, w2, wab, mkb, mvb, caw, dcw, alog_row, dtb_row, dnorm, wb, wo, nb):
    bsz, length, d = x.shape
    c = CHUNK
    return pl.pallas_call(
        functools.partial(_front_prompt_kernel, nb=nb),
        grid=(bsz // nb, length // c),
        in_specs=[pl.BlockSpec((nb, c, d), lambda g, t: (g, t, 0)),
                  _resident((1, d)), _resident(w1.shape), _resident(w2.shape), _resident(wab.shape),
                  pl.BlockSpec((nb, MEM_TOKENS, XA_W), lambda g, t: (g, 0, 0), pipeline_mode=pl.Buffered(1)),
                  pl.BlockSpec((nb, MEM_TOKENS, XA_W), lambda g, t: (g, 0, 0), pipeline_mode=pl.Buffered(1)),
                  _resident((CONV_A_K, A_W)), _resident((DN_CONV_K, DN_CONV_CH)),
                  _resident((TAIL, LANE)), _resident((TAIL, LANE)), _resident((1, LANE)),
                  _resident(wb.shape), _resident(wo.shape)],
        out_specs=[pl.BlockSpec((nb, c, d), lambda g, t: (g, t, 0)),
                   pl.BlockSpec((nb, CONV_A_K - 1, A_W), lambda g, t: (g, 0, 0)),
                   pl.BlockSpec((nb, DN_CONV_K - 1, DN_CONV_CH), lambda g, t: (g, 0, 0)),
                   pl.BlockSpec((nb, DN_HEADS, DN_DK, DN_DV), lambda g, t: (g, 0, 0, 0))],
        out_shape=[jax.ShapeDtypeStruct((bsz, length, d), F32),
                   jax.ShapeDtypeStruct((bsz, CONV_A_K - 1, A_W), F32),
                   jax.ShapeDtypeStruct((bsz, DN_CONV_K - 1, DN_CONV_CH), F32),
                   jax.ShapeDtypeStruct((bsz, DN_HEADS, DN_DK, DN_DV), F32)],
        scratch_shapes=[pltpu.VMEM((nb, TAIL, A_W), F32), pltpu.VMEM((nb, TAIL, DN_CONV_CH), F32),
                        pltpu.VMEM((nb * c, A_W + DN_V + XA_W), BF16), pltpu.VMEM((nb * c, P_W), F32),
                        pltpu.VMEM((nb * c, 3 * d), F32)],
        compiler_params=_cparams(("arbitrary", "arbitrary")),
        name="front_prompt",
    )(x, gmix, w1, w2, wab, mkb, mvb, caw, dcw, alog_row, dtb_row, dnorm, wb, wo)


def _proj_kernel(x_ref, g_ref, w1_ref, w2_ref, wab_ref, p1_ref, p2_ref, pab_ref):
    xn = _rms(x_ref[...], g_ref[...]).astype(BF16)
    p1_ref[...] = jnp.dot(xn, w1_ref[...], preferred_element_type=F32)
    p2_ref[...] = jnp.dot(xn, w2_ref[...], preferred_element_type=F32)
    pab_ref[...] = jnp.dot(xn, wab_ref[...], preferred_element_type=F32)


def _proj(x2d, gain, w1, w2, wab, tm):
    t, d = x2d.shape
    row = lambda n: pl.BlockSpec((tm, n), lambda i: (i, 0))
    return pl.pallas_call(
        _proj_kernel,
        grid=(t // tm,),
        in_specs=[row(d), _resident((1, d)), _resident(w1.shape), _resident(w2.shape), _resident(wab.shape)],
        out_specs=[row(W1), row(W2), row(LANE)],
        out_shape=[jax.ShapeDtypeStruct((t, W1), F32), jax.ShapeDtypeStruct((t, W2), F32),
                   jax.ShapeDtypeStruct((t, LANE), F32)],
        compiler_params=_cparams(("arbitrary",)),
        name="proj",
    )(x2d, gain, w1, w2, wab)


SEQ_S = 4
NB_S = CHUNK // SEQ_S
NB_ATTN_S = 16


def _seg_conv(x, e, wts, width):
    rows, w = x.shape
    xt = x.reshape(rows // TAIL, TAIL, w)
    et = e.reshape(rows // TAIL, TAIL, w)
    tmod = lax.broadcasted_iota(jnp.int32, (1, TAIL, 1), 1) & (SEQ_S - 1)
    acc = None
    for i in range(width):
        s = width - 1 - i
        term = xt if s == 0 else jnp.where(tmod >= s, pltpu.roll(xt, s, 1), 0.0)
        if i < width - 1:
            hist = et if i == 0 else pltpu.roll(et, TAIL - i, 1)
            term = term + jnp.where(tmod < SEQ_S - i, hist, 0.0)
        term = wts[i:i + 1][None] * term
        acc = term if acc is None else acc + term
    return acc.reshape(rows, w)


def _branch_sample_kernel(pa_ref, pq_ref, pz_ref, pab_ref, ea_ref, eq_ref, s0_ref, caw_ref, dcw_ref,
                          alog_ref, dtb_ref, dnorm_ref, mexp2_ref, mexpt_ref, yad_ref, u_ref, s_ref):
    c = CHUNK
    m = _dn_masks(SEQ_S)

    pa = pa_ref[...]
    u_in = pa[:, A_W:2 * A_W] * pa[:, 2 * A_W:3 * A_W]
    u_ref[...] = u_in
    conv = _seg_conv(u_in, ea_ref[...], caw_ref[...], CONV_A_K)
    yad_ref[:, 0:A_W] = (pa[:, 0:A_W] * conv).astype(BF16)

    qkv = _silu(_seg_conv(pq_ref[...], eq_ref[...], dcw_ref[...], DN_CONV_K))
    d, dl, cols = _dn_gates(pab_ref[...], alog_ref[...], dtb_ref[...], SEQ_S)
    dec_t = jnp.exp(dl)
    dnorm = dnorm_ref[...]

    wide = NB_S * DN_DK
    mexp2 = mexp2_ref[...]
    mexp_t = mexpt_ref[...]

    q, k, v, z, d_col, d_row, dl_col, beta_col = _head_lists(qkv, pz_ref[...], d, cols)
    q, k, u, w, qk = _dn_intra(q, k, v, d_col, d_row, beta_col, m, SEQ_S)
    heads = list(range(DN_HEADS))
    s_old = [s0_ref[:, h].reshape(wide, DN_DV) for h in heads]
    x_exp = _each(lambda wx, qx, dc: jnp.concatenate(
        [jnp.concatenate([wx, qx * jnp.exp(dc)], axis=0).astype(BF16)] * NB_S, axis=1) * mexp2, w, q, d_col)
    ws = _each(_mm, x_exp, s_old)
    vn = _each(lambda ux, x: ux - x[:c], u, ws)
    o = _each(lambda x, y, vx: x[c:] + _mm(y, vx), ws, qk, vn)
    k_exp = _each(lambda kx, dlc, dc: jnp.concatenate(
        [(kx * jnp.exp(dlc - dc)).T.astype(BF16)] * NB_S, axis=0) * mexp_t, k, dl_col, d_col)
    dec = [jnp.concatenate([jnp.broadcast_to(dec_t[h:h + 1, SEQ_S * b:SEQ_S * b + 1], (DN_DK, DN_DV))
                            for b in range(NB_S)], axis=0) for h in heads]
    s_new = _each(lambda s, dx, kx, vx: s * dx + _mm(kx, vx), s_old, dec, k_exp, vn)
    for h, sx, ox, zx in zip(heads, s_new, o, z):
        s_ref[:, h] = sx.reshape(NB_S, DN_DK, DN_DV)
        yad_ref[:, A_W + h * LANE:A_W + (h + 1) * LANE] = _dn_out(ox, zx, dnorm).astype(BF16)


def _branch_sample(p1, pab, ea, eq, state, caw, dcw, alog_row, dtb_row, dnorm):
    t = p1.shape[0]
    c = CHUNK
    full = lambda shape: pl.BlockSpec(shape, lambda i: (0,) * len(shape))
    wide = NB_S * DN_DK
    owner = jnp.arange(wide, dtype=jnp.int32) // DN_DK
    seq = (jnp.arange(2 * c, dtype=jnp.int32) % c) // SEQ_S
    mexp2 = (seq[:, None] == owner[None, :]).astype(BF16)
    mexp_t = (owner[:, None] == seq[None, :c]).astype(BF16)
    return pl.pallas_call(
        _branch_sample_kernel,
        grid=(t // c,),
        in_specs=[pl.BlockSpec((c, 3 * A_W), lambda i: (i, 0)),
                  pl.BlockSpec((c, DN_CONV_CH), lambda i: (i, OFF_QKV // DN_CONV_CH)),
                  pl.BlockSpec((c, DN_V), lambda i: (i, OFF_Z // DN_V)),
                  pl.BlockSpec((c, LANE), lambda i: (i, 0)),
                  pl.BlockSpec((c, A_W), lambda i: (i, 0)),
                  pl.BlockSpec((c, DN_CONV_CH), lambda i: (i, 0)),
                  pl.BlockSpec((NB_S, DN_HEADS, DN_DK, DN_DV), lambda i: (i, 0, 0, 0)),
                  full((CONV_A_K, A_W)), full((DN_CONV_K, DN_CONV_CH)),
                  full((TAIL, LANE)), full((TAIL, LANE)), full((1, LANE)),
                  _resident(mexp2.shape), _resident(mexp_t.shape)],
        out_specs=[pl.BlockSpec((c, A_W + DN_V), lambda i: (i, 0)),
                   pl.BlockSpec((c, A_W), lambda i: (i, 0)),
                   pl.BlockSpec((NB_S, DN_HEADS, DN_DK, DN_DV), lambda i: (i, 0, 0, 0))],
        out_shape=[jax.ShapeDtypeStruct((t, A_W + DN_V), BF16),
                   jax.ShapeDtypeStruct((t, A_W), F32),
                   jax.ShapeDtypeStruct(state.shape, F32)],
        compiler_params=_cparams(("arbitrary",)),
        name="branch_sample",
    )(p1, p1, p1, pab, ea, eq, state, caw, dcw, alog_row, dtb_row, dnorm, mexp2, mexp_t)


def _attn_sample_kernel(q_ref, k_ref, v_ref, o_ref):
    for h in range(XA_HEADS):
        sl = slice(h * LANE, (h + 1) * LANE)
        rows = pl.ds(h, MEM_TOKENS, stride=XA_HEADS)
        q = q_ref[:, :, sl].astype(BF16)
        s = jnp.einsum("bqd,bkd->bqk", q, k_ref[:, rows, :].astype(BF16),
                       preferred_element_type=F32) * (XA_DH ** -0.5)
        e = jnp.exp(s - jnp.max(s, axis=-1, keepdims=True))
        p = e / jnp.sum(e, axis=-1, keepdims=True)
        o_ref[:, :, sl] = jnp.einsum("bqk,bkd->bqd", p.astype(BF16), v_ref[:, rows, :].astype(BF16),
                                     preferred_element_type=F32).astype(BF16)


def _attn_sample(q3, ck, cv, nb):
    bsz, length, _ = q3.shape
    return pl.pallas_call(
        _attn_sample_kernel,
        grid=(bsz // nb,),
        in_specs=[pl.BlockSpec((nb, length, XA_W), lambda i: (i, 0, 0)),
                  pl.BlockSpec((nb, MEM_TOKENS * XA_HEADS, XA_DH), lambda i: (i, 0, 0)),
                  pl.BlockSpec((nb, MEM_TOKENS * XA_HEADS, XA_DH), lambda i: (i, 0, 0))],
        out_specs=pl.BlockSpec((nb, length, XA_W), lambda i: (i, 0, 0)),
        out_shape=jax.ShapeDtypeStruct((bsz, length, XA_W), BF16),
        compiler_params=_cparams(("arbitrary",)),
        name="attn_sample",
    )(q3, ck, cv)


def _merge_kernel(yad_ref, ym_ref, p2_ref, x_ref, wb_ref, wo_ref, o_ref):
    yad = yad_ref[...]
    gate = lambda j: jax.nn.sigmoid(p2_ref[:, OFF_G + j * D_MODEL:OFF_G + (j + 1) * D_MODEL])
    merged = (gate(0) * jnp.dot(yad[:, :A_W], wb_ref[0:A_W, :], preferred_element_type=F32)
              + gate(1) * jnp.dot(yad[:, A_W:], wb_ref[A_W:A_W + DN_V, :], preferred_element_type=F32)
              + gate(2) * jnp.dot(ym_ref[...], wb_ref[A_W + DN_V:, :], preferred_element_type=F32))
    o_ref[...] = x_ref[...] + jnp.dot(merged.astype(BF16), wo_ref[...], preferred_element_type=F32)


def _merge(yad, ym, p2, x2d, wb, wo, tm):
    t, d = x2d.shape
    row = lambda n: pl.BlockSpec((tm, n), lambda i: (i, 0))
    return pl.pallas_call(
        _merge_kernel,
        grid=(t // tm,),
        in_specs=[row(A_W + DN_V), row(XA_W), row(W2), row(d), _resident(wb.shape), _resident(wo.shape)],
        out_specs=row(d),
        out_shape=jax.ShapeDtypeStruct((t, d), F32),
        compiler_params=_cparams(("arbitrary",)),
        name="merge",
    )(yad, ym, p2, x2d, wb, wo)


MXU_K = 256
FF_EDGES = (0, 6 * MXU_K, D_FF)
FF_SPLIT = len(FF_EDGES) - 1
FF_SUB = 512
TM_FF = 2 * FF_SUB


def _ffn_kernel(x_ref, gf_ref, wu_ref, wd_ref, gl_ref, o_ref):
    n_sub = x_ref.shape[0] // FF_SUB

    def block(xn, acc, j):
        lo, hi = FF_EDGES[j], FF_EDGES[j + 1]
        gate = jnp.dot(xn, wu_ref[:, lo:hi], preferred_element_type=F32)
        up = jnp.dot(xn, wu_ref[:, D_FF + lo:D_FF + hi], preferred_element_type=F32)
        hid = (_silu(gate) * up).astype(BF16)
        return acc + jnp.dot(hid, wd_ref[lo:hi, :], preferred_element_type=F32)

    state = [None] * n_sub
    for i in range(n_sub + 1):
        if i < n_sub:
            x = x_ref[i * FF_SUB:(i + 1) * FF_SUB, :]
            xn = _rms(x, gf_ref[...]).astype(BF16)
            state[i] = (xn, block(xn, x, 0))
        if i > 0:
            xn, acc = state[i - 1]
            for j in range(1, FF_SPLIT):
                acc = block(xn, acc, j)
            o_ref[(i - 1) * FF_SUB:i * FF_SUB, :] = _rms(acc, gl_ref[...])


def _ffn(x2d, gf, wu, wd, gl, tm):
    t, d = x2d.shape
    return pl.pallas_call(
        _ffn_kernel,
        grid=(t // tm,),
        in_specs=[pl.BlockSpec((tm, d), lambda i: (i, 0)),
                  _resident((1, d)), _resident(wu.shape), _resident(wd.shape), _resident((1, d))],
        out_specs=pl.BlockSpec((tm, d), lambda i: (i, 0)),
        out_shape=jax.ShapeDtypeStruct((t, d), F32),
        compiler_params=_cparams(("arbitrary",)),
        name="ffn",
    )(x2d, gf, wu, wd, gl)


def _head_rows(v):
    col = jnp.zeros((TAIL, 1), F32).at[:v.shape[0], 0].set(v.astype(F32))
    return jnp.broadcast_to(col, (TAIL, LANE))


def kernel(x_prompt, x_sample, mem_prompt, state_conv_a, state_dn_conv, state_dn, cache_mem_k, cache_mem_v,
           norm_mix, w_in, conv_a_w, dn_conv_w, dn_a_log, dn_dt_bias, dn_norm, norm_mem, w_mem_kv, w_branch,
           w_o, norm_ffn, w_ffn_up, w_ffn_down, norm_final):
    bp, lp, d = x_prompt.shape
    bs, ls, _ = x_sample.shape
    assert norm_mix.shape[0] == 1 and ls == SEQ_S and lp % CHUNK == 0 and (bs * ls) % CHUNK == 0
    assert w_in.shape[2] == W1 + N_AB + W2

    w = w_in[0]
    w1 = w[:, :W1].astype(BF16)
    w2 = w[:, W1 + N_AB:].astype(BF16)
    wab = jnp.pad(w[:, W1:W1 + N_AB], ((0, 0), (0, LANE - N_AB))).astype(BF16)
    wb = w_branch[0].astype(BF16)
    wo = w_o[0].astype(BF16)
    wu = w_ffn_up[0].astype(BF16)
    wd = w_ffn_down[0].astype(BF16)
    wkv = w_mem_kv[0].astype(BF16)
    g_mix = norm_mix[0][None, :]
    g_ffn = norm_ffn[0][None, :]
    g_fin = norm_final[None, :]
    g_mem = norm_mem[0][None, :]
    caw = conv_a_w[0]
    dcw = dn_conv_w[0]
    alog_row = _head_rows(dn_a_log[0])
    dtb_row = _head_rows(dn_dt_bias[0])
    dnorm = dn_norm[0][None, :]

    tp = bp * lp
    mk, mv, mkb, mvb = _memkv(mem_prompt.reshape(bp * MEM_TOKENS, d), g_mem, wkv, TM)
    x1_p, ca_p, dc_p, s_p = _front_prompt(x_prompt, g_mix, w1, w2, wab, mkb.reshape(bp, MEM_TOKENS, XA_W),
                                          mvb.reshape(bp, MEM_TOKENS, XA_W), caw, dcw, alog_row, dtb_row, dnorm,
                                          wb, wo, NB_P)

    ts = bs * ls
    xs2 = x_sample.reshape(ts, d)
    p1_s, p2_s, pab_s = _proj(xs2, g_mix, w1, w2, wab, CHUNK)
    ea = jnp.pad(state_conv_a[0], ((0, 0), (0, ls - (CONV_A_K - 1)), (0, 0))).reshape(ts, A_W)
    eq = jnp.pad(state_dn_conv[0], ((0, 0), (0, ls - (DN_CONV_K - 1)), (0, 0))).reshape(ts, DN_CONV_CH)
    yad_s, u_s, s_s = _branch_sample(p1_s, pab_s, ea, eq, state_dn[0], caw, dcw, alog_row, dtb_row, dnorm)
    ym_s = _attn_sample(p2_s[:, :XA_W].reshape(bs, ls, XA_W),
                        cache_mem_k.reshape(bs, MEM_TOKENS * XA_HEADS, XA_DH),
                        cache_mem_v.reshape(bs, MEM_TOKENS * XA_HEADS, XA_DH), NB_ATTN_S)
    x1_s = _merge(yad_s, ym_s.reshape(ts, XA_W), p2_s, xs2, wb, wo, ts)
    ca_s = u_s.reshape(bs, ls, A_W)[:, ls - (CONV_A_K - 1):]
    dc_s = p1_s[:, OFF_QKV:OFF_Z].reshape(bs, ls, DN_CONV_CH)[:, ls - (DN_CONV_K - 1):]

    y_p = _ffn(x1_p.reshape(tp, d), g_ffn, wu, wd, g_fin, TM_FF).reshape(bp, lp, d)
    y_s = _ffn(x1_s, g_ffn, wu, wd, g_fin, ts).reshape(bs, ls, d)

    return (y_p, y_s, ca_p[None], dc_p[None], s_p[None],
            mk.reshape(1, bp, MEM_TOKENS, XA_HEADS, XA_DH), mv.reshape(1, bp, MEM_TOKENS, XA_HEADS, XA_DH),
            ca_s[None], dc_s[None], s_s[None])
```

```python
import functools

import jax
import jax.numpy as jnp
from jax import lax
from jax.experimental import pallas as pl
from jax.experimental.pallas import tpu as pltpu

F32 = jnp.float32
BF16 = jnp.bfloat16

D_MODEL = 1024
A_W = 512
CONV_A_K = 3
DN_HEADS = 4
DN_DK = 128
DN_DV = 128
DN_QK = DN_HEADS * DN_DK
DN_V = DN_HEADS * DN_DV
DN_CONV_CH = 2 * DN_QK + DN_V
DN_CONV_K = 4
MEM_TOKENS = 256
XA_HEADS = 4
XA_DH = 128
XA_W = XA_HEADS * XA_DH
D_FF = 2816
EPS = 1e-6

LANE = 128
CHUNK = 128
TAIL = 8
NB_P = 4
TM = 512
GATE_BLK = 256
PROJ_BLK = 512
ATTN_RESERVE = 4

W1 = 3 * A_W + DN_CONV_CH + DN_V
OFF_QKV = 3 * A_W
OFF_Z = OFF_QKV + DN_CONV_CH
W2 = XA_W + 3 * D_MODEL
OFF_G = XA_W
N_AB = 2 * DN_HEADS
P_Z = 3 * A_W
P_XQ = P_Z + DN_V
P_W = P_XQ + XA_W

VMEM_LIMIT = 60 * 1024 * 1024


def _cparams(sem):
    return pltpu.CompilerParams(dimension_semantics=sem, vmem_limit_bytes=VMEM_LIMIT)


def _resident(shape):
    return pl.BlockSpec(shape, lambda *_: (0,) * len(shape), pipeline_mode=pl.Buffered(1))


def _mm(a, b):
    return jnp.dot(a.astype(BF16), b.astype(BF16), preferred_element_type=F32)


def _mm_nt(a, b):
    return lax.dot_general(a.astype(BF16), b.astype(BF16), (((1,), (1,)), ((), ())),
                           preferred_element_type=F32)


def _mm_tn(a, b):
    return lax.dot_general(a.astype(BF16), b.astype(BF16), (((0,), (0,)), ((), ())),
                           preferred_element_type=F32)


def _rms(x, g):
    return x * lax.rsqrt(jnp.mean(x * x, axis=-1, keepdims=True) + EPS) * g


def _silu(x):
    return x * jax.nn.sigmoid(x)


def _softplus(x):
    return jnp.maximum(x, 0.0) + jnp.log1p(jnp.exp(-jnp.abs(x)))


def _memkv_kernel(x_ref, g_ref, w_ref, k_ref, v_ref, kb_ref, vb_ref):
    xn = _rms(x_ref[...], g_ref[...]).astype(BF16)
    kv = jnp.dot(xn, w_ref[...], preferred_element_type=F32)
    k = kv[:, :XA_W]
    v = kv[:, XA_W:]
    tm = k.shape[0]
    for h in range(XA_HEADS):
        rows = pl.ds(h, tm, stride=XA_HEADS)
        k_ref[rows, :] = k[:, h * LANE:(h + 1) * LANE]
        v_ref[rows, :] = v[:, h * LANE:(h + 1) * LANE]
    kb_ref[...] = k.astype(BF16)
    vb_ref[...] = v.astype(BF16)


def _memkv(mem2d, gain, w, tm):
    t, d = mem2d.shape
    blk = pl.BlockSpec((tm, XA_W), lambda i: (i, 0))
    blk_rows = pl.BlockSpec((tm * XA_HEADS, XA_DH), lambda i: (i, 0))
    return pl.pallas_call(
        _memkv_kernel,
        grid=(t // tm,),
        in_specs=[pl.BlockSpec((tm, d), lambda i: (i, 0)),
                  pl.BlockSpec((1, d), lambda i: (0, 0)),
                  pl.BlockSpec((d, 2 * XA_W), lambda i: (0, 0))],
        out_specs=[blk_rows, blk_rows, blk, blk],
        out_shape=[jax.ShapeDtypeStruct((t * XA_HEADS, XA_DH), F32),
                   jax.ShapeDtypeStruct((t * XA_HEADS, XA_DH), F32),
                   jax.ShapeDtypeStruct((t, XA_W), BF16), jax.ShapeDtypeStruct((t, XA_W), BF16)],
        compiler_params=_cparams(("arbitrary",)),
        name="memkv",
    )(mem2d, gain, w)


def _log2(n):
    return n.bit_length() - 1


def _dn_masks(seg):
    r = lax.broadcasted_iota(jnp.int32, (CHUNK, CHUNK), 0)
    c = lax.broadcasted_iota(jnp.int32, (CHUNK, CHUNK), 1)
    ls = _log2(seg)
    same = (r >> ls) == (c >> ls)
    base = min(8, seg)
    lb = _log2(base)
    m = {
        "causal": (same & (r >= c)).astype(F32),
        "strict": (same & (r > c)).astype(F32),
        "eye": (r == c).astype(F32),
        "neg_diag": -((r >> lb) == (c >> lb)).astype(F32),
        "off": {},
        "base": base,
    }
    s = base
    while s < seg:
        l1, l2 = _log2(s), _log2(2 * s)
        m["off"][s] = (((r >> l2) == (c >> l2)) & ((r >> l1) != (c >> l1))).astype(F32)
        s *= 2
    return m


def _each(f, *lists):
    return [f(*args) for args in zip(*lists)]


def _low_rows(x, s):
    return jnp.concatenate([x[i + s:i + 2 * s] for i in range(0, x.shape[0], 2 * s)], axis=0)


def _merge_low(x, low, s):
    parts = []
    for j, i in enumerate(range(0, x.shape[0], 2 * s)):
        parts += [x[i:i + s], low[j * s:(j + 1) * s]]
    return jnp.concatenate(parts, axis=0)


def _spread_low(low, s):
    return _merge_low(jnp.zeros((2 * low.shape[0], low.shape[1]), low.dtype), low, s)


def _tri_inv(a_list, m, seg, tick):
    add = lambda x, y: x + y
    b = _each(lambda a: a * m["neg_diag"], a_list)
    p = _each(lambda x: m["eye"] + x, b)
    b2 = _each(_mm, b, b)
    tick()
    p = _each(add, p, _each(_mm, p, b2))
    tick()
    if m["base"] == 8:
        b4 = _each(_mm, b2, b2)
        tick()
        p = _each(add, p, _each(_mm, p, b4))
        tick()
    s = m["base"]
    while s < seg:
        low = lambda t, s=s: _low_rows(t, s)
        x = _each(_mm, _each(lambda a, s=s: low(a) * low(m["off"][s]), a_list), p)
        tick()
        r = _each(_mm, _each(low, p), _each(lambda y, s=s: _spread_low(y, s), x))
        p = _each(lambda t, y, s=s: _merge_low(t, low(t) - y, s), p, r)
        tick()
        s *= 2
    return p


def _seg_scan(x, seg, reverse):
    n = x.shape[1]
    pos = lax.broadcasted_iota(jnp.int32, x.shape, 1) & (seg - 1)
    s = 1
    while s < seg:
        shifted = pltpu.roll(x, n - s if reverse else s, 1)
        x = x + jnp.where(pos < seg - s if reverse else pos >= s, shifted, 0.0)
        s *= 2
    return x


def _dn_gates(ab, alog8, dtb8, seg):
    abt = ab.T[0:TAIL]
    g = -jnp.exp(alog8) * _softplus(abt + dtb8)
    d = _seg_scan(g, seg, False)
    dl = d + _seg_scan(g, seg, True) - g
    beta = jax.nn.sigmoid(abt)
    pad = jnp.zeros((CHUNK - 3 * TAIL, CHUNK), F32)
    return d, dl, jnp.concatenate([d, dl, beta, pad], axis=0).T


def _l2n(x, scale=1.0):
    return x * (lax.rsqrt(jnp.sum(x * x, axis=-1, keepdims=True) + EPS) * scale)


def _dn_intra(q, k, v, d_col, d_row, beta_col, m, seg, tick=lambda: None):
    q = _each(lambda x: _l2n(x, DN_DK ** -0.5), q)
    k = _each(_l2n, k)
    gamma = _each(lambda dc, dr: jnp.exp((dc - dr) * m["causal"]) * m["causal"], d_col, d_row)
    kk = _each(_mm_nt, k, k)
    a = _each(lambda bc, x, g: (bc * x) * g * m["strict"], beta_col, kk, gamma)
    t = _tri_inv(a, m, seg, tick)
    rhs = _each(lambda vv, kx, bc, dc: jnp.concatenate([vv * bc, kx * (bc * jnp.exp(dc))], axis=1),
                v, k, beta_col, d_col)
    sol = _each(_mm, t, rhs)
    u = [x[:, :DN_DV] for x in sol]
    w = [x[:, DN_DV:] for x in sol]
    qk = _each(lambda x, g: x * g, _each(_mm_nt, q, k), gamma)
    return q, k, u, w, qk


def _dn_out(o, z, dnorm):
    return _rms(o, dnorm) * _silu(z)


def _head_lists(qkv, z, d, cols):
    out = [[] for _ in range(8)]
    for h in range(DN_HEADS):
        beta_lane = 2 * TAIL + DN_HEADS + h
        vals = (qkv[:, h * LANE:(h + 1) * LANE],
                qkv[:, DN_QK + h * LANE:DN_QK + (h + 1) * LANE],
                qkv[:, 2 * DN_QK + h * LANE:2 * DN_QK + (h + 1) * LANE],
                None if z is None else z[:, h * LANE:(h + 1) * LANE],
                cols[:, h:h + 1], d[h:h + 1, :], cols[:, TAIL + h:TAIL + h + 1],
                cols[:, beta_lane:beta_lane + 1])
        for lst, val in zip(out, vals):
            lst.append(val)
    return out


def _causal_conv(x, tail_ref, b, wts, width):
    c, w = x.shape
    tiles = jnp.concatenate([tail_ref[b][None], x.reshape(c // TAIL, TAIL, w)], axis=0)
    sub = lax.broadcasted_iota(jnp.int32, (1, TAIL, 1), 1)
    acc = None
    for i in range(width):
        s = width - 1 - i
        if s == 0:
            y = tiles[1:]
        else:
            r = pltpu.roll(tiles, s, 1)
            y = jnp.where(sub >= s, r[1:], r[:-1])
        term = wts[i:i + 1][None] * y
        acc = term if acc is None else acc + term
    tail_ref[b] = tiles[c // TAIL]
    return acc.reshape(c, w)


def _front_prompt_kernel(x_ref, gmix_ref, w1_ref, w2_ref, wab_ref, mk_ref, mv_ref, caw_ref, dcw_ref,
                         alog_ref, dtb_ref, dnorm_ref, wb_ref, wo_ref,
                         x1_ref, ca_ref, dc_ref, s_ref, utail, qtail, ybuf, pbuf, gbuf, *, nb):
    c = CHUNK
    rows = nb * c
    t_idx = pl.program_id(1)

    @pl.when(t_idx == 0)
    def _():
        utail[...] = jnp.zeros(utail.shape, F32)
        qtail[...] = jnp.zeros(qtail.shape, F32)
        s_ref[...] = jnp.zeros(s_ref.shape, F32)

    x = x_ref[...].reshape(rows, D_MODEL)
    xn = _rms(x, gmix_ref[...]).astype(BF16)
    proj = lambda w_ref, lo, hi: jnp.dot(xn, w_ref[:, lo:hi], preferred_element_type=F32)

    queue = []

    def enqueue(dst, w_ref, src, dst_lo, width, act=None):
        def run():
            r = proj(w_ref, src, src + width)
            dst[:, dst_lo:dst_lo + width] = r if act is None else act(r)
        queue.append(run)

    for lo in range(0, 3 * A_W, PROJ_BLK):
        enqueue(pbuf, w1_ref, lo, lo, PROJ_BLK)
    enqueue(pbuf, w1_ref, OFF_Z, P_Z, DN_V)
    enqueue(pbuf, w2_ref, 0, P_XQ, XA_W)
    for lo in range(0, 3 * D_MODEL, GATE_BLK):
        enqueue(gbuf, w2_ref, OFF_G + lo, lo, GATE_BLK, jax.nn.sigmoid)
    queue.reverse()

    def tick(keep=ATTN_RESERVE):
        if len(queue) > keep:
            queue.pop()()

    m = _dn_masks(c)
    caw = caw_ref[...]
    dcw = dcw_ref[...]
    dnorm = dnorm_ref[...]
    pq_all = proj(w1_ref, OFF_QKV, OFF_Z)
    pab_all = jnp.dot(xn, wab_ref[...], preferred_element_type=F32)

    qkv_l, gates_l = [], []
    for b in range(nb):
        rb = slice(b * c, (b + 1) * c)
        tick()
        qkv_in = pq_all[rb]
        dc_ref[b] = qkv_in[c - (DN_CONV_K - 1):]
        qkv_l.append(_silu(_causal_conv(qkv_in, qtail, b, dcw, DN_CONV_K)))
        gates_l.append(_dn_gates(pab_all[rb], alog_ref[...], dtb_ref[...], c))

    for b in range(nb):
        rb = slice(b * c, (b + 1) * c)
        tick()
        u_in = pbuf[rb, A_W:2 * A_W] * pbuf[rb, 2 * A_W:3 * A_W]
        ca_ref[b] = u_in[c - (CONV_A_K - 1):]
        ybuf[rb, 0:A_W] = (pbuf[rb, 0:A_W] * _causal_conv(u_in, utail, b, caw, CONV_A_K)).astype(BF16)

    lists = [[] for _ in range(8)]
    for b in range(nb):
        d, _, cols = gates_l[b]
        for lst, val in zip(lists, _head_lists(qkv_l[b], None, d, cols)):
            lst.extend(val)
    idx = [(b, h) for b in range(nb) for h in range(DN_HEADS)]
    q, k, v, _, d_col, d_row, dl_col, beta_col = lists
    q, k, u, w, qk = _dn_intra(q, k, v, d_col, d_row, beta_col, m, c, tick)
    s_old = [s_ref[b, h] for b, h in idx]
    vn = _each(lambda ux, wx, s: ux - _mm(wx, s), u, w, s_old)
    tick()
    o = _each(lambda qx, dc, y, s, vx: _mm(jnp.concatenate([qx * jnp.exp(dc), y], axis=1),
                                           jnp.concatenate([s, vx], axis=0)), q, d_col, qk, s_old, vn)
    tick()
    s_new = _each(lambda s, dlc, kx, dc, vx: s * jnp.exp(dlc[0:1, :]) + _mm_tn(kx * jnp.exp(dlc - dc), vx),
                  s_old, dl_col, k, d_col, vn)
    for (b, h), sx, ox in zip(idx, s_new, o):
        s_ref[b, h] = sx
        zx = pbuf[b * c:(b + 1) * c, P_Z + h * LANE:P_Z + (h + 1) * LANE]
        ybuf[b * c:(b + 1) * c, A_W + h * LANE:A_W + (h + 1) * LANE] = _dn_out(ox, zx, dnorm).astype(BF16)

    for b in range(nb):
        tick(0)
        heads = range(XA_HEADS)
        sc = [_mm_nt(pbuf[b * c:(b + 1) * c, P_XQ + h * LANE:P_XQ + (h + 1) * LANE],
                     mk_ref[b, :, h * LANE:(h + 1) * LANE]) * (XA_DH ** -0.5) for h in heads]
        e = _each(lambda x_: jnp.exp(x_ - jnp.max(x_, axis=-1, keepdims=True)), sc)
        inv = _each(lambda x_: 1.0 / jnp.sum(x_, axis=-1, keepdims=True), e)
        for h, ex, ix in zip(heads, e, inv):
            ybuf[b * c:(b + 1) * c, A_W + DN_V + h * LANE:A_W + DN_V + (h + 1) * LANE] = (
                _mm(ex, mv_ref[b, :, h * LANE:(h + 1) * LANE]) * ix).astype(BF16)
    while queue:
        tick(0)

    merged = None
    for j, (lo, hi) in enumerate(((0, A_W), (A_W, A_W + DN_V), (A_W + DN_V, A_W + DN_V + XA_W))):
        term = gbuf[:, j * D_MODEL:(j + 1) * D_MODEL] * jnp.dot(ybuf[:, lo:hi], wb_ref[lo:hi, :],
                                                                preferred_element_type=F32)
        merged = term if merged is None else merged + term
    x1 = x + jnp.dot(merged.astype(BF16), wo_ref[...], preferred_element_type=F32)
    x1_ref[...] = x1.reshape(nb, c, D_MODEL)


def _front_prompt(x, gmix, w1, w2, wab, mkb, mvb, caw, dcw, alog_row, dtb_row, dnorm, wb, wo, nb):
    bsz, length, d = x.shape
    c = CHUNK
    return pl.pallas_call(
        functools.partial(_front_prompt_kernel, nb=nb),
        grid=(bsz // nb, length // c),
        in_specs=[pl.BlockSpec((nb, c, d), lambda g, t: (g, t, 0)),
                  _resident((1, d)), _resident(w1.shape), _resident(w2.shape), _resident(wab.shape),
                  pl.BlockSpec((nb, MEM_TOKENS, XA_W), lambda g, t: (g, 0, 0), pipeline_mode=pl.Buffered(1)),
                  pl.BlockSpec((nb, MEM_TOKENS, XA_W), lambda g, t: (g, 0, 0), pipeline_mode=pl.Buffered(1)),
                  _resident((CONV_A_K, A_W)), _resident((DN_CONV_K, DN_CONV_CH)),
                  _resident((TAIL, LANE)), _resident((TAIL, LANE)), _resident((1, LANE)),
                  _resident(wb.shape), _resident(wo.shape)],
        out_specs=[pl.BlockSpec((nb, c, d), lambda g, t: (g, t, 0)),
                   pl.BlockSpec((nb, CONV_A_K - 1, A_W), lambda g, t: (g, 0, 0)),
                   pl.BlockSpec((nb, DN_CONV_K - 1, DN_CONV_CH), lambda g, t: (g, 0, 0)),
                   pl.BlockSpec((nb, DN_HEADS, DN_DK, DN_DV), lambda g, t: (g, 0, 0, 0))],
        out_shape=[jax.ShapeDtypeStruct((bsz, length, d), F32),
                   jax.ShapeDtypeStruct((bsz, CONV_A_K - 1, A_W), F32),
                   jax.ShapeDtypeStruct((bsz, DN_CONV_K - 1, DN_CONV_CH), F32),
                   jax.ShapeDtypeStruct((bsz, DN_HEADS, DN_DK, DN_DV), F32)],
        scratch_shapes=[pltpu.VMEM((nb, TAIL, A_W), F32), pltpu.VMEM((nb, TAIL, DN_CONV_CH), F32),
                        pltpu.VMEM((nb * c, A_W + DN_V + XA_W), BF16), pltpu.VMEM((nb * c, P_W), F32),
                        pltpu.VMEM((nb * c, 3 * d), F32)],
        compiler_params=_cparams(("arbitrary", "arbitrary")),
        name="front_prompt",
    )(x, gmix, w1, w2, wab, mkb, mvb, caw, dcw, alog_row, dtb_row, dnorm, wb, wo)


def _proj_kernel(x_ref, g_ref, w1_ref, w2_ref, wab_ref, p1_ref, p2_ref, pab_ref):
    xn = _rms(x_ref[...], g_ref[...]).astype(BF16)
    p1_ref[...] = jnp.dot(xn, w1_ref[...], preferred_element_type=F32)
    p2_ref[...] = jnp.dot(xn, w2_ref[...], preferred_element_type=F32)
    pab_ref[...] = jnp.dot(xn, wab_ref[...], preferred_element_type=F32)


def _proj(x2d, gain, w1, w2, wab, tm):
    t, d = x2d.shape
    row = lambda n: pl.BlockSpec((tm, n), lambda i: (i, 0))
    return pl.pallas_call(
        _proj_kernel,
        grid=(t // tm,),
        in_specs=[row(d), _resident((1, d)), _resident(w1.shape), _resident(w2.shape), _resident(wab.shape)],
        out_specs=[row(W1), row(W2), row(LANE)],
        out_shape=[jax.ShapeDtypeStruct((t, W1), F32), jax.ShapeDtypeStruct((t, W2), F32),
                   jax.ShapeDtypeStruct((t, LANE), F32)],
        compiler_params=_cparams(("arbitrary",)),
        name="proj",
    )(x2d, gain, w1, w2, wab)


SEQ_S = 4
NB_S = CHUNK // SEQ_S
NB_ATTN_S = 16


def _seg_conv(x, e, wts, width):
    rows, w = x.shape
    xt = x.reshape(rows // TAIL, TAIL, w)
    et = e.reshape(rows // TAIL, TAIL, w)
    tmod = lax.broadcasted_iota(jnp.int32, (1, TAIL, 1), 1) & (SEQ_S - 1)
    acc = None
    for i in range(width):
        s = width - 1 - i
        term = xt if s == 0 else jnp.where(tmod >= s, pltpu.roll(xt, s, 1), 0.0)
        if i < width - 1:
            hist = et if i == 0 else pltpu.roll(et, TAIL - i, 1)
            term = term + jnp.where(tmod < SEQ_S - i, hist, 0.0)
        term = wts[i:i + 1][None] * term
        acc = term if acc is None else acc + term
    return acc.reshape(rows, w)


def _branch_sample_kernel(pa_ref, pq_ref, pz_ref, pab_ref, ea_ref, eq_ref, s0_ref, caw_ref, dcw_ref,
                          alog_ref, dtb_ref, dnorm_ref, mexp2_ref, mexpt_ref, yad_ref, u_ref, s_ref):
    c = CHUNK
    m = _dn_masks(SEQ_S)

    pa = pa_ref[...]
    u_in = pa[:, A_W:2 * A_W] * pa[:, 2 * A_W:3 * A_W]
    u_ref[...] = u_in
    conv = _seg_conv(u_in, ea_ref[...], caw_ref[...], CONV_A_K)
    yad_ref[:, 0:A_W] = (pa[:, 0:A_W] * conv).astype(BF16)

    qkv = _silu(_seg_conv(pq_ref[...], eq_ref[...], dcw_ref[...], DN_CONV_K))
    d, dl, cols = _dn_gates(pab_ref[...], alog_ref[...], dtb_ref[...], SEQ_S)
    dec_t = jnp.exp(dl)
    dnorm = dnorm_ref[...]

    wide = NB_S * DN_DK
    mexp2 = mexp2_ref[...]
    mexp_t = mexpt_ref[...]

    q, k, v, z, d_col, d_row, dl_col, beta_col = _head_lists(qkv, pz_ref[...], d, cols)
    q, k, u, w, qk = _dn_intra(q, k, v, d_col, d_row, beta_col, m, SEQ_S)
    heads = list(range(DN_HEADS))
    s_old = [s0_ref[:, h].reshape(wide, DN_DV) for h in heads]
    x_exp = _each(lambda wx, qx, dc: jnp.concatenate(
        [jnp.concatenate([wx, qx * jnp.exp(dc)], axis=0).astype(BF16)] * NB_S, axis=1) * mexp2, w, q, d_col)
    ws = _each(_mm, x_exp, s_old)
    vn = _each(lambda ux, x: ux - x[:c], u, ws)
    o = _each(lambda x, y, vx: x[c:] + _mm(y, vx), ws, qk, vn)
    k_exp = _each(lambda kx, dlc, dc: jnp.concatenate(
        [(kx * jnp.exp(dlc - dc)).T.astype(BF16)] * NB_S, axis=0) * mexp_t, k, dl_col, d_col)
    dec = [jnp.concatenate([jnp.broadcast_to(dec_t[h:h + 1, SEQ_S * b:SEQ_S * b + 1], (DN_DK, DN_DV))
                            for b in range(NB_S)], axis=0) for h in heads]
    s_new = _each(lambda s, dx, kx, vx: s * dx + _mm(kx, vx), s_old, dec, k_exp, vn)
    for h, sx, ox, zx in zip(heads, s_new, o, z):
        s_ref[:, h] = sx.reshape(NB_S, DN_DK, DN_DV)
        yad_ref[:, A_W + h * LANE:A_W + (h + 1) * LANE] = _dn_out(ox, zx, dnorm).astype(BF16)


def _branch_sample(p1, pab, ea, eq, state, caw, dcw, alog_row, dtb_row, dnorm):
    t = p1.shape[0]
    c = CHUNK
    full = lambda shape: pl.BlockSpec(shape, lambda i: (0,) * len(shape))
    wide = NB_S * DN_DK
    owner = jnp.arange(wide, dtype=jnp.int32) // DN_DK
    seq = (jnp.arange(2 * c, dtype=jnp.int32) % c) // SEQ_S
    mexp2 = (seq[:, None] == owner[None, :]).astype(BF16)
    mexp_t = (owner[:, None] == seq[None, :c]).astype(BF16)
    return pl.pallas_call(
        _branch_sample_kernel,
        grid=(t // c,),
        in_specs=[pl.BlockSpec((c, 3 * A_W), lambda i: (i, 0)),
                  pl.BlockSpec((c, DN_CONV_CH), lambda i: (i, OFF_QKV // DN_CONV_CH)),
                  pl.BlockSpec((c, DN_V), lambda i: (i, OFF_Z // DN_V)),
                  pl.BlockSpec((c, LANE), lambda i: (i, 0)),
                  pl.BlockSpec((c, A_W), lambda i: (i, 0)),
                  pl.BlockSpec((c, DN_CONV_CH), lambda i: (i, 0)),
                  pl.BlockSpec((NB_S, DN_HEADS, DN_DK, DN_DV), lambda i: (i, 0, 0, 0)),
                  full((CONV_A_K, A_W)), full((DN_CONV_K, DN_CONV_CH)),
                  full((TAIL, LANE)), full((TAIL, LANE)), full((1, LANE)),
                  _resident(mexp2.shape), _resident(mexp_t.shape)],
        out_specs=[pl.BlockSpec((c, A_W + DN_V), lambda i: (i, 0)),
                   pl.BlockSpec((c, A_W), lambda i: (i, 0)),
                   pl.BlockSpec((NB_S, DN_HEADS, DN_DK, DN_DV), lambda i: (i, 0, 0, 0))],
        out_shape=[jax.ShapeDtypeStruct((t, A_W + DN_V), BF16),
                   jax.ShapeDtypeStruct((t, A_W), F32),
                   jax.ShapeDtypeStruct(state.shape, F32)],
        compiler_params=_cparams(("arbitrary",)),
        name="branch_sample",
    )(p1, p1, p1, pab, ea, eq, state, caw, dcw, alog_row, dtb_row, dnorm, mexp2, mexp_t)


def _attn_sample_kernel(q_ref, k_ref, v_ref, o_ref):
    for h in range(XA_HEADS):
        sl = slice(h * LANE, (h + 1) * LANE)
        rows = pl.ds(h, MEM_TOKENS, stride=XA_HEADS)
        q = q_ref[:, :, sl].astype(BF16)
        s = jnp.einsum("bqd,bkd->bqk", q, k_ref[:, rows, :].astype(BF16),
                       preferred_element_type=F32) * (XA_DH ** -0.5)
        e = jnp.exp(s - jnp.max(s, axis=-1, keepdims=True))
        p = e / jnp.sum(e, axis=-1, keepdims=True)
        o_ref[:, :, sl] = jnp.einsum("bqk,bkd->bqd", p.astype(BF16), v_ref[:, rows, :].astype(BF16),
                                     preferred_element_type=F32).astype(BF16)


def _attn_sample(q3, ck, cv, nb):
    bsz, length, _ = q3.shape
    return pl.pallas_call(
        _attn_sample_kernel,
        grid=(bsz // nb,),
        in_specs=[pl.BlockSpec((nb, length, XA_W), lambda i: (i, 0, 0)),
                  pl.BlockSpec((nb, MEM_TOKENS * XA_HEADS, XA_DH), lambda i: (i, 0, 0)),
                  pl.BlockSpec((nb, MEM_TOKENS * XA_HEADS, XA_DH), lambda i: (i, 0, 0))],
        out_specs=pl.BlockSpec((nb, length, XA_W), lambda i: (i, 0, 0)),
        out_shape=jax.ShapeDtypeStruct((bsz, length, XA_W), BF16),
        compiler_params=_cparams(("arbitrary",)),
        name="attn_sample",
    )(q3, ck, cv)


def _merge_kernel(yad_ref, ym_ref, p2_ref, x_ref, wb_ref, wo_ref, o_ref):
    yad = yad_ref[...]
    gate = lambda j: jax.nn.sigmoid(p2_ref[:, OFF_G + j * D_MODEL:OFF_G + (j + 1) * D_MODEL])
    merged = (gate(0) * jnp.dot(yad[:, :A_W], wb_ref[0:A_W, :], preferred_element_type=F32)
              + gate(1) * jnp.dot(yad[:, A_W:], wb_ref[A_W:A_W + DN_V, :], preferred_element_type=F32)
              + gate(2) * jnp.dot(ym_ref[...], wb_ref[A_W + DN_V:, :], preferred_element_type=F32))
    o_ref[...] = x_ref[...] + jnp.dot(merged.astype(BF16), wo_ref[...], preferred_element_type=F32)


def _merge(yad, ym, p2, x2d, wb, wo, tm):
    t, d = x2d.shape
    row = lambda n: pl.BlockSpec((tm, n), lambda i: (i, 0))
    return pl.pallas_call(
        _merge_kernel,
        grid=(t // tm,),
        in_specs=[row(A_W + DN_V), row(XA_W), row(W2), row(d), _resident(wb.shape), _resident(wo.shape)],
        out_specs=row(d),
        out_shape=jax.ShapeDtypeStruct((t, d), F32),
        compiler_params=_cparams(("arbitrary",)),
        name="merge",
    )(yad, ym, p2, x2d, wb, wo)


MXU_K = 256
FF_EDGES = (0, 6 * MXU_K, D_FF)
FF_SPLIT = len(FF_EDGES) - 1
FF_SUB = 256
TM_FF = 4 * FF_SUB


def _ffn_kernel(x_ref, gf_ref, wu_ref, wd_ref, gl_ref, o_ref):
    n_sub = x_ref.shape[0] // FF_SUB

    def block(xn, acc, j):
        lo, hi = FF_EDGES[j], FF_EDGES[j + 1]
        gate = jnp.dot(xn, wu_ref[:, lo:hi], preferred_element_type=F32)
        up = jnp.dot(xn, wu_ref[:, D_FF + lo:D_FF + hi], preferred_element_type=F32)
        hid = (_silu(gate) * up).astype(BF16)
        return acc + jnp.dot(hid, wd_ref[lo:hi, :], preferred_element_type=F32)

    state = [None] * n_sub
    for i in range(n_sub + 1):
        if i < n_sub:
            x = x_ref[i * FF_SUB:(i + 1) * FF_SUB, :]
            xn = _rms(x, gf_ref[...]).astype(BF16)
            state[i] = (xn, block(xn, x, 0))
        if i > 0:
            xn, acc = state[i - 1]
            for j in range(1, FF_SPLIT):
                acc = block(xn, acc, j)
            o_ref[(i - 1) * FF_SUB:i * FF_SUB, :] = _rms(acc, gl_ref[...])


def _ffn(x2d, gf, wu, wd, gl, tm):
    t, d = x2d.shape
    return pl.pallas_call(
        _ffn_kernel,
        grid=(t // tm,),
        in_specs=[pl.BlockSpec((tm, d), lambda i: (i, 0)),
                  _resident((1, d)), _resident(wu.shape), _resident(wd.shape), _resident((1, d))],
        out_specs=pl.BlockSpec((tm, d), lambda i: (i, 0)),
        out_shape=jax.ShapeDtypeStruct((t, d), F32),
        compiler_params=_cparams(("arbitrary",)),
        name="ffn",
    )(x2d, gf, wu, wd, gl)


def _head_rows(v):
    col = jnp.zeros((TAIL, 1), F32).at[:v.shape[0], 0].set(v.astype(F32))
    return jnp.broadcast_to(col, (TAIL, LANE))


def kernel(x_prompt, x_sample, mem_prompt, state_conv_a, state_dn_conv, state_dn, cache_mem_k, cache_mem_v,
           norm_mix, w_in, conv_a_w, dn_conv_w, dn_a_log, dn_dt_bias, dn_norm, norm_mem, w_mem_kv, w_branch,
           w_o, norm_ffn, w_ffn_up, w_ffn_down, norm_final):
    bp, lp, d = x_prompt.shape
    bs, ls, _ = x_sample.shape
    assert norm_mix.shape[0] == 1 and ls == SEQ_S and lp % CHUNK == 0 and (bs * ls) % CHUNK == 0
    assert w_in.shape[2] == W1 + N_AB + W2

    w = w_in[0]
    w1 = w[:, :W1].astype(BF16)
    w2 = w[:, W1 + N_AB:].astype(BF16)
    wab = jnp.pad(w[:, W1:W1 + N_AB], ((0, 0), (0, LANE - N_AB))).astype(BF16)
    wb = w_branch[0].astype(BF16)
    wo = w_o[0].astype(BF16)
    wu = w_ffn_up[0].astype(BF16)
    wd = w_ffn_down[0].astype(BF16)
    wkv = w_mem_kv[0].astype(BF16)
    g_mix = norm_mix[0][None, :]
    g_ffn = norm_ffn[0][None, :]
    g_fin = norm_final[None, :]
    g_mem = norm_mem[0][None, :]
    caw = conv_a_w[0]
    dcw = dn_conv_w[0]
    alog_row = _head_rows(dn_a_log[0])
    dtb_row = _head_rows(dn_dt_bias[0])
    dnorm = dn_norm[0][None, :]

    tp = bp * lp
    mk, mv, mkb, mvb = _memkv(mem_prompt.reshape(bp * MEM_TOKENS, d), g_mem, wkv, TM)
    x1_p, ca_p, dc_p, s_p = _front_prompt(x_prompt, g_mix, w1, w2, wab, mkb.reshape(bp, MEM_TOKENS, XA_W),
                                          mvb.reshape(bp, MEM_TOKENS, XA_W), caw, dcw, alog_row, dtb_row, dnorm,
                                          wb, wo, NB_P)

    ts = bs * ls
    xs2 = x_sample.reshape(ts, d)
    p1_s, p2_s, pab_s = _proj(xs2, g_mix, w1, w2, wab, CHUNK)
    ea = jnp.pad(state_conv_a[0], ((0, 0), (0, ls - (CONV_A_K - 1)), (0, 0))).reshape(ts, A_W)
    eq = jnp.pad(state_dn_conv[0], ((0, 0), (0, ls - (DN_CONV_K - 1)), (0, 0))).reshape(ts, DN_CONV_CH)
    yad_s, u_s, s_s = _branch_sample(p1_s, pab_s, ea, eq, state_dn[0], caw, dcw, alog_row, dtb_row, dnorm)
    ym_s = _attn_sample(p2_s[:, :XA_W].reshape(bs, ls, XA_W),
                        cache_mem_k.reshape(bs, MEM_TOKENS * XA_HEADS, XA_DH),
                        cache_mem_v.reshape(bs, MEM_TOKENS * XA_HEADS, XA_DH), NB_ATTN_S)
    x1_s = _merge(yad_s, ym_s.reshape(ts, XA_W), p2_s, xs2, wb, wo, ts)
    ca_s = u_s.reshape(bs, ls, A_W)[:, ls - (CONV_A_K - 1):]
    dc_s = p1_s[:, OFF_QKV:OFF_Z].reshape(bs, ls, DN_CONV_CH)[:, ls - (DN_CONV_K - 1):]

    y_p = _ffn(x1_p.reshape(tp, d), g_ffn, wu, wd, g_fin, TM_FF).reshape(bp, lp, d)
    y_s = _ffn(x1_s, g_ffn, wu, wd, g_fin, ts).reshape(bs, ls, d)

    return (y_p, y_s, ca_p[None], dc_p[None], s_p[None],
            mk.reshape(1, bp, MEM_TOKENS, XA_HEADS, XA_DH), mv.reshape(1, bp, MEM_TOKENS, XA_HEADS, XA_DH),
            ca_s[None], dc_s[None], s_s[None])
```

```python
import functools

import jax
import jax.numpy as jnp
from jax import lax
from jax.experimental import pallas as pl
from jax.experimental.pallas import tpu as pltpu

F32 = jnp.float32
BF16 = jnp.bfloat16

D_MODEL = 1024
A_W = 512
CONV_A_K = 3
DN_HEADS = 4
DN_DK = 128
DN_DV = 128
DN_QK = DN_HEADS * DN_DK
DN_V = DN_HEADS * DN_DV
DN_CONV_CH = 2 * DN_QK + DN_V
DN_CONV_K = 4
MEM_TOKENS = 256
XA_HEADS = 4
XA_DH = 128
XA_W = XA_HEADS * XA_DH
D_FF = 2816
EPS = 1e-6

LANE = 128
CHUNK = 128
TAIL = 8
NB_P = 4
TM = 512
GATE_BLK = 256
PROJ_BLK = 512
ATTN_RESERVE = 4

W1 = 3 * A_W + DN_CONV_CH + DN_V
OFF_QKV = 3 * A_W
OFF_Z = OFF_QKV + DN_CONV_CH
W2 = XA_W + 3 * D_MODEL
OFF_G = XA_W
N_AB = 2 * DN_HEADS
P_Z = 3 * A_W
P_XQ = P_Z + DN_V
P_W = P_XQ + XA_W

VMEM_LIMIT = 60 * 1024 * 1024


def _cparams(sem):
    return pltpu.CompilerParams(dimension_semantics=sem, vmem_limit_bytes=VMEM_LIMIT)


def _resident(shape):
    return pl.BlockSpec(shape, lambda *_: (0,) * len(shape), pipeline_mode=pl.Buffered(1))


def _mm(a, b):
    return jnp.dot(a.astype(BF16), b.astype(BF16), preferred_element_type=F32)


def _mm_nt(a, b):
    return lax.dot_general(a.astype(BF16), b.astype(BF16), (((1,), (1,)), ((), ())),
                           preferred_element_type=F32)


def _mm_tn(a, b):
    return lax.dot_general(a.astype(BF16), b.astype(BF16), (((0,), (0,)), ((), ())),
                           preferred_element_type=F32)


def _rms(x, g):
    return x * lax.rsqrt(jnp.mean(x * x, axis=-1, keepdims=True) + EPS) * g


def _silu(x):
    return x * jax.nn.sigmoid(x)


def _softplus(x):
    return jnp.maximum(x, 0.0) + jnp.log1p(jnp.exp(-jnp.abs(x)))


def _memkv_kernel(x_ref, g_ref, w_ref, k_ref, v_ref, kb_ref, vb_ref):
    xn = _rms(x_ref[...], g_ref[...]).astype(BF16)
    kv = jnp.dot(xn, w_ref[...], preferred_element_type=F32)
    k = kv[:, :XA_W]
    v = kv[:, XA_W:]
    tm = k.shape[0]
    for h in range(XA_HEADS):
        rows = pl.ds(h, tm, stride=XA_HEADS)
        k_ref[rows, :] = k[:, h * LANE:(h + 1) * LANE]
        v_ref[rows, :] = v[:, h * LANE:(h + 1) * LANE]
    kb_ref[...] = k.astype(BF16)
    vb_ref[...] = v.astype(BF16)


def _memkv(mem2d, gain, w, tm):
    t, d = mem2d.shape
    blk = pl.BlockSpec((tm, XA_W), lambda i: (i, 0))
    blk_rows = pl.BlockSpec((tm * XA_HEADS, XA_DH), lambda i: (i, 0))
    return pl.pallas_call(
        _memkv_kernel,
        grid=(t // tm,),
        in_specs=[pl.BlockSpec((tm, d), lambda i: (i, 0)),
                  pl.BlockSpec((1, d), lambda i: (0, 0)),
                  pl.BlockSpec((d, 2 * XA_W), lambda i: (0, 0))],
        out_specs=[blk_rows, blk_rows, blk, blk],
        out_shape=[jax.ShapeDtypeStruct((t * XA_HEADS, XA_DH), F32),
                   jax.ShapeDtypeStruct((t * XA_HEADS, XA_DH), F32),
                   jax.ShapeDtypeStruct((t, XA_W), BF16), jax.ShapeDtypeStruct((t, XA_W), BF16)],
        compiler_params=_cparams(("arbitrary",)),
        name="memkv",
    )(mem2d, gain, w)


def _log2(n):
    return n.bit_length() - 1


def _dn_masks(seg):
    r = lax.broadcasted_iota(jnp.int32, (CHUNK, CHUNK), 0)
    c = lax.broadcasted_iota(jnp.int32, (CHUNK, CHUNK), 1)
    ls = _log2(seg)
    same = (r >> ls) == (c >> ls)
    base = min(8, seg)
    lb = _log2(base)
    m = {
        "causal": (same & (r >= c)).astype(F32),
        "strict": (same & (r > c)).astype(F32),
        "eye": (r == c).astype(F32),
        "neg_diag": -((r >> lb) == (c >> lb)).astype(F32),
        "off": {},
        "base": base,
    }
    s = base
    while s < seg:
        l1, l2 = _log2(s), _log2(2 * s)
        m["off"][s] = (((r >> l2) == (c >> l2)) & ((r >> l1) != (c >> l1))).astype(F32)
        s *= 2
    return m


def _each(f, *lists):
    return [f(*args) for args in zip(*lists)]


def _low_rows(x, s):
    return jnp.concatenate([x[i + s:i + 2 * s] for i in range(0, x.shape[0], 2 * s)], axis=0)


def _merge_low(x, low, s):
    parts = []
    for j, i in enumerate(range(0, x.shape[0], 2 * s)):
        parts += [x[i:i + s], low[j * s:(j + 1) * s]]
    return jnp.concatenate(parts, axis=0)


def _spread_low(low, s):
    return _merge_low(jnp.zeros((2 * low.shape[0], low.shape[1]), low.dtype), low, s)


def _tri_inv(a_list, m, seg, tick):
    add = lambda x, y: x + y
    b = _each(lambda a: a * m["neg_diag"], a_list)
    p = _each(lambda x: m["eye"] + x, b)
    b2 = _each(_mm, b, b)
    tick()
    p = _each(add, p, _each(_mm, p, b2))
    tick()
    if m["base"] == 8:
        b4 = _each(_mm, b2, b2)
        tick()
        p = _each(add, p, _each(_mm, p, b4))
        tick()
    s = m["base"]
    while s < seg:
        low = lambda t, s=s: _low_rows(t, s)
        x = _each(_mm, _each(lambda a, s=s: low(a) * low(m["off"][s]), a_list), p)
        tick()
        r = _each(_mm, _each(low, p), _each(lambda y, s=s: _spread_low(y, s), x))
        p = _each(lambda t, y, s=s: _merge_low(t, low(t) - y, s), p, r)
        tick()
        s *= 2
    return p


def _seg_scan(x, seg, reverse):
    n = x.shape[1]
    pos = lax.broadcasted_iota(jnp.int32, x.shape, 1) & (seg - 1)
    s = 1
    while s < seg:
        shifted = pltpu.roll(x, n - s if reverse else s, 1)
        x = x + jnp.where(pos < seg - s if reverse else pos >= s, shifted, 0.0)
        s *= 2
    return x


def _dn_gates(ab, alog8, dtb8, seg):
    abt = ab.T[0:TAIL]
    g = -jnp.exp(alog8) * _softplus(abt + dtb8)
    d = _seg_scan(g, seg, False)
    dl = d + _seg_scan(g, seg, True) - g
    beta = jax.nn.sigmoid(abt)
    pad = jnp.zeros((CHUNK - 3 * TAIL, CHUNK), F32)
    return d, dl, jnp.concatenate([d, dl, beta, pad], axis=0).T


def _l2n(x, scale=1.0):
    return x * (lax.rsqrt(jnp.sum(x * x, axis=-1, keepdims=True) + EPS) * scale)


def _dn_intra(q, k, v, d_col, d_row, beta_col, m, seg, tick=lambda: None):
    q = _each(lambda x: _l2n(x, DN_DK ** -0.5), q)
    k = _each(_l2n, k)
    gamma = _each(lambda dc, dr: jnp.exp((dc - dr) * m["causal"]) * m["causal"], d_col, d_row)
    kk = _each(_mm_nt, k, k)
    a = _each(lambda bc, x, g: (bc * x) * g * m["strict"], beta_col, kk, gamma)
    t = _tri_inv(a, m, seg, tick)
    rhs = _each(lambda vv, kx, bc, dc: jnp.concatenate([vv * bc, kx * (bc * jnp.exp(dc))], axis=1),
                v, k, beta_col, d_col)
    sol = _each(_mm, t, rhs)
    u = [x[:, :DN_DV] for x in sol]
    w = [x[:, DN_DV:] for x in sol]
    qk = _each(lambda x, g: x * g, _each(_mm_nt, q, k), gamma)
    return q, k, u, w, qk


def _dn_out(o, z, dnorm):
    return _rms(o, dnorm) * _silu(z)


def _head_lists(qkv, z, d, cols):
    out = [[] for _ in range(8)]
    for h in range(DN_HEADS):
        beta_lane = 2 * TAIL + DN_HEADS + h
        vals = (qkv[:, h * LANE:(h + 1) * LANE],
                qkv[:, DN_QK + h * LANE:DN_QK + (h + 1) * LANE],
                qkv[:, 2 * DN_QK + h * LANE:2 * DN_QK + (h + 1) * LANE],
                None if z is None else z[:, h * LANE:(h + 1) * LANE],
                cols[:, h:h + 1], d[h:h + 1, :], cols[:, TAIL + h:TAIL + h + 1],
                cols[:, beta_lane:beta_lane + 1])
        for lst, val in zip(out, vals):
            lst.append(val)
    return out


def _causal_conv(x, tail_ref, b, wts, width):
    c, w = x.shape
    tiles = jnp.concatenate([tail_ref[b][None], x.reshape(c // TAIL, TAIL, w)], axis=0)
    sub = lax.broadcasted_iota(jnp.int32, (1, TAIL, 1), 1)
    acc = None
    for i in range(width):
        s = width - 1 - i
        if s == 0:
            y = tiles[1:]
        else:
            r = pltpu.roll(tiles, s, 1)
            y = jnp.where(sub >= s, r[1:], r[:-1])
        term = wts[i:i + 1][None] * y
        acc = term if acc is None else acc + term
    tail_ref[b] = tiles[c // TAIL]
    return acc.reshape(c, w)


def _front_prompt_kernel(x_ref, gmix_ref, w1_ref, w2_ref, wab_ref, mk_ref, mv_ref, caw_ref, dcw_ref,
                         alog_ref, dtb_ref, dnorm_ref, wb_ref, wo_ref,
                         x1_ref, ca_ref, dc_ref, s_ref, utail, qtail, ybuf, pbuf, gbuf, *, nb):
    c = CHUNK
    rows = nb * c
    t_idx = pl.program_id(1)

    @pl.when(t_idx == 0)
    def _():
        utail[...] = jnp.zeros(utail.shape, F32)
        qtail[...] = jnp.zeros(qtail.shape, F32)
        s_ref[...] = jnp.zeros(s_ref.shape, F32)

    x = x_ref[...].reshape(rows, D_MODEL)
    xn = _rms(x, gmix_ref[...]).astype(BF16)
    proj = lambda w_ref, lo, hi: jnp.dot(xn, w_ref[:, lo:hi], preferred_element_type=F32)

    queue = []

    def enqueue(dst, w_ref, src, dst_lo, width, act=None):
        def run():
            r = proj(w_ref, src, src + width)
            dst[:, dst_lo:dst_lo + width] = r if act is None else act(r)
        queue.append(run)

    for lo in range(0, 3 * A_W, PROJ_BLK):
        enqueue(pbuf, w1_ref, lo, lo, PROJ_BLK)
    enqueue(pbuf, w1_ref, OFF_Z, P_Z, DN_V)
    enqueue(pbuf, w2_ref, 0, P_XQ, XA_W)
    for lo in range(0, 3 * D_MODEL, GATE_BLK):
        enqueue(gbuf, w2_ref, OFF_G + lo, lo, GATE_BLK, jax.nn.sigmoid)
    queue.reverse()

    def tick(keep=ATTN_RESERVE):
        if len(queue) > keep:
            queue.pop()()

    m = _dn_masks(c)
    caw = caw_ref[...]
    dcw = dcw_ref[...]
    dnorm = dnorm_ref[...]
    pq_all = proj(w1_ref, OFF_QKV, OFF_Z)
    pab_all = jnp.dot(xn, wab_ref[...], preferred_element_type=F32)

    qkv_l, gates_l = [], []
    for b in range(nb):
        rb = slice(b * c, (b + 1) * c)
        tick()
        qkv_in = pq_all[rb]
        dc_ref[b] = qkv_in[c - (DN_CONV_K - 1):]
        qkv_l.append(_silu(_causal_conv(qkv_in, qtail, b, dcw, DN_CONV_K)))
        gates_l.append(_dn_gates(pab_all[rb], alog_ref[...], dtb_ref[...], c))

    for b in range(nb):
        rb = slice(b * c, (b + 1) * c)
        tick()
        u_in = pbuf[rb, A_W:2 * A_W] * pbuf[rb, 2 * A_W:3 * A_W]
        ca_ref[b] = u_in[c - (CONV_A_K - 1):]
        ybuf[rb, 0:A_W] = (pbuf[rb, 0:A_W] * _causal_conv(u_in, utail, b, caw, CONV_A_K)).astype(BF16)

    lists = [[] for _ in range(8)]
    for b in range(nb):
        d, _, cols = gates_l[b]
        for lst, val in zip(lists, _head_lists(qkv_l[b], None, d, cols)):
            lst.extend(val)
    idx = [(b, h) for b in range(nb) for h in range(DN_HEADS)]
    q, k, v, _, d_col, d_row, dl_col, beta_col = lists
    q, k, u, w, qk = _dn_intra(q, k, v, d_col, d_row, beta_col, m, c, tick)
    s_old = [s_ref[b, h] for b, h in idx]
    vn = _each(lambda ux, wx, s: ux - _mm(wx, s), u, w, s_old)
    tick()
    o = _each(lambda qx, dc, y, s, vx: _mm(jnp.concatenate([qx * jnp.exp(dc), y], axis=1),
                                           jnp.concatenate([s, vx], axis=0)), q, d_col, qk, s_old, vn)
    tick()
    s_new = _each(lambda s, dlc, kx, dc, vx: s * jnp.exp(dlc[0:1, :]) + _mm_tn(kx * jnp.exp(dlc - dc), vx),
                  s_old, dl_col, k, d_col, vn)
    for (b, h), sx, ox in zip(idx, s_new, o):
        s_ref[b, h] = sx
        zx = pbuf[b * c:(b + 1) * c, P_Z + h * LANE:P_Z + (h + 1) * LANE]
        ybuf[b * c:(b + 1) * c, A_W + h * LANE:A_W + (h + 1) * LANE] = _dn_out(ox, zx, dnorm).astype(BF16)

    for b in range(nb):
        tick(0)
        heads = range(XA_HEADS)
        sc = [_mm_nt(pbuf[b * c:(b + 1) * c, P_XQ + h * LANE:P_XQ + (h + 1) * LANE],
                     mk_ref[b, :, h * LANE:(h + 1) * LANE]) * (XA_DH ** -0.5) for h in heads]
        e = _each(lambda x_: jnp.exp(x_ - jnp.max(x_, axis=-1, keepdims=True)), sc)
        inv = _each(lambda x_: 1.0 / jnp.sum(x_, axis=-1, keepdims=True), e)
        for h, ex, ix in zip(heads, e, inv):
            ybuf[b * c:(b + 1) * c, A_W + DN_V + h * LANE:A_W + DN_V + (h + 1) * LANE] = (
                _mm(ex, mv_ref[b, :, h * LANE:(h + 1) * LANE]) * ix).astype(BF16)
    while queue:
        tick(0)

    merged = None
    for j, (lo, hi) in enumerate(((0, A_W), (A_W, A_W + DN_V), (A_W + DN_V, A_W + DN_V + XA_W))):
        term = gbuf[:, j * D_MODEL:(j + 1) * D_MODEL] * jnp.dot(ybuf[:, lo:hi], wb_ref[lo:hi, :],
                                                                preferred_element_type=F32)
        merged = term if merged is None else merged + term
    x1 = x + jnp.dot(merged.astype(BF16), wo_ref[...], preferred_element_type=F32)
    x1_ref[...] = x1.reshape(nb, c, D_MODEL)


def _front_prompt(x, gmix, w1, w2, wab, mkb, mvb, caw, dcw, alog_row, dtb_row, dnorm, wb, wo, nb):
    bsz, length, d = x.shape
    c = CHUNK
    return pl.pallas_call(
        functools.partial(_front_prompt_kernel, nb=nb),
        grid=(bsz // nb, length // c),
        in_specs=[pl.BlockSpec((nb, c, d), lambda g, t: (g, t, 0)),
                  _resident((1, d)), _resident(w1.shape), _resident(w2.shape), _resident(wab.shape),
                  pl.BlockSpec((nb, MEM_TOKENS, XA_W), lambda g, t: (g, 0, 0), pipeline_mode=pl.Buffered(1)),
                  pl.BlockSpec((nb, MEM_TOKENS, XA_W), lambda g, t: (g, 0, 0), pipeline_mode=pl.Buffered(1)),
                  _resident((CONV_A_K, A_W)), _resident((DN_CONV_K, DN_CONV_CH)),
                  _resident((TAIL, LANE)), _resident((TAIL, LANE)), _resident((1, LANE)),
                  _resident(wb.shape), _resident(wo.shape)],
        out_specs=[pl.BlockSpec((nb, c, d), lambda g, t: (g, t, 0)),
                   pl.BlockSpec((nb, CONV_A_K - 1, A_W), lambda g, t: (g, 0, 0)),
                   pl.BlockSpec((nb, DN_CONV_K - 1, DN_CONV_CH), lambda g, t: (g, 0, 0)),
                   pl.BlockSpec((nb, DN_HEADS, DN_DK, DN_DV), lambda g, t: (g, 0, 0, 0))],
        out_shape=[jax.ShapeDtypeStruct((bsz, length, d), F32),
                   jax.ShapeDtypeStruct((bsz, CONV_A_K - 1, A_W), F32),
                   jax.ShapeDtypeStruct((bsz, DN_CONV_K - 1, DN_CONV_CH), F32),
                   jax.ShapeDtypeStruct((bsz, DN_HEADS, DN_DK, DN_DV), F32)],
        scratch_shapes=[pltpu.VMEM((nb, TAIL, A_W), F32), pltpu.VMEM((nb, TAIL, DN_CONV_CH), F32),
                        pltpu.VMEM((nb * c, A_W + DN_V + XA_W), BF16), pltpu.VMEM((nb * c, P_W), F32),
                        pltpu.VMEM((nb * c, 3 * d), F32)],
        compiler_params=_cparams(("arbitrary", "arbitrary")),
        name="front_prompt",
    )(x, gmix, w1, w2, wab, mkb, mvb, caw, dcw, alog_row, dtb_row, dnorm, wb, wo)


def _proj_kernel(x_ref, g_ref, w1_ref, w2_ref, wab_ref, p1_ref, p2_ref, pab_ref):
    xn = _rms(x_ref[...], g_ref[...]).astype(BF16)
    p1_ref[...] = jnp.dot(xn, w1_ref[...], preferred_element_type=F32)
    p2_ref[...] = jnp.dot(xn, w2_ref[...], preferred_element_type=F32)
    pab_ref[...] = jnp.dot(xn, wab_ref[...], preferred_element_type=F32)


def _proj(x2d, gain, w1, w2, wab, tm):
    t, d = x2d.shape
    row = lambda n: pl.BlockSpec((tm, n), lambda i: (i, 0))
    return pl.pallas_call(
        _proj_kernel,
        grid=(t // tm,),
        in_specs=[row(d), _resident((1, d)), _resident(w1.shape), _resident(w2.shape), _resident(wab.shape)],
        out_specs=[row(W1), row(W2), row(LANE)],
        out_shape=[jax.ShapeDtypeStruct((t, W1), F32), jax.ShapeDtypeStruct((t, W2), F32),
                   jax.ShapeDtypeStruct((t, LANE), F32)],
        compiler_params=_cparams(("arbitrary",)),
        name="proj",
    )(x2d, gain, w1, w2, wab)


SEQ_S = 4
NB_S = CHUNK // SEQ_S
NB_ATTN_S = 16


def _seg_conv(x, e, wts, width):
    rows, w = x.shape
    xt = x.reshape(rows // TAIL, TAIL, w)
    et = e.reshape(rows // TAIL, TAIL, w)
    tmod = lax.broadcasted_iota(jnp.int32, (1, TAIL, 1), 1) & (SEQ_S - 1)
    acc = None
    for i in range(width):
        s = width - 1 - i
        term = xt if s == 0 else jnp.where(tmod >= s, pltpu.roll(xt, s, 1), 0.0)
        if i < width - 1:
            hist = et if i == 0 else pltpu.roll(et, TAIL - i, 1)
            term = term + jnp.where(tmod < SEQ_S - i, hist, 0.0)
        term = wts[i:i + 1][None] * term
        acc = term if acc is None else acc + term
    return acc.reshape(rows, w)


def _branch_sample_kernel(pa_ref, pq_ref, pz_ref, pab_ref, ea_ref, eq_ref, s0_ref, caw_ref, dcw_ref,
                          alog_ref, dtb_ref, dnorm_ref, mexp2_ref, mexpt_ref, yad_ref, u_ref, s_ref):
    c = CHUNK
    m = _dn_masks(SEQ_S)

    pa = pa_ref[...]
    u_in = pa[:, A_W:2 * A_W] * pa[:, 2 * A_W:3 * A_W]
    u_ref[...] = u_in
    conv = _seg_conv(u_in, ea_ref[...], caw_ref[...], CONV_A_K)
    yad_ref[:, 0:A_W] = (pa[:, 0:A_W] * conv).astype(BF16)

    qkv = _silu(_seg_conv(pq_ref[...], eq_ref[...], dcw_ref[...], DN_CONV_K))
    d, dl, cols = _dn_gates(pab_ref[...], alog_ref[...], dtb_ref[...], SEQ_S)
    dec_t = jnp.exp(dl)
    dnorm = dnorm_ref[...]

    wide = NB_S * DN_DK
    mexp2 = mexp2_ref[...]
    mexp_t = mexpt_ref[...]

    q, k, v, z, d_col, d_row, dl_col, beta_col = _head_lists(qkv, pz_ref[...], d, cols)
    q, k, u, w, qk = _dn_intra(q, k, v, d_col, d_row, beta_col, m, SEQ_S)
    heads = list(range(DN_HEADS))
    s_old = [s0_ref[:, h].reshape(wide, DN_DV) for h in heads]
    x_exp = _each(lambda wx, qx, dc: jnp.concatenate(
        [jnp.concatenate([wx, qx * jnp.exp(dc)], axis=0).astype(BF16)] * NB_S, axis=1) * mexp2, w, q, d_col)
    ws = _each(_mm, x_exp, s_old)
    vn = _each(lambda ux, x: ux - x[:c], u, ws)
    o = _each(lambda x, y, vx: x[c:] + _mm(y, vx), ws, qk, vn)
    k_exp = _each(lambda kx, dlc, dc: jnp.concatenate(
        [(kx * jnp.exp(dlc - dc)).T.astype(BF16)] * NB_S, axis=0) * mexp_t, k, dl_col, d_col)
    dec = [jnp.concatenate([jnp.broadcast_to(dec_t[h:h + 1, SEQ_S * b:SEQ_S * b + 1], (DN_DK, DN_DV))
                            for b in range(NB_S)], axis=0) for h in heads]
    s_new = _each(lambda s, dx, kx, vx: s * dx + _mm(kx, vx), s_old, dec, k_exp, vn)
    for h, sx, ox, zx in zip(heads, s_new, o, z):
        s_ref[:, h] = sx.reshape(NB_S, DN_DK, DN_DV)
        yad_ref[:, A_W + h * LANE:A_W + (h + 1) * LANE] = _dn_out(ox, zx, dnorm).astype(BF16)


def _branch_sample(p1, pab, ea, eq, state, caw, dcw, alog_row, dtb_row, dnorm):
    t = p1.shape[0]
    c = CHUNK
    full = lambda shape: pl.BlockSpec(shape, lambda i: (0,) * len(shape))
    wide = NB_S * DN_DK
    owner = jnp.arange(wide, dtype=jnp.int32) // DN_DK
    seq = (jnp.arange(2 * c, dtype=jnp.int32) % c) // SEQ_S
    mexp2 = (seq[:, None] == owner[None, :]).astype(BF16)
    mexp_t = (owner[:, None] == seq[None, :c]).astype(BF16)
    return pl.pallas_call(
        _branch_sample_kernel,
        grid=(t // c,),
        in_specs=[pl.BlockSpec((c, 3 * A_W), lambda i: (i, 0)),
                  pl.BlockSpec((c, DN_CONV_CH), lambda i: (i, OFF_QKV // DN_CONV_CH)),
                  pl.BlockSpec((c, DN_V), lambda i: (i, OFF_Z // DN_V)),
                  pl.BlockSpec((c, LANE), lambda i: (i, 0)),
                  pl.BlockSpec((c, A_W), lambda i: (i, 0)),
                  pl.BlockSpec((c, DN_CONV_CH), lambda i: (i, 0)),
                  pl.BlockSpec((NB_S, DN_HEADS, DN_DK, DN_DV), lambda i: (i, 0, 0, 0)),
                  full((CONV_A_K, A_W)), full((DN_CONV_K, DN_CONV_CH)),
                  full((TAIL, LANE)), full((TAIL, LANE)), full((1, LANE)),
                  _resident(mexp2.shape), _resident(mexp_t.shape)],
        out_specs=[pl.BlockSpec((c, A_W + DN_V), lambda i: (i, 0)),
                   pl.BlockSpec((c, A_W), lambda i: (i, 0)),
                   pl.BlockSpec((NB_S, DN_HEADS, DN_DK, DN_DV), lambda i: (i, 0, 0, 0))],
        out_shape=[jax.ShapeDtypeStruct((t, A_W + DN_V), BF16),
                   jax.ShapeDtypeStruct((t, A_W), F32),
                   jax.ShapeDtypeStruct(state.shape, F32)],
        compiler_params=_cparams(("arbitrary",)),
        name="branch_sample",
    )(p1, p1, p1, pab, ea, eq, state, caw, dcw, alog_row, dtb_row, dnorm, mexp2, mexp_t)


def _attn_sample_kernel(q_ref, k_ref, v_ref, o_ref):
    for h in range(XA_HEADS):
        sl = slice(h * LANE, (h + 1) * LANE)
        rows = pl.ds(h, MEM_TOKENS, stride=XA_HEADS)
        q = q_ref[:, :, sl].astype(BF16)
        s = jnp.einsum("bqd,bkd->bqk", q, k_ref[:, rows, :].astype(BF16),
                       preferred_element_type=F32) * (XA_DH ** -0.5)
        e = jnp.exp(s - jnp.max(s, axis=-1, keepdims=True))
        p = e / jnp.sum(e, axis=-1, keepdims=True)
        o_ref[:, :, sl] = jnp.einsum("bqk,bkd->bqd", p.astype(BF16), v_ref[:, rows, :].astype(BF16),
                                     preferred_element_type=F32).astype(BF16)


def _attn_sample(q3, ck, cv, nb):
    bsz, length, _ = q3.shape
    return pl.pallas_call(
        _attn_sample_kernel,
        grid=(bsz // nb,),
        in_specs=[pl.BlockSpec((nb, length, XA_W), lambda i: (i, 0, 0)),
                  pl.BlockSpec((nb, MEM_TOKENS * XA_HEADS, XA_DH), lambda i: (i, 0, 0)),
                  pl.BlockSpec((nb, MEM_TOKENS * XA_HEADS, XA_DH), lambda i: (i, 0, 0))],
        out_specs=pl.BlockSpec((nb, length, XA_W), lambda i: (i, 0, 0)),
        out_shape=jax.ShapeDtypeStruct((bsz, length, XA_W), BF16),
        compiler_params=_cparams(("arbitrary",)),
        name="attn_sample",
    )(q3, ck, cv)


def _merge_kernel(yad_ref, ym_ref, p2_ref, x_ref, wb_ref, wo_ref, o_ref):
    yad = yad_ref[...]
    gate = lambda j: jax.nn.sigmoid(p2_ref[:, OFF_G + j * D_MODEL:OFF_G + (j + 1) * D_MODEL])
    merged = (gate(0) * jnp.dot(yad[:, :A_W], wb_ref[0:A_W, :], preferred_element_type=F32)
              + gate(1) * jnp.dot(yad[:, A_W:], wb_ref[A_W:A_W + DN_V, :], preferred_element_type=F32)
              + gate(2) * jnp.dot(ym_ref[...], wb_ref[A_W + DN_V:, :], preferred_element_type=F32))
    o_ref[...] = x_ref[...] + jnp.dot(merged.astype(BF16), wo_ref[...], preferred_element_type=F32)


MXU_K = 256
FF_EDGES = (0, 6 * MXU_K, D_FF)
FF_SPLIT = len(FF_EDGES) - 1
FF_SUB = 256
TM_FF = 4 * FF_SUB


def _ffn_kernel(x_ref, gf_ref, wu_ref, wd_ref, gl_ref, o_ref):
    n_sub = x_ref.shape[0] // FF_SUB

    def block(xn, acc, j):
        lo, hi = FF_EDGES[j], FF_EDGES[j + 1]
        gate = jnp.dot(xn, wu_ref[:, lo:hi], preferred_element_type=F32)
        up = jnp.dot(xn, wu_ref[:, D_FF + lo:D_FF + hi], preferred_element_type=F32)
        hid = (_silu(gate) * up).astype(BF16)
        return acc + jnp.dot(hid, wd_ref[lo:hi, :], preferred_element_type=F32)

    state = [None] * n_sub
    for i in range(n_sub + 1):
        if i < n_sub:
            x = x_ref[i * FF_SUB:(i + 1) * FF_SUB, :]
            xn = _rms(x, gf_ref[...]).astype(BF16)
            state[i] = (xn, block(xn, x, 0))
        if i > 0:
            xn, acc = state[i - 1]
            for j in range(1, FF_SPLIT):
                acc = block(xn, acc, j)
            o_ref[(i - 1) * FF_SUB:i * FF_SUB, :] = _rms(acc, gl_ref[...])


def _ffn(x2d, gf, wu, wd, gl, tm):
    t, d = x2d.shape
    return pl.pallas_call(
        _ffn_kernel,
        grid=(t // tm,),
        in_specs=[pl.BlockSpec((tm, d), lambda i: (i, 0)),
                  _resident((1, d)), _resident(wu.shape), _resident(wd.shape), _resident((1, d))],
        out_specs=pl.BlockSpec((tm, d), lambda i: (i, 0)),
        out_shape=jax.ShapeDtypeStruct((t, d), F32),
        compiler_params=_cparams(("arbitrary",)),
        name="ffn",
    )(x2d, gf, wu, wd, gl)


def _tail_sample_kernel(yad_ref, ym_ref, p2_ref, x_ref, wb_ref, wo_ref, gf_ref, wu_ref, wd_ref, gl_ref,
                        o_ref, x1_buf):
    _merge_kernel(yad_ref, ym_ref, p2_ref, x_ref, wb_ref, wo_ref, x1_buf)
    _ffn_kernel(x1_buf, gf_ref, wu_ref, wd_ref, gl_ref, o_ref)


def _tail_sample(yad, ym, p2, x2d, wb, wo, gf, wu, wd, gl):
    t, d = x2d.shape
    args = (yad, ym, p2, x2d, wb, wo, gf, wu, wd, gl)
    return pl.pallas_call(
        _tail_sample_kernel,
        grid=(1,),
        in_specs=[_resident(a.shape) for a in args],
        out_specs=pl.BlockSpec((t, d), lambda i: (0, 0)),
        out_shape=jax.ShapeDtypeStruct((t, d), F32),
        scratch_shapes=[pltpu.VMEM((t, d), F32)],
        compiler_params=_cparams(("arbitrary",)),
        name="tail_sample",
    )(*args)


def _head_rows(v):
    col = jnp.zeros((TAIL, 1), F32).at[:v.shape[0], 0].set(v.astype(F32))
    return jnp.broadcast_to(col, (TAIL, LANE))


def kernel(x_prompt, x_sample, mem_prompt, state_conv_a, state_dn_conv, state_dn, cache_mem_k, cache_mem_v,
           norm_mix, w_in, conv_a_w, dn_conv_w, dn_a_log, dn_dt_bias, dn_norm, norm_mem, w_mem_kv, w_branch,
           w_o, norm_ffn, w_ffn_up, w_ffn_down, norm_final):
    bp, lp, d = x_prompt.shape
    bs, ls, _ = x_sample.shape
    assert norm_mix.shape[0] == 1 and ls == SEQ_S and lp % CHUNK == 0 and (bs * ls) % CHUNK == 0
    assert w_in.shape[2] == W1 + N_AB + W2

    w = w_in[0]
    w1 = w[:, :W1].astype(BF16)
    w2 = w[:, W1 + N_AB:].astype(BF16)
    wab = jnp.pad(w[:, W1:W1 + N_AB], ((0, 0), (0, LANE - N_AB))).astype(BF16)
    wb = w_branch[0].astype(BF16)
    wo = w_o[0].astype(BF16)
    wu = w_ffn_up[0].astype(BF16)
    wd = w_ffn_down[0].astype(BF16)
    wkv = w_mem_kv[0].astype(BF16)
    g_mix = norm_mix[0][None, :]
    g_ffn = norm_ffn[0][None, :]
    g_fin = norm_final[None, :]
    g_mem = norm_mem[0][None, :]
    caw = conv_a_w[0]
    dcw = dn_conv_w[0]
    alog_row = _head_rows(dn_a_log[0])
    dtb_row = _head_rows(dn_dt_bias[0])
    dnorm = dn_norm[0][None, :]

    tp = bp * lp
    mk, mv, mkb, mvb = _memkv(mem_prompt.reshape(bp * MEM_TOKENS, d), g_mem, wkv, TM)
    x1_p, ca_p, dc_p, s_p = _front_prompt(x_prompt, g_mix, w1, w2, wab, mkb.reshape(bp, MEM_TOKENS, XA_W),
                                          mvb.reshape(bp, MEM_TOKENS, XA_W), caw, dcw, alog_row, dtb_row, dnorm,
                                          wb, wo, NB_P)

    ts = bs * ls
    xs2 = x_sample.reshape(ts, d)
    p1_s, p2_s, pab_s = _proj(xs2, g_mix, w1, w2, wab, CHUNK)
    ea = jnp.pad(state_conv_a[0], ((0, 0), (0, ls - (CONV_A_K - 1)), (0, 0))).reshape(ts, A_W)
    eq = jnp.pad(state_dn_conv[0], ((0, 0), (0, ls - (DN_CONV_K - 1)), (0, 0))).reshape(ts, DN_CONV_CH)
    yad_s, u_s, s_s = _branch_sample(p1_s, pab_s, ea, eq, state_dn[0], caw, dcw, alog_row, dtb_row, dnorm)
    ym_s = _attn_sample(p2_s[:, :XA_W].reshape(bs, ls, XA_W),
                        cache_mem_k.reshape(bs, MEM_TOKENS * XA_HEADS, XA_DH),
                        cache_mem_v.reshape(bs, MEM_TOKENS * XA_HEADS, XA_DH), NB_ATTN_S)
    ca_s = u_s.reshape(bs, ls, A_W)[:, ls - (CONV_A_K - 1):]
    dc_s = p1_s[:, OFF_QKV:OFF_Z].reshape(bs, ls, DN_CONV_CH)[:, ls - (DN_CONV_K - 1):]

    y_p = _ffn(x1_p.reshape(tp, d), g_ffn, wu, wd, g_fin, TM_FF).reshape(bp, lp, d)
    y_s = _tail_sample(yad_s, ym_s.reshape(ts, XA_W), p2_s, xs2, wb, wo, g_ffn, wu, wd, g_fin).reshape(bs, ls, d)

    return (y_p, y_s, ca_p[None], dc_p[None], s_p[None],
            mk.reshape(1, bp, MEM_TOKENS, XA_HEADS, XA_DH), mv.reshape(1, bp, MEM_TOKENS, XA_HEADS, XA_DH),
            ca_s[None], dc_s[None], s_s[None])
```

```python
import functools

import jax
import jax.numpy as jnp
import numpy as np
from jax import lax
from jax.experimental import pallas as pl
from jax.experimental.pallas import tpu as pltpu

F32 = jnp.float32
BF16 = jnp.bfloat16

D_MODEL = 1024
A_W = 512
CONV_A_K = 3
DN_HEADS = 4
DN_DK = 128
DN_DV = 128
DN_QK = DN_HEADS * DN_DK
DN_V = DN_HEADS * DN_DV
DN_CONV_CH = 2 * DN_QK + DN_V
DN_CONV_K = 4
MEM_TOKENS = 256
XA_HEADS = 4
XA_DH = 128
XA_W = XA_HEADS * XA_DH
D_FF = 2816
EPS = 1e-6

LANE = 128
CHUNK = 128
TAIL = 8
NB_P = 4
TM = 512
GATE_BLK = 256
PROJ_BLK = 512
ATTN_RESERVE = 4

W1 = 3 * A_W + DN_CONV_CH + DN_V
OFF_QKV = 3 * A_W
OFF_Z = OFF_QKV + DN_CONV_CH
W2 = XA_W + 3 * D_MODEL
OFF_G = XA_W
N_AB = 2 * DN_HEADS
P_Z = 3 * A_W
P_XQ = P_Z + DN_V
P_W = P_XQ + XA_W

VMEM_LIMIT = 60 * 1024 * 1024


def _cparams(sem):
    return pltpu.CompilerParams(dimension_semantics=sem, vmem_limit_bytes=VMEM_LIMIT)


def _resident(shape):
    return pl.BlockSpec(shape, lambda *_: (0,) * len(shape), pipeline_mode=pl.Buffered(1))


def _mm(a, b):
    return jnp.dot(a.astype(BF16), b.astype(BF16), preferred_element_type=F32)


def _mm_nt(a, b):
    return lax.dot_general(a.astype(BF16), b.astype(BF16), (((1,), (1,)), ((), ())),
                           preferred_element_type=F32)


def _mm_tn(a, b):
    return lax.dot_general(a.astype(BF16), b.astype(BF16), (((0,), (0,)), ((), ())),
                           preferred_element_type=F32)


def _rms(x, g):
    return x * lax.rsqrt(jnp.mean(x * x, axis=-1, keepdims=True) + EPS) * g


def _silu(x):
    return x * jax.nn.sigmoid(x)


def _softplus(x):
    return jnp.maximum(x, 0.0) + jnp.log1p(jnp.exp(-jnp.abs(x)))


def _memkv_kernel(x_ref, g_ref, w_ref, k_ref, v_ref, kb_ref, vb_ref):
    xn = _rms(x_ref[...], g_ref[...]).astype(BF16)
    kv = jnp.dot(xn, w_ref[...], preferred_element_type=F32)
    k = kv[:, :XA_W]
    v = kv[:, XA_W:]
    tm = k.shape[0]
    for h in range(XA_HEADS):
        rows = pl.ds(h, tm, stride=XA_HEADS)
        k_ref[rows, :] = k[:, h * LANE:(h + 1) * LANE]
        v_ref[rows, :] = v[:, h * LANE:(h + 1) * LANE]
    kb_ref[...] = k.astype(BF16)
    vb_ref[...] = v.astype(BF16)


def _memkv(mem2d, gain, w, tm):
    t, d = mem2d.shape
    blk = pl.BlockSpec((tm, XA_W), lambda i: (i, 0))
    blk_rows = pl.BlockSpec((tm * XA_HEADS, XA_DH), lambda i: (i, 0))
    return pl.pallas_call(
        _memkv_kernel,
        grid=(t // tm,),
        in_specs=[pl.BlockSpec((tm, d), lambda i: (i, 0)),
                  pl.BlockSpec((1, d), lambda i: (0, 0)),
                  pl.BlockSpec((d, 2 * XA_W), lambda i: (0, 0))],
        out_specs=[blk_rows, blk_rows, blk, blk],
        out_shape=[jax.ShapeDtypeStruct((t * XA_HEADS, XA_DH), F32),
                   jax.ShapeDtypeStruct((t * XA_HEADS, XA_DH), F32),
                   jax.ShapeDtypeStruct((t, XA_W), BF16), jax.ShapeDtypeStruct((t, XA_W), BF16)],
        compiler_params=_cparams(("arbitrary",)),
        name="memkv",
    )(mem2d, gain, w)


def _log2(n):
    return n.bit_length() - 1


def _dn_masks(seg):
    r = lax.broadcasted_iota(jnp.int32, (CHUNK, CHUNK), 0)
    c = lax.broadcasted_iota(jnp.int32, (CHUNK, CHUNK), 1)
    ls = _log2(seg)
    same = (r >> ls) == (c >> ls)
    base = min(8, seg)
    lb = _log2(base)
    m = {
        "causal": (same & (r >= c)).astype(F32),
        "strict": (same & (r > c)).astype(F32),
        "eye": (r == c).astype(F32),
        "neg_diag": -((r >> lb) == (c >> lb)).astype(F32),
        "off": {},
        "base": base,
    }
    s = base
    while s < seg:
        l1, l2 = _log2(s), _log2(2 * s)
        m["off"][s] = (((r >> l2) == (c >> l2)) & ((r >> l1) != (c >> l1))).astype(F32)
        s *= 2
    return m


def _each(f, *lists):
    return [f(*args) for args in zip(*lists)]


def _low_rows(x, s):
    return jnp.concatenate([x[i + s:i + 2 * s] for i in range(0, x.shape[0], 2 * s)], axis=0)


def _merge_low(x, low, s):
    parts = []
    for j, i in enumerate(range(0, x.shape[0], 2 * s)):
        parts += [x[i:i + s], low[j * s:(j + 1) * s]]
    return jnp.concatenate(parts, axis=0)


def _spread_low(low, s):
    return _merge_low(jnp.zeros((2 * low.shape[0], low.shape[1]), low.dtype), low, s)


def _tri_inv(a_list, m, seg, tick):
    add = lambda x, y: x + y
    b = _each(lambda a: a * m["neg_diag"], a_list)
    p = _each(lambda x: m["eye"] + x, b)
    b2 = _each(_mm, b, b)
    tick()
    p = _each(add, p, _each(_mm, p, b2))
    tick()
    if m["base"] == 8:
        b4 = _each(_mm, b2, b2)
        tick()
        p = _each(add, p, _each(_mm, p, b4))
        tick()
    s = m["base"]
    while s < seg:
        low = lambda t, s=s: _low_rows(t, s)
        x = _each(_mm, _each(lambda a, s=s: low(a) * low(m["off"][s]), a_list), p)
        tick()
        r = _each(_mm, _each(low, p), _each(lambda y, s=s: _spread_low(y, s), x))
        p = _each(lambda t, y, s=s: _merge_low(t, low(t) - y, s), p, r)
        tick()
        s *= 2
    return p


def _seg_scan(x, seg, reverse):
    n = x.shape[1]
    pos = lax.broadcasted_iota(jnp.int32, x.shape, 1) & (seg - 1)
    s = 1
    while s < seg:
        shifted = pltpu.roll(x, n - s if reverse else s, 1)
        x = x + jnp.where(pos < seg - s if reverse else pos >= s, shifted, 0.0)
        s *= 2
    return x


def _dn_gates(ab, alog8, dtb8, seg):
    abt = ab.T[0:TAIL]
    g = -jnp.exp(alog8) * _softplus(abt + dtb8)
    d = _seg_scan(g, seg, False)
    dl = d + _seg_scan(g, seg, True) - g
    beta = jax.nn.sigmoid(abt)
    pad = jnp.zeros((CHUNK - 3 * TAIL, CHUNK), F32)
    return d, dl, jnp.concatenate([d, dl, beta, pad], axis=0).T


def _l2n(x, scale=1.0):
    return x * (lax.rsqrt(jnp.sum(x * x, axis=-1, keepdims=True) + EPS) * scale)


def _dn_intra(q, k, v, d_col, d_row, beta_col, m, seg, tick=lambda: None):
    q = _each(lambda x: _l2n(x, DN_DK ** -0.5), q)
    k = _each(_l2n, k)
    gamma = _each(lambda dc, dr: jnp.exp((dc - dr) * m["causal"]) * m["causal"], d_col, d_row)
    kk = _each(_mm_nt, k, k)
    a = _each(lambda bc, x, g: (bc * x) * g * m["strict"], beta_col, kk, gamma)
    t = _tri_inv(a, m, seg, tick)
    rhs = _each(lambda vv, kx, bc, dc: jnp.concatenate([vv * bc, kx * (bc * jnp.exp(dc))], axis=1),
                v, k, beta_col, d_col)
    sol = _each(_mm, t, rhs)
    u = [x[:, :DN_DV] for x in sol]
    w = [x[:, DN_DV:] for x in sol]
    qk = _each(lambda x, g: x * g, _each(_mm_nt, q, k), gamma)
    return q, k, u, w, qk


def _dn_out(o, z, dnorm):
    return _rms(o, dnorm) * _silu(z)


def _head_lists(qkv, z, d, cols):
    out = [[] for _ in range(8)]
    for h in range(DN_HEADS):
        beta_lane = 2 * TAIL + DN_HEADS + h
        vals = (qkv[:, h * LANE:(h + 1) * LANE],
                qkv[:, DN_QK + h * LANE:DN_QK + (h + 1) * LANE],
                qkv[:, 2 * DN_QK + h * LANE:2 * DN_QK + (h + 1) * LANE],
                None if z is None else z[:, h * LANE:(h + 1) * LANE],
                cols[:, h:h + 1], d[h:h + 1, :], cols[:, TAIL + h:TAIL + h + 1],
                cols[:, beta_lane:beta_lane + 1])
        for lst, val in zip(out, vals):
            lst.append(val)
    return out


def _causal_conv(x, tail_ref, b, wts, width):
    c, w = x.shape
    tiles = jnp.concatenate([tail_ref[b][None], x.reshape(c // TAIL, TAIL, w)], axis=0)
    sub = lax.broadcasted_iota(jnp.int32, (1, TAIL, 1), 1)
    acc = None
    for i in range(width):
        s = width - 1 - i
        if s == 0:
            y = tiles[1:]
        else:
            r = pltpu.roll(tiles, s, 1)
            y = jnp.where(sub >= s, r[1:], r[:-1])
        term = wts[i:i + 1][None] * y
        acc = term if acc is None else acc + term
    tail_ref[b] = tiles[c // TAIL]
    return acc.reshape(c, w)


def _front_prompt_kernel(x_ref, gmix_ref, w1_ref, w2_ref, wab_ref, mk_ref, mv_ref, caw_ref, dcw_ref,
                         alog_ref, dtb_ref, dnorm_ref, wb_ref, wo_ref,
                         x1_ref, ca_ref, dc_ref, s_ref, utail, qtail, ybuf, pbuf, gbuf, *, nb):
    c = CHUNK
    rows = nb * c
    t_idx = pl.program_id(1)

    @pl.when(t_idx == 0)
    def _():
        utail[...] = jnp.zeros(utail.shape, F32)
        qtail[...] = jnp.zeros(qtail.shape, F32)
        s_ref[...] = jnp.zeros(s_ref.shape, F32)

    x = x_ref[...].reshape(rows, D_MODEL)
    xn = _rms(x, gmix_ref[...]).astype(BF16)
    proj = lambda w_ref, lo, hi: jnp.dot(xn, w_ref[:, lo:hi], preferred_element_type=F32)

    queue = []

    def enqueue(dst, w_ref, src, dst_lo, width, act=None):
        def run():
            r = proj(w_ref, src, src + width)
            dst[:, dst_lo:dst_lo + width] = r if act is None else act(r)
        queue.append(run)

    for lo in range(0, 3 * A_W, PROJ_BLK):
        enqueue(pbuf, w1_ref, lo, lo, PROJ_BLK)
    enqueue(pbuf, w1_ref, OFF_Z, P_Z, DN_V)
    enqueue(pbuf, w2_ref, 0, P_XQ, XA_W)
    for lo in range(0, 3 * D_MODEL, GATE_BLK):
        enqueue(gbuf, w2_ref, OFF_G + lo, lo, GATE_BLK, jax.nn.sigmoid)
    queue.reverse()

    def tick(keep=ATTN_RESERVE):
        if len(queue) > keep:
            queue.pop()()

    m = _dn_masks(c)
    caw = caw_ref[...]
    dcw = dcw_ref[...]
    dnorm = dnorm_ref[...]
    pq_all = proj(w1_ref, OFF_QKV, OFF_Z)
    pab_all = jnp.dot(xn, wab_ref[...], preferred_element_type=F32)

    qkv_l, gates_l = [], []
    for b in range(nb):
        rb = slice(b * c, (b + 1) * c)
        tick()
        qkv_in = pq_all[rb]
        dc_ref[b] = qkv_in[c - (DN_CONV_K - 1):]
        qkv_l.append(_silu(_causal_conv(qkv_in, qtail, b, dcw, DN_CONV_K)))
        gates_l.append(_dn_gates(pab_all[rb], alog_ref[...], dtb_ref[...], c))

    for b in range(nb):
        rb = slice(b * c, (b + 1) * c)
        tick()
        u_in = pbuf[rb, A_W:2 * A_W] * pbuf[rb, 2 * A_W:3 * A_W]
        ca_ref[b] = u_in[c - (CONV_A_K - 1):]
        ybuf[rb, 0:A_W] = (pbuf[rb, 0:A_W] * _causal_conv(u_in, utail, b, caw, CONV_A_K)).astype(BF16)

    lists = [[] for _ in range(8)]
    for b in range(nb):
        d, _, cols = gates_l[b]
        for lst, val in zip(lists, _head_lists(qkv_l[b], None, d, cols)):
            lst.extend(val)
    idx = [(b, h) for b in range(nb) for h in range(DN_HEADS)]
    q, k, v, _, d_col, d_row, dl_col, beta_col = lists
    q, k, u, w, qk = _dn_intra(q, k, v, d_col, d_row, beta_col, m, c, tick)
    s_old = [s_ref[b, h] for b, h in idx]
    vn = _each(lambda ux, wx, s: ux - _mm(wx, s), u, w, s_old)
    tick()
    o = _each(lambda qx, dc, y, s, vx: _mm(jnp.concatenate([qx * jnp.exp(dc), y], axis=1),
                                           jnp.concatenate([s, vx], axis=0)), q, d_col, qk, s_old, vn)
    tick()
    s_new = _each(lambda s, dlc, kx, dc, vx: s * jnp.exp(dlc[0:1, :]) + _mm_tn(kx * jnp.exp(dlc - dc), vx),
                  s_old, dl_col, k, d_col, vn)
    for (b, h), sx, ox in zip(idx, s_new, o):
        s_ref[b, h] = sx
        zx = pbuf[b * c:(b + 1) * c, P_Z + h * LANE:P_Z + (h + 1) * LANE]
        ybuf[b * c:(b + 1) * c, A_W + h * LANE:A_W + (h + 1) * LANE] = _dn_out(ox, zx, dnorm).astype(BF16)

    for b in range(nb):
        tick(0)
        heads = range(XA_HEADS)
        sc = [_mm_nt(pbuf[b * c:(b + 1) * c, P_XQ + h * LANE:P_XQ + (h + 1) * LANE],
                     mk_ref[b, :, h * LANE:(h + 1) * LANE]) * (XA_DH ** -0.5) for h in heads]
        e = _each(lambda x_: jnp.exp(x_ - jnp.max(x_, axis=-1, keepdims=True)), sc)
        inv = _each(lambda x_: 1.0 / jnp.sum(x_, axis=-1, keepdims=True), e)
        for h, ex, ix in zip(heads, e, inv):
            ybuf[b * c:(b + 1) * c, A_W + DN_V + h * LANE:A_W + DN_V + (h + 1) * LANE] = (
                _mm(ex, mv_ref[b, :, h * LANE:(h + 1) * LANE]) * ix).astype(BF16)
    while queue:
        tick(0)

    merged = None
    for j, (lo, hi) in enumerate(((0, A_W), (A_W, A_W + DN_V), (A_W + DN_V, A_W + DN_V + XA_W))):
        term = gbuf[:, j * D_MODEL:(j + 1) * D_MODEL] * jnp.dot(ybuf[:, lo:hi], wb_ref[lo:hi, :],
                                                                preferred_element_type=F32)
        merged = term if merged is None else merged + term
    x1 = x + jnp.dot(merged.astype(BF16), wo_ref[...], preferred_element_type=F32)
    x1_ref[...] = x1.reshape(nb, c, D_MODEL)


def _front_prompt(x, gmix, w1, w2, wab, mkb, mvb, caw, dcw, alog_row, dtb_row, dnorm, wb, wo, nb):
    bsz, length, d = x.shape
    c = CHUNK
    return pl.pallas_call(
        functools.partial(_front_prompt_kernel, nb=nb),
        grid=(bsz // nb, length // c),
        in_specs=[pl.BlockSpec((nb, c, d), lambda g, t: (g, t, 0)),
                  _resident((1, d)), _resident(w1.shape), _resident(w2.shape), _resident(wab.shape),
                  pl.BlockSpec((nb, MEM_TOKENS, XA_W), lambda g, t: (g, 0, 0), pipeline_mode=pl.Buffered(1)),
                  pl.BlockSpec((nb, MEM_TOKENS, XA_W), lambda g, t: (g, 0, 0), pipeline_mode=pl.Buffered(1)),
                  _resident((CONV_A_K, A_W)), _resident((DN_CONV_K, DN_CONV_CH)),
                  _resident((TAIL, LANE)), _resident((TAIL, LANE)), _resident((1, LANE)),
                  _resident(wb.shape), _resident(wo.shape)],
        out_specs=[pl.BlockSpec((nb, c, d), lambda g, t: (g, t, 0)),
                   pl.BlockSpec((nb, CONV_A_K - 1, A_W), lambda g, t: (g, 0, 0)),
                   pl.BlockSpec((nb, DN_CONV_K - 1, DN_CONV_CH), lambda g, t: (g, 0, 0)),
                   pl.BlockSpec((nb, DN_HEADS, DN_DK, DN_DV), lambda g, t: (g, 0, 0, 0))],
        out_shape=[jax.ShapeDtypeStruct((bsz, length, d), F32),
                   jax.ShapeDtypeStruct((bsz, CONV_A_K - 1, A_W), F32),
                   jax.ShapeDtypeStruct((bsz, DN_CONV_K - 1, DN_CONV_CH), F32),
                   jax.ShapeDtypeStruct((bsz, DN_HEADS, DN_DK, DN_DV), F32)],
        scratch_shapes=[pltpu.VMEM((nb, TAIL, A_W), F32), pltpu.VMEM((nb, TAIL, DN_CONV_CH), F32),
                        pltpu.VMEM((nb * c, A_W + DN_V + XA_W), BF16), pltpu.VMEM((nb * c, P_W), F32),
                        pltpu.VMEM((nb * c, 3 * d), F32)],
        compiler_params=_cparams(("arbitrary", "arbitrary")),
        name="front_prompt",
    )(x, gmix, w1, w2, wab, mkb, mvb, caw, dcw, alog_row, dtb_row, dnorm, wb, wo)


def _proj_kernel(x_ref, g_ref, w1_ref, w2_ref, wab_ref, p1_ref, p2_ref, pab_ref):
    xn = _rms(x_ref[...], g_ref[...]).astype(BF16)
    p1_ref[...] = jnp.dot(xn, w1_ref[...], preferred_element_type=F32)
    p2_ref[...] = jnp.dot(xn, w2_ref[...], preferred_element_type=F32)
    pab_ref[...] = jnp.dot(xn, wab_ref[...], preferred_element_type=F32)


def _proj(x2d, gain, w1, w2, wab, tm):
    t, d = x2d.shape
    row = lambda n: pl.BlockSpec((tm, n), lambda i: (i, 0))
    return pl.pallas_call(
        _proj_kernel,
        grid=(t // tm,),
        in_specs=[row(d), _resident((1, d)), _resident(w1.shape), _resident(w2.shape), _resident(wab.shape)],
        out_specs=[row(W1), row(W2), row(LANE)],
        out_shape=[jax.ShapeDtypeStruct((t, W1), F32), jax.ShapeDtypeStruct((t, W2), F32),
                   jax.ShapeDtypeStruct((t, LANE), F32)],
        compiler_params=_cparams(("arbitrary",)),
        name="proj",
    )(x2d, gain, w1, w2, wab)


SEQ_S = 4
NB_S = CHUNK // SEQ_S
NB_ATTN_S = 16


def _seg_conv(x, e, wts, width):
    rows, w = x.shape
    xt = x.reshape(rows // TAIL, TAIL, w)
    et = e.reshape(rows // TAIL, TAIL, w)
    tmod = lax.broadcasted_iota(jnp.int32, (1, TAIL, 1), 1) & (SEQ_S - 1)
    acc = None
    for i in range(width):
        s = width - 1 - i
        term = xt if s == 0 else jnp.where(tmod >= s, pltpu.roll(xt, s, 1), 0.0)
        if i < width - 1:
            hist = et if i == 0 else pltpu.roll(et, TAIL - i, 1)
            term = term + jnp.where(tmod < SEQ_S - i, hist, 0.0)
        term = wts[i:i + 1][None] * term
        acc = term if acc is None else acc + term
    return acc.reshape(rows, w)


def _branch_sample_kernel(pa_ref, pq_ref, pz_ref, pab_ref, ea_ref, eq_ref, s0_ref, caw_ref, dcw_ref,
                          alog_ref, dtb_ref, dnorm_ref, mexp2_ref, mexpt_ref, yad_ref, u_ref, s_ref):
    c = CHUNK
    m = _dn_masks(SEQ_S)

    pa = pa_ref[...]
    u_in = pa[:, A_W:2 * A_W] * pa[:, 2 * A_W:3 * A_W]
    u_ref[...] = u_in
    conv = _seg_conv(u_in, ea_ref[...], caw_ref[...], CONV_A_K)
    yad_ref[:, 0:A_W] = (pa[:, 0:A_W] * conv).astype(BF16)

    qkv = _silu(_seg_conv(pq_ref[...], eq_ref[...], dcw_ref[...], DN_CONV_K))
    d, dl, cols = _dn_gates(pab_ref[...], alog_ref[...], dtb_ref[...], SEQ_S)
    dec_t = jnp.exp(dl)
    dnorm = dnorm_ref[...]

    wide = NB_S * DN_DK
    mexp2 = mexp2_ref[...]
    mexp_t = mexpt_ref[...]

    q, k, v, z, d_col, d_row, dl_col, beta_col = _head_lists(qkv, pz_ref[...], d, cols)
    q, k, u, w, qk = _dn_intra(q, k, v, d_col, d_row, beta_col, m, SEQ_S)
    heads = list(range(DN_HEADS))
    s_old = [s0_ref[:, h].reshape(wide, DN_DV) for h in heads]
    x_exp = _each(lambda wx, qx, dc: jnp.concatenate(
        [jnp.concatenate([wx, qx * jnp.exp(dc)], axis=0).astype(BF16)] * NB_S, axis=1) * mexp2, w, q, d_col)
    ws = _each(_mm, x_exp, s_old)
    vn = _each(lambda ux, x: ux - x[:c], u, ws)
    o = _each(lambda x, y, vx: x[c:] + _mm(y, vx), ws, qk, vn)
    k_exp = _each(lambda kx, dlc, dc: jnp.concatenate(
        [(kx * jnp.exp(dlc - dc)).T.astype(BF16)] * NB_S, axis=0) * mexp_t, k, dl_col, d_col)
    dec = [jnp.concatenate([jnp.broadcast_to(dec_t[h:h + 1, SEQ_S * b:SEQ_S * b + 1], (DN_DK, DN_DV))
                            for b in range(NB_S)], axis=0) for h in heads]
    s_new = _each(lambda s, dx, kx, vx: s * dx + _mm(kx, vx), s_old, dec, k_exp, vn)
    for h, sx, ox, zx in zip(heads, s_new, o, z):
        s_ref[:, h] = sx.reshape(NB_S, DN_DK, DN_DV)
        yad_ref[:, A_W + h * LANE:A_W + (h + 1) * LANE] = _dn_out(ox, zx, dnorm).astype(BF16)


def _branch_sample(p1, pab, ea, eq, state, caw, dcw, alog_row, dtb_row, dnorm):
    t = p1.shape[0]
    c = CHUNK
    full = lambda shape: pl.BlockSpec(shape, lambda i: (0,) * len(shape))
    wide = NB_S * DN_DK
    owner = np.arange(wide) // DN_DK
    seq = (np.arange(2 * c) % c) // SEQ_S
    mexp2 = jnp.asarray(seq[:, None] == owner[None, :], BF16)
    mexp_t = jnp.asarray(owner[:, None] == seq[None, :c], BF16)
    return pl.pallas_call(
        _branch_sample_kernel,
        grid=(t // c,),
        in_specs=[pl.BlockSpec((c, 3 * A_W), lambda i: (i, 0)),
                  pl.BlockSpec((c, DN_CONV_CH), lambda i: (i, OFF_QKV // DN_CONV_CH)),
                  pl.BlockSpec((c, DN_V), lambda i: (i, OFF_Z // DN_V)),
                  pl.BlockSpec((c, LANE), lambda i: (i, 0)),
                  pl.BlockSpec((c, A_W), lambda i: (i, 0)),
                  pl.BlockSpec((c, DN_CONV_CH), lambda i: (i, 0)),
                  pl.BlockSpec((NB_S, DN_HEADS, DN_DK, DN_DV), lambda i: (i, 0, 0, 0)),
                  full((CONV_A_K, A_W)), full((DN_CONV_K, DN_CONV_CH)),
                  full((TAIL, LANE)), full((TAIL, LANE)), full((1, LANE)),
                  _resident(mexp2.shape), _resident(mexp_t.shape)],
        out_specs=[pl.BlockSpec((c, A_W + DN_V), lambda i: (i, 0)),
                   pl.BlockSpec((c, A_W), lambda i: (i, 0)),
                   pl.BlockSpec((NB_S, DN_HEADS, DN_DK, DN_DV), lambda i: (i, 0, 0, 0))],
        out_shape=[jax.ShapeDtypeStruct((t, A_W + DN_V), BF16),
                   jax.ShapeDtypeStruct((t, A_W), F32),
                   jax.ShapeDtypeStruct(state.shape, F32)],
        compiler_params=_cparams(("arbitrary",)),
        name="branch_sample",
    )(p1, p1, p1, pab, ea, eq, state, caw, dcw, alog_row, dtb_row, dnorm, mexp2, mexp_t)


def _attn_sample_kernel(q_ref, k_ref, v_ref, o_ref):
    for h in range(XA_HEADS):
        sl = slice(h * LANE, (h + 1) * LANE)
        rows = pl.ds(h, MEM_TOKENS, stride=XA_HEADS)
        q = q_ref[:, :, sl].astype(BF16)
        s = jnp.einsum("bqd,bkd->bqk", q, k_ref[:, rows, :].astype(BF16),
                       preferred_element_type=F32) * (XA_DH ** -0.5)
        e = jnp.exp(s - jnp.max(s, axis=-1, keepdims=True))
        p = e / jnp.sum(e, axis=-1, keepdims=True)
        o_ref[:, :, sl] = jnp.einsum("bqk,bkd->bqd", p.astype(BF16), v_ref[:, rows, :].astype(BF16),
                                     preferred_element_type=F32).astype(BF16)


def _attn_sample(q3, ck, cv, nb):
    bsz, length, _ = q3.shape
    return pl.pallas_call(
        _attn_sample_kernel,
        grid=(bsz // nb,),
        in_specs=[pl.BlockSpec((nb, length, XA_W), lambda i: (i, 0, 0)),
                  pl.BlockSpec((nb, MEM_TOKENS * XA_HEADS, XA_DH), lambda i: (i, 0, 0)),
                  pl.BlockSpec((nb, MEM_TOKENS * XA_HEADS, XA_DH), lambda i: (i, 0, 0))],
        out_specs=pl.BlockSpec((nb, length, XA_W), lambda i: (i, 0, 0)),
        out_shape=jax.ShapeDtypeStruct((bsz, length, XA_W), BF16),
        compiler_params=_cparams(("arbitrary",)),
        name="attn_sample",
    )(q3, ck, cv)


def _merge_kernel(yad_ref, ym_ref, p2_ref, x_ref, wb_ref, wo_ref, o_ref):
    yad = yad_ref[...]
    gate = lambda j: jax.nn.sigmoid(p2_ref[:, OFF_G + j * D_MODEL:OFF_G + (j + 1) * D_MODEL])
    merged = (gate(0) * jnp.dot(yad[:, :A_W], wb_ref[0:A_W, :], preferred_element_type=F32)
              + gate(1) * jnp.dot(yad[:, A_W:], wb_ref[A_W:A_W + DN_V, :], preferred_element_type=F32)
              + gate(2) * jnp.dot(ym_ref[...], wb_ref[A_W + DN_V:, :], preferred_element_type=F32))
    o_ref[...] = x_ref[...] + jnp.dot(merged.astype(BF16), wo_ref[...], preferred_element_type=F32)


MXU_K = 256
FF_EDGES = (0, 6 * MXU_K, D_FF)
FF_SPLIT = len(FF_EDGES) - 1
FF_SUB = 256
TM_FF = 4 * FF_SUB


def _ffn_kernel(x_ref, gf_ref, wu_ref, wd_ref, gl_ref, o_ref):
    n_sub = x_ref.shape[0] // FF_SUB

    def block(xn, acc, j):
        lo, hi = FF_EDGES[j], FF_EDGES[j + 1]
        gate = jnp.dot(xn, wu_ref[:, lo:hi], preferred_element_type=F32)
        up = jnp.dot(xn, wu_ref[:, D_FF + lo:D_FF + hi], preferred_element_type=F32)
        hid = (_silu(gate) * up).astype(BF16)
        return acc + jnp.dot(hid, wd_ref[lo:hi, :], preferred_element_type=F32)

    state = [None] * n_sub
    for i in range(n_sub + 1):
        if i < n_sub:
            x = x_ref[i * FF_SUB:(i + 1) * FF_SUB, :]
            xn = _rms(x, gf_ref[...]).astype(BF16)
            state[i] = (xn, block(xn, x, 0))
        if i > 0:
            xn, acc = state[i - 1]
            for j in range(1, FF_SPLIT):
                acc = block(xn, acc, j)
            o_ref[(i - 1) * FF_SUB:i * FF_SUB, :] = _rms(acc, gl_ref[...])


def _ffn(x2d, gf, wu, wd, gl, tm):
    t, d = x2d.shape
    return pl.pallas_call(
        _ffn_kernel,
        grid=(t // tm,),
        in_specs=[pl.BlockSpec((tm, d), lambda i: (i, 0)),
                  _resident((1, d)), _resident(wu.shape), _resident(wd.shape), _resident((1, d))],
        out_specs=pl.BlockSpec((tm, d), lambda i: (i, 0)),
        out_shape=jax.ShapeDtypeStruct((t, d), F32),
        compiler_params=_cparams(("arbitrary",)),
        name="ffn",
    )(x2d, gf, wu, wd, gl)


def _tail_sample_kernel(yad_ref, ym_ref, p2_ref, x_ref, wb_ref, wo_ref, gf_ref, wu_ref, wd_ref, gl_ref,
                        o_ref, x1_buf):
    _merge_kernel(yad_ref, ym_ref, p2_ref, x_ref, wb_ref, wo_ref, x1_buf)
    _ffn_kernel(x1_buf, gf_ref, wu_ref, wd_ref, gl_ref, o_ref)


def _tail_sample(yad, ym, p2, x2d, wb, wo, gf, wu, wd, gl):
    t, d = x2d.shape
    args = (yad, ym, p2, x2d, wb, wo, gf, wu, wd, gl)
    return pl.pallas_call(
        _tail_sample_kernel,
        grid=(1,),
        in_specs=[_resident(a.shape) for a in args],
        out_specs=pl.BlockSpec((t, d), lambda i: (0, 0)),
        out_shape=jax.ShapeDtypeStruct((t, d), F32),
        scratch_shapes=[pltpu.VMEM((t, d), F32)],
        compiler_params=_cparams(("arbitrary",)),
        name="tail_sample",
    )(*args)


def _head_rows(v):
    col = jnp.zeros((TAIL, 1), F32).at[:v.shape[0], 0].set(v.astype(F32))
    return jnp.broadcast_to(col, (TAIL, LANE))


def kernel(x_prompt, x_sample, mem_prompt, state_conv_a, state_dn_conv, state_dn, cache_mem_k, cache_mem_v,
           norm_mix, w_in, conv_a_w, dn_conv_w, dn_a_log, dn_dt_bias, dn_norm, norm_mem, w_mem_kv, w_branch,
           w_o, norm_ffn, w_ffn_up, w_ffn_down, norm_final):
    bp, lp, d = x_prompt.shape
    bs, ls, _ = x_sample.shape
    assert norm_mix.shape[0] == 1 and ls == SEQ_S and lp % CHUNK == 0 and (bs * ls) % CHUNK == 0
    assert w_in.shape[2] == W1 + N_AB + W2

    w = w_in[0]
    w1 = w[:, :W1].astype(BF16)
    w2 = w[:, W1 + N_AB:].astype(BF16)
    wab = jnp.pad(w[:, W1:W1 + N_AB], ((0, 0), (0, LANE - N_AB))).astype(BF16)
    wb = w_branch[0].astype(BF16)
    wo = w_o[0].astype(BF16)
    wu = w_ffn_up[0].astype(BF16)
    wd = w_ffn_down[0].astype(BF16)
    wkv = w_mem_kv[0].astype(BF16)
    g_mix = norm_mix[0][None, :]
    g_ffn = norm_ffn[0][None, :]
    g_fin = norm_final[None, :]
    g_mem = norm_mem[0][None, :]
    caw = conv_a_w[0]
    dcw = dn_conv_w[0]
    alog_row = _head_rows(dn_a_log[0])
    dtb_row = _head_rows(dn_dt_bias[0])
    dnorm = dn_norm[0][None, :]

    tp = bp * lp
    mk, mv, mkb, mvb = _memkv(mem_prompt.reshape(bp * MEM_TOKENS, d), g_mem, wkv, TM)
    x1_p, ca_p, dc_p, s_p = _front_prompt(x_prompt, g_mix, w1, w2, wab, mkb.reshape(bp, MEM_TOKENS, XA_W),
                                          mvb.reshape(bp, MEM_TOKENS, XA_W), caw, dcw, alog_row, dtb_row, dnorm,
                                          wb, wo, NB_P)

    ts = bs * ls
    xs2 = x_sample.reshape(ts, d)
    p1_s, p2_s, pab_s = _proj(xs2, g_mix, w1, w2, wab, CHUNK)
    ea = jnp.pad(state_conv_a[0], ((0, 0), (0, ls - (CONV_A_K - 1)), (0, 0))).reshape(ts, A_W)
    eq = jnp.pad(state_dn_conv[0], ((0, 0), (0, ls - (DN_CONV_K - 1)), (0, 0))).reshape(ts, DN_CONV_CH)
    yad_s, u_s, s_s = _branch_sample(p1_s, pab_s, ea, eq, state_dn[0], caw, dcw, alog_row, dtb_row, dnorm)
    ym_s = _attn_sample(p2_s[:, :XA_W].reshape(bs, ls, XA_W),
                        cache_mem_k.reshape(bs, MEM_TOKENS * XA_HEADS, XA_DH),
                        cache_mem_v.reshape(bs, MEM_TOKENS * XA_HEADS, XA_DH), NB_ATTN_S)
    ca_s = u_s.reshape(bs, ls, A_W)[:, ls - (CONV_A_K - 1):]
    dc_s = p1_s[:, OFF_QKV:OFF_Z].reshape(bs, ls, DN_CONV_CH)[:, ls - (DN_CONV_K - 1):]

    y_p = _ffn(x1_p.reshape(tp, d), g_ffn, wu, wd, g_fin, TM_FF).reshape(bp, lp, d)
    y_s = _tail_sample(yad_s, ym_s.reshape(ts, XA_W), p2_s, xs2, wb, wo, g_ffn, wu, wd, g_fin).reshape(bs, ls, d)

    return (y_p, y_s, ca_p[None], dc_p[None], s_p[None],
            mk.reshape(1, bp, MEM_TOKENS, XA_HEADS, XA_DH), mv.reshape(1, bp, MEM_TOKENS, XA_HEADS, XA_DH),
            ca_s[None], dc_s[None], s_s[None])
```

```python
import functools

import jax
import jax.numpy as jnp
import numpy as np
from jax import lax
from jax.experimental import pallas as pl
from jax.experimental.pallas import tpu as pltpu

F32 = jnp.float32
BF16 = jnp.bfloat16

D_MODEL = 1024
A_W = 512
CONV_A_K = 3
DN_HEADS = 4
DN_DK = 128
DN_DV = 128
DN_QK = DN_HEADS * DN_DK
DN_V = DN_HEADS * DN_DV
DN_CONV_CH = 2 * DN_QK + DN_V
DN_CONV_K = 4
MEM_TOKENS = 256
XA_HEADS = 4
XA_DH = 128
XA_W = XA_HEADS * XA_DH
D_FF = 2816
EPS = 1e-6

LANE = 128
CHUNK = 128
TAIL = 8
NB_P = 4
TM = 512
GATE_BLK = 256
PROJ_BLK = 512
ATTN_RESERVE = 4

W1 = 3 * A_W + DN_CONV_CH + DN_V
OFF_QKV = 3 * A_W
OFF_Z = OFF_QKV + DN_CONV_CH
W2 = XA_W + 3 * D_MODEL
OFF_G = XA_W
N_AB = 2 * DN_HEADS
P_Z = 3 * A_W
P_XQ = P_Z + DN_V
P_W = P_XQ + XA_W

VMEM_LIMIT = 60 * 1024 * 1024


def _cparams(sem):
    return pltpu.CompilerParams(dimension_semantics=sem, vmem_limit_bytes=VMEM_LIMIT)


def _resident(shape):
    return pl.BlockSpec(shape, lambda *_: (0,) * len(shape), pipeline_mode=pl.Buffered(1))


def _mm(a, b):
    return jnp.dot(a.astype(BF16), b.astype(BF16), preferred_element_type=F32)


def _mm_nt(a, b):
    return lax.dot_general(a.astype(BF16), b.astype(BF16), (((1,), (1,)), ((), ())),
                           preferred_element_type=F32)


def _mm_tn(a, b):
    return lax.dot_general(a.astype(BF16), b.astype(BF16), (((0,), (0,)), ((), ())),
                           preferred_element_type=F32)


def _rms(x, g):
    return x * lax.rsqrt(jnp.mean(x * x, axis=-1, keepdims=True) + EPS) * g


def _silu(x):
    return x * jax.nn.sigmoid(x)


def _softplus(x):
    return jnp.maximum(x, 0.0) + jnp.log1p(jnp.exp(-jnp.abs(x)))


def _memkv_kernel(x_ref, g_ref, w_ref, k_ref, v_ref, kb_ref, vb_ref):
    xn = _rms(x_ref[...], g_ref[...]).astype(BF16)
    kv = jnp.dot(xn, w_ref[...], preferred_element_type=F32)
    k = kv[:, :XA_W]
    v = kv[:, XA_W:]
    tm = k.shape[0]
    for h in range(XA_HEADS):
        rows = pl.ds(h, tm, stride=XA_HEADS)
        k_ref[rows, :] = k[:, h * LANE:(h + 1) * LANE]
        v_ref[rows, :] = v[:, h * LANE:(h + 1) * LANE]
    kb_ref[...] = k.astype(BF16)
    vb_ref[...] = v.astype(BF16)


def _memkv(mem2d, gain, w, tm):
    t, d = mem2d.shape
    blk = pl.BlockSpec((tm, XA_W), lambda i: (i, 0))
    blk_rows = pl.BlockSpec((tm * XA_HEADS, XA_DH), lambda i: (i, 0))
    return pl.pallas_call(
        _memkv_kernel,
        grid=(t // tm,),
        in_specs=[pl.BlockSpec((tm, d), lambda i: (i, 0)),
                  pl.BlockSpec((1, d), lambda i: (0, 0)),
                  pl.BlockSpec((d, 2 * XA_W), lambda i: (0, 0))],
        out_specs=[blk_rows, blk_rows, blk, blk],
        out_shape=[jax.ShapeDtypeStruct((t * XA_HEADS, XA_DH), F32),
                   jax.ShapeDtypeStruct((t * XA_HEADS, XA_DH), F32),
                   jax.ShapeDtypeStruct((t, XA_W), BF16), jax.ShapeDtypeStruct((t, XA_W), BF16)],
        compiler_params=_cparams(("arbitrary",)),
        name="memkv",
    )(mem2d, gain, w)


def _log2(n):
    return n.bit_length() - 1


def _dn_masks(seg):
    r = lax.broadcasted_iota(jnp.int32, (CHUNK, CHUNK), 0)
    c = lax.broadcasted_iota(jnp.int32, (CHUNK, CHUNK), 1)
    ls = _log2(seg)
    same = (r >> ls) == (c >> ls)
    base = min(8, seg)
    lb = _log2(base)
    m = {
        "causal": (same & (r >= c)).astype(F32),
        "strict": (same & (r > c)).astype(F32),
        "eye": (r == c).astype(F32),
        "neg_diag": -((r >> lb) == (c >> lb)).astype(F32),
        "off": {},
        "base": base,
    }
    s = base
    while s < seg:
        l1, l2 = _log2(s), _log2(2 * s)
        m["off"][s] = (((r >> l2) == (c >> l2)) & ((r >> l1) != (c >> l1))).astype(F32)
        s *= 2
    return m


def _each(f, *lists):
    return [f(*args) for args in zip(*lists)]


def _low_rows(x, s):
    return jnp.concatenate([x[i + s:i + 2 * s] for i in range(0, x.shape[0], 2 * s)], axis=0)


def _merge_low(x, low, s):
    parts = []
    for j, i in enumerate(range(0, x.shape[0], 2 * s)):
        parts += [x[i:i + s], low[j * s:(j + 1) * s]]
    return jnp.concatenate(parts, axis=0)


def _spread_low(low, s):
    return _merge_low(jnp.zeros((2 * low.shape[0], low.shape[1]), low.dtype), low, s)


def _tri_inv(a_list, m, seg, tick):
    add = lambda x, y: x + y
    b = _each(lambda a: a * m["neg_diag"], a_list)
    p = _each(lambda x: m["eye"] + x, b)
    b2 = _each(_mm, b, b)
    tick()
    p = _each(add, p, _each(_mm, p, b2))
    tick()
    if m["base"] == 8:
        b4 = _each(_mm, b2, b2)
        tick()
        p = _each(add, p, _each(_mm, p, b4))
        tick()
    s = m["base"]
    while s < seg:
        low = lambda t, s=s: _low_rows(t, s)
        x = _each(_mm, _each(lambda a, s=s: low(a) * low(m["off"][s]), a_list), p)
        tick()
        r = _each(_mm, _each(low, p), _each(lambda y, s=s: _spread_low(y, s), x))
        p = _each(lambda t, y, s=s: _merge_low(t, low(t) - y, s), p, r)
        tick()
        s *= 2
    return p


def _seg_scan(x, seg, reverse):
    n = x.shape[1]
    pos = lax.broadcasted_iota(jnp.int32, x.shape, 1) & (seg - 1)
    s = 1
    while s < seg:
        shifted = pltpu.roll(x, n - s if reverse else s, 1)
        x = x + jnp.where(pos < seg - s if reverse else pos >= s, shifted, 0.0)
        s *= 2
    return x


def _dn_gates(ab, alog8, dtb8, seg):
    abt = ab.T[0:TAIL]
    g = -jnp.exp(alog8) * _softplus(abt + dtb8)
    d = _seg_scan(g, seg, False)
    dl = d + _seg_scan(g, seg, True) - g
    beta = jax.nn.sigmoid(abt)
    pad = jnp.zeros((CHUNK - 3 * TAIL, CHUNK), F32)
    return d, dl, jnp.concatenate([d, dl, beta, pad], axis=0).T


def _l2n(x, scale=1.0):
    return x * (lax.rsqrt(jnp.sum(x * x, axis=-1, keepdims=True) + EPS) * scale)


def _dn_intra(q, k, v, d_col, d_row, beta_col, m, seg, tick=lambda: None):
    q = _each(lambda x: _l2n(x, DN_DK ** -0.5), q)
    k = _each(_l2n, k)
    gamma = _each(lambda dc, dr: jnp.exp((dc - dr) * m["causal"]) * m["causal"], d_col, d_row)
    kk = _each(_mm_nt, k, k)
    a = _each(lambda bc, x, g: (bc * x) * g * m["strict"], beta_col, kk, gamma)
    t = _tri_inv(a, m, seg, tick)
    rhs = _each(lambda vv, kx, bc, dc: jnp.concatenate([vv * bc, kx * (bc * jnp.exp(dc))], axis=1),
                v, k, beta_col, d_col)
    sol = _each(_mm, t, rhs)
    u = [x[:, :DN_DV] for x in sol]
    w = [x[:, DN_DV:] for x in sol]
    qk = _each(lambda x, g: x * g, _each(_mm_nt, q, k), gamma)
    return q, k, u, w, qk


def _dn_out(o, z, dnorm):
    return _rms(o, dnorm) * _silu(z)


def _head_lists(qkv, z, d, cols):
    out = [[] for _ in range(8)]
    for h in range(DN_HEADS):
        beta_lane = 2 * TAIL + DN_HEADS + h
        vals = (qkv[:, h * LANE:(h + 1) * LANE],
                qkv[:, DN_QK + h * LANE:DN_QK + (h + 1) * LANE],
                qkv[:, 2 * DN_QK + h * LANE:2 * DN_QK + (h + 1) * LANE],
                None if z is None else z[:, h * LANE:(h + 1) * LANE],
                cols[:, h:h + 1], d[h:h + 1, :], cols[:, TAIL + h:TAIL + h + 1],
                cols[:, beta_lane:beta_lane + 1])
        for lst, val in zip(out, vals):
            lst.append(val)
    return out


def _causal_conv(x, tail_ref, b, wts, width):
    c, w = x.shape
    tiles = jnp.concatenate([tail_ref[b][None], x.reshape(c // TAIL, TAIL, w)], axis=0)
    sub = lax.broadcasted_iota(jnp.int32, (1, TAIL, 1), 1)
    acc = None
    for i in range(width):
        s = width - 1 - i
        if s == 0:
            y = tiles[1:]
        else:
            r = pltpu.roll(tiles, s, 1)
            y = jnp.where(sub >= s, r[1:], r[:-1])
        term = wts[i:i + 1][None] * y
        acc = term if acc is None else acc + term
    tail_ref[b] = tiles[c // TAIL]
    return acc.reshape(c, w)


def _front_prompt_kernel(x_ref, gmix_ref, w1_ref, w2_ref, wab_ref, mk_ref, mv_ref, caw_ref, dcw_ref,
                         alog_ref, dtb_ref, dnorm_ref, wb_ref, wo_ref,
                         x1_ref, ca_ref, dc_ref, s_ref, utail, qtail, ybuf, pbuf, gbuf, *, nb):
    c = CHUNK
    rows = nb * c
    t_idx = pl.program_id(1)

    @pl.when(t_idx == 0)
    def _():
        utail[...] = jnp.zeros(utail.shape, F32)
        qtail[...] = jnp.zeros(qtail.shape, F32)
        s_ref[...] = jnp.zeros(s_ref.shape, F32)

    x = x_ref[...].reshape(rows, D_MODEL)
    xn = _rms(x, gmix_ref[...]).astype(BF16)
    proj = lambda w_ref, lo, hi: jnp.dot(xn, w_ref[:, lo:hi], preferred_element_type=F32)

    queue = []

    def enqueue(dst, w_ref, src, dst_lo, width, act=None):
        def run():
            r = proj(w_ref, src, src + width)
            dst[:, dst_lo:dst_lo + width] = r if act is None else act(r)
        queue.append(run)

    for lo in range(0, 3 * A_W, PROJ_BLK):
        enqueue(pbuf, w1_ref, lo, lo, PROJ_BLK)
    enqueue(pbuf, w1_ref, OFF_Z, P_Z, DN_V)
    enqueue(pbuf, w2_ref, 0, P_XQ, XA_W)
    for lo in range(0, 3 * D_MODEL, GATE_BLK):
        enqueue(gbuf, w2_ref, OFF_G + lo, lo, GATE_BLK, jax.nn.sigmoid)
    queue.reverse()

    def tick(keep=ATTN_RESERVE):
        if len(queue) > keep:
            queue.pop()()

    m = _dn_masks(c)
    caw = caw_ref[...]
    dcw = dcw_ref[...]
    dnorm = dnorm_ref[...]
    pq_all = proj(w1_ref, OFF_QKV, OFF_Z)
    pab_all = jnp.dot(xn, wab_ref[...], preferred_element_type=F32)

    qkv_l, gates_l = [], []
    for b in range(nb):
        rb = slice(b * c, (b + 1) * c)
        tick()
        qkv_in = pq_all[rb]
        dc_ref[b] = qkv_in[c - (DN_CONV_K - 1):]
        qkv_l.append(_silu(_causal_conv(qkv_in, qtail, b, dcw, DN_CONV_K)))
        gates_l.append(_dn_gates(pab_all[rb], alog_ref[...], dtb_ref[...], c))

    for b in range(nb):
        rb = slice(b * c, (b + 1) * c)
        tick()
        u_in = pbuf[rb, A_W:2 * A_W] * pbuf[rb, 2 * A_W:3 * A_W]
        ca_ref[b] = u_in[c - (CONV_A_K - 1):]
        ybuf[rb, 0:A_W] = (pbuf[rb, 0:A_W] * _causal_conv(u_in, utail, b, caw, CONV_A_K)).astype(BF16)

    lists = [[] for _ in range(8)]
    for b in range(nb):
        d, _, cols = gates_l[b]
        for lst, val in zip(lists, _head_lists(qkv_l[b], None, d, cols)):
            lst.extend(val)
    idx = [(b, h) for b in range(nb) for h in range(DN_HEADS)]
    q, k, v, _, d_col, d_row, dl_col, beta_col = lists
    q, k, u, w, qk = _dn_intra(q, k, v, d_col, d_row, beta_col, m, c, tick)
    s_old = [s_ref[b, h] for b, h in idx]
    vn = _each(lambda ux, wx, s: ux - _mm(wx, s), u, w, s_old)
    tick()
    o = _each(lambda qx, dc, y, s, vx: _mm(jnp.concatenate([qx * jnp.exp(dc), y], axis=1),
                                           jnp.concatenate([s, vx], axis=0)), q, d_col, qk, s_old, vn)
    tick()
    s_new = _each(lambda s, dlc, kx, dc, vx: s * jnp.exp(dlc[0:1, :]) + _mm_tn(kx * jnp.exp(dlc - dc), vx),
                  s_old, dl_col, k, d_col, vn)
    for (b, h), sx, ox in zip(idx, s_new, o):
        s_ref[b, h] = sx
        zx = pbuf[b * c:(b + 1) * c, P_Z + h * LANE:P_Z + (h + 1) * LANE]
        ybuf[b * c:(b + 1) * c, A_W + h * LANE:A_W + (h + 1) * LANE] = _dn_out(ox, zx, dnorm).astype(BF16)

    for b in range(nb):
        tick(0)
        heads = range(XA_HEADS)
        sc = [_mm_nt(pbuf[b * c:(b + 1) * c, P_XQ + h * LANE:P_XQ + (h + 1) * LANE],
                     mk_ref[b, :, h * LANE:(h + 1) * LANE]) * (XA_DH ** -0.5) for h in heads]
        e = _each(lambda x_: jnp.exp(x_ - jnp.max(x_, axis=-1, keepdims=True)), sc)
        inv = _each(lambda x_: 1.0 / jnp.sum(x_, axis=-1, keepdims=True), e)
        for h, ex, ix in zip(heads, e, inv):
            ybuf[b * c:(b + 1) * c, A_W + DN_V + h * LANE:A_W + DN_V + (h + 1) * LANE] = (
                _mm(ex, mv_ref[b, :, h * LANE:(h + 1) * LANE]) * ix).astype(BF16)
    while queue:
        tick(0)

    merged = None
    for j, (lo, hi) in enumerate(((0, A_W), (A_W, A_W + DN_V), (A_W + DN_V, A_W + DN_V + XA_W))):
        term = gbuf[:, j * D_MODEL:(j + 1) * D_MODEL] * jnp.dot(ybuf[:, lo:hi], wb_ref[lo:hi, :],
                                                                preferred_element_type=F32)
        merged = term if merged is None else merged + term
    x1 = x + jnp.dot(merged.astype(BF16), wo_ref[...], preferred_element_type=F32)
    x1_ref[...] = x1.reshape(nb, c, D_MODEL)


def _front_prompt(x, gmix, w1, w2, wab, mkb, mvb, caw, dcw, alog_row, dtb_row, dnorm, wb, wo, nb):
    bsz, length, d = x.shape
    c = CHUNK
    return pl.pallas_call(
        functools.partial(_front_prompt_kernel, nb=nb),
        grid=(bsz // nb, length // c),
        in_specs=[pl.BlockSpec((nb, c, d), lambda g, t: (g, t, 0)),
                  _resident((1, d)), _resident(w1.shape), _resident(w2.shape), _resident(wab.shape),
                  pl.BlockSpec((nb, MEM_TOKENS, XA_W), lambda g, t: (g, 0, 0), pipeline_mode=pl.Buffered(1)),
                  pl.BlockSpec((nb, MEM_TOKENS, XA_W), lambda g, t: (g, 0, 0), pipeline_mode=pl.Buffered(1)),
                  _resident((CONV_A_K, A_W)), _resident((DN_CONV_K, DN_CONV_CH)),
                  _resident((TAIL, LANE)), _resident((TAIL, LANE)), _resident((1, LANE)),
                  _resident(wb.shape), _resident(wo.shape)],
        out_specs=[pl.BlockSpec((nb, c, d), lambda g, t: (g, t, 0)),
                   pl.BlockSpec((nb, CONV_A_K - 1, A_W), lambda g, t: (g, 0, 0)),
                   pl.BlockSpec((nb, DN_CONV_K - 1, DN_CONV_CH), lambda g, t: (g, 0, 0)),
                   pl.BlockSpec((nb, DN_HEADS, DN_DK, DN_DV), lambda g, t: (g, 0, 0, 0))],
        out_shape=[jax.ShapeDtypeStruct((bsz, length, d), F32),
                   jax.ShapeDtypeStruct((bsz, CONV_A_K - 1, A_W), F32),
                   jax.ShapeDtypeStruct((bsz, DN_CONV_K - 1, DN_CONV_CH), F32),
                   jax.ShapeDtypeStruct((bsz, DN_HEADS, DN_DK, DN_DV), F32)],
        scratch_shapes=[pltpu.VMEM((nb, TAIL, A_W), F32), pltpu.VMEM((nb, TAIL, DN_CONV_CH), F32),
                        pltpu.VMEM((nb * c, A_W + DN_V + XA_W), BF16), pltpu.VMEM((nb * c, P_W), F32),
                        pltpu.VMEM((nb * c, 3 * d), F32)],
        compiler_params=_cparams(("arbitrary", "arbitrary")),
        name="front_prompt",
    )(x, gmix, w1, w2, wab, mkb, mvb, caw, dcw, alog_row, dtb_row, dnorm, wb, wo)


def _proj_kernel(x_ref, g_ref, w1_ref, w2_ref, wab_ref, p1_ref, p2_ref, pab_ref, xq_ref):
    xn = _rms(x_ref[...], g_ref[...]).astype(BF16)
    p1_ref[...] = jnp.dot(xn, w1_ref[...], preferred_element_type=F32)
    p2 = jnp.dot(xn, w2_ref[...], preferred_element_type=F32)
    p2_ref[...] = p2
    pab_ref[...] = jnp.dot(xn, wab_ref[...], preferred_element_type=F32)
    for h in range(XA_HEADS):
        xq_ref[h] = p2[:, h * LANE:(h + 1) * LANE]


def _proj(x2d, gain, w1, w2, wab, tm):
    t, d = x2d.shape
    row = lambda n: pl.BlockSpec((tm, n), lambda i: (i, 0))
    return pl.pallas_call(
        _proj_kernel,
        grid=(t // tm,),
        in_specs=[row(d), _resident((1, d)), _resident(w1.shape), _resident(w2.shape), _resident(wab.shape)],
        out_specs=[row(W1), row(W2), row(LANE), pl.BlockSpec((XA_HEADS, tm, LANE), lambda i: (0, i, 0))],
        out_shape=[jax.ShapeDtypeStruct((t, W1), F32), jax.ShapeDtypeStruct((t, W2), F32),
                   jax.ShapeDtypeStruct((t, LANE), F32), jax.ShapeDtypeStruct((XA_HEADS, t, LANE), F32)],
        compiler_params=_cparams(("arbitrary",)),
        name="proj",
    )(x2d, gain, w1, w2, wab)


SEQ_S = 4
NB_S = CHUNK // SEQ_S
NB_ATTN_S = 16


def _seg_conv(x, e, wts, width):
    rows, w = x.shape
    xt = x.reshape(rows // TAIL, TAIL, w)
    et = e.reshape(rows // TAIL, TAIL, w)
    tmod = lax.broadcasted_iota(jnp.int32, (1, TAIL, 1), 1) & (SEQ_S - 1)
    acc = None
    for i in range(width):
        s = width - 1 - i
        term = xt if s == 0 else jnp.where(tmod >= s, pltpu.roll(xt, s, 1), 0.0)
        if i < width - 1:
            hist = et if i == 0 else pltpu.roll(et, TAIL - i, 1)
            term = term + jnp.where(tmod < SEQ_S - i, hist, 0.0)
        term = wts[i:i + 1][None] * term
        acc = term if acc is None else acc + term
    return acc.reshape(rows, w)


def _branch_sample_kernel(pa_ref, pq_ref, pz_ref, pab_ref, ea_ref, eq_ref, s0_ref, caw_ref, dcw_ref,
                          alog_ref, dtb_ref, dnorm_ref, mexp2_ref, mexpt_ref, yad_ref, u_ref, s_ref):
    c = CHUNK
    m = _dn_masks(SEQ_S)

    pa = pa_ref[...]
    u_in = pa[:, A_W:2 * A_W] * pa[:, 2 * A_W:3 * A_W]
    u_ref[...] = u_in
    conv = _seg_conv(u_in, ea_ref[...], caw_ref[...], CONV_A_K)
    yad_ref[:, 0:A_W] = (pa[:, 0:A_W] * conv).astype(BF16)

    qkv = _silu(_seg_conv(pq_ref[...], eq_ref[...], dcw_ref[...], DN_CONV_K))
    d, dl, cols = _dn_gates(pab_ref[...], alog_ref[...], dtb_ref[...], SEQ_S)
    dec_t = jnp.exp(dl)
    dnorm = dnorm_ref[...]

    wide = NB_S * DN_DK
    mexp2 = mexp2_ref[...]
    mexp_t = mexpt_ref[...]

    q, k, v, z, d_col, d_row, dl_col, beta_col = _head_lists(qkv, pz_ref[...], d, cols)
    q, k, u, w, qk = _dn_intra(q, k, v, d_col, d_row, beta_col, m, SEQ_S)
    heads = list(range(DN_HEADS))
    s_old = [s0_ref[:, h].reshape(wide, DN_DV) for h in heads]
    x_exp = _each(lambda wx, qx, dc: jnp.concatenate(
        [jnp.concatenate([wx, qx * jnp.exp(dc)], axis=0).astype(BF16)] * NB_S, axis=1) * mexp2, w, q, d_col)
    ws = _each(_mm, x_exp, s_old)
    vn = _each(lambda ux, x: ux - x[:c], u, ws)
    o = _each(lambda x, y, vx: x[c:] + _mm(y, vx), ws, qk, vn)
    k_exp = _each(lambda kx, dlc, dc: jnp.concatenate(
        [(kx * jnp.exp(dlc - dc)).T.astype(BF16)] * NB_S, axis=0) * mexp_t, k, dl_col, d_col)
    dec = [jnp.concatenate([jnp.broadcast_to(dec_t[h:h + 1, SEQ_S * b:SEQ_S * b + 1], (DN_DK, DN_DV))
                            for b in range(NB_S)], axis=0) for h in heads]
    s_new = _each(lambda s, dx, kx, vx: s * dx + _mm(kx, vx), s_old, dec, k_exp, vn)
    for h, sx, ox, zx in zip(heads, s_new, o, z):
        s_ref[:, h] = sx.reshape(NB_S, DN_DK, DN_DV)
        yad_ref[:, A_W + h * LANE:A_W + (h + 1) * LANE] = _dn_out(ox, zx, dnorm).astype(BF16)


def _branch_sample(p1, pab, ea, eq, state, caw, dcw, alog_row, dtb_row, dnorm):
    t = p1.shape[0]
    c = CHUNK
    full = lambda shape: pl.BlockSpec(shape, lambda i: (0,) * len(shape))
    wide = NB_S * DN_DK
    owner = np.arange(wide) // DN_DK
    seq = (np.arange(2 * c) % c) // SEQ_S
    mexp2 = jnp.asarray(seq[:, None] == owner[None, :], BF16)
    mexp_t = jnp.asarray(owner[:, None] == seq[None, :c], BF16)
    return pl.pallas_call(
        _branch_sample_kernel,
        grid=(t // c,),
        in_specs=[pl.BlockSpec((c, 3 * A_W), lambda i: (i, 0)),
                  pl.BlockSpec((c, DN_CONV_CH), lambda i: (i, OFF_QKV // DN_CONV_CH)),
                  pl.BlockSpec((c, DN_V), lambda i: (i, OFF_Z // DN_V)),
                  pl.BlockSpec((c, LANE), lambda i: (i, 0)),
                  pl.BlockSpec((c, A_W), lambda i: (i, 0)),
                  pl.BlockSpec((c, DN_CONV_CH), lambda i: (i, 0)),
                  pl.BlockSpec((NB_S, DN_HEADS, DN_DK, DN_DV), lambda i: (i, 0, 0, 0)),
                  full((CONV_A_K, A_W)), full((DN_CONV_K, DN_CONV_CH)),
                  full((TAIL, LANE)), full((TAIL, LANE)), full((1, LANE)),
                  _resident(mexp2.shape), _resident(mexp_t.shape)],
        out_specs=[pl.BlockSpec((c, A_W + DN_V), lambda i: (i, 0)),
                   pl.BlockSpec((c, A_W), lambda i: (i, 0)),
                   pl.BlockSpec((NB_S, DN_HEADS, DN_DK, DN_DV), lambda i: (i, 0, 0, 0))],
        out_shape=[jax.ShapeDtypeStruct((t, A_W + DN_V), BF16),
                   jax.ShapeDtypeStruct((t, A_W), F32),
                   jax.ShapeDtypeStruct(state.shape, F32)],
        compiler_params=_cparams(("arbitrary",)),
        name="branch_sample",
    )(p1, p1, p1, pab, ea, eq, state, caw, dcw, alog_row, dtb_row, dnorm, mexp2, mexp_t)


def _attn_sample_kernel(q_ref, k_ref, v_ref, o_ref):
    for h in range(XA_HEADS):
        rows = pl.ds(h, MEM_TOKENS, stride=XA_HEADS)
        s = jnp.einsum("bqd,bkd->bqk", q_ref[h].astype(BF16), k_ref[:, rows, :].astype(BF16),
                       preferred_element_type=F32) * (XA_DH ** -0.5)
        e = jnp.exp(s - jnp.max(s, axis=-1, keepdims=True))
        p = e / jnp.sum(e, axis=-1, keepdims=True)
        o_ref[h] = jnp.einsum("bqk,bkd->bqd", p.astype(BF16), v_ref[:, rows, :].astype(BF16),
                              preferred_element_type=F32)


def _attn_sample(q4, ck, cv, nb):
    _, bsz, length, _ = q4.shape
    head_major = pl.BlockSpec((XA_HEADS, nb, length, XA_DH), lambda i: (0, i, 0, 0))
    return pl.pallas_call(
        _attn_sample_kernel,
        grid=(bsz // nb,),
        in_specs=[head_major,
                  pl.BlockSpec((nb, MEM_TOKENS * XA_HEADS, XA_DH), lambda i: (i, 0, 0)),
                  pl.BlockSpec((nb, MEM_TOKENS * XA_HEADS, XA_DH), lambda i: (i, 0, 0))],
        out_specs=head_major,
        out_shape=jax.ShapeDtypeStruct(q4.shape, F32),
        compiler_params=_cparams(("arbitrary",)),
        name="attn_sample",
    )(q4, ck, cv)


def _merge_kernel(yad_ref, ym_ref, p2_ref, x_ref, wb_ref, wo_ref, o_ref):
    yad = yad_ref[...]
    ym = jnp.concatenate([ym_ref[h] for h in range(XA_HEADS)], axis=1).astype(BF16)
    gate = lambda j: jax.nn.sigmoid(p2_ref[:, OFF_G + j * D_MODEL:OFF_G + (j + 1) * D_MODEL])
    merged = (gate(0) * jnp.dot(yad[:, :A_W], wb_ref[0:A_W, :], preferred_element_type=F32)
              + gate(1) * jnp.dot(yad[:, A_W:], wb_ref[A_W:A_W + DN_V, :], preferred_element_type=F32)
              + gate(2) * jnp.dot(ym, wb_ref[A_W + DN_V:, :], preferred_element_type=F32))
    o_ref[...] = x_ref[...] + jnp.dot(merged.astype(BF16), wo_ref[...], preferred_element_type=F32)


MXU_K = 256
FF_EDGES = (0, 6 * MXU_K, D_FF)
FF_SPLIT = len(FF_EDGES) - 1
FF_SUB = 256
TM_FF = 4 * FF_SUB


def _ffn_kernel(x_ref, gf_ref, wu_ref, wd_ref, gl_ref, o_ref):
    n_sub = x_ref.shape[0] // FF_SUB

    def block(xn, acc, j):
        lo, hi = FF_EDGES[j], FF_EDGES[j + 1]
        gate = jnp.dot(xn, wu_ref[:, lo:hi], preferred_element_type=F32)
        up = jnp.dot(xn, wu_ref[:, D_FF + lo:D_FF + hi], preferred_element_type=F32)
        hid = (_silu(gate) * up).astype(BF16)
        return acc + jnp.dot(hid, wd_ref[lo:hi, :], preferred_element_type=F32)

    state = [None] * n_sub
    for i in range(n_sub + 1):
        if i < n_sub:
            x = x_ref[i * FF_SUB:(i + 1) * FF_SUB, :]
            xn = _rms(x, gf_ref[...]).astype(BF16)
            state[i] = (xn, block(xn, x, 0))
        if i > 0:
            xn, acc = state[i - 1]
            for j in range(1, FF_SPLIT):
                acc = block(xn, acc, j)
            o_ref[(i - 1) * FF_SUB:i * FF_SUB, :] = _rms(acc, gl_ref[...])


def _ffn(x2d, gf, wu, wd, gl, tm):
    t, d = x2d.shape
    return pl.pallas_call(
        _ffn_kernel,
        grid=(t // tm,),
        in_specs=[pl.BlockSpec((tm, d), lambda i: (i, 0)),
                  _resident((1, d)), _resident(wu.shape), _resident(wd.shape), _resident((1, d))],
        out_specs=pl.BlockSpec((tm, d), lambda i: (i, 0)),
        out_shape=jax.ShapeDtypeStruct((t, d), F32),
        compiler_params=_cparams(("arbitrary",)),
        name="ffn",
    )(x2d, gf, wu, wd, gl)


def _tail_sample_kernel(yad_ref, ym_ref, p2_ref, x_ref, wb_ref, wo_ref, gf_ref, wu_ref, wd_ref, gl_ref,
                        o_ref, x1_buf):
    _merge_kernel(yad_ref, ym_ref, p2_ref, x_ref, wb_ref, wo_ref, x1_buf)
    _ffn_kernel(x1_buf, gf_ref, wu_ref, wd_ref, gl_ref, o_ref)


def _tail_sample(yad, ym, p2, x2d, wb, wo, gf, wu, wd, gl):
    t, d = x2d.shape
    args = (yad, ym, p2, x2d, wb, wo, gf, wu, wd, gl)
    return pl.pallas_call(
        _tail_sample_kernel,
        grid=(1,),
        in_specs=[_resident(a.shape) for a in args],
        out_specs=pl.BlockSpec((t, d), lambda i: (0, 0)),
        out_shape=jax.ShapeDtypeStruct((t, d), F32),
        scratch_shapes=[pltpu.VMEM((t, d), F32)],
        compiler_params=_cparams(("arbitrary",)),
        name="tail_sample",
    )(*args)


def _head_rows(v):
    col = jnp.zeros((TAIL, 1), F32).at[:v.shape[0], 0].set(v.astype(F32))
    return jnp.broadcast_to(col, (TAIL, LANE))


def kernel(x_prompt, x_sample, mem_prompt, state_conv_a, state_dn_conv, state_dn, cache_mem_k, cache_mem_v,
           norm_mix, w_in, conv_a_w, dn_conv_w, dn_a_log, dn_dt_bias, dn_norm, norm_mem, w_mem_kv, w_branch,
           w_o, norm_ffn, w_ffn_up, w_ffn_down, norm_final):
    bp, lp, d = x_prompt.shape
    bs, ls, _ = x_sample.shape
    assert norm_mix.shape[0] == 1 and ls == SEQ_S and lp % CHUNK == 0 and (bs * ls) % CHUNK == 0
    assert w_in.shape[2] == W1 + N_AB + W2

    w = w_in[0]
    w1 = w[:, :W1].astype(BF16)
    w2 = w[:, W1 + N_AB:].astype(BF16)
    wab = jnp.pad(w[:, W1:W1 + N_AB], ((0, 0), (0, LANE - N_AB))).astype(BF16)
    wb = w_branch[0].astype(BF16)
    wo = w_o[0].astype(BF16)
    wu = w_ffn_up[0].astype(BF16)
    wd = w_ffn_down[0].astype(BF16)
    wkv = w_mem_kv[0].astype(BF16)
    g_mix = norm_mix[0][None, :]
    g_ffn = norm_ffn[0][None, :]
    g_fin = norm_final[None, :]
    g_mem = norm_mem[0][None, :]
    caw = conv_a_w[0]
    dcw = dn_conv_w[0]
    alog_row = _head_rows(dn_a_log[0])
    dtb_row = _head_rows(dn_dt_bias[0])
    dnorm = dn_norm[0][None, :]

    tp = bp * lp
    mk, mv, mkb, mvb = _memkv(mem_prompt.reshape(bp * MEM_TOKENS, d), g_mem, wkv, TM)
    x1_p, ca_p, dc_p, s_p = _front_prompt(x_prompt, g_mix, w1, w2, wab, mkb.reshape(bp, MEM_TOKENS, XA_W),
                                          mvb.reshape(bp, MEM_TOKENS, XA_W), caw, dcw, alog_row, dtb_row, dnorm,
                                          wb, wo, NB_P)

    ts = bs * ls
    xs2 = x_sample.reshape(ts, d)
    p1_s, p2_s, pab_s, xq_s = _proj(xs2, g_mix, w1, w2, wab, CHUNK)
    ea = jnp.pad(state_conv_a[0], ((0, 0), (0, ls - (CONV_A_K - 1)), (0, 0))).reshape(ts, A_W)
    eq = jnp.pad(state_dn_conv[0], ((0, 0), (0, ls - (DN_CONV_K - 1)), (0, 0))).reshape(ts, DN_CONV_CH)
    yad_s, u_s, s_s = _branch_sample(p1_s, pab_s, ea, eq, state_dn[0], caw, dcw, alog_row, dtb_row, dnorm)
    ym_s = _attn_sample(xq_s.reshape(XA_HEADS, bs, ls, XA_DH),
                        cache_mem_k.reshape(bs, MEM_TOKENS * XA_HEADS, XA_DH),
                        cache_mem_v.reshape(bs, MEM_TOKENS * XA_HEADS, XA_DH), NB_ATTN_S)
    ca_s = u_s.reshape(bs, ls, A_W)[:, ls - (CONV_A_K - 1):]
    dc_s = p1_s[:, OFF_QKV:OFF_Z].reshape(bs, ls, DN_CONV_CH)[:, ls - (DN_CONV_K - 1):]

    y_p = _ffn(x1_p.reshape(tp, d), g_ffn, wu, wd, g_fin, TM_FF).reshape(bp, lp, d)
    y_s = _tail_sample(yad_s, ym_s.reshape(XA_HEADS, ts, XA_DH), p2_s, xs2, wb, wo, g_ffn, wu, wd, g_fin)
    y_s = y_s.reshape(bs, ls, d)

    return (y_p, y_s, ca_p[None], dc_p[None], s_p[None],
            mk.reshape(1, bp, MEM_TOKENS, XA_HEADS, XA_DH), mv.reshape(1, bp, MEM_TOKENS, XA_HEADS, XA_DH),
            ca_s[None], dc_s[None], s_s[None])
```

```python
import functools

import jax
import jax.numpy as jnp
import numpy as np
from jax import lax
from jax.experimental import pallas as pl
from jax.experimental.pallas import tpu as pltpu

F32 = jnp.float32
BF16 = jnp.bfloat16

D_MODEL = 1024
A_W = 512
CONV_A_K = 3
DN_HEADS = 4
DN_DK = 128
DN_DV = 128
DN_QK = DN_HEADS * DN_DK
DN_V = DN_HEADS * DN_DV
DN_CONV_CH = 2 * DN_QK + DN_V
DN_CONV_K = 4
MEM_TOKENS = 256
XA_HEADS = 4
XA_DH = 128
XA_W = XA_HEADS * XA_DH
D_FF = 2816
EPS = 1e-6

LANE = 128
CHUNK = 128
TAIL = 8
NB_P = 4
TM = 512
GATE_BLK = 256
PROJ_BLK = 512
ATTN_RESERVE = 4

W1 = 3 * A_W + DN_CONV_CH + DN_V
OFF_QKV = 3 * A_W
OFF_Z = OFF_QKV + DN_CONV_CH
W2 = XA_W + 3 * D_MODEL
OFF_G = XA_W
N_AB = 2 * DN_HEADS
P_Z = 3 * A_W
P_XQ = P_Z + DN_V
P_W = P_XQ + XA_W

VMEM_LIMIT = 60 * 1024 * 1024


def _cparams(sem):
    return pltpu.CompilerParams(dimension_semantics=sem, vmem_limit_bytes=VMEM_LIMIT)


def _resident(shape):
    return pl.BlockSpec(shape, lambda *_: (0,) * len(shape), pipeline_mode=pl.Buffered(1))


def _mm(a, b):
    return jnp.dot(a.astype(BF16), b.astype(BF16), preferred_element_type=F32)


def _mm_nt(a, b):
    return lax.dot_general(a.astype(BF16), b.astype(BF16), (((1,), (1,)), ((), ())),
                           preferred_element_type=F32)


def _mm_tn(a, b):
    return lax.dot_general(a.astype(BF16), b.astype(BF16), (((0,), (0,)), ((), ())),
                           preferred_element_type=F32)


def _rms(x, g):
    return x * lax.rsqrt(jnp.mean(x * x, axis=-1, keepdims=True) + EPS) * g


def _silu(x):
    return x * jax.nn.sigmoid(x)


def _softplus(x):
    return jnp.maximum(x, 0.0) + jnp.log1p(jnp.exp(-jnp.abs(x)))


def _memkv_kernel(x_ref, g_ref, w_ref, k_ref, v_ref, kb_ref, vb_ref):
    xn = _rms(x_ref[...], g_ref[...]).astype(BF16)
    kv = jnp.dot(xn, w_ref[...], preferred_element_type=F32)
    k = kv[:, :XA_W]
    v = kv[:, XA_W:]
    tm = k.shape[0]
    for h in range(XA_HEADS):
        rows = pl.ds(h, tm, stride=XA_HEADS)
        k_ref[rows, :] = k[:, h * LANE:(h + 1) * LANE]
        v_ref[rows, :] = v[:, h * LANE:(h + 1) * LANE]
    kb_ref[...] = k.astype(BF16)
    vb_ref[...] = v.astype(BF16)


def _memkv(mem2d, gain, w, tm):
    t, d = mem2d.shape
    blk = pl.BlockSpec((tm, XA_W), lambda i: (i, 0))
    blk_rows = pl.BlockSpec((tm * XA_HEADS, XA_DH), lambda i: (i, 0))
    return pl.pallas_call(
        _memkv_kernel,
        grid=(t // tm,),
        in_specs=[pl.BlockSpec((tm, d), lambda i: (i, 0)),
                  pl.BlockSpec((1, d), lambda i: (0, 0)),
                  pl.BlockSpec((d, 2 * XA_W), lambda i: (0, 0))],
        out_specs=[blk_rows, blk_rows, blk, blk],
        out_shape=[jax.ShapeDtypeStruct((t * XA_HEADS, XA_DH), F32),
                   jax.ShapeDtypeStruct((t * XA_HEADS, XA_DH), F32),
                   jax.ShapeDtypeStruct((t, XA_W), BF16), jax.ShapeDtypeStruct((t, XA_W), BF16)],
        compiler_params=_cparams(("arbitrary",)),
        name="memkv",
    )(mem2d, gain, w)


def _log2(n):
    return n.bit_length() - 1


def _dn_masks(seg):
    r = lax.broadcasted_iota(jnp.int32, (CHUNK, CHUNK), 0)
    c = lax.broadcasted_iota(jnp.int32, (CHUNK, CHUNK), 1)
    ls = _log2(seg)
    same = (r >> ls) == (c >> ls)
    base = min(8, seg)
    lb = _log2(base)
    m = {
        "causal": (same & (r >= c)).astype(F32),
        "strict": (same & (r > c)).astype(F32),
        "eye": (r == c).astype(F32),
        "neg_diag": -((r >> lb) == (c >> lb)).astype(F32),
        "off": {},
        "base": base,
    }
    s = base
    while s < seg:
        l1, l2 = _log2(s), _log2(2 * s)
        m["off"][s] = (((r >> l2) == (c >> l2)) & ((r >> l1) != (c >> l1))).astype(F32)
        s *= 2
    return m


def _each(f, *lists):
    return [f(*args) for args in zip(*lists)]


def _low_rows(x, s):
    return jnp.concatenate([x[i + s:i + 2 * s] for i in range(0, x.shape[0], 2 * s)], axis=0)


def _merge_low(x, low, s):
    parts = []
    for j, i in enumerate(range(0, x.shape[0], 2 * s)):
        parts += [x[i:i + s], low[j * s:(j + 1) * s]]
    return jnp.concatenate(parts, axis=0)


def _spread_low(low, s):
    return _merge_low(jnp.zeros((2 * low.shape[0], low.shape[1]), low.dtype), low, s)


def _tri_inv(a_list, m, seg, tick):
    add = lambda x, y: x + y
    b = _each(lambda a: a * m["neg_diag"], a_list)
    p = _each(lambda x: m["eye"] + x, b)
    b2 = _each(_mm, b, b)
    tick()
    p = _each(add, p, _each(_mm, p, b2))
    tick()
    if m["base"] == 8:
        b4 = _each(_mm, b2, b2)
        tick()
        p = _each(add, p, _each(_mm, p, b4))
        tick()
    s = m["base"]
    while s < seg:
        low = lambda t, s=s: _low_rows(t, s)
        x = _each(_mm, _each(lambda a, s=s: low(a) * low(m["off"][s]), a_list), p)
        tick()
        r = _each(_mm, _each(low, p), _each(lambda y, s=s: _spread_low(y, s), x))
        p = _each(lambda t, y, s=s: _merge_low(t, low(t) - y, s), p, r)
        tick()
        s *= 2
    return p


def _seg_scan(x, seg, reverse):
    n = x.shape[1]
    pos = lax.broadcasted_iota(jnp.int32, x.shape, 1) & (seg - 1)
    s = 1
    while s < seg:
        shifted = pltpu.roll(x, n - s if reverse else s, 1)
        x = x + jnp.where(pos < seg - s if reverse else pos >= s, shifted, 0.0)
        s *= 2
    return x


def _dn_gates(ab, alog8, dtb8, seg):
    abt = ab.T[0:TAIL]
    g = -jnp.exp(alog8) * _softplus(abt + dtb8)
    d = _seg_scan(g, seg, False)
    dl = d + _seg_scan(g, seg, True) - g
    beta = jax.nn.sigmoid(abt)
    pad = jnp.zeros((CHUNK - 3 * TAIL, CHUNK), F32)
    return d, dl, jnp.concatenate([d, dl, beta, pad], axis=0).T


def _l2n(x, scale=1.0):
    return x * (lax.rsqrt(jnp.sum(x * x, axis=-1, keepdims=True) + EPS) * scale)


def _dn_intra(q, k, v, d_col, d_row, beta_col, m, seg, tick=lambda: None):
    q = _each(lambda x: _l2n(x, DN_DK ** -0.5), q)
    k = _each(_l2n, k)
    gamma = _each(lambda dc, dr: jnp.exp((dc - dr) * m["causal"]) * m["causal"], d_col, d_row)
    kk = _each(_mm_nt, k, k)
    a = _each(lambda bc, x, g: (bc * x) * g * m["strict"], beta_col, kk, gamma)
    t = _tri_inv(a, m, seg, tick)
    rhs = _each(lambda vv, kx, bc, dc: jnp.concatenate([vv * bc, kx * (bc * jnp.exp(dc))], axis=1),
                v, k, beta_col, d_col)
    sol = _each(_mm, t, rhs)
    u = [x[:, :DN_DV] for x in sol]
    w = [x[:, DN_DV:] for x in sol]
    qk = _each(lambda x, g: x * g, _each(_mm_nt, q, k), gamma)
    return q, k, u, w, qk


def _dn_out(o, z, dnorm):
    return _rms(o, dnorm) * _silu(z)


def _head_lists(qkv, z, d, cols):
    out = [[] for _ in range(8)]
    for h in range(DN_HEADS):
        beta_lane = 2 * TAIL + DN_HEADS + h
        vals = (qkv[:, h * LANE:(h + 1) * LANE],
                qkv[:, DN_QK + h * LANE:DN_QK + (h + 1) * LANE],
                qkv[:, 2 * DN_QK + h * LANE:2 * DN_QK + (h + 1) * LANE],
                None if z is None else z[:, h * LANE:(h + 1) * LANE],
                cols[:, h:h + 1], d[h:h + 1, :], cols[:, TAIL + h:TAIL + h + 1],
                cols[:, beta_lane:beta_lane + 1])
        for lst, val in zip(out, vals):
            lst.append(val)
    return out


def _causal_conv(x, tail_ref, b, wts, width):
    c, w = x.shape
    tiles = jnp.concatenate([tail_ref[b][None], x.reshape(c // TAIL, TAIL, w)], axis=0)
    sub = lax.broadcasted_iota(jnp.int32, (1, TAIL, 1), 1)
    acc = None
    for i in range(width):
        s = width - 1 - i
        if s == 0:
            y = tiles[1:]
        else:
            r = pltpu.roll(tiles, s, 1)
            y = jnp.where(sub >= s, r[1:], r[:-1])
        term = wts[i:i + 1][None] * y
        acc = term if acc is None else acc + term
    tail_ref[b] = tiles[c // TAIL]
    return acc.reshape(c, w)


def _front_prompt_kernel(x_ref, gmix_ref, w1_ref, w2_ref, wab_ref, mk_ref, mv_ref, caw_ref, dcw_ref,
                         alog_ref, dtb_ref, dnorm_ref, wb_ref, wo_ref,
                         x1_ref, ca_ref, dc_ref, s_ref, utail, qtail, ybuf, pbuf, gbuf, *, nb):
    c = CHUNK
    rows = nb * c
    t_idx = pl.program_id(1)

    @pl.when(t_idx == 0)
    def _():
        utail[...] = jnp.zeros(utail.shape, F32)
        qtail[...] = jnp.zeros(qtail.shape, F32)
        s_ref[...] = jnp.zeros(s_ref.shape, F32)

    x = x_ref[...].reshape(rows, D_MODEL)
    xn = _rms(x, gmix_ref[...]).astype(BF16)
    proj = lambda w_ref, lo, hi: jnp.dot(xn, w_ref[:, lo:hi], preferred_element_type=F32)

    queue = []

    def enqueue(dst, w_ref, src, dst_lo, width, act=None):
        def run():
            r = proj(w_ref, src, src + width)
            dst[:, dst_lo:dst_lo + width] = r if act is None else act(r)
        queue.append(run)

    for lo in range(0, 3 * A_W, PROJ_BLK):
        enqueue(pbuf, w1_ref, lo, lo, PROJ_BLK)
    enqueue(pbuf, w1_ref, OFF_Z, P_Z, DN_V)
    enqueue(pbuf, w2_ref, 0, P_XQ, XA_W)
    for lo in range(0, 3 * D_MODEL, GATE_BLK):
        enqueue(gbuf, w2_ref, OFF_G + lo, lo, GATE_BLK, jax.nn.sigmoid)
    queue.reverse()

    def tick(keep=ATTN_RESERVE):
        if len(queue) > keep:
            queue.pop()()

    m = _dn_masks(c)
    caw = caw_ref[...]
    dcw = dcw_ref[...]
    dnorm = dnorm_ref[...]
    pq_all = proj(w1_ref, OFF_QKV, OFF_Z)
    pab_all = jnp.dot(xn, wab_ref[...], preferred_element_type=F32)

    qkv_l, gates_l = [], []
    for b in range(nb):
        rb = slice(b * c, (b + 1) * c)
        tick()
        qkv_in = pq_all[rb]
        dc_ref[b] = qkv_in[c - (DN_CONV_K - 1):]
        qkv_l.append(_silu(_causal_conv(qkv_in, qtail, b, dcw, DN_CONV_K)))
        gates_l.append(_dn_gates(pab_all[rb], alog_ref[...], dtb_ref[...], c))

    for b in range(nb):
        rb = slice(b * c, (b + 1) * c)
        tick()
        u_in = pbuf[rb, A_W:2 * A_W] * pbuf[rb, 2 * A_W:3 * A_W]
        ca_ref[b] = u_in[c - (CONV_A_K - 1):]
        ybuf[rb, 0:A_W] = (pbuf[rb, 0:A_W] * _causal_conv(u_in, utail, b, caw, CONV_A_K)).astype(BF16)

    lists = [[] for _ in range(8)]
    for b in range(nb):
        d, _, cols = gates_l[b]
        for lst, val in zip(lists, _head_lists(qkv_l[b], None, d, cols)):
            lst.extend(val)
    idx = [(b, h) for b in range(nb) for h in range(DN_HEADS)]
    q, k, v, _, d_col, d_row, dl_col, beta_col = lists
    q, k, u, w, qk = _dn_intra(q, k, v, d_col, d_row, beta_col, m, c, tick)
    s_old = [s_ref[b, h] for b, h in idx]
    vn = _each(lambda ux, wx, s: ux - _mm(wx, s), u, w, s_old)
    tick()
    o = _each(lambda qx, dc, y, s, vx: _mm(jnp.concatenate([qx * jnp.exp(dc), y], axis=1),
                                           jnp.concatenate([s, vx], axis=0)), q, d_col, qk, s_old, vn)
    tick()
    s_new = _each(lambda s, dlc, kx, dc, vx: s * jnp.exp(dlc[0:1, :]) + _mm_tn(kx * jnp.exp(dlc - dc), vx),
                  s_old, dl_col, k, d_col, vn)
    for (b, h), sx, ox in zip(idx, s_new, o):
        s_ref[b, h] = sx
        zx = pbuf[b * c:(b + 1) * c, P_Z + h * LANE:P_Z + (h + 1) * LANE]
        ybuf[b * c:(b + 1) * c, A_W + h * LANE:A_W + (h + 1) * LANE] = _dn_out(ox, zx, dnorm).astype(BF16)

    for b in range(nb):
        tick(0)
        heads = range(XA_HEADS)
        sc = [_mm_nt(pbuf[b * c:(b + 1) * c, P_XQ + h * LANE:P_XQ + (h + 1) * LANE],
                     mk_ref[b, :, h * LANE:(h + 1) * LANE]) * (XA_DH ** -0.5) for h in heads]
        e = _each(lambda x_: jnp.exp(x_ - jnp.max(x_, axis=-1, keepdims=True)), sc)
        inv = _each(lambda x_: 1.0 / jnp.sum(x_, axis=-1, keepdims=True), e)
        for h, ex, ix in zip(heads, e, inv):
            ybuf[b * c:(b + 1) * c, A_W + DN_V + h * LANE:A_W + DN_V + (h + 1) * LANE] = (
                _mm(ex, mv_ref[b, :, h * LANE:(h + 1) * LANE]) * ix).astype(BF16)
    while queue:
        tick(0)

    merged = None
    for j, (lo, hi) in enumerate(((0, A_W), (A_W, A_W + DN_V), (A_W + DN_V, A_W + DN_V + XA_W))):
        term = gbuf[:, j * D_MODEL:(j + 1) * D_MODEL] * jnp.dot(ybuf[:, lo:hi], wb_ref[lo:hi, :],
                                                                preferred_element_type=F32)
        merged = term if merged is None else merged + term
    x1 = x + jnp.dot(merged.astype(BF16), wo_ref[...], preferred_element_type=F32)
    x1_ref[...] = x1.reshape(nb, c, D_MODEL)


def _front_prompt(x, gmix, w1, w2, wab, mkb, mvb, caw, dcw, alog_row, dtb_row, dnorm, wb, wo, nb):
    bsz, length, d = x.shape
    c = CHUNK
    return pl.pallas_call(
        functools.partial(_front_prompt_kernel, nb=nb),
        grid=(bsz // nb, length // c),
        in_specs=[pl.BlockSpec((nb, c, d), lambda g, t: (g, t, 0)),
                  _resident((1, d)), _resident((d, W1)), _resident(w2.shape), _resident(wab.shape),
                  pl.BlockSpec((nb, MEM_TOKENS, XA_W), lambda g, t: (g, 0, 0), pipeline_mode=pl.Buffered(1)),
                  pl.BlockSpec((nb, MEM_TOKENS, XA_W), lambda g, t: (g, 0, 0), pipeline_mode=pl.Buffered(1)),
                  _resident((CONV_A_K, A_W)), _resident((DN_CONV_K, DN_CONV_CH)),
                  _resident((TAIL, LANE)), _resident((TAIL, LANE)), _resident((1, LANE)),
                  _resident(wb.shape), _resident(wo.shape)],
        out_specs=[pl.BlockSpec((nb, c, d), lambda g, t: (g, t, 0)),
                   pl.BlockSpec((nb, CONV_A_K - 1, A_W), lambda g, t: (g, 0, 0)),
                   pl.BlockSpec((nb, DN_CONV_K - 1, DN_CONV_CH), lambda g, t: (g, 0, 0)),
                   pl.BlockSpec((nb, DN_HEADS, DN_DK, DN_DV), lambda g, t: (g, 0, 0, 0))],
        out_shape=[jax.ShapeDtypeStruct((bsz, length, d), F32),
                   jax.ShapeDtypeStruct((bsz, CONV_A_K - 1, A_W), F32),
                   jax.ShapeDtypeStruct((bsz, DN_CONV_K - 1, DN_CONV_CH), F32),
                   jax.ShapeDtypeStruct((bsz, DN_HEADS, DN_DK, DN_DV), F32)],
        scratch_shapes=[pltpu.VMEM((nb, TAIL, A_W), F32), pltpu.VMEM((nb, TAIL, DN_CONV_CH), F32),
                        pltpu.VMEM((nb * c, A_W + DN_V + XA_W), BF16), pltpu.VMEM((nb * c, P_W), F32),
                        pltpu.VMEM((nb * c, 3 * d), F32)],
        compiler_params=_cparams(("arbitrary", "arbitrary")),
        name="front_prompt",
    )(x, gmix, w1, w2, wab, mkb, mvb, caw, dcw, alog_row, dtb_row, dnorm, wb, wo)


def _proj_kernel(x_ref, g_ref, w1_ref, w2_ref, wab_ref, p1_ref, p2_ref, pab_ref, xq_ref):
    xn = _rms(x_ref[...], g_ref[...]).astype(BF16)
    p1_ref[...] = jnp.dot(xn, w1_ref[...], preferred_element_type=F32)
    p2 = jnp.dot(xn, w2_ref[...], preferred_element_type=F32)
    p2_ref[...] = p2
    pab_ref[...] = jnp.dot(xn, wab_ref[...], preferred_element_type=F32)
    for h in range(XA_HEADS):
        xq_ref[h] = p2[:, h * LANE:(h + 1) * LANE]


def _proj(x2d, gain, w1, w2, wab, tm):
    t, d = x2d.shape
    row = lambda n: pl.BlockSpec((tm, n), lambda i: (i, 0))
    return pl.pallas_call(
        _proj_kernel,
        grid=(t // tm,),
        in_specs=[row(d), _resident((1, d)), _resident((d, W1)), _resident(w2.shape), _resident(wab.shape)],
        out_specs=[row(W1), row(W2), row(LANE), pl.BlockSpec((XA_HEADS, tm, LANE), lambda i: (0, i, 0))],
        out_shape=[jax.ShapeDtypeStruct((t, W1), F32), jax.ShapeDtypeStruct((t, W2), F32),
                   jax.ShapeDtypeStruct((t, LANE), F32), jax.ShapeDtypeStruct((XA_HEADS, t, LANE), F32)],
        compiler_params=_cparams(("arbitrary",)),
        name="proj",
    )(x2d, gain, w1, w2, wab)


SEQ_S = 4
NB_S = CHUNK // SEQ_S
NB_ATTN_S = 16


def _seg_conv(x, e, wts, width):
    rows, w = x.shape
    xt = x.reshape(rows // TAIL, TAIL, w)
    et = e.reshape(rows // TAIL, TAIL, w)
    tmod = lax.broadcasted_iota(jnp.int32, (1, TAIL, 1), 1) & (SEQ_S - 1)
    acc = None
    for i in range(width):
        s = width - 1 - i
        term = xt if s == 0 else jnp.where(tmod >= s, pltpu.roll(xt, s, 1), 0.0)
        if i < width - 1:
            hist = et if i == 0 else pltpu.roll(et, TAIL - i, 1)
            term = term + jnp.where(tmod < SEQ_S - i, hist, 0.0)
        term = wts[i:i + 1][None] * term
        acc = term if acc is None else acc + term
    return acc.reshape(rows, w)


def _branch_sample_kernel(pa_ref, pq_ref, pz_ref, pab_ref, ea_ref, eq_ref, s0_ref, caw_ref, dcw_ref,
                          alog_ref, dtb_ref, dnorm_ref, mexp2_ref, mexpt_ref, yad_ref, u_ref, s_ref):
    c = CHUNK
    m = _dn_masks(SEQ_S)

    pa = pa_ref[...]
    u_in = pa[:, A_W:2 * A_W] * pa[:, 2 * A_W:3 * A_W]
    u_ref[...] = u_in
    conv = _seg_conv(u_in, ea_ref[...], caw_ref[...], CONV_A_K)
    yad_ref[:, 0:A_W] = (pa[:, 0:A_W] * conv).astype(BF16)

    qkv = _silu(_seg_conv(pq_ref[...], eq_ref[...], dcw_ref[...], DN_CONV_K))
    d, dl, cols = _dn_gates(pab_ref[...], alog_ref[...], dtb_ref[...], SEQ_S)
    dec_t = jnp.exp(dl)
    dnorm = dnorm_ref[...]

    wide = NB_S * DN_DK
    mexp2 = mexp2_ref[...]
    mexp_t = mexpt_ref[...]

    q, k, v, z, d_col, d_row, dl_col, beta_col = _head_lists(qkv, pz_ref[...], d, cols)
    q, k, u, w, qk = _dn_intra(q, k, v, d_col, d_row, beta_col, m, SEQ_S)
    heads = list(range(DN_HEADS))
    s_old = [s0_ref[:, h].reshape(wide, DN_DV) for h in heads]
    x_exp = _each(lambda wx, qx, dc: jnp.concatenate(
        [jnp.concatenate([wx, qx * jnp.exp(dc)], axis=0).astype(BF16)] * NB_S, axis=1) * mexp2, w, q, d_col)
    ws = _each(_mm, x_exp, s_old)
    vn = _each(lambda ux, x: ux - x[:c], u, ws)
    o = _each(lambda x, y, vx: x[c:] + _mm(y, vx), ws, qk, vn)
    k_exp = _each(lambda kx, dlc, dc: jnp.concatenate(
        [(kx * jnp.exp(dlc - dc)).T.astype(BF16)] * NB_S, axis=0) * mexp_t, k, dl_col, d_col)
    dec = [jnp.concatenate([jnp.broadcast_to(dec_t[h:h + 1, SEQ_S * b:SEQ_S * b + 1], (DN_DK, DN_DV))
                            for b in range(NB_S)], axis=0) for h in heads]
    s_new = _each(lambda s, dx, kx, vx: s * dx + _mm(kx, vx), s_old, dec, k_exp, vn)
    for h, sx, ox, zx in zip(heads, s_new, o, z):
        s_ref[:, h] = sx.reshape(NB_S, DN_DK, DN_DV)
        yad_ref[:, A_W + h * LANE:A_W + (h + 1) * LANE] = _dn_out(ox, zx, dnorm).astype(BF16)


def _branch_sample(p1, pab, ea, eq, state, caw, dcw, alog_row, dtb_row, dnorm):
    t = p1.shape[0]
    c = CHUNK
    full = lambda shape: pl.BlockSpec(shape, lambda i: (0,) * len(shape))
    wide = NB_S * DN_DK
    owner = np.arange(wide) // DN_DK
    seq = (np.arange(2 * c) % c) // SEQ_S
    mexp2 = jnp.asarray(seq[:, None] == owner[None, :], BF16)
    mexp_t = jnp.asarray(owner[:, None] == seq[None, :c], BF16)
    return pl.pallas_call(
        _branch_sample_kernel,
        grid=(t // c,),
        in_specs=[pl.BlockSpec((c, 3 * A_W), lambda i: (i, 0)),
                  pl.BlockSpec((c, DN_CONV_CH), lambda i: (i, OFF_QKV // DN_CONV_CH)),
                  pl.BlockSpec((c, DN_V), lambda i: (i, OFF_Z // DN_V)),
                  pl.BlockSpec((c, LANE), lambda i: (i, 0)),
                  pl.BlockSpec((c, A_W), lambda i: (i, 0)),
                  pl.BlockSpec((c, DN_CONV_CH), lambda i: (i, 0)),
                  pl.BlockSpec((NB_S, DN_HEADS, DN_DK, DN_DV), lambda i: (i, 0, 0, 0)),
                  full((CONV_A_K, A_W)), full((DN_CONV_K, DN_CONV_CH)),
                  full((TAIL, LANE)), full((TAIL, LANE)), full((1, LANE)),
                  _resident(mexp2.shape), _resident(mexp_t.shape)],
        out_specs=[pl.BlockSpec((c, A_W + DN_V), lambda i: (i, 0)),
                   pl.BlockSpec((c, A_W), lambda i: (i, 0)),
                   pl.BlockSpec((NB_S, DN_HEADS, DN_DK, DN_DV), lambda i: (i, 0, 0, 0))],
        out_shape=[jax.ShapeDtypeStruct((t, A_W + DN_V), BF16),
                   jax.ShapeDtypeStruct((t, A_W), F32),
                   jax.ShapeDtypeStruct(state.shape, F32)],
        compiler_params=_cparams(("arbitrary",)),
        name="branch_sample",
    )(p1, p1, p1, pab, ea, eq, state, caw, dcw, alog_row, dtb_row, dnorm, mexp2, mexp_t)


def _attn_sample_kernel(q_ref, k_ref, v_ref, o_ref):
    for h in range(XA_HEADS):
        rows = pl.ds(h, MEM_TOKENS, stride=XA_HEADS)
        s = jnp.einsum("bqd,bkd->bqk", q_ref[h].astype(BF16), k_ref[:, rows, :].astype(BF16),
                       preferred_element_type=F32) * (XA_DH ** -0.5)
        e = jnp.exp(s - jnp.max(s, axis=-1, keepdims=True))
        p = e / jnp.sum(e, axis=-1, keepdims=True)
        o_ref[h] = jnp.einsum("bqk,bkd->bqd", p.astype(BF16), v_ref[:, rows, :].astype(BF16),
                              preferred_element_type=F32)


def _attn_sample(q4, ck, cv, nb):
    _, bsz, length, _ = q4.shape
    head_major = pl.BlockSpec((XA_HEADS, nb, length, XA_DH), lambda i: (0, i, 0, 0))
    return pl.pallas_call(
        _attn_sample_kernel,
        grid=(bsz // nb,),
        in_specs=[head_major,
                  pl.BlockSpec((nb, MEM_TOKENS * XA_HEADS, XA_DH), lambda i: (i, 0, 0)),
                  pl.BlockSpec((nb, MEM_TOKENS * XA_HEADS, XA_DH), lambda i: (i, 0, 0))],
        out_specs=head_major,
        out_shape=jax.ShapeDtypeStruct(q4.shape, F32),
        compiler_params=_cparams(("arbitrary",)),
        name="attn_sample",
    )(q4, ck, cv)


def _merge_kernel(yad_ref, ym_ref, p2_ref, x_ref, wb_ref, wo_ref, o_ref):
    yad = yad_ref[...]
    ym = jnp.concatenate([ym_ref[h] for h in range(XA_HEADS)], axis=1).astype(BF16)
    gate = lambda j: jax.nn.sigmoid(p2_ref[:, OFF_G + j * D_MODEL:OFF_G + (j + 1) * D_MODEL])
    merged = (gate(0) * jnp.dot(yad[:, :A_W], wb_ref[0:A_W, :], preferred_element_type=F32)
              + gate(1) * jnp.dot(yad[:, A_W:], wb_ref[A_W:A_W + DN_V, :], preferred_element_type=F32)
              + gate(2) * jnp.dot(ym, wb_ref[A_W + DN_V:, :], preferred_element_type=F32))
    o_ref[...] = x_ref[...] + jnp.dot(merged.astype(BF16), wo_ref[...], preferred_element_type=F32)


MXU_K = 256
FF_EDGES = (0, 6 * MXU_K, D_FF)
FF_SPLIT = len(FF_EDGES) - 1
FF_SUB = 256
TM_FF = 4 * FF_SUB


def _ffn_kernel(x_ref, gf_ref, wu_ref, wd_ref, gl_ref, o_ref):
    n_sub = x_ref.shape[0] // FF_SUB

    def block(xn, acc, j):
        lo, hi = FF_EDGES[j], FF_EDGES[j + 1]
        gate = jnp.dot(xn, wu_ref[:, lo:hi], preferred_element_type=F32)
        up = jnp.dot(xn, wu_ref[:, D_FF + lo:D_FF + hi], preferred_element_type=F32)
        hid = (_silu(gate) * up).astype(BF16)
        return acc + jnp.dot(hid, wd_ref[lo:hi, :], preferred_element_type=F32)

    state = [None] * n_sub
    for i in range(n_sub + 1):
        if i < n_sub:
            x = x_ref[i * FF_SUB:(i + 1) * FF_SUB, :]
            xn = _rms(x, gf_ref[...]).astype(BF16)
            state[i] = (xn, block(xn, x, 0))
        if i > 0:
            xn, acc = state[i - 1]
            for j in range(1, FF_SPLIT):
                acc = block(xn, acc, j)
            o_ref[(i - 1) * FF_SUB:i * FF_SUB, :] = _rms(acc, gl_ref[...])


def _ffn(x2d, gf, wu, wd, gl, tm):
    t, d = x2d.shape
    return pl.pallas_call(
        _ffn_kernel,
        grid=(t // tm,),
        in_specs=[pl.BlockSpec((tm, d), lambda i: (i, 0)),
                  _resident((1, d)), _resident(wu.shape), _resident(wd.shape), _resident((1, d))],
        out_specs=pl.BlockSpec((tm, d), lambda i: (i, 0)),
        out_shape=jax.ShapeDtypeStruct((t, d), F32),
        compiler_params=_cparams(("arbitrary",)),
        name="ffn",
    )(x2d, gf, wu, wd, gl)


def _tail_sample_kernel(yad_ref, ym_ref, p2_ref, x_ref, wb_ref, wo_ref, gf_ref, wu_ref, wd_ref, gl_ref,
                        o_ref, x1_buf):
    _merge_kernel(yad_ref, ym_ref, p2_ref, x_ref, wb_ref, wo_ref, x1_buf)
    _ffn_kernel(x1_buf, gf_ref, wu_ref, wd_ref, gl_ref, o_ref)


def _tail_sample(yad, ym, p2, x2d, wb, wo, gf, wu, wd, gl):
    t, d = x2d.shape
    args = (yad, ym, p2, x2d, wb, wo, gf, wu, wd, gl)
    return pl.pallas_call(
        _tail_sample_kernel,
        grid=(1,),
        in_specs=[_resident(a.shape) for a in args],
        out_specs=pl.BlockSpec((t, d), lambda i: (0, 0)),
        out_shape=jax.ShapeDtypeStruct((t, d), F32),
        scratch_shapes=[pltpu.VMEM((t, d), F32)],
        compiler_params=_cparams(("arbitrary",)),
        name="tail_sample",
    )(*args)


def _head_rows(v):
    col = jnp.zeros((TAIL, 1), F32).at[:v.shape[0], 0].set(v.astype(F32))
    return jnp.broadcast_to(col, (TAIL, LANE))


def kernel(x_prompt, x_sample, mem_prompt, state_conv_a, state_dn_conv, state_dn, cache_mem_k, cache_mem_v,
           norm_mix, w_in, conv_a_w, dn_conv_w, dn_a_log, dn_dt_bias, dn_norm, norm_mem, w_mem_kv, w_branch,
           w_o, norm_ffn, w_ffn_up, w_ffn_down, norm_final):
    bp, lp, d = x_prompt.shape
    bs, ls, _ = x_sample.shape
    assert norm_mix.shape[0] == 1 and ls == SEQ_S and lp % CHUNK == 0 and (bs * ls) % CHUNK == 0
    assert w_in.shape[2] == W1 + N_AB + W2

    w = w_in[0]
    w1 = w.astype(BF16)
    w2 = w1[:, W1 + N_AB:]
    wab = jnp.pad(w1[:, W1:W1 + N_AB], ((0, 0), (0, LANE - N_AB)))
    wb = w_branch[0].astype(BF16)
    wo = w_o[0].astype(BF16)
    wu = w_ffn_up[0].astype(BF16)
    wd = w_ffn_down[0].astype(BF16)
    wkv = w_mem_kv[0].astype(BF16)
    g_mix = norm_mix[0][None, :]
    g_ffn = norm_ffn[0][None, :]
    g_fin = norm_final[None, :]
    g_mem = norm_mem[0][None, :]
    caw = conv_a_w[0]
    dcw = dn_conv_w[0]
    alog_row = _head_rows(dn_a_log[0])
    dtb_row = _head_rows(dn_dt_bias[0])
    dnorm = dn_norm[0][None, :]

    tp = bp * lp
    mk, mv, mkb, mvb = _memkv(mem_prompt.reshape(bp * MEM_TOKENS, d), g_mem, wkv, TM)
    x1_p, ca_p, dc_p, s_p = _front_prompt(x_prompt, g_mix, w1, w2, wab, mkb.reshape(bp, MEM_TOKENS, XA_W),
                                          mvb.reshape(bp, MEM_TOKENS, XA_W), caw, dcw, alog_row, dtb_row, dnorm,
                                          wb, wo, NB_P)

    ts = bs * ls
    xs2 = x_sample.reshape(ts, d)
    p1_s, p2_s, pab_s, xq_s = _proj(xs2, g_mix, w1, w2, wab, CHUNK)
    ea = jnp.pad(state_conv_a[0], ((0, 0), (0, ls - (CONV_A_K - 1)), (0, 0))).reshape(ts, A_W)
    eq = jnp.pad(state_dn_conv[0], ((0, 0), (0, ls - (DN_CONV_K - 1)), (0, 0))).reshape(ts, DN_CONV_CH)
    yad_s, u_s, s_s = _branch_sample(p1_s, pab_s, ea, eq, state_dn[0], caw, dcw, alog_row, dtb_row, dnorm)
    ym_s = _attn_sample(xq_s.reshape(XA_HEADS, bs, ls, XA_DH),
                        cache_mem_k.reshape(bs, MEM_TOKENS * XA_HEADS, XA_DH),
                        cache_mem_v.reshape(bs, MEM_TOKENS * XA_HEADS, XA_DH), NB_ATTN_S)
    ca_s = u_s.reshape(bs, ls, A_W)[:, ls - (CONV_A_K - 1):]
    dc_s = p1_s[:, OFF_QKV:OFF_Z].reshape(bs, ls, DN_CONV_CH)[:, ls - (DN_CONV_K - 1):]

    y_p = _ffn(x1_p.reshape(tp, d), g_ffn, wu, wd, g_fin, TM_FF).reshape(bp, lp, d)
    y_s = _tail_sample(yad_s, ym_s.reshape(XA_HEADS, ts, XA_DH), p2_s, xs2, wb, wo, g_ffn, wu, wd, g_fin)
    y_s = y_s.reshape(bs, ls, d)

    return (y_p, y_s, ca_p[None], dc_p[None], s_p[None],
            mk.reshape(1, bp, MEM_TOKENS, XA_HEADS, XA_DH), mv.reshape(1, bp, MEM_TOKENS, XA_HEADS, XA_DH),
            ca_s[None], dc_s[None], s_s[None])
```

```python
import functools

import jax
import jax.numpy as jnp
import numpy as np
from jax import lax
from jax.experimental import pallas as pl
from jax.experimental.pallas import tpu as pltpu

F32 = jnp.float32
BF16 = jnp.bfloat16

D_MODEL = 1024
A_W = 512
CONV_A_K = 3
DN_HEADS = 4
DN_DK = 128
DN_DV = 128
DN_QK = DN_HEADS * DN_DK
DN_V = DN_HEADS * DN_DV
DN_CONV_CH = 2 * DN_QK + DN_V
DN_CONV_K = 4
MEM_TOKENS = 256
XA_HEADS = 4
XA_DH = 128
XA_W = XA_HEADS * XA_DH
D_FF = 2816
EPS = 1e-6
MASKED = 1e30

LANE = 128
CHUNK = 128
TAIL = 8
NB_P = 4
TM = 512
GATE_BLK = 256
PROJ_BLK = 512
ATTN_RESERVE = 4

W1 = 3 * A_W + DN_CONV_CH + DN_V
OFF_QKV = 3 * A_W
OFF_Z = OFF_QKV + DN_CONV_CH
W2 = XA_W + 3 * D_MODEL
OFF_G = XA_W
N_AB = 2 * DN_HEADS
P_Z = 3 * A_W
P_XQ = P_Z + DN_V
P_W = P_XQ + XA_W

VMEM_LIMIT = 60 * 1024 * 1024


def _cparams(sem):
    return pltpu.CompilerParams(dimension_semantics=sem, vmem_limit_bytes=VMEM_LIMIT)


def _resident(shape):
    return pl.BlockSpec(shape, lambda *_: (0,) * len(shape), pipeline_mode=pl.Buffered(1))


def _mm(a, b):
    return jnp.dot(a.astype(BF16), b.astype(BF16), preferred_element_type=F32)


def _mm_nt(a, b):
    return lax.dot_general(a.astype(BF16), b.astype(BF16), (((1,), (1,)), ((), ())),
                           preferred_element_type=F32)


def _mm_tn(a, b):
    return lax.dot_general(a.astype(BF16), b.astype(BF16), (((0,), (0,)), ((), ())),
                           preferred_element_type=F32)


def _rms(x, g):
    return x * lax.rsqrt(jnp.mean(x * x, axis=-1, keepdims=True) + EPS) * g


def _silu(x):
    return x * jax.nn.sigmoid(x)


def _softplus(x):
    return jnp.maximum(x, 0.0) + jnp.log1p(jnp.exp(-jnp.abs(x)))


def _memkv_kernel(x_ref, g_ref, w_ref, k_ref, v_ref, kb_ref, vb_ref):
    xn = _rms(x_ref[...], g_ref[...]).astype(BF16)
    kv = jnp.dot(xn, w_ref[...], preferred_element_type=F32)
    k = kv[:, :XA_W]
    v = kv[:, XA_W:]
    tm = k.shape[0]
    for h in range(XA_HEADS):
        rows = pl.ds(h, tm, stride=XA_HEADS)
        k_ref[rows, :] = k[:, h * LANE:(h + 1) * LANE]
        v_ref[rows, :] = v[:, h * LANE:(h + 1) * LANE]
    kb_ref[...] = k.astype(BF16)
    vb_ref[...] = v.astype(BF16)


def _memkv(mem2d, gain, w, tm):
    t, d = mem2d.shape
    blk = pl.BlockSpec((tm, XA_W), lambda i: (i, 0))
    blk_rows = pl.BlockSpec((tm * XA_HEADS, XA_DH), lambda i: (i, 0))
    return pl.pallas_call(
        _memkv_kernel,
        grid=(t // tm,),
        in_specs=[pl.BlockSpec((tm, d), lambda i: (i, 0)),
                  pl.BlockSpec((1, d), lambda i: (0, 0)),
                  pl.BlockSpec((d, 2 * XA_W), lambda i: (0, 0))],
        out_specs=[blk_rows, blk_rows, blk, blk],
        out_shape=[jax.ShapeDtypeStruct((t * XA_HEADS, XA_DH), F32),
                   jax.ShapeDtypeStruct((t * XA_HEADS, XA_DH), F32),
                   jax.ShapeDtypeStruct((t, XA_W), BF16), jax.ShapeDtypeStruct((t, XA_W), BF16)],
        compiler_params=_cparams(("arbitrary",)),
        name="memkv",
    )(mem2d, gain, w)


def _log2(n):
    return n.bit_length() - 1


def _dn_masks(seg):
    r = lax.broadcasted_iota(jnp.int32, (CHUNK, CHUNK), 0)
    c = lax.broadcasted_iota(jnp.int32, (CHUNK, CHUNK), 1)
    ls = _log2(seg)
    same = (r >> ls) == (c >> ls)
    base = min(8, seg)
    lb = _log2(base)
    m = {
        "causal_bias": jnp.where(same & (r >= c), 0.0, -MASKED).astype(F32),
        "strict": (same & (r > c)).astype(F32),
        "eye": (r == c).astype(F32),
        "neg_diag": -((r >> lb) == (c >> lb)).astype(F32),
        "off": {},
        "base": base,
    }
    s = base
    while s < seg:
        l1, l2 = _log2(s), _log2(2 * s)
        m["off"][s] = (((r >> l2) == (c >> l2)) & ((r >> l1) != (c >> l1))).astype(F32)
        s *= 2
    return m


def _each(f, *lists):
    return [f(*args) for args in zip(*lists)]


def _low_rows(x, s):
    return jnp.concatenate([x[i + s:i + 2 * s] for i in range(0, x.shape[0], 2 * s)], axis=0)


def _merge_low(x, low, s):
    parts = []
    for j, i in enumerate(range(0, x.shape[0], 2 * s)):
        parts += [x[i:i + s], low[j * s:(j + 1) * s]]
    return jnp.concatenate(parts, axis=0)


def _spread_low(low, s):
    return _merge_low(jnp.zeros((2 * low.shape[0], low.shape[1]), low.dtype), low, s)


def _tri_inv(a_list, m, seg, tick):
    add = lambda x, y: x + y
    b = _each(lambda a: a * m["neg_diag"], a_list)
    p = _each(lambda x: m["eye"] + x, b)
    b2 = _each(_mm, b, b)
    tick()
    p = _each(add, p, _each(_mm, p, b2))
    tick()
    if m["base"] == 8:
        b4 = _each(_mm, b2, b2)
        tick()
        p = _each(add, p, _each(_mm, p, b4))
        tick()
    s = m["base"]
    while s < seg:
        low = lambda t, s=s: _low_rows(t, s)
        x = _each(_mm, _each(lambda a, s=s: low(a) * low(m["off"][s]), a_list), p)
        tick()
        r = _each(_mm, _each(low, p), _each(lambda y, s=s: _spread_low(y, s), x))
        p = _each(lambda t, y, s=s: _merge_low(t, low(t) - y, s), p, r)
        tick()
        s *= 2
    return p


def _seg_scan(x, seg, reverse):
    n = x.shape[1]
    pos = lax.broadcasted_iota(jnp.int32, x.shape, 1) & (seg - 1)
    s = 1
    while s < seg:
        shifted = pltpu.roll(x, n - s if reverse else s, 1)
        x = x + jnp.where(pos < seg - s if reverse else pos >= s, shifted, 0.0)
        s *= 2
    return x


def _dn_gates(ab, alog8, dtb8, seg):
    abt = ab.T[0:TAIL]
    g = -jnp.exp(alog8) * _softplus(abt + dtb8)
    d = _seg_scan(g, seg, False)
    dl = d + _seg_scan(g, seg, True) - g
    beta = jax.nn.sigmoid(abt)
    pad = jnp.zeros((CHUNK - 3 * TAIL, CHUNK), F32)
    return d, dl, jnp.concatenate([d, dl, beta, pad], axis=0).T


def _l2n(x, scale=1.0):
    return x * (lax.rsqrt(jnp.sum(x * x, axis=-1, keepdims=True) + EPS) * scale)


def _dn_intra(q, k, v, d_col, d_row, beta_col, m, seg, tick=lambda: None):
    q = _each(lambda x: _l2n(x, DN_DK ** -0.5), q)
    k = _each(_l2n, k)
    gamma = _each(lambda dc, dr: jnp.exp((dc - dr) + m["causal_bias"]), d_col, d_row)
    kk = _each(_mm_nt, k, k)
    a = _each(lambda bc, x, g: (bc * x) * g * m["strict"], beta_col, kk, gamma)
    t = _tri_inv(a, m, seg, tick)
    rhs = _each(lambda vv, kx, bc, dc: jnp.concatenate([vv * bc, kx * (bc * jnp.exp(dc))], axis=1),
                v, k, beta_col, d_col)
    sol = _each(_mm, t, rhs)
    u = [x[:, :DN_DV] for x in sol]
    w = [x[:, DN_DV:] for x in sol]
    qk = _each(lambda x, g: x * g, _each(_mm_nt, q, k), gamma)
    return q, k, u, w, qk


def _dn_out(o, z, dnorm):
    return _rms(o, dnorm) * _silu(z)


def _head_lists(qkv, z, d, cols):
    out = [[] for _ in range(8)]
    for h in range(DN_HEADS):
        beta_lane = 2 * TAIL + DN_HEADS + h
        vals = (qkv[:, h * LANE:(h + 1) * LANE],
                qkv[:, DN_QK + h * LANE:DN_QK + (h + 1) * LANE],
                qkv[:, 2 * DN_QK + h * LANE:2 * DN_QK + (h + 1) * LANE],
                None if z is None else z[:, h * LANE:(h + 1) * LANE],
                cols[:, h:h + 1], d[h:h + 1, :], cols[:, TAIL + h:TAIL + h + 1],
                cols[:, beta_lane:beta_lane + 1])
        for lst, val in zip(out, vals):
            lst.append(val)
    return out


def _causal_conv(x, tail_ref, b, wts, width):
    c, w = x.shape
    tiles = jnp.concatenate([tail_ref[b][None], x.reshape(c // TAIL, TAIL, w)], axis=0)
    sub = lax.broadcasted_iota(jnp.int32, (1, TAIL, 1), 1)
    acc = None
    for i in range(width):
        s = width - 1 - i
        if s == 0:
            y = tiles[1:]
        else:
            r = pltpu.roll(tiles, s, 1)
            y = jnp.where(sub >= s, r[1:], r[:-1])
        term = wts[i:i + 1][None] * y
        acc = term if acc is None else acc + term
    tail_ref[b] = tiles[c // TAIL]
    return acc.reshape(c, w)


def _front_prompt_kernel(x_ref, gmix_ref, w1_ref, w2_ref, wab_ref, mk_ref, mv_ref, caw_ref, dcw_ref,
                         alog_ref, dtb_ref, dnorm_ref, wb_ref, wo_ref,
                         x1_ref, ca_ref, dc_ref, s_ref, utail, qtail, ybuf, pbuf, gbuf, *, nb):
    c = CHUNK
    rows = nb * c
    t_idx = pl.program_id(1)

    @pl.when(t_idx == 0)
    def _():
        utail[...] = jnp.zeros(utail.shape, F32)
        qtail[...] = jnp.zeros(qtail.shape, F32)
        s_ref[...] = jnp.zeros(s_ref.shape, F32)

    x = x_ref[...].reshape(rows, D_MODEL)
    xn = _rms(x, gmix_ref[...]).astype(BF16)
    proj = lambda w_ref, lo, hi: jnp.dot(xn, w_ref[:, lo:hi], preferred_element_type=F32)

    queue = []

    def enqueue(dst, w_ref, src, dst_lo, width, act=None):
        def run():
            r = proj(w_ref, src, src + width)
            dst[:, dst_lo:dst_lo + width] = r if act is None else act(r)
        queue.append(run)

    for lo in range(0, 3 * A_W, PROJ_BLK):
        enqueue(pbuf, w1_ref, lo, lo, PROJ_BLK)
    enqueue(pbuf, w1_ref, OFF_Z, P_Z, DN_V)
    enqueue(pbuf, w2_ref, 0, P_XQ, XA_W)
    for lo in range(0, 3 * D_MODEL, GATE_BLK):
        enqueue(gbuf, w2_ref, OFF_G + lo, lo, GATE_BLK, jax.nn.sigmoid)
    queue.reverse()

    def tick(keep=ATTN_RESERVE):
        if len(queue) > keep:
            queue.pop()()

    m = _dn_masks(c)
    caw = caw_ref[...]
    dcw = dcw_ref[...]
    dnorm = dnorm_ref[...]
    pq_all = proj(w1_ref, OFF_QKV, OFF_Z)
    pab_all = jnp.dot(xn, wab_ref[...], preferred_element_type=F32)

    qkv_l, gates_l = [], []
    for b in range(nb):
        rb = slice(b * c, (b + 1) * c)
        tick()
        qkv_in = pq_all[rb]
        dc_ref[b] = qkv_in[c - (DN_CONV_K - 1):]
        qkv_l.append(_silu(_causal_conv(qkv_in, qtail, b, dcw, DN_CONV_K)))
        gates_l.append(_dn_gates(pab_all[rb], alog_ref[...], dtb_ref[...], c))

    for b in range(nb):
        rb = slice(b * c, (b + 1) * c)
        tick()
        u_in = pbuf[rb, A_W:2 * A_W] * pbuf[rb, 2 * A_W:3 * A_W]
        ca_ref[b] = u_in[c - (CONV_A_K - 1):]
        ybuf[rb, 0:A_W] = (pbuf[rb, 0:A_W] * _causal_conv(u_in, utail, b, caw, CONV_A_K)).astype(BF16)

    lists = [[] for _ in range(8)]
    for b in range(nb):
        d, _, cols = gates_l[b]
        for lst, val in zip(lists, _head_lists(qkv_l[b], None, d, cols)):
            lst.extend(val)
    idx = [(b, h) for b in range(nb) for h in range(DN_HEADS)]
    q, k, v, _, d_col, d_row, dl_col, beta_col = lists
    q, k, u, w, qk = _dn_intra(q, k, v, d_col, d_row, beta_col, m, c, tick)
    s_old = [s_ref[b, h] for b, h in idx]
    vn = _each(lambda ux, wx, s: ux - _mm(wx, s), u, w, s_old)
    tick()
    o = _each(lambda qx, dc, y, s, vx: _mm(jnp.concatenate([qx * jnp.exp(dc), y], axis=1),
                                           jnp.concatenate([s, vx], axis=0)), q, d_col, qk, s_old, vn)
    tick()
    s_new = _each(lambda s, dlc, kx, dc, vx: s * jnp.exp(dlc[0:1, :]) + _mm_tn(kx * jnp.exp(dlc - dc), vx),
                  s_old, dl_col, k, d_col, vn)
    for (b, h), sx, ox in zip(idx, s_new, o):
        s_ref[b, h] = sx
        zx = pbuf[b * c:(b + 1) * c, P_Z + h * LANE:P_Z + (h + 1) * LANE]
        ybuf[b * c:(b + 1) * c, A_W + h * LANE:A_W + (h + 1) * LANE] = _dn_out(ox, zx, dnorm).astype(BF16)

    for b in range(nb):
        tick(0)
        heads = range(XA_HEADS)
        sc = [_mm_nt(pbuf[b * c:(b + 1) * c, P_XQ + h * LANE:P_XQ + (h + 1) * LANE],
                     mk_ref[b, :, h * LANE:(h + 1) * LANE]) * (XA_DH ** -0.5) for h in heads]
        e = _each(lambda x_: jnp.exp(x_ - jnp.max(x_, axis=-1, keepdims=True)), sc)
        inv = _each(lambda x_: 1.0 / jnp.sum(x_, axis=-1, keepdims=True), e)
        for h, ex, ix in zip(heads, e, inv):
            ybuf[b * c:(b + 1) * c, A_W + DN_V + h * LANE:A_W + DN_V + (h + 1) * LANE] = (
                _mm(ex, mv_ref[b, :, h * LANE:(h + 1) * LANE]) * ix).astype(BF16)
    while queue:
        tick(0)

    merged = None
    for j, (lo, hi) in enumerate(((0, A_W), (A_W, A_W + DN_V), (A_W + DN_V, A_W + DN_V + XA_W))):
        term = gbuf[:, j * D_MODEL:(j + 1) * D_MODEL] * jnp.dot(ybuf[:, lo:hi], wb_ref[lo:hi, :],
                                                                preferred_element_type=F32)
        merged = term if merged is None else merged + term
    x1 = x + jnp.dot(merged.astype(BF16), wo_ref[...], preferred_element_type=F32)
    x1_ref[...] = x1.reshape(nb, c, D_MODEL)


def _front_prompt(x, gmix, w1, w2, wab, mkb, mvb, caw, dcw, alog_row, dtb_row, dnorm, wb, wo, nb):
    bsz, length, d = x.shape
    c = CHUNK
    return pl.pallas_call(
        functools.partial(_front_prompt_kernel, nb=nb),
        grid=(bsz // nb, length // c),
        in_specs=[pl.BlockSpec((nb, c, d), lambda g, t: (g, t, 0)),
                  _resident((1, d)), _resident((d, W1)), _resident(w2.shape), _resident(wab.shape),
                  pl.BlockSpec((nb, MEM_TOKENS, XA_W), lambda g, t: (g, 0, 0), pipeline_mode=pl.Buffered(1)),
                  pl.BlockSpec((nb, MEM_TOKENS, XA_W), lambda g, t: (g, 0, 0), pipeline_mode=pl.Buffered(1)),
                  _resident((CONV_A_K, A_W)), _resident((DN_CONV_K, DN_CONV_CH)),
                  _resident((TAIL, LANE)), _resident((TAIL, LANE)), _resident((1, LANE)),
                  _resident(wb.shape), _resident(wo.shape)],
        out_specs=[pl.BlockSpec((nb, c, d), lambda g, t: (g, t, 0)),
                   pl.BlockSpec((nb, CONV_A_K - 1, A_W), lambda g, t: (g, 0, 0)),
                   pl.BlockSpec((nb, DN_CONV_K - 1, DN_CONV_CH), lambda g, t: (g, 0, 0)),
                   pl.BlockSpec((nb, DN_HEADS, DN_DK, DN_DV), lambda g, t: (g, 0, 0, 0))],
        out_shape=[jax.ShapeDtypeStruct((bsz, length, d), F32),
                   jax.ShapeDtypeStruct((bsz, CONV_A_K - 1, A_W), F32),
                   jax.ShapeDtypeStruct((bsz, DN_CONV_K - 1, DN_CONV_CH), F32),
                   jax.ShapeDtypeStruct((bsz, DN_HEADS, DN_DK, DN_DV), F32)],
        scratch_shapes=[pltpu.VMEM((nb, TAIL, A_W), F32), pltpu.VMEM((nb, TAIL, DN_CONV_CH), F32),
                        pltpu.VMEM((nb * c, A_W + DN_V + XA_W), BF16), pltpu.VMEM((nb * c, P_W), F32),
                        pltpu.VMEM((nb * c, 3 * d), F32)],
        compiler_params=_cparams(("arbitrary", "arbitrary")),
        name="front_prompt",
    )(x, gmix, w1, w2, wab, mkb, mvb, caw, dcw, alog_row, dtb_row, dnorm, wb, wo)


def _proj_kernel(x_ref, g_ref, w1_ref, w2_ref, wab_ref, p1_ref, p2_ref, pab_ref, xq_ref):
    xn = _rms(x_ref[...], g_ref[...]).astype(BF16)
    p1_ref[...] = jnp.dot(xn, w1_ref[...], preferred_element_type=F32)
    p2 = jnp.dot(xn, w2_ref[...], preferred_element_type=F32)
    p2_ref[...] = p2
    pab_ref[...] = jnp.dot(xn, wab_ref[...], preferred_element_type=F32)
    for h in range(XA_HEADS):
        xq_ref[h] = p2[:, h * LANE:(h + 1) * LANE]


def _proj(x2d, gain, w1, w2, wab, tm):
    t, d = x2d.shape
    row = lambda n: pl.BlockSpec((tm, n), lambda i: (i, 0))
    return pl.pallas_call(
        _proj_kernel,
        grid=(t // tm,),
        in_specs=[row(d), _resident((1, d)), _resident((d, W1)), _resident(w2.shape), _resident(wab.shape)],
        out_specs=[row(W1), row(W2), row(LANE), pl.BlockSpec((XA_HEADS, tm, LANE), lambda i: (0, i, 0))],
        out_shape=[jax.ShapeDtypeStruct((t, W1), F32), jax.ShapeDtypeStruct((t, W2), F32),
                   jax.ShapeDtypeStruct((t, LANE), F32), jax.ShapeDtypeStruct((XA_HEADS, t, LANE), F32)],
        compiler_params=_cparams(("arbitrary",)),
        name="proj",
    )(x2d, gain, w1, w2, wab)


SEQ_S = 4
NB_S = CHUNK // SEQ_S
NB_ATTN_S = 16


def _seg_conv(x, e, wts, width):
    rows, w = x.shape
    xt = x.reshape(rows // TAIL, TAIL, w)
    et = e.reshape(rows // TAIL, TAIL, w)
    tmod = lax.broadcasted_iota(jnp.int32, (1, TAIL, 1), 1) & (SEQ_S - 1)
    acc = None
    for i in range(width):
        s = width - 1 - i
        term = xt if s == 0 else jnp.where(tmod >= s, pltpu.roll(xt, s, 1), 0.0)
        if i < width - 1:
            hist = et if i == 0 else pltpu.roll(et, TAIL - i, 1)
            term = term + jnp.where(tmod < SEQ_S - i, hist, 0.0)
        term = wts[i:i + 1][None] * term
        acc = term if acc is None else acc + term
    return acc.reshape(rows, w)


def _branch_sample_kernel(pa_ref, pq_ref, pz_ref, pab_ref, ea_ref, eq_ref, s0_ref, caw_ref, dcw_ref,
                          alog_ref, dtb_ref, dnorm_ref, mexp2_ref, mexpt_ref, yad_ref, u_ref, s_ref):
    c = CHUNK
    m = _dn_masks(SEQ_S)

    pa = pa_ref[...]
    u_in = pa[:, A_W:2 * A_W] * pa[:, 2 * A_W:3 * A_W]
    u_ref[...] = u_in
    conv = _seg_conv(u_in, ea_ref[...], caw_ref[...], CONV_A_K)
    yad_ref[:, 0:A_W] = (pa[:, 0:A_W] * conv).astype(BF16)

    qkv = _silu(_seg_conv(pq_ref[...], eq_ref[...], dcw_ref[...], DN_CONV_K))
    d, dl, cols = _dn_gates(pab_ref[...], alog_ref[...], dtb_ref[...], SEQ_S)
    dec_t = jnp.exp(dl)
    dnorm = dnorm_ref[...]

    wide = NB_S * DN_DK
    mexp2 = mexp2_ref[...]
    mexp_t = mexpt_ref[...]

    q, k, v, z, d_col, d_row, dl_col, beta_col = _head_lists(qkv, pz_ref[...], d, cols)
    q, k, u, w, qk = _dn_intra(q, k, v, d_col, d_row, beta_col, m, SEQ_S)
    heads = list(range(DN_HEADS))
    s_old = [s0_ref[:, h].reshape(wide, DN_DV) for h in heads]
    x_exp = _each(lambda wx, qx, dc: jnp.concatenate(
        [jnp.concatenate([wx, qx * jnp.exp(dc)], axis=0).astype(BF16)] * NB_S, axis=1) * mexp2, w, q, d_col)
    ws = _each(_mm, x_exp, s_old)
    vn = _each(lambda ux, x: ux - x[:c], u, ws)
    o = _each(lambda x, y, vx: x[c:] + _mm(y, vx), ws, qk, vn)
    k_exp = _each(lambda kx, dlc, dc: jnp.concatenate(
        [(kx * jnp.exp(dlc - dc)).T.astype(BF16)] * NB_S, axis=0) * mexp_t, k, dl_col, d_col)
    dec = [jnp.concatenate([jnp.broadcast_to(dec_t[h:h + 1, SEQ_S * b:SEQ_S * b + 1], (DN_DK, DN_DV))
                            for b in range(NB_S)], axis=0) for h in heads]
    s_new = _each(lambda s, dx, kx, vx: s * dx + _mm(kx, vx), s_old, dec, k_exp, vn)
    for h, sx, ox, zx in zip(heads, s_new, o, z):
        s_ref[:, h] = sx.reshape(NB_S, DN_DK, DN_DV)
        yad_ref[:, A_W + h * LANE:A_W + (h + 1) * LANE] = _dn_out(ox, zx, dnorm).astype(BF16)


def _branch_sample(p1, pab, ea, eq, state, caw, dcw, alog_row, dtb_row, dnorm):
    t = p1.shape[0]
    c = CHUNK
    full = lambda shape: pl.BlockSpec(shape, lambda i: (0,) * len(shape))
    wide = NB_S * DN_DK
    owner = np.arange(wide) // DN_DK
    seq = (np.arange(2 * c) % c) // SEQ_S
    mexp2 = jnp.asarray(seq[:, None] == owner[None, :], BF16)
    mexp_t = jnp.asarray(owner[:, None] == seq[None, :c], BF16)
    return pl.pallas_call(
        _branch_sample_kernel,
        grid=(t // c,),
        in_specs=[pl.BlockSpec((c, 3 * A_W), lambda i: (i, 0)),
                  pl.BlockSpec((c, DN_CONV_CH), lambda i: (i, OFF_QKV // DN_CONV_CH)),
                  pl.BlockSpec((c, DN_V), lambda i: (i, OFF_Z // DN_V)),
                  pl.BlockSpec((c, LANE), lambda i: (i, 0)),
                  pl.BlockSpec((c, A_W), lambda i: (i, 0)),
                  pl.BlockSpec((c, DN_CONV_CH), lambda i: (i, 0)),
                  pl.BlockSpec((NB_S, DN_HEADS, DN_DK, DN_DV), lambda i: (i, 0, 0, 0)),
                  full((CONV_A_K, A_W)), full((DN_CONV_K, DN_CONV_CH)),
                  full((TAIL, LANE)), full((TAIL, LANE)), full((1, LANE)),
                  _resident(mexp2.shape), _resident(mexp_t.shape)],
        out_specs=[pl.BlockSpec((c, A_W + DN_V), lambda i: (i, 0)),
                   pl.BlockSpec((c, A_W), lambda i: (i, 0)),
                   pl.BlockSpec((NB_S, DN_HEADS, DN_DK, DN_DV), lambda i: (i, 0, 0, 0))],
        out_shape=[jax.ShapeDtypeStruct((t, A_W + DN_V), BF16),
                   jax.ShapeDtypeStruct((t, A_W), F32),
                   jax.ShapeDtypeStruct(state.shape, F32)],
        compiler_params=_cparams(("arbitrary",)),
        name="branch_sample",
    )(p1, p1, p1, pab, ea, eq, state, caw, dcw, alog_row, dtb_row, dnorm, mexp2, mexp_t)


def _attn_sample_kernel(q_ref, k_ref, v_ref, o_ref):
    for h in range(XA_HEADS):
        rows = pl.ds(h, MEM_TOKENS, stride=XA_HEADS)
        s = jnp.einsum("bqd,bkd->bqk", q_ref[h].astype(BF16), k_ref[:, rows, :].astype(BF16),
                       preferred_element_type=F32) * (XA_DH ** -0.5)
        e = jnp.exp(s - jnp.max(s, axis=-1, keepdims=True))
        p = e / jnp.sum(e, axis=-1, keepdims=True)
        o_ref[h] = jnp.einsum("bqk,bkd->bqd", p.astype(BF16), v_ref[:, rows, :].astype(BF16),
                              preferred_element_type=F32)


def _attn_sample(q4, ck, cv, nb):
    _, bsz, length, _ = q4.shape
    head_major = pl.BlockSpec((XA_HEADS, nb, length, XA_DH), lambda i: (0, i, 0, 0))
    return pl.pallas_call(
        _attn_sample_kernel,
        grid=(bsz // nb,),
        in_specs=[head_major,
                  pl.BlockSpec((nb, MEM_TOKENS * XA_HEADS, XA_DH), lambda i: (i, 0, 0)),
                  pl.BlockSpec((nb, MEM_TOKENS * XA_HEADS, XA_DH), lambda i: (i, 0, 0))],
        out_specs=head_major,
        out_shape=jax.ShapeDtypeStruct(q4.shape, F32),
        compiler_params=_cparams(("arbitrary",)),
        name="attn_sample",
    )(q4, ck, cv)


def _merge_kernel(yad_ref, ym_ref, p2_ref, x_ref, wb_ref, wo_ref, o_ref):
    yad = yad_ref[...]
    ym = jnp.concatenate([ym_ref[h] for h in range(XA_HEADS)], axis=1).astype(BF16)
    gate = lambda j: jax.nn.sigmoid(p2_ref[:, OFF_G + j * D_MODEL:OFF_G + (j + 1) * D_MODEL])
    merged = (gate(0) * jnp.dot(yad[:, :A_W], wb_ref[0:A_W, :], preferred_element_type=F32)
              + gate(1) * jnp.dot(yad[:, A_W:], wb_ref[A_W:A_W + DN_V, :], preferred_element_type=F32)
              + gate(2) * jnp.dot(ym, wb_ref[A_W + DN_V:, :], preferred_element_type=F32))
    o_ref[...] = x_ref[...] + jnp.dot(merged.astype(BF16), wo_ref[...], preferred_element_type=F32)


MXU_K = 256
FF_EDGES = (0, 6 * MXU_K, D_FF)
FF_SPLIT = len(FF_EDGES) - 1
FF_SUB = 256
TM_FF = 8 * FF_SUB


def _ffn_kernel(x_ref, gf_ref, wu_ref, wd_ref, gl_ref, o_ref):
    n_sub = x_ref.shape[0] // FF_SUB

    def block(xn, acc, j):
        lo, hi = FF_EDGES[j], FF_EDGES[j + 1]
        gate = jnp.dot(xn, wu_ref[:, lo:hi], preferred_element_type=F32)
        up = jnp.dot(xn, wu_ref[:, D_FF + lo:D_FF + hi], preferred_element_type=F32)
        hid = (_silu(gate) * up).astype(BF16)
        return acc + jnp.dot(hid, wd_ref[lo:hi, :], preferred_element_type=F32)

    state = [None] * n_sub
    for i in range(n_sub + 1):
        if i < n_sub:
            x = x_ref[i * FF_SUB:(i + 1) * FF_SUB, :]
            xn = _rms(x, gf_ref[...]).astype(BF16)
            state[i] = (xn, block(xn, x, 0))
        if i > 0:
            xn, acc = state[i - 1]
            for j in range(1, FF_SPLIT):
                acc = block(xn, acc, j)
            o_ref[(i - 1) * FF_SUB:i * FF_SUB, :] = _rms(acc, gl_ref[...])


def _ffn(x2d, gf, wu, wd, gl, tm):
    t, d = x2d.shape
    return pl.pallas_call(
        _ffn_kernel,
        grid=(t // tm,),
        in_specs=[pl.BlockSpec((tm, d), lambda i: (i, 0)),
                  _resident((1, d)), _resident(wu.shape), _resident(wd.shape), _resident((1, d))],
        out_specs=pl.BlockSpec((tm, d), lambda i: (i, 0)),
        out_shape=jax.ShapeDtypeStruct((t, d), F32),
        compiler_params=_cparams(("arbitrary",)),
        name="ffn",
    )(x2d, gf, wu, wd, gl)


def _tail_sample_kernel(yad_ref, ym_ref, p2_ref, x_ref, wb_ref, wo_ref, gf_ref, wu_ref, wd_ref, gl_ref,
                        o_ref, x1_buf):
    _merge_kernel(yad_ref, ym_ref, p2_ref, x_ref, wb_ref, wo_ref, x1_buf)
    _ffn_kernel(x1_buf, gf_ref, wu_ref, wd_ref, gl_ref, o_ref)


def _tail_sample(yad, ym, p2, x2d, wb, wo, gf, wu, wd, gl):
    t, d = x2d.shape
    args = (yad, ym, p2, x2d, wb, wo, gf, wu, wd, gl)
    return pl.pallas_call(
        _tail_sample_kernel,
        grid=(1,),
        in_specs=[_resident(a.shape) for a in args],
        out_specs=pl.BlockSpec((t, d), lambda i: (0, 0)),
        out_shape=jax.ShapeDtypeStruct((t, d), F32),
        scratch_shapes=[pltpu.VMEM((t, d), F32)],
        compiler_params=_cparams(("arbitrary",)),
        name="tail_sample",
    )(*args)


def _head_rows(v):
    col = jnp.zeros((TAIL, 1), F32).at[:v.shape[0], 0].set(v.astype(F32))
    return jnp.broadcast_to(col, (TAIL, LANE))


def kernel(x_prompt, x_sample, mem_prompt, state_conv_a, state_dn_conv, state_dn, cache_mem_k, cache_mem_v,
           norm_mix, w_in, conv_a_w, dn_conv_w, dn_a_log, dn_dt_bias, dn_norm, norm_mem, w_mem_kv, w_branch,
           w_o, norm_ffn, w_ffn_up, w_ffn_down, norm_final):
    bp, lp, d = x_prompt.shape
    bs, ls, _ = x_sample.shape
    assert norm_mix.shape[0] == 1 and ls == SEQ_S and lp % CHUNK == 0 and (bs * ls) % CHUNK == 0
    assert w_in.shape[2] == W1 + N_AB + W2

    w = w_in[0]
    w1 = w.astype(BF16)
    w2 = w1[:, W1 + N_AB:]
    wab = jnp.pad(w1[:, W1:W1 + N_AB], ((0, 0), (0, LANE - N_AB)))
    wb = w_branch[0].astype(BF16)
    wo = w_o[0].astype(BF16)
    wu = w_ffn_up[0].astype(BF16)
    wd = w_ffn_down[0].astype(BF16)
    wkv = w_mem_kv[0].astype(BF16)
    g_mix = norm_mix[0][None, :]
    g_ffn = norm_ffn[0][None, :]
    g_fin = norm_final[None, :]
    g_mem = norm_mem[0][None, :]
    caw = conv_a_w[0]
    dcw = dn_conv_w[0]
    alog_row = _head_rows(dn_a_log[0])
    dtb_row = _head_rows(dn_dt_bias[0])
    dnorm = dn_norm[0][None, :]

    tp = bp * lp
    mk, mv, mkb, mvb = _memkv(mem_prompt.reshape(bp * MEM_TOKENS, d), g_mem, wkv, TM)
    x1_p, ca_p, dc_p, s_p = _front_prompt(x_prompt, g_mix, w1, w2, wab, mkb.reshape(bp, MEM_TOKENS, XA_W),
                                          mvb.reshape(bp, MEM_TOKENS, XA_W), caw, dcw, alog_row, dtb_row, dnorm,
                                          wb, wo, NB_P)

    ts = bs * ls
    xs2 = x_sample.reshape(ts, d)
    p1_s, p2_s, pab_s, xq_s = _proj(xs2, g_mix, w1, w2, wab, CHUNK)
    ea = jnp.pad(state_conv_a[0], ((0, 0), (0, ls - (CONV_A_K - 1)), (0, 0))).reshape(ts, A_W)
    eq = jnp.pad(state_dn_conv[0], ((0, 0), (0, ls - (DN_CONV_K - 1)), (0, 0))).reshape(ts, DN_CONV_CH)
    yad_s, u_s, s_s = _branch_sample(p1_s, pab_s, ea, eq, state_dn[0], caw, dcw, alog_row, dtb_row, dnorm)
    ym_s = _attn_sample(xq_s.reshape(XA_HEADS, bs, ls, XA_DH),
                        cache_mem_k.reshape(bs, MEM_TOKENS * XA_HEADS, XA_DH),
                        cache_mem_v.reshape(bs, MEM_TOKENS * XA_HEADS, XA_DH), NB_ATTN_S)
    ca_s = u_s.reshape(bs, ls, A_W)[:, ls - (CONV_A_K - 1):]
    dc_s = p1_s[:, OFF_QKV:OFF_Z].reshape(bs, ls, DN_CONV_CH)[:, ls - (DN_CONV_K - 1):]

    y_p = _ffn(x1_p.reshape(tp, d), g_ffn, wu, wd, g_fin, TM_FF).reshape(bp, lp, d)
    y_s = _tail_sample(yad_s, ym_s.reshape(XA_HEADS, ts, XA_DH), p2_s, xs2, wb, wo, g_ffn, wu, wd, g_fin)
    y_s = y_s.reshape(bs, ls, d)

    return (y_p, y_s, ca_p[None], dc_p[None], s_p[None],
            mk.reshape(1, bp, MEM_TOKENS, XA_HEADS, XA_DH), mv.reshape(1, bp, MEM_TOKENS, XA_HEADS, XA_DH),
            ca_s[None], dc_s[None], s_s[None])
```

```python
import functools

import jax
import jax.numpy as jnp
import numpy as np
from jax import lax
from jax.experimental import pallas as pl
from jax.experimental.pallas import tpu as pltpu

F32 = jnp.float32
BF16 = jnp.bfloat16

D_MODEL = 1024
A_W = 512
CONV_A_K = 3
DN_HEADS = 4
DN_DK = 128
DN_DV = 128
DN_QK = DN_HEADS * DN_DK
DN_V = DN_HEADS * DN_DV
DN_CONV_CH = 2 * DN_QK + DN_V
DN_CONV_K = 4
MEM_TOKENS = 256
XA_HEADS = 4
XA_DH = 128
XA_W = XA_HEADS * XA_DH
D_FF = 2816
EPS = 1e-6

LANE = 128
CHUNK = 128
TAIL = 8
NB_P = 4
TM = 512
GATE_BLK = 256
PROJ_BLK = 512
ATTN_RESERVE = 4

W1 = 3 * A_W + DN_CONV_CH + DN_V
OFF_QKV = 3 * A_W
OFF_Z = OFF_QKV + DN_CONV_CH
W2 = XA_W + 3 * D_MODEL
OFF_G = XA_W
N_AB = 2 * DN_HEADS
P_Z = 3 * A_W
P_XQ = P_Z + DN_V
P_W = P_XQ + XA_W

VMEM_LIMIT = 60 * 1024 * 1024


def _cparams(sem):
    return pltpu.CompilerParams(dimension_semantics=sem, vmem_limit_bytes=VMEM_LIMIT)


def _resident(shape):
    return pl.BlockSpec(shape, lambda *_: (0,) * len(shape), pipeline_mode=pl.Buffered(1))


def _mm(a, b):
    return jnp.dot(a.astype(BF16), b.astype(BF16), preferred_element_type=F32)


def _mm_nt(a, b):
    return lax.dot_general(a.astype(BF16), b.astype(BF16), (((1,), (1,)), ((), ())),
                           preferred_element_type=F32)


def _mm_tn(a, b):
    return lax.dot_general(a.astype(BF16), b.astype(BF16), (((0,), (0,)), ((), ())),
                           preferred_element_type=F32)


def _rms(x, g):
    return x * lax.rsqrt(jnp.mean(x * x, axis=-1, keepdims=True) + EPS) * g


def _silu(x):
    return x * jax.nn.sigmoid(x)


def _softplus(x):
    return jnp.maximum(x, 0.0) + jnp.log1p(jnp.exp(-jnp.abs(x)))


def _memkv_kernel(x_ref, g_ref, w_ref, k_ref, v_ref, kb_ref, vb_ref):
    xn = _rms(x_ref[...], g_ref[...]).astype(BF16)
    kv = jnp.dot(xn, w_ref[...], preferred_element_type=F32)
    k = kv[:, :XA_W]
    v = kv[:, XA_W:]
    tm = k.shape[0]
    for h in range(XA_HEADS):
        rows = pl.ds(h, tm, stride=XA_HEADS)
        k_ref[rows, :] = k[:, h * LANE:(h + 1) * LANE]
        v_ref[rows, :] = v[:, h * LANE:(h + 1) * LANE]
    kb_ref[...] = k.astype(BF16)
    vb_ref[...] = v.astype(BF16)


def _memkv(mem2d, gain, w, tm):
    t, d = mem2d.shape
    blk = pl.BlockSpec((tm, XA_W), lambda i: (i, 0))
    blk_rows = pl.BlockSpec((tm * XA_HEADS, XA_DH), lambda i: (i, 0))
    return pl.pallas_call(
        _memkv_kernel,
        grid=(t // tm,),
        in_specs=[pl.BlockSpec((tm, d), lambda i: (i, 0)),
                  pl.BlockSpec((1, d), lambda i: (0, 0)),
                  pl.BlockSpec((d, 2 * XA_W), lambda i: (0, 0))],
        out_specs=[blk_rows, blk_rows, blk, blk],
        out_shape=[jax.ShapeDtypeStruct((t * XA_HEADS, XA_DH), F32),
                   jax.ShapeDtypeStruct((t * XA_HEADS, XA_DH), F32),
                   jax.ShapeDtypeStruct((t, XA_W), BF16), jax.ShapeDtypeStruct((t, XA_W), BF16)],
        compiler_params=_cparams(("arbitrary",)),
        name="memkv",
    )(mem2d, gain, w)


def _log2(n):
    return n.bit_length() - 1


def _dn_masks(seg):
    r = lax.broadcasted_iota(jnp.int32, (CHUNK, CHUNK), 0)
    c = lax.broadcasted_iota(jnp.int32, (CHUNK, CHUNK), 1)
    ls = _log2(seg)
    same = (r >> ls) == (c >> ls)
    base = min(8, seg)
    lb = _log2(base)
    m = {
        "causal": (same & (r >= c)).astype(F32),
        "strict": (same & (r > c)).astype(F32),
        "eye": (r == c).astype(F32),
        "neg_diag": -((r >> lb) == (c >> lb)).astype(F32),
        "off": {},
        "base": base,
    }
    s = base
    while s < seg:
        l1, l2 = _log2(s), _log2(2 * s)
        m["off"][s] = (((r >> l2) == (c >> l2)) & ((r >> l1) != (c >> l1))).astype(F32)
        s *= 2
    return m


def _each(f, *lists):
    return [f(*args) for args in zip(*lists)]


def _low_rows(x, s):
    return jnp.concatenate([x[i + s:i + 2 * s] for i in range(0, x.shape[0], 2 * s)], axis=0)


def _merge_low(x, low, s):
    parts = []
    for j, i in enumerate(range(0, x.shape[0], 2 * s)):
        parts += [x[i:i + s], low[j * s:(j + 1) * s]]
    return jnp.concatenate(parts, axis=0)


def _spread_low(low, s):
    return _merge_low(jnp.zeros((2 * low.shape[0], low.shape[1]), low.dtype), low, s)


def _tri_inv(a_list, m, seg, tick):
    add = lambda x, y: x + y
    b = _each(lambda a: a * m["neg_diag"], a_list)
    p = _each(lambda x: m["eye"] + x, b)
    b2 = _each(_mm, b, b)
    tick()
    p = _each(add, p, _each(_mm, p, b2))
    tick()
    if m["base"] == 8:
        b4 = _each(_mm, b2, b2)
        tick()
        p = _each(add, p, _each(_mm, p, b4))
        tick()
    s = m["base"]
    while s < seg:
        low = lambda t, s=s: _low_rows(t, s)
        x = _each(_mm, _each(lambda a, s=s: low(a) * low(m["off"][s]), a_list), p)
        tick()
        r = _each(_mm, _each(low, p), _each(lambda y, s=s: _spread_low(y, s), x))
        p = _each(lambda t, y, s=s: _merge_low(t, low(t) - y, s), p, r)
        tick()
        s *= 2
    return p


def _seg_scan(x, seg, reverse):
    n = x.shape[1]
    pos = lax.broadcasted_iota(jnp.int32, x.shape, 1) & (seg - 1)
    s = 1
    while s < seg:
        shifted = pltpu.roll(x, n - s if reverse else s, 1)
        x = x + jnp.where(pos < seg - s if reverse else pos >= s, shifted, 0.0)
        s *= 2
    return x


def _dn_gates(ab, alog8, dtb8, seg):
    abt = ab.T[0:TAIL]
    g = -jnp.exp(alog8) * _softplus(abt + dtb8)
    d = _seg_scan(g, seg, False)
    dl = d + _seg_scan(g, seg, True) - g
    beta = jax.nn.sigmoid(abt)
    pad = jnp.zeros((CHUNK - 3 * TAIL, CHUNK), F32)
    return d, dl, jnp.concatenate([d, dl, beta, pad], axis=0).T


def _l2n(x, scale=1.0):
    return x * (lax.rsqrt(jnp.sum(x * x, axis=-1, keepdims=True) + EPS) * scale)


def _dn_intra(q, k, v, d_col, d_row, beta_col, m, seg, tick=lambda: None):
    q = _each(lambda x: _l2n(x, DN_DK ** -0.5), q)
    k = _each(_l2n, k)
    gamma = _each(lambda dc, dr: jnp.exp((dc - dr) * m["causal"]) * m["causal"], d_col, d_row)
    kk = _each(_mm_nt, k, k)
    a = _each(lambda bc, x, g: (bc * x) * g * m["strict"], beta_col, kk, gamma)
    t = _tri_inv(a, m, seg, tick)
    rhs = _each(lambda vv, kx, bc, dc: jnp.concatenate([vv * bc, kx * (bc * jnp.exp(dc))], axis=1),
                v, k, beta_col, d_col)
    sol = _each(_mm, t, rhs)
    u = [x[:, :DN_DV] for x in sol]
    w = [x[:, DN_DV:] for x in sol]
    qk = _each(lambda x, g: x * g, _each(_mm_nt, q, k), gamma)
    return q, k, u, w, qk


def _dn_out(o, z, dnorm):
    return _rms(o, dnorm) * _silu(z)


def _head_lists(qkv, z, d, cols):
    out = [[] for _ in range(8)]
    for h in range(DN_HEADS):
        beta_lane = 2 * TAIL + DN_HEADS + h
        vals = (qkv[:, h * LANE:(h + 1) * LANE],
                qkv[:, DN_QK + h * LANE:DN_QK + (h + 1) * LANE],
                qkv[:, 2 * DN_QK + h * LANE:2 * DN_QK + (h + 1) * LANE],
                None if z is None else z[:, h * LANE:(h + 1) * LANE],
                cols[:, h:h + 1], d[h:h + 1, :], cols[:, TAIL + h:TAIL + h + 1],
                cols[:, beta_lane:beta_lane + 1])
        for lst, val in zip(out, vals):
            lst.append(val)
    return out


def _causal_conv(x, tail_ref, b, wts, width):
    c, w = x.shape
    tiles = jnp.concatenate([tail_ref[b][None], x.reshape(c // TAIL, TAIL, w)], axis=0)
    sub = lax.broadcasted_iota(jnp.int32, (1, TAIL, 1), 1)
    acc = None
    for i in range(width):
        s = width - 1 - i
        if s == 0:
            y = tiles[1:]
        else:
            r = pltpu.roll(tiles, s, 1)
            y = jnp.where(sub >= s, r[1:], r[:-1])
        term = wts[i:i + 1][None] * y
        acc = term if acc is None else acc + term
    tail_ref[b] = tiles[c // TAIL]
    return acc.reshape(c, w)


def _front_prompt_kernel(x_ref, gmix_ref, w1_ref, w2_ref, wab_ref, mk_ref, mv_ref, caw_ref, dcw_ref,
                         alog_ref, dtb_ref, dnorm_ref, wb_ref, wo_ref,
                         x1_ref, ca_ref, dc_ref, s_ref, utail, qtail, ybuf, pbuf, gbuf, *, nb):
    c = CHUNK
    rows = nb * c
    t_idx = pl.program_id(1)

    @pl.when(t_idx == 0)
    def _():
        utail[...] = jnp.zeros(utail.shape, F32)
        qtail[...] = jnp.zeros(qtail.shape, F32)
        s_ref[...] = jnp.zeros(s_ref.shape, F32)

    x = x_ref[...].reshape(rows, D_MODEL)
    xn = _rms(x, gmix_ref[...]).astype(BF16)
    proj = lambda w_ref, lo, hi: jnp.dot(xn, w_ref[:, lo:hi], preferred_element_type=F32)

    queue = []

    def enqueue(dst, w_ref, src, dst_lo, width, act=None):
        def run():
            r = proj(w_ref, src, src + width)
            dst[:, dst_lo:dst_lo + width] = r if act is None else act(r)
        queue.append(run)

    for lo in range(0, 3 * A_W, PROJ_BLK):
        enqueue(pbuf, w1_ref, lo, lo, PROJ_BLK)
    enqueue(pbuf, w1_ref, OFF_Z, P_Z, DN_V)
    enqueue(pbuf, w2_ref, 0, P_XQ, XA_W)
    for lo in range(0, 3 * D_MODEL, GATE_BLK):
        enqueue(gbuf, w2_ref, OFF_G + lo, lo, GATE_BLK, jax.nn.sigmoid)
    queue.reverse()

    def tick(keep=ATTN_RESERVE):
        if len(queue) > keep:
            queue.pop()()

    m = _dn_masks(c)
    caw = caw_ref[...]
    dcw = dcw_ref[...]
    dnorm = dnorm_ref[...]
    pq_all = proj(w1_ref, OFF_QKV, OFF_Z)
    pab_all = jnp.dot(xn, wab_ref[...], preferred_element_type=F32)

    qkv_l, gates_l = [], []
    for b in range(nb):
        rb = slice(b * c, (b + 1) * c)
        tick()
        qkv_in = pq_all[rb]
        dc_ref[b] = qkv_in[c - (DN_CONV_K - 1):]
        qkv_l.append(_silu(_causal_conv(qkv_in, qtail, b, dcw, DN_CONV_K)))
        gates_l.append(_dn_gates(pab_all[rb], alog_ref[...], dtb_ref[...], c))

    for b in range(nb):
        rb = slice(b * c, (b + 1) * c)
        tick()
        u_in = pbuf[rb, A_W:2 * A_W] * pbuf[rb, 2 * A_W:3 * A_W]
        ca_ref[b] = u_in[c - (CONV_A_K - 1):]
        ybuf[rb, 0:A_W] = (pbuf[rb, 0:A_W] * _causal_conv(u_in, utail, b, caw, CONV_A_K)).astype(BF16)

    lists = [[] for _ in range(8)]
    for b in range(nb):
        d, _, cols = gates_l[b]
        for lst, val in zip(lists, _head_lists(qkv_l[b], None, d, cols)):
            lst.extend(val)
    idx = [(b, h) for b in range(nb) for h in range(DN_HEADS)]
    q, k, v, _, d_col, d_row, dl_col, beta_col = lists
    q, k, u, w, qk = _dn_intra(q, k, v, d_col, d_row, beta_col, m, c, tick)
    s_old = [s_ref[b, h] for b, h in idx]
    vn = _each(lambda ux, wx, s: ux - _mm(wx, s), u, w, s_old)
    tick()
    o = _each(lambda qx, dc, y, s, vx: _mm(jnp.concatenate([qx * jnp.exp(dc), y], axis=1),
                                           jnp.concatenate([s, vx], axis=0)), q, d_col, qk, s_old, vn)
    tick()
    s_new = _each(lambda s, dlc, kx, dc, vx: s * jnp.exp(dlc[0:1, :]) + _mm_tn(kx * jnp.exp(dlc - dc), vx),
                  s_old, dl_col, k, d_col, vn)
    for (b, h), sx, ox in zip(idx, s_new, o):
        s_ref[b, h] = sx
        zx = pbuf[b * c:(b + 1) * c, P_Z + h * LANE:P_Z + (h + 1) * LANE]
        ybuf[b * c:(b + 1) * c, A_W + h * LANE:A_W + (h + 1) * LANE] = _dn_out(ox, zx, dnorm).astype(BF16)

    for b in range(nb):
        tick(0)
        heads = range(XA_HEADS)
        sc = [_mm_nt(pbuf[b * c:(b + 1) * c, P_XQ + h * LANE:P_XQ + (h + 1) * LANE],
                     mk_ref[b, :, h * LANE:(h + 1) * LANE]) * (XA_DH ** -0.5) for h in heads]
        e = _each(lambda x_: jnp.exp(x_ - jnp.max(x_, axis=-1, keepdims=True)), sc)
        inv = _each(lambda x_: 1.0 / jnp.sum(x_, axis=-1, keepdims=True), e)
        for h, ex, ix in zip(heads, e, inv):
            ybuf[b * c:(b + 1) * c, A_W + DN_V + h * LANE:A_W + DN_V + (h + 1) * LANE] = (
                _mm(ex, mv_ref[b, :, h * LANE:(h + 1) * LANE]) * ix).astype(BF16)
    while queue:
        tick(0)

    merged = None
    for j, (lo, hi) in enumerate(((0, A_W), (A_W, A_W + DN_V), (A_W + DN_V, A_W + DN_V + XA_W))):
        term = gbuf[:, j * D_MODEL:(j + 1) * D_MODEL] * jnp.dot(ybuf[:, lo:hi], wb_ref[lo:hi, :],
                                                                preferred_element_type=F32)
        merged = term if merged is None else merged + term
    x1 = x + jnp.dot(merged.astype(BF16), wo_ref[...], preferred_element_type=F32)
    x1_ref[...] = x1.reshape(nb, c, D_MODEL)


def _front_prompt(x, gmix, w1, w2, wab, mkb, mvb, caw, dcw, alog_row, dtb_row, dnorm, wb, wo, nb):
    bsz, length, d = x.shape
    c = CHUNK
    return pl.pallas_call(
        functools.partial(_front_prompt_kernel, nb=nb),
        grid=(bsz // nb, length // c),
        in_specs=[pl.BlockSpec((nb, c, d), lambda g, t: (g, t, 0)),
                  _resident((1, d)), _resident((d, W1)), _resident(w2.shape), _resident(wab.shape),
                  pl.BlockSpec((nb, MEM_TOKENS, XA_W), lambda g, t: (g, 0, 0), pipeline_mode=pl.Buffered(1)),
                  pl.BlockSpec((nb, MEM_TOKENS, XA_W), lambda g, t: (g, 0, 0), pipeline_mode=pl.Buffered(1)),
                  _resident((CONV_A_K, A_W)), _resident((DN_CONV_K, DN_CONV_CH)),
                  _resident((TAIL, LANE)), _resident((TAIL, LANE)), _resident((1, LANE)),
                  _resident(wb.shape), _resident(wo.shape)],
        out_specs=[pl.BlockSpec((nb, c, d), lambda g, t: (g, t, 0)),
                   pl.BlockSpec((nb, CONV_A_K - 1, A_W), lambda g, t: (g, 0, 0)),
                   pl.BlockSpec((nb, DN_CONV_K - 1, DN_CONV_CH), lambda g, t: (g, 0, 0)),
                   pl.BlockSpec((nb, DN_HEADS, DN_DK, DN_DV), lambda g, t: (g, 0, 0, 0))],
        out_shape=[jax.ShapeDtypeStruct((bsz, length, d), F32),
                   jax.ShapeDtypeStruct((bsz, CONV_A_K - 1, A_W), F32),
                   jax.ShapeDtypeStruct((bsz, DN_CONV_K - 1, DN_CONV_CH), F32),
                   jax.ShapeDtypeStruct((bsz, DN_HEADS, DN_DK, DN_DV), F32)],
        scratch_shapes=[pltpu.VMEM((nb, TAIL, A_W), F32), pltpu.VMEM((nb, TAIL, DN_CONV_CH), F32),
                        pltpu.VMEM((nb * c, A_W + DN_V + XA_W), BF16), pltpu.VMEM((nb * c, P_W), F32),
                        pltpu.VMEM((nb * c, 3 * d), F32)],
        compiler_params=_cparams(("arbitrary", "arbitrary")),
        name="front_prompt",
    )(x, gmix, w1, w2, wab, mkb, mvb, caw, dcw, alog_row, dtb_row, dnorm, wb, wo)


def _proj_kernel(x_ref, g_ref, w1_ref, w2_ref, wab_ref, p1_ref, p2_ref, pab_ref, xq_ref):
    xn = _rms(x_ref[...], g_ref[...]).astype(BF16)
    p1_ref[...] = jnp.dot(xn, w1_ref[...], preferred_element_type=F32)
    p2 = jnp.dot(xn, w2_ref[...], preferred_element_type=F32)
    p2_ref[...] = p2
    pab_ref[...] = jnp.dot(xn, wab_ref[...], preferred_element_type=F32)
    for h in range(XA_HEADS):
        xq_ref[h] = p2[:, h * LANE:(h + 1) * LANE]


def _proj(x2d, gain, w1, w2, wab, tm):
    t, d = x2d.shape
    row = lambda n: pl.BlockSpec((tm, n), lambda i: (i, 0))
    return pl.pallas_call(
        _proj_kernel,
        grid=(t // tm,),
        in_specs=[row(d), _resident((1, d)), _resident((d, W1)), _resident(w2.shape), _resident(wab.shape)],
        out_specs=[row(W1), row(W2), row(LANE), pl.BlockSpec((XA_HEADS, tm, LANE), lambda i: (0, i, 0))],
        out_shape=[jax.ShapeDtypeStruct((t, W1), F32), jax.ShapeDtypeStruct((t, W2), F32),
                   jax.ShapeDtypeStruct((t, LANE), F32), jax.ShapeDtypeStruct((XA_HEADS, t, LANE), F32)],
        compiler_params=_cparams(("arbitrary",)),
        name="proj",
    )(x2d, gain, w1, w2, wab)


SEQ_S = 4
NB_S = CHUNK // SEQ_S
NB_ATTN_S = 16


def _seg_conv(x, e, wts, width):
    rows, w = x.shape
    xt = x.reshape(rows // TAIL, TAIL, w)
    et = e.reshape(rows // TAIL, TAIL, w)
    tmod = lax.broadcasted_iota(jnp.int32, (1, TAIL, 1), 1) & (SEQ_S - 1)
    acc = None
    for i in range(width):
        s = width - 1 - i
        term = xt if s == 0 else jnp.where(tmod >= s, pltpu.roll(xt, s, 1), 0.0)
        if i < width - 1:
            hist = et if i == 0 else pltpu.roll(et, TAIL - i, 1)
            term = term + jnp.where(tmod < SEQ_S - i, hist, 0.0)
        term = wts[i:i + 1][None] * term
        acc = term if acc is None else acc + term
    return acc.reshape(rows, w)


def _branch_sample_kernel(pa_ref, pq_ref, pz_ref, pab_ref, ea_ref, eq_ref, s0_ref, caw_ref, dcw_ref,
                          alog_ref, dtb_ref, dnorm_ref, mexp2_ref, mexpt_ref, yad_ref, u_ref, s_ref):
    c = CHUNK
    m = _dn_masks(SEQ_S)

    pa = pa_ref[...]
    u_in = pa[:, A_W:2 * A_W] * pa[:, 2 * A_W:3 * A_W]
    u_ref[...] = u_in
    conv = _seg_conv(u_in, ea_ref[...], caw_ref[...], CONV_A_K)
    yad_ref[:, 0:A_W] = (pa[:, 0:A_W] * conv).astype(BF16)

    qkv = _silu(_seg_conv(pq_ref[...], eq_ref[...], dcw_ref[...], DN_CONV_K))
    d, dl, cols = _dn_gates(pab_ref[...], alog_ref[...], dtb_ref[...], SEQ_S)
    dec_t = jnp.exp(dl)
    dnorm = dnorm_ref[...]

    wide = NB_S * DN_DK
    mexp2 = mexp2_ref[...]
    mexp_t = mexpt_ref[...]

    q, k, v, z, d_col, d_row, dl_col, beta_col = _head_lists(qkv, pz_ref[...], d, cols)
    q, k, u, w, qk = _dn_intra(q, k, v, d_col, d_row, beta_col, m, SEQ_S)
    heads = list(range(DN_HEADS))
    s_old = [s0_ref[:, h].reshape(wide, DN_DV) for h in heads]
    x_exp = _each(lambda wx, qx, dc: jnp.concatenate(
        [jnp.concatenate([wx, qx * jnp.exp(dc)], axis=0).astype(BF16)] * NB_S, axis=1) * mexp2, w, q, d_col)
    ws = _each(_mm, x_exp, s_old)
    vn = _each(lambda ux, x: ux - x[:c], u, ws)
    o = _each(lambda x, y, vx: x[c:] + _mm(y, vx), ws, qk, vn)
    k_exp = _each(lambda kx, dlc, dc: jnp.concatenate(
        [(kx * jnp.exp(dlc - dc)).T.astype(BF16)] * NB_S, axis=0) * mexp_t, k, dl_col, d_col)
    dec = [jnp.concatenate([jnp.broadcast_to(dec_t[h:h + 1, SEQ_S * b:SEQ_S * b + 1], (DN_DK, DN_DV))
                            for b in range(NB_S)], axis=0) for h in heads]
    s_new = _each(lambda s, dx, kx, vx: s * dx + _mm(kx, vx), s_old, dec, k_exp, vn)
    for h, sx, ox, zx in zip(heads, s_new, o, z):
        s_ref[:, h] = sx.reshape(NB_S, DN_DK, DN_DV)
        yad_ref[:, A_W + h * LANE:A_W + (h + 1) * LANE] = _dn_out(ox, zx, dnorm).astype(BF16)


def _branch_sample(p1, pab, ea, eq, state, caw, dcw, alog_row, dtb_row, dnorm):
    t = p1.shape[0]
    c = CHUNK
    full = lambda shape: pl.BlockSpec(shape, lambda i: (0,) * len(shape))
    wide = NB_S * DN_DK
    owner = np.arange(wide) // DN_DK
    seq = (np.arange(2 * c) % c) // SEQ_S
    mexp2 = jnp.asarray(seq[:, None] == owner[None, :], BF16)
    mexp_t = jnp.asarray(owner[:, None] == seq[None, :c], BF16)
    return pl.pallas_call(
        _branch_sample_kernel,
        grid=(t // c,),
        in_specs=[pl.BlockSpec((c, 3 * A_W), lambda i: (i, 0)),
                  pl.BlockSpec((c, DN_CONV_CH), lambda i: (i, OFF_QKV // DN_CONV_CH)),
                  pl.BlockSpec((c, DN_V), lambda i: (i, OFF_Z // DN_V)),
                  pl.BlockSpec((c, LANE), lambda i: (i, 0)),
                  pl.BlockSpec((c, A_W), lambda i: (i, 0)),
                  pl.BlockSpec((c, DN_CONV_CH), lambda i: (i, 0)),
                  pl.BlockSpec((NB_S, DN_HEADS, DN_DK, DN_DV), lambda i: (i, 0, 0, 0)),
                  full((CONV_A_K, A_W)), full((DN_CONV_K, DN_CONV_CH)),
                  full((TAIL, LANE)), full((TAIL, LANE)), full((1, LANE)),
                  _resident(mexp2.shape), _resident(mexp_t.shape)],
        out_specs=[pl.BlockSpec((c, A_W + DN_V), lambda i: (i, 0)),
                   pl.BlockSpec((c, A_W), lambda i: (i, 0)),
                   pl.BlockSpec((NB_S, DN_HEADS, DN_DK, DN_DV), lambda i: (i, 0, 0, 0))],
        out_shape=[jax.ShapeDtypeStruct((t, A_W + DN_V), BF16),
                   jax.ShapeDtypeStruct((t, A_W), F32),
                   jax.ShapeDtypeStruct(state.shape, F32)],
        compiler_params=_cparams(("arbitrary",)),
        name="branch_sample",
    )(p1, p1, p1, pab, ea, eq, state, caw, dcw, alog_row, dtb_row, dnorm, mexp2, mexp_t)


def _attn_sample_kernel(q_ref, k_ref, v_ref, o_ref):
    for h in range(XA_HEADS):
        rows = pl.ds(h, MEM_TOKENS, stride=XA_HEADS)
        s = jnp.einsum("bqd,bkd->bqk", q_ref[h].astype(BF16), k_ref[:, rows, :].astype(BF16),
                       preferred_element_type=F32) * (XA_DH ** -0.5)
        e = jnp.exp(s - jnp.max(s, axis=-1, keepdims=True))
        p = e / jnp.sum(e, axis=-1, keepdims=True)
        o_ref[h] = jnp.einsum("bqk,bkd->bqd", p.astype(BF16), v_ref[:, rows, :].astype(BF16),
                              preferred_element_type=F32)


def _attn_sample(q4, ck, cv, nb):
    _, bsz, length, _ = q4.shape
    head_major = pl.BlockSpec((XA_HEADS, nb, length, XA_DH), lambda i: (0, i, 0, 0))
    return pl.pallas_call(
        _attn_sample_kernel,
        grid=(bsz // nb,),
        in_specs=[head_major,
                  pl.BlockSpec((nb, MEM_TOKENS * XA_HEADS, XA_DH), lambda i: (i, 0, 0)),
                  pl.BlockSpec((nb, MEM_TOKENS * XA_HEADS, XA_DH), lambda i: (i, 0, 0))],
        out_specs=head_major,
        out_shape=jax.ShapeDtypeStruct(q4.shape, F32),
        compiler_params=_cparams(("arbitrary",)),
        name="attn_sample",
    )(q4, ck, cv)


def _merge_kernel(yad_ref, ym_ref, p2_ref, x_ref, wb_ref, wo_ref, o_ref):
    yad = yad_ref[...]
    ym = jnp.concatenate([ym_ref[h] for h in range(XA_HEADS)], axis=1).astype(BF16)
    gate = lambda j: jax.nn.sigmoid(p2_ref[:, OFF_G + j * D_MODEL:OFF_G + (j + 1) * D_MODEL])
    merged = (gate(0) * jnp.dot(yad[:, :A_W], wb_ref[0:A_W, :], preferred_element_type=F32)
              + gate(1) * jnp.dot(yad[:, A_W:], wb_ref[A_W:A_W + DN_V, :], preferred_element_type=F32)
              + gate(2) * jnp.dot(ym, wb_ref[A_W + DN_V:, :], preferred_element_type=F32))
    o_ref[...] = x_ref[...] + jnp.dot(merged.astype(BF16), wo_ref[...], preferred_element_type=F32)


MXU_K = 256
FF_EDGES = (0, 6 * MXU_K, D_FF)
FF_SPLIT = len(FF_EDGES) - 1
FF_SUB = 256
TM_FF = 4 * FF_SUB


def _ffn_kernel(x_ref, gf_ref, wu_ref, wd_ref, gl_ref, o_ref):
    n_sub = x_ref.shape[0] // FF_SUB

    def block(xn, acc, j):
        lo, hi = FF_EDGES[j], FF_EDGES[j + 1]
        gate = jnp.dot(xn, wu_ref[:, lo:hi], preferred_element_type=F32)
        up = jnp.dot(xn, wu_ref[:, D_FF + lo:D_FF + hi], preferred_element_type=F32)
        hid = (_silu(gate) * up).astype(BF16)
        return acc + jnp.dot(hid, wd_ref[lo:hi, :], preferred_element_type=F32)

    state = [None] * n_sub
    for i in range(n_sub + 1):
        if i < n_sub:
            x = x_ref[i * FF_SUB:(i + 1) * FF_SUB, :]
            xn = _rms(x, gf_ref[...]).astype(BF16)
            state[i] = (xn, block(xn, x, 0))
        if i > 0:
            xn, acc = state[i - 1]
            for j in range(1, FF_SPLIT):
                acc = block(xn, acc, j)
            o_ref[(i - 1) * FF_SUB:i * FF_SUB, :] = _rms(acc, gl_ref[...])


STAGE_U = 128
STAGE_D = D_FF // 4


def _ffn_prompt_kernel(x_ref, gf_ref, wu_hbm, wd_hbm, gl_ref, o_ref, wub_hbm, wdb_hbm,
                       wu_s, wd_s, stg_u, stg_d, sem):
    first = pl.program_id(0) == 0
    writebacks = ((wu_s, wub_hbm, 4), (wd_s, wdb_hbm, 5))

    @pl.when(first)
    def _stage_weights():
        jobs = ([(wu_hbm, wu_s, stg_u, 0, r * STAGE_U, STAGE_U) for r in range(wu_s.shape[0] // STAGE_U)]
                + [(wd_hbm, wd_s, stg_d, 2, r * STAGE_D, STAGE_D) for r in range(wd_s.shape[0] // STAGE_D)])

        def copy(n):
            src, _, stg, s0, lo, sz = jobs[n]
            return pltpu.make_async_copy(src.at[pl.ds(lo, sz), :], stg.at[n % 2], sem.at[s0 + n % 2])

        for n in range(min(2, len(jobs))):
            copy(n).start()
        for n, (_, dst, stg, _, lo, sz) in enumerate(jobs):
            copy(n).wait()
            dst[lo:lo + sz, :] = stg[n % 2].astype(BF16)
            if n + 2 < len(jobs):
                copy(n + 2).start()
        for src, dst, s in writebacks:
            pltpu.make_async_copy(src, dst, sem.at[s]).start()

    _ffn_kernel(x_ref, gf_ref, wu_s, wd_s, gl_ref, o_ref)

    @pl.when(first)
    def _finish_writeback():
        for src, dst, s in writebacks:
            pltpu.make_async_copy(src, dst, sem.at[s]).wait()


def _ffn(x2d, gf, wu32, wd32, gl, tm):
    t, d = x2d.shape
    assert wu32.shape[0] % (2 * STAGE_U) == 0 and wd32.shape[0] % (2 * STAGE_D) == 0
    anywhere = pl.BlockSpec(memory_space=pl.ANY)
    return pl.pallas_call(
        _ffn_prompt_kernel,
        grid=(t // tm,),
        in_specs=[pl.BlockSpec((tm, d), lambda i: (i, 0)), _resident((1, d)), anywhere, anywhere,
                  _resident((1, d))],
        out_specs=[pl.BlockSpec((tm, d), lambda i: (i, 0)), anywhere, anywhere],
        out_shape=[jax.ShapeDtypeStruct((t, d), F32), jax.ShapeDtypeStruct(wu32.shape, BF16),
                   jax.ShapeDtypeStruct(wd32.shape, BF16)],
        scratch_shapes=[pltpu.VMEM(wu32.shape, BF16), pltpu.VMEM(wd32.shape, BF16),
                        pltpu.VMEM((2, STAGE_U, wu32.shape[1]), F32),
                        pltpu.VMEM((2, STAGE_D, wd32.shape[1]), F32),
                        pltpu.SemaphoreType.DMA((6,))],
        compiler_params=_cparams(("arbitrary",)),
        name="ffn",
    )(x2d, gf, wu32, wd32, gl)


def _tail_sample_kernel(yad_ref, ym_ref, p2_ref, x_ref, wb_ref, wo_ref, gf_ref, wu_ref, wd_ref, gl_ref,
                        o_ref, x1_buf):
    _merge_kernel(yad_ref, ym_ref, p2_ref, x_ref, wb_ref, wo_ref, x1_buf)
    _ffn_kernel(x1_buf, gf_ref, wu_ref, wd_ref, gl_ref, o_ref)


def _tail_sample(yad, ym, p2, x2d, wb, wo, gf, wu, wd, gl):
    t, d = x2d.shape
    args = (yad, ym, p2, x2d, wb, wo, gf, wu, wd, gl)
    return pl.pallas_call(
        _tail_sample_kernel,
        grid=(1,),
        in_specs=[_resident(a.shape) for a in args],
        out_specs=pl.BlockSpec((t, d), lambda i: (0, 0)),
        out_shape=jax.ShapeDtypeStruct((t, d), F32),
        scratch_shapes=[pltpu.VMEM((t, d), F32)],
        compiler_params=_cparams(("arbitrary",)),
        name="tail_sample",
    )(*args)


def _head_rows(v):
    col = jnp.zeros((TAIL, 1), F32).at[:v.shape[0], 0].set(v.astype(F32))
    return jnp.broadcast_to(col, (TAIL, LANE))


def kernel(x_prompt, x_sample, mem_prompt, state_conv_a, state_dn_conv, state_dn, cache_mem_k, cache_mem_v,
           norm_mix, w_in, conv_a_w, dn_conv_w, dn_a_log, dn_dt_bias, dn_norm, norm_mem, w_mem_kv, w_branch,
           w_o, norm_ffn, w_ffn_up, w_ffn_down, norm_final):
    bp, lp, d = x_prompt.shape
    bs, ls, _ = x_sample.shape
    assert norm_mix.shape[0] == 1 and ls == SEQ_S and lp % CHUNK == 0 and (bs * ls) % CHUNK == 0
    assert w_in.shape[2] == W1 + N_AB + W2

    w = w_in[0]
    w1 = w.astype(BF16)
    w2 = w1[:, W1 + N_AB:]
    wab = jnp.pad(w1[:, W1:W1 + N_AB], ((0, 0), (0, LANE - N_AB)))
    wb = w_branch[0].astype(BF16)
    wo = w_o[0].astype(BF16)
    wkv = w_mem_kv[0].astype(BF16)
    g_mix = norm_mix[0][None, :]
    g_ffn = norm_ffn[0][None, :]
    g_fin = norm_final[None, :]
    g_mem = norm_mem[0][None, :]
    caw = conv_a_w[0]
    dcw = dn_conv_w[0]
    alog_row = _head_rows(dn_a_log[0])
    dtb_row = _head_rows(dn_dt_bias[0])
    dnorm = dn_norm[0][None, :]

    tp = bp * lp
    mk, mv, mkb, mvb = _memkv(mem_prompt.reshape(bp * MEM_TOKENS, d), g_mem, wkv, TM)
    x1_p, ca_p, dc_p, s_p = _front_prompt(x_prompt, g_mix, w1, w2, wab, mkb.reshape(bp, MEM_TOKENS, XA_W),
                                          mvb.reshape(bp, MEM_TOKENS, XA_W), caw, dcw, alog_row, dtb_row, dnorm,
                                          wb, wo, NB_P)

    ts = bs * ls
    xs2 = x_sample.reshape(ts, d)
    p1_s, p2_s, pab_s, xq_s = _proj(xs2, g_mix, w1, w2, wab, CHUNK)
    ea = jnp.pad(state_conv_a[0], ((0, 0), (0, ls - (CONV_A_K - 1)), (0, 0))).reshape(ts, A_W)
    eq = jnp.pad(state_dn_conv[0], ((0, 0), (0, ls - (DN_CONV_K - 1)), (0, 0))).reshape(ts, DN_CONV_CH)
    yad_s, u_s, s_s = _branch_sample(p1_s, pab_s, ea, eq, state_dn[0], caw, dcw, alog_row, dtb_row, dnorm)
    ym_s = _attn_sample(xq_s.reshape(XA_HEADS, bs, ls, XA_DH),
                        cache_mem_k.reshape(bs, MEM_TOKENS * XA_HEADS, XA_DH),
                        cache_mem_v.reshape(bs, MEM_TOKENS * XA_HEADS, XA_DH), NB_ATTN_S)
    ca_s = u_s.reshape(bs, ls, A_W)[:, ls - (CONV_A_K - 1):]
    dc_s = p1_s[:, OFF_QKV:OFF_Z].reshape(bs, ls, DN_CONV_CH)[:, ls - (DN_CONV_K - 1):]

    y_p, wu, wd = _ffn(x1_p.reshape(tp, d), g_ffn, w_ffn_up[0], w_ffn_down[0], g_fin, TM_FF)
    y_p = y_p.reshape(bp, lp, d)
    y_s = _tail_sample(yad_s, ym_s.reshape(XA_HEADS, ts, XA_DH), p2_s, xs2, wb, wo, g_ffn, wu, wd, g_fin)
    y_s = y_s.reshape(bs, ls, d)

    return (y_p, y_s, ca_p[None], dc_p[None], s_p[None],
            mk.reshape(1, bp, MEM_TOKENS, XA_HEADS, XA_DH), mv.reshape(1, bp, MEM_TOKENS, XA_HEADS, XA_DH),
            ca_s[None], dc_s[None], s_s[None])
```

```python
import functools

import jax
import jax.numpy as jnp
import numpy as np
from jax import lax
from jax.experimental import pallas as pl
from jax.experimental.pallas import tpu as pltpu

F32 = jnp.float32
BF16 = jnp.bfloat16

D_MODEL = 1024
A_W = 512
CONV_A_K = 3
DN_HEADS = 4
DN_DK = 128
DN_DV = 128
DN_QK = DN_HEADS * DN_DK
DN_V = DN_HEADS * DN_DV
DN_CONV_CH = 2 * DN_QK + DN_V
DN_CONV_K = 4
MEM_TOKENS = 256
XA_HEADS = 4
XA_DH = 128
XA_W = XA_HEADS * XA_DH
D_FF = 2816
EPS = 1e-6

LANE = 128
CHUNK = 128
TAIL = 8
NB_P = 4
TM = 512
GATE_BLK = 256
PROJ_BLK = 512
ATTN_RESERVE = 4

W1 = 3 * A_W + DN_CONV_CH + DN_V
OFF_QKV = 3 * A_W
OFF_Z = OFF_QKV + DN_CONV_CH
W2 = XA_W + 3 * D_MODEL
OFF_G = XA_W
N_AB = 2 * DN_HEADS
P_Z = 3 * A_W
P_XQ = P_Z + DN_V
P_W = P_XQ + XA_W

VMEM_LIMIT = 60 * 1024 * 1024


def _cparams(sem):
    return pltpu.CompilerParams(dimension_semantics=sem, vmem_limit_bytes=VMEM_LIMIT)


def _resident(shape):
    return pl.BlockSpec(shape, lambda *_: (0,) * len(shape), pipeline_mode=pl.Buffered(1))


def _mm(a, b):
    return jnp.dot(a.astype(BF16), b.astype(BF16), preferred_element_type=F32)


def _mm_nt(a, b):
    return lax.dot_general(a.astype(BF16), b.astype(BF16), (((1,), (1,)), ((), ())),
                           preferred_element_type=F32)


def _mm_tn(a, b):
    return lax.dot_general(a.astype(BF16), b.astype(BF16), (((0,), (0,)), ((), ())),
                           preferred_element_type=F32)


def _rms(x, g):
    return x * lax.rsqrt(jnp.mean(x * x, axis=-1, keepdims=True) + EPS) * g


def _silu(x):
    return x * jax.nn.sigmoid(x)


def _softplus(x):
    return jnp.maximum(x, 0.0) + jnp.log1p(jnp.exp(-jnp.abs(x)))


def _memkv_kernel(x_ref, g_ref, w_ref, k_ref, v_ref, kb_ref, vb_ref):
    xn = _rms(x_ref[...], g_ref[...]).astype(BF16)
    kv = jnp.dot(xn, w_ref[...], preferred_element_type=F32)
    k = kv[:, :XA_W]
    v = kv[:, XA_W:]
    tm = k.shape[0]
    for h in range(XA_HEADS):
        rows = pl.ds(h, tm, stride=XA_HEADS)
        k_ref[rows, :] = k[:, h * LANE:(h + 1) * LANE]
        v_ref[rows, :] = v[:, h * LANE:(h + 1) * LANE]
    kb_ref[...] = k.astype(BF16)
    vb_ref[...] = v.astype(BF16)


def _memkv(mem2d, gain, w, tm):
    t, d = mem2d.shape
    blk = pl.BlockSpec((tm, XA_W), lambda i: (i, 0))
    blk_rows = pl.BlockSpec((tm * XA_HEADS, XA_DH), lambda i: (i, 0))
    return pl.pallas_call(
        _memkv_kernel,
        grid=(t // tm,),
        in_specs=[pl.BlockSpec((tm, d), lambda i: (i, 0)),
                  pl.BlockSpec((1, d), lambda i: (0, 0)),
                  pl.BlockSpec((d, 2 * XA_W), lambda i: (0, 0))],
        out_specs=[blk_rows, blk_rows, blk, blk],
        out_shape=[jax.ShapeDtypeStruct((t * XA_HEADS, XA_DH), F32),
                   jax.ShapeDtypeStruct((t * XA_HEADS, XA_DH), F32),
                   jax.ShapeDtypeStruct((t, XA_W), BF16), jax.ShapeDtypeStruct((t, XA_W), BF16)],
        compiler_params=_cparams(("arbitrary",)),
        name="memkv",
    )(mem2d, gain, w)


def _log2(n):
    return n.bit_length() - 1


def _dn_masks(seg):
    r = lax.broadcasted_iota(jnp.int32, (CHUNK, CHUNK), 0)
    c = lax.broadcasted_iota(jnp.int32, (CHUNK, CHUNK), 1)
    ls = _log2(seg)
    same = (r >> ls) == (c >> ls)
    base = min(8, seg)
    lb = _log2(base)
    m = {
        "causal": (same & (r >= c)).astype(F32),
        "strict": (same & (r > c)).astype(F32),
        "eye": (r == c).astype(F32),
        "neg_diag": -((r >> lb) == (c >> lb)).astype(F32),
        "off": {},
        "base": base,
    }
    s = base
    while s < seg:
        l1, l2 = _log2(s), _log2(2 * s)
        m["off"][s] = (((r >> l2) == (c >> l2)) & ((r >> l1) != (c >> l1))).astype(F32)
        s *= 2
    return m


def _each(f, *lists):
    return [f(*args) for args in zip(*lists)]


def _low_rows(x, s):
    return jnp.concatenate([x[i + s:i + 2 * s] for i in range(0, x.shape[0], 2 * s)], axis=0)


def _merge_low(x, low, s):
    parts = []
    for j, i in enumerate(range(0, x.shape[0], 2 * s)):
        parts += [x[i:i + s], low[j * s:(j + 1) * s]]
    return jnp.concatenate(parts, axis=0)


def _spread_low(low, s):
    return _merge_low(jnp.zeros((2 * low.shape[0], low.shape[1]), low.dtype), low, s)


def _tri_inv(a_list, m, seg, tick):
    add = lambda x, y: x + y
    b = _each(lambda a: a * m["neg_diag"], a_list)
    p = _each(lambda x: m["eye"] + x, b)
    b2 = _each(_mm, b, b)
    tick()
    p = _each(add, p, _each(_mm, p, b2))
    tick()
    if m["base"] == 8:
        b4 = _each(_mm, b2, b2)
        tick()
        p = _each(add, p, _each(_mm, p, b4))
        tick()
    s = m["base"]
    while s < seg:
        low = lambda t, s=s: _low_rows(t, s)
        x = _each(_mm, _each(lambda a, s=s: low(a) * low(m["off"][s]), a_list), p)
        tick()
        r = _each(_mm, _each(low, p), _each(lambda y, s=s: _spread_low(y, s), x))
        p = _each(lambda t, y, s=s: _merge_low(t, low(t) - y, s), p, r)
        tick()
        s *= 2
    return p


def _seg_scan(x, seg, reverse):
    n = x.shape[1]
    pos = lax.broadcasted_iota(jnp.int32, x.shape, 1) & (seg - 1)
    s = 1
    while s < seg:
        shifted = pltpu.roll(x, n - s if reverse else s, 1)
        x = x + jnp.where(pos < seg - s if reverse else pos >= s, shifted, 0.0)
        s *= 2
    return x


def _dn_gates(ab, alog8, dtb8, seg):
    abt = ab.T[0:TAIL]
    g = -jnp.exp(alog8) * _softplus(abt + dtb8)
    d = _seg_scan(g, seg, False)
    dl = d + _seg_scan(g, seg, True) - g
    beta = jax.nn.sigmoid(abt)
    pad = jnp.zeros((CHUNK - 3 * TAIL, CHUNK), F32)
    return d, dl, jnp.concatenate([d, dl, beta, pad], axis=0).T


def _l2n(x, scale=1.0):
    return x * (lax.rsqrt(jnp.sum(x * x, axis=-1, keepdims=True) + EPS) * scale)


def _dn_intra(q, k, v, d_col, d_row, beta_col, m, seg, tick=lambda: None):
    q = _each(lambda x: _l2n(x, DN_DK ** -0.5), q)
    k = _each(_l2n, k)
    gamma = _each(lambda dc, dr: jnp.exp((dc - dr) * m["causal"]) * m["causal"], d_col, d_row)
    kk = _each(_mm_nt, k, k)
    a = _each(lambda bc, x, g: (bc * x) * g * m["strict"], beta_col, kk, gamma)
    t = _tri_inv(a, m, seg, tick)
    rhs = _each(lambda vv, kx, bc, dc: jnp.concatenate([vv * bc, kx * (bc * jnp.exp(dc))], axis=1),
                v, k, beta_col, d_col)
    sol = _each(_mm, t, rhs)
    u = [x[:, :DN_DV] for x in sol]
    w = [x[:, DN_DV:] for x in sol]
    qk = _each(lambda x, g: x * g, _each(_mm_nt, q, k), gamma)
    return q, k, u, w, qk


def _dn_out(o, z, dnorm):
    return _rms(o, dnorm) * _silu(z)


def _head_lists(qkv, z, d, cols):
    out = [[] for _ in range(8)]
    for h in range(DN_HEADS):
        beta_lane = 2 * TAIL + DN_HEADS + h
        vals = (qkv[:, h * LANE:(h + 1) * LANE],
                qkv[:, DN_QK + h * LANE:DN_QK + (h + 1) * LANE],
                qkv[:, 2 * DN_QK + h * LANE:2 * DN_QK + (h + 1) * LANE],
                None if z is None else z[:, h * LANE:(h + 1) * LANE],
                cols[:, h:h + 1], d[h:h + 1, :], cols[:, TAIL + h:TAIL + h + 1],
                cols[:, beta_lane:beta_lane + 1])
        for lst, val in zip(out, vals):
            lst.append(val)
    return out


def _causal_conv(x, tail_ref, b, wts, width):
    c, w = x.shape
    tiles = jnp.concatenate([tail_ref[b][None], x.reshape(c // TAIL, TAIL, w)], axis=0)
    sub = lax.broadcasted_iota(jnp.int32, (1, TAIL, 1), 1)
    acc = None
    for i in range(width):
        s = width - 1 - i
        if s == 0:
            y = tiles[1:]
        else:
            r = pltpu.roll(tiles, s, 1)
            y = jnp.where(sub >= s, r[1:], r[:-1])
        term = wts[i:i + 1][None] * y
        acc = term if acc is None else acc + term
    tail_ref[b] = tiles[c // TAIL]
    return acc.reshape(c, w)


def _front_prompt_kernel(x_ref, gmix_ref, w1_ref, w2_ref, wab_ref, mk_ref, mv_ref, caw_ref, dcw_ref,
                         alog_ref, dtb_ref, dnorm_ref, wb_ref, wo_ref,
                         x1_ref, ca_ref, dc_ref, s_ref, utail, qtail, ybuf, pbuf, gbuf, *, nb):
    c = CHUNK
    rows = nb * c
    t_idx = pl.program_id(1)

    @pl.when(t_idx == 0)
    def _():
        utail[...] = jnp.zeros(utail.shape, F32)
        qtail[...] = jnp.zeros(qtail.shape, F32)
        s_ref[...] = jnp.zeros(s_ref.shape, F32)

    x = x_ref[...].reshape(rows, D_MODEL)
    xn = _rms(x, gmix_ref[...]).astype(BF16)
    proj = lambda w_ref, lo, hi: jnp.dot(xn, w_ref[:, lo:hi], preferred_element_type=F32)

    queue = []

    def enqueue(dst, w_ref, src, dst_lo, width, act=None):
        def run():
            r = proj(w_ref, src, src + width)
            dst[:, dst_lo:dst_lo + width] = r if act is None else act(r)
        queue.append(run)

    for lo in range(0, 3 * A_W, PROJ_BLK):
        enqueue(pbuf, w1_ref, lo, lo, PROJ_BLK)
    enqueue(pbuf, w1_ref, OFF_Z, P_Z, DN_V)
    enqueue(pbuf, w2_ref, 0, P_XQ, XA_W)
    for lo in range(0, 3 * D_MODEL, GATE_BLK):
        enqueue(gbuf, w2_ref, OFF_G + lo, lo, GATE_BLK, jax.nn.sigmoid)
    queue.reverse()

    def tick(keep=ATTN_RESERVE):
        if len(queue) > keep:
            queue.pop()()

    m = _dn_masks(c)
    caw = caw_ref[...]
    dcw = dcw_ref[...]
    dnorm = dnorm_ref[...]
    pq_all = proj(w1_ref, OFF_QKV, OFF_Z)
    pab_all = jnp.dot(xn, wab_ref[...], preferred_element_type=F32)

    qkv_l, gates_l = [], []
    for b in range(nb):
        rb = slice(b * c, (b + 1) * c)
        tick()
        qkv_in = pq_all[rb]
        dc_ref[b] = qkv_in[c - (DN_CONV_K - 1):]
        qkv_l.append(_silu(_causal_conv(qkv_in, qtail, b, dcw, DN_CONV_K)))
        gates_l.append(_dn_gates(pab_all[rb], alog_ref[...], dtb_ref[...], c))

    for b in range(nb):
        rb = slice(b * c, (b + 1) * c)
        tick()
        u_in = pbuf[rb, A_W:2 * A_W] * pbuf[rb, 2 * A_W:3 * A_W]
        ca_ref[b] = u_in[c - (CONV_A_K - 1):]
        ybuf[rb, 0:A_W] = (pbuf[rb, 0:A_W] * _causal_conv(u_in, utail, b, caw, CONV_A_K)).astype(BF16)

    lists = [[] for _ in range(8)]
    for b in range(nb):
        d, _, cols = gates_l[b]
        for lst, val in zip(lists, _head_lists(qkv_l[b], None, d, cols)):
            lst.extend(val)
    idx = [(b, h) for b in range(nb) for h in range(DN_HEADS)]
    q, k, v, _, d_col, d_row, dl_col, beta_col = lists
    q, k, u, w, qk = _dn_intra(q, k, v, d_col, d_row, beta_col, m, c, tick)
    s_old = [s_ref[b, h] for b, h in idx]
    vn = _each(lambda ux, wx, s: ux - _mm(wx, s), u, w, s_old)
    tick()
    o = _each(lambda qx, dc, y, s, vx: _mm(jnp.concatenate([qx * jnp.exp(dc), y], axis=1),
                                           jnp.concatenate([s, vx], axis=0)), q, d_col, qk, s_old, vn)
    tick()
    s_new = _each(lambda s, dlc, kx, dc, vx: s * jnp.exp(dlc[0:1, :]) + _mm_tn(kx * jnp.exp(dlc - dc), vx),
                  s_old, dl_col, k, d_col, vn)
    for (b, h), sx, ox in zip(idx, s_new, o):
        s_ref[b, h] = sx
        zx = pbuf[b * c:(b + 1) * c, P_Z + h * LANE:P_Z + (h + 1) * LANE]
        ybuf[b * c:(b + 1) * c, A_W + h * LANE:A_W + (h + 1) * LANE] = _dn_out(ox, zx, dnorm).astype(BF16)

    for b in range(nb):
        tick(0)
        heads = range(XA_HEADS)
        sc = [_mm_nt(pbuf[b * c:(b + 1) * c, P_XQ + h * LANE:P_XQ + (h + 1) * LANE],
                     mk_ref[b, :, h * LANE:(h + 1) * LANE]) * (XA_DH ** -0.5) for h in heads]
        e = _each(lambda x_: jnp.exp(x_ - jnp.max(x_, axis=-1, keepdims=True)), sc)
        inv = _each(lambda x_: 1.0 / jnp.sum(x_, axis=-1, keepdims=True), e)
        for h, ex, ix in zip(heads, e, inv):
            ybuf[b * c:(b + 1) * c, A_W + DN_V + h * LANE:A_W + DN_V + (h + 1) * LANE] = (
                _mm(ex, mv_ref[b, :, h * LANE:(h + 1) * LANE]) * ix).astype(BF16)
    while queue:
        tick(0)

    merged = None
    for j, (lo, hi) in enumerate(((0, A_W), (A_W, A_W + DN_V), (A_W + DN_V, A_W + DN_V + XA_W))):
        term = gbuf[:, j * D_MODEL:(j + 1) * D_MODEL] * jnp.dot(ybuf[:, lo:hi], wb_ref[lo:hi, :],
                                                                preferred_element_type=F32)
        merged = term if merged is None else merged + term
    x1 = x + jnp.dot(merged.astype(BF16), wo_ref[...], preferred_element_type=F32)
    x1_ref[...] = x1.reshape(nb, c, D_MODEL)


def _front_prompt(x, gmix, w1, w2, wab, mkb, mvb, caw, dcw, alog_row, dtb_row, dnorm, wb, wo, nb):
    bsz, length, d = x.shape
    c = CHUNK
    return pl.pallas_call(
        functools.partial(_front_prompt_kernel, nb=nb),
        grid=(bsz // nb, length // c),
        in_specs=[pl.BlockSpec((nb, c, d), lambda g, t: (g, t, 0)),
                  _resident((1, d)), _resident((d, W1)), _resident(w2.shape), _resident(wab.shape),
                  pl.BlockSpec((nb, MEM_TOKENS, XA_W), lambda g, t: (g, 0, 0), pipeline_mode=pl.Buffered(1)),
                  pl.BlockSpec((nb, MEM_TOKENS, XA_W), lambda g, t: (g, 0, 0), pipeline_mode=pl.Buffered(1)),
                  _resident((CONV_A_K, A_W)), _resident((DN_CONV_K, DN_CONV_CH)),
                  _resident((TAIL, LANE)), _resident((TAIL, LANE)), _resident((1, LANE)),
                  _resident(wb.shape), _resident(wo.shape)],
        out_specs=[pl.BlockSpec((nb, c, d), lambda g, t: (g, t, 0)),
                   pl.BlockSpec((nb, CONV_A_K - 1, A_W), lambda g, t: (g, 0, 0)),
                   pl.BlockSpec((nb, DN_CONV_K - 1, DN_CONV_CH), lambda g, t: (g, 0, 0)),
                   pl.BlockSpec((nb, DN_HEADS, DN_DK, DN_DV), lambda g, t: (g, 0, 0, 0))],
        out_shape=[jax.ShapeDtypeStruct((bsz, length, d), F32),
                   jax.ShapeDtypeStruct((bsz, CONV_A_K - 1, A_W), F32),
                   jax.ShapeDtypeStruct((bsz, DN_CONV_K - 1, DN_CONV_CH), F32),
                   jax.ShapeDtypeStruct((bsz, DN_HEADS, DN_DK, DN_DV), F32)],
        scratch_shapes=[pltpu.VMEM((nb, TAIL, A_W), F32), pltpu.VMEM((nb, TAIL, DN_CONV_CH), F32),
                        pltpu.VMEM((nb * c, A_W + DN_V + XA_W), BF16), pltpu.VMEM((nb * c, P_W), F32),
                        pltpu.VMEM((nb * c, 3 * d), F32)],
        compiler_params=_cparams(("arbitrary", "arbitrary")),
        name="front_prompt",
    )(x, gmix, w1, w2, wab, mkb, mvb, caw, dcw, alog_row, dtb_row, dnorm, wb, wo)


def _proj_kernel(x_ref, g_ref, w1_ref, w2_ref, wab_ref, p1_ref, p2_ref, pab_ref, xq_ref):
    xn = _rms(x_ref[...], g_ref[...]).astype(BF16)
    p1_ref[...] = jnp.dot(xn, w1_ref[...], preferred_element_type=F32)
    p2 = jnp.dot(xn, w2_ref[...], preferred_element_type=F32)
    p2_ref[...] = p2
    pab_ref[...] = jnp.dot(xn, wab_ref[...], preferred_element_type=F32)
    for h in range(XA_HEADS):
        xq_ref[h] = p2[:, h * LANE:(h + 1) * LANE]


def _proj(x2d, gain, w1, w2, wab, tm):
    t, d = x2d.shape
    row = lambda n: pl.BlockSpec((tm, n), lambda i: (i, 0))
    return pl.pallas_call(
        _proj_kernel,
        grid=(t // tm,),
        in_specs=[row(d), _resident((1, d)), _resident((d, W1)), _resident(w2.shape), _resident(wab.shape)],
        out_specs=[row(W1), row(W2), row(LANE), pl.BlockSpec((XA_HEADS, tm, LANE), lambda i: (0, i, 0))],
        out_shape=[jax.ShapeDtypeStruct((t, W1), F32), jax.ShapeDtypeStruct((t, W2), F32),
                   jax.ShapeDtypeStruct((t, LANE), F32), jax.ShapeDtypeStruct((XA_HEADS, t, LANE), F32)],
        compiler_params=_cparams(("arbitrary",)),
        name="proj",
    )(x2d, gain, w1, w2, wab)


SEQ_S = 4
NB_S = CHUNK // SEQ_S
NB_ATTN_S = 16


def _seg_conv(x, e, wts, width):
    rows, w = x.shape
    xt = x.reshape(rows // TAIL, TAIL, w)
    et = e.reshape(rows // TAIL, TAIL, w)
    tmod = lax.broadcasted_iota(jnp.int32, (1, TAIL, 1), 1) & (SEQ_S - 1)
    acc = None
    for i in range(width):
        s = width - 1 - i
        term = xt if s == 0 else jnp.where(tmod >= s, pltpu.roll(xt, s, 1), 0.0)
        if i < width - 1:
            hist = et if i == 0 else pltpu.roll(et, TAIL - i, 1)
            term = term + jnp.where(tmod < SEQ_S - i, hist, 0.0)
        term = wts[i:i + 1][None] * term
        acc = term if acc is None else acc + term
    return acc.reshape(rows, w)


def _branch_sample_kernel(pa_ref, pq_ref, pz_ref, pab_ref, ea_ref, eq_ref, s0_ref, caw_ref, dcw_ref,
                          alog_ref, dtb_ref, dnorm_ref, mexp2_ref, mexpt_ref, yad_ref, u_ref, s_ref):
    c = CHUNK
    m = _dn_masks(SEQ_S)

    pa = pa_ref[...]
    u_in = pa[:, A_W:2 * A_W] * pa[:, 2 * A_W:3 * A_W]
    u_ref[...] = u_in
    conv = _seg_conv(u_in, ea_ref[...], caw_ref[...], CONV_A_K)
    yad_ref[:, 0:A_W] = (pa[:, 0:A_W] * conv).astype(BF16)

    qkv = _silu(_seg_conv(pq_ref[...], eq_ref[...], dcw_ref[...], DN_CONV_K))
    d, dl, cols = _dn_gates(pab_ref[...], alog_ref[...], dtb_ref[...], SEQ_S)
    dec_t = jnp.exp(dl)
    dnorm = dnorm_ref[...]

    wide = NB_S * DN_DK
    mexp2 = mexp2_ref[...]
    mexp_t = mexpt_ref[...]

    q, k, v, z, d_col, d_row, dl_col, beta_col = _head_lists(qkv, pz_ref[...], d, cols)
    q, k, u, w, qk = _dn_intra(q, k, v, d_col, d_row, beta_col, m, SEQ_S)
    heads = list(range(DN_HEADS))
    s_old = [s0_ref[:, h].reshape(wide, DN_DV) for h in heads]
    x_exp = _each(lambda wx, qx, dc: jnp.concatenate(
        [jnp.concatenate([wx, qx * jnp.exp(dc)], axis=0).astype(BF16)] * NB_S, axis=1) * mexp2, w, q, d_col)
    ws = _each(_mm, x_exp, s_old)
    vn = _each(lambda ux, x: ux - x[:c], u, ws)
    o = _each(lambda x, y, vx: x[c:] + _mm(y, vx), ws, qk, vn)
    k_exp = _each(lambda kx, dlc, dc: jnp.concatenate(
        [(kx * jnp.exp(dlc - dc)).T.astype(BF16)] * NB_S, axis=0) * mexp_t, k, dl_col, d_col)
    dec = [jnp.concatenate([jnp.broadcast_to(dec_t[h:h + 1, SEQ_S * b:SEQ_S * b + 1], (DN_DK, DN_DV))
                            for b in range(NB_S)], axis=0) for h in heads]
    s_new = _each(lambda s, dx, kx, vx: s * dx + _mm(kx, vx), s_old, dec, k_exp, vn)
    for h, sx, ox, zx in zip(heads, s_new, o, z):
        s_ref[:, h] = sx.reshape(NB_S, DN_DK, DN_DV)
        yad_ref[:, A_W + h * LANE:A_W + (h + 1) * LANE] = _dn_out(ox, zx, dnorm).astype(BF16)


def _branch_sample(p1, pab, ea, eq, state, caw, dcw, alog_row, dtb_row, dnorm):
    t = p1.shape[0]
    c = CHUNK
    full = lambda shape: pl.BlockSpec(shape, lambda i: (0,) * len(shape))
    wide = NB_S * DN_DK
    owner = np.arange(wide) // DN_DK
    seq = (np.arange(2 * c) % c) // SEQ_S
    mexp2 = jnp.asarray(seq[:, None] == owner[None, :], BF16)
    mexp_t = jnp.asarray(owner[:, None] == seq[None, :c], BF16)
    return pl.pallas_call(
        _branch_sample_kernel,
        grid=(t // c,),
        in_specs=[pl.BlockSpec((c, 3 * A_W), lambda i: (i, 0)),
                  pl.BlockSpec((c, DN_CONV_CH), lambda i: (i, OFF_QKV // DN_CONV_CH)),
                  pl.BlockSpec((c, DN_V), lambda i: (i, OFF_Z // DN_V)),
                  pl.BlockSpec((c, LANE), lambda i: (i, 0)),
                  pl.BlockSpec((c, A_W), lambda i: (i, 0)),
                  pl.BlockSpec((c, DN_CONV_CH), lambda i: (i, 0)),
                  pl.BlockSpec((NB_S, DN_HEADS, DN_DK, DN_DV), lambda i: (i, 0, 0, 0)),
                  full((CONV_A_K, A_W)), full((DN_CONV_K, DN_CONV_CH)),
                  full((TAIL, LANE)), full((TAIL, LANE)), full((1, LANE)),
                  _resident(mexp2.shape), _resident(mexp_t.shape)],
        out_specs=[pl.BlockSpec((c, A_W + DN_V), lambda i: (i, 0)),
                   pl.BlockSpec((c, A_W), lambda i: (i, 0)),
                   pl.BlockSpec((NB_S, DN_HEADS, DN_DK, DN_DV), lambda i: (i, 0, 0, 0))],
        out_shape=[jax.ShapeDtypeStruct((t, A_W + DN_V), BF16),
                   jax.ShapeDtypeStruct((t, A_W), F32),
                   jax.ShapeDtypeStruct(state.shape, F32)],
        compiler_params=_cparams(("arbitrary",)),
        name="branch_sample",
    )(p1, p1, p1, pab, ea, eq, state, caw, dcw, alog_row, dtb_row, dnorm, mexp2, mexp_t)


def _attn_sample_kernel(q_ref, k_ref, v_ref, o_ref):
    for h in range(XA_HEADS):
        rows = pl.ds(h, MEM_TOKENS, stride=XA_HEADS)
        s = jnp.einsum("bqd,bkd->bqk", q_ref[h].astype(BF16), k_ref[:, rows, :].astype(BF16),
                       preferred_element_type=F32) * (XA_DH ** -0.5)
        e = jnp.exp(s - jnp.max(s, axis=-1, keepdims=True))
        p = e / jnp.sum(e, axis=-1, keepdims=True)
        o_ref[h] = jnp.einsum("bqk,bkd->bqd", p.astype(BF16), v_ref[:, rows, :].astype(BF16),
                              preferred_element_type=F32)


def _attn_sample(q4, ck, cv, nb):
    _, bsz, length, _ = q4.shape
    head_major = pl.BlockSpec((XA_HEADS, nb, length, XA_DH), lambda i: (0, i, 0, 0))
    return pl.pallas_call(
        _attn_sample_kernel,
        grid=(bsz // nb,),
        in_specs=[head_major,
                  pl.BlockSpec((nb, MEM_TOKENS * XA_HEADS, XA_DH), lambda i: (i, 0, 0)),
                  pl.BlockSpec((nb, MEM_TOKENS * XA_HEADS, XA_DH), lambda i: (i, 0, 0))],
        out_specs=head_major,
        out_shape=jax.ShapeDtypeStruct(q4.shape, F32),
        compiler_params=_cparams(("arbitrary",)),
        name="attn_sample",
    )(q4, ck, cv)


def _merge_kernel(yad_ref, ym_ref, p2_ref, x_ref, wb_ref, wo_ref, o_ref):
    yad = yad_ref[...]
    ym = jnp.concatenate([ym_ref[h] for h in range(XA_HEADS)], axis=1).astype(BF16)
    gate = lambda j: jax.nn.sigmoid(p2_ref[:, OFF_G + j * D_MODEL:OFF_G + (j + 1) * D_MODEL])
    merged = (gate(0) * jnp.dot(yad[:, :A_W], wb_ref[0:A_W, :], preferred_element_type=F32)
              + gate(1) * jnp.dot(yad[:, A_W:], wb_ref[A_W:A_W + DN_V, :], preferred_element_type=F32)
              + gate(2) * jnp.dot(ym, wb_ref[A_W + DN_V:, :], preferred_element_type=F32))
    o_ref[...] = x_ref[...] + jnp.dot(merged.astype(BF16), wo_ref[...], preferred_element_type=F32)


MXU_K = 256
FF_EDGES = (0, 6 * MXU_K, D_FF)
FF_SPLIT = len(FF_EDGES) - 1
FF_SUB = 256
TM_FF = 4 * FF_SUB


def _ffn_kernel(x_ref, gf_ref, wu_ref, wd_ref, gl_ref, o_ref):
    n_sub = x_ref.shape[0] // FF_SUB

    def block(xn, acc, j):
        lo, hi = FF_EDGES[j], FF_EDGES[j + 1]
        gate = jnp.dot(xn, wu_ref[:, lo:hi], preferred_element_type=F32)
        up = jnp.dot(xn, wu_ref[:, D_FF + lo:D_FF + hi], preferred_element_type=F32)
        hid = (_silu(gate) * up).astype(BF16)
        return acc + jnp.dot(hid, wd_ref[lo:hi, :], preferred_element_type=F32)

    state = [None] * n_sub
    for i in range(n_sub + 1):
        if i < n_sub:
            x = x_ref[i * FF_SUB:(i + 1) * FF_SUB, :]
            xn = _rms(x, gf_ref[...]).astype(BF16)
            state[i] = (xn, block(xn, x, 0))
        if i > 0:
            xn, acc = state[i - 1]
            for j in range(1, FF_SPLIT):
                acc = block(xn, acc, j)
            o_ref[(i - 1) * FF_SUB:i * FF_SUB, :] = _rms(acc, gl_ref[...])


STAGE_U = 64
STAGE_D = D_FF // 8
STAGE_SLOTS = 4


def _ffn_prompt_kernel(x_ref, gf_ref, wu_hbm, wd_hbm, gl_ref, o_ref, wub_hbm, wdb_hbm,
                       wu_s, wd_s, stg_u, stg_d, sem):
    first = pl.program_id(0) == 0
    ns = STAGE_SLOTS
    writebacks = ((wu_s, wub_hbm, 2 * ns), (wd_s, wdb_hbm, 2 * ns + 1))

    @pl.when(first)
    def _stage_weights():
        jobs = ([(wu_hbm, wu_s, stg_u, 0, r * STAGE_U, STAGE_U) for r in range(wu_s.shape[0] // STAGE_U)]
                + [(wd_hbm, wd_s, stg_d, ns, r * STAGE_D, STAGE_D) for r in range(wd_s.shape[0] // STAGE_D)])

        def copy(n):
            src, _, stg, s0, lo, sz = jobs[n]
            return pltpu.make_async_copy(src.at[pl.ds(lo, sz), :], stg.at[n % ns], sem.at[s0 + n % ns])

        for n in range(min(ns, len(jobs))):
            copy(n).start()
        for n, (_, dst, stg, _, lo, sz) in enumerate(jobs):
            copy(n).wait()
            dst[lo:lo + sz, :] = stg[n % ns].astype(BF16)
            if n + ns < len(jobs):
                copy(n + ns).start()
        for src, dst, s in writebacks:
            pltpu.make_async_copy(src, dst, sem.at[s]).start()

    _ffn_kernel(x_ref, gf_ref, wu_s, wd_s, gl_ref, o_ref)

    @pl.when(first)
    def _finish_writeback():
        for src, dst, s in writebacks:
            pltpu.make_async_copy(src, dst, sem.at[s]).wait()


def _ffn(x2d, gf, wu32, wd32, gl, tm):
    t, d = x2d.shape
    assert wu32.shape[0] % (STAGE_SLOTS * STAGE_U) == 0 and wd32.shape[0] % (STAGE_SLOTS * STAGE_D) == 0
    anywhere = pl.BlockSpec(memory_space=pl.ANY)
    return pl.pallas_call(
        _ffn_prompt_kernel,
        grid=(t // tm,),
        in_specs=[pl.BlockSpec((tm, d), lambda i: (i, 0)), _resident((1, d)), anywhere, anywhere,
                  _resident((1, d))],
        out_specs=[pl.BlockSpec((tm, d), lambda i: (i, 0)), anywhere, anywhere],
        out_shape=[jax.ShapeDtypeStruct((t, d), F32), jax.ShapeDtypeStruct(wu32.shape, BF16),
                   jax.ShapeDtypeStruct(wd32.shape, BF16)],
        scratch_shapes=[pltpu.VMEM(wu32.shape, BF16), pltpu.VMEM(wd32.shape, BF16),
                        pltpu.VMEM((STAGE_SLOTS, STAGE_U, wu32.shape[1]), F32),
                        pltpu.VMEM((STAGE_SLOTS, STAGE_D, wd32.shape[1]), F32),
                        pltpu.SemaphoreType.DMA((2 * STAGE_SLOTS + 2,))],
        compiler_params=_cparams(("arbitrary",)),
        name="ffn",
    )(x2d, gf, wu32, wd32, gl)


def _tail_sample_kernel(yad_ref, ym_ref, p2_ref, x_ref, wb_ref, wo_ref, gf_ref, wu_ref, wd_ref, gl_ref,
                        o_ref, x1_buf):
    _merge_kernel(yad_ref, ym_ref, p2_ref, x_ref, wb_ref, wo_ref, x1_buf)
    _ffn_kernel(x1_buf, gf_ref, wu_ref, wd_ref, gl_ref, o_ref)


def _tail_sample(yad, ym, p2, x2d, wb, wo, gf, wu, wd, gl):
    t, d = x2d.shape
    args = (yad, ym, p2, x2d, wb, wo, gf, wu, wd, gl)
    return pl.pallas_call(
        _tail_sample_kernel,
        grid=(1,),
        in_specs=[_resident(a.shape) for a in args],
        out_specs=pl.BlockSpec((t, d), lambda i: (0, 0)),
        out_shape=jax.ShapeDtypeStruct((t, d), F32),
        scratch_shapes=[pltpu.VMEM((t, d), F32)],
        compiler_params=_cparams(("arbitrary",)),
        name="tail_sample",
    )(*args)


def _head_rows(v):
    col = jnp.zeros((TAIL, 1), F32).at[:v.shape[0], 0].set(v.astype(F32))
    return jnp.broadcast_to(col, (TAIL, LANE))


def kernel(x_prompt, x_sample, mem_prompt, state_conv_a, state_dn_conv, state_dn, cache_mem_k, cache_mem_v,
           norm_mix, w_in, conv_a_w, dn_conv_w, dn_a_log, dn_dt_bias, dn_norm, norm_mem, w_mem_kv, w_branch,
           w_o, norm_ffn, w_ffn_up, w_ffn_down, norm_final):
    bp, lp, d = x_prompt.shape
    bs, ls, _ = x_sample.shape
    assert norm_mix.shape[0] == 1 and ls == SEQ_S and lp % CHUNK == 0 and (bs * ls) % CHUNK == 0
    assert w_in.shape[2] == W1 + N_AB + W2

    w = w_in[0]
    w1 = w.astype(BF16)
    w2 = w1[:, W1 + N_AB:]
    wab = jnp.pad(w1[:, W1:W1 + N_AB], ((0, 0), (0, LANE - N_AB)))
    wb = w_branch[0].astype(BF16)
    wo = w_o[0].astype(BF16)
    wkv = w_mem_kv[0].astype(BF16)
    g_mix = norm_mix[0][None, :]
    g_ffn = norm_ffn[0][None, :]
    g_fin = norm_final[None, :]
    g_mem = norm_mem[0][None, :]
    caw = conv_a_w[0]
    dcw = dn_conv_w[0]
    alog_row = _head_rows(dn_a_log[0])
    dtb_row = _head_rows(dn_dt_bias[0])
    dnorm = dn_norm[0][None, :]

    tp = bp * lp
    mk, mv, mkb, mvb = _memkv(mem_prompt.reshape(bp * MEM_TOKENS, d), g_mem, wkv, TM)
    x1_p, ca_p, dc_p, s_p = _front_prompt(x_prompt, g_mix, w1, w2, wab, mkb.reshape(bp, MEM_TOKENS, XA_W),
                                          mvb.reshape(bp, MEM_TOKENS, XA_W), caw, dcw, alog_row, dtb_row, dnorm,
                                          wb, wo, NB_P)

    ts = bs * ls
    xs2 = x_sample.reshape(ts, d)
    p1_s, p2_s, pab_s, xq_s = _proj(xs2, g_mix, w1, w2, wab, CHUNK)
    ea = jnp.pad(state_conv_a[0], ((0, 0), (0, ls - (CONV_A_K - 1)), (0, 0))).reshape(ts, A_W)
    eq = jnp.pad(state_dn_conv[0], ((0, 0), (0, ls - (DN_CONV_K - 1)), (0, 0))).reshape(ts, DN_CONV_CH)
    yad_s, u_s, s_s = _branch_sample(p1_s, pab_s, ea, eq, state_dn[0], caw, dcw, alog_row, dtb_row, dnorm)
    ym_s = _attn_sample(xq_s.reshape(XA_HEADS, bs, ls, XA_DH),
                        cache_mem_k.reshape(bs, MEM_TOKENS * XA_HEADS, XA_DH),
                        cache_mem_v.reshape(bs, MEM_TOKENS * XA_HEADS, XA_DH), NB_ATTN_S)
    ca_s = u_s.reshape(bs, ls, A_W)[:, ls - (CONV_A_K - 1):]
    dc_s = p1_s[:, OFF_QKV:OFF_Z].reshape(bs, ls, DN_CONV_CH)[:, ls - (DN_CONV_K - 1):]

    y_p, wu, wd = _ffn(x1_p.reshape(tp, d), g_ffn, w_ffn_up[0], w_ffn_down[0], g_fin, TM_FF)
    y_p = y_p.reshape(bp, lp, d)
    y_s = _tail_sample(yad_s, ym_s.reshape(XA_HEADS, ts, XA_DH), p2_s, xs2, wb, wo, g_ffn, wu, wd, g_fin)
    y_s = y_s.reshape(bs, ls, d)

    return (y_p, y_s, ca_p[None], dc_p[None], s_p[None],
            mk.reshape(1, bp, MEM_TOKENS, XA_HEADS, XA_DH), mv.reshape(1, bp, MEM_TOKENS, XA_HEADS, XA_DH),
            ca_s[None], dc_s[None], s_s[None])
```

```python
import functools

import jax
import jax.numpy as jnp
import numpy as np
from jax import lax
from jax.experimental import pallas as pl
from jax.experimental.pallas import tpu as pltpu

F32 = jnp.float32
BF16 = jnp.bfloat16

D_MODEL = 1024
A_W = 512
CONV_A_K = 3
DN_HEADS = 4
DN_DK = 128
DN_DV = 128
DN_QK = DN_HEADS * DN_DK
DN_V = DN_HEADS * DN_DV
DN_CONV_CH = 2 * DN_QK + DN_V
DN_CONV_K = 4
MEM_TOKENS = 256
XA_HEADS = 4
XA_DH = 128
XA_W = XA_HEADS * XA_DH
D_FF = 2816
EPS = 1e-6

LANE = 128
CHUNK = 128
TAIL = 8
NB_P = 4
TM = 512
GATE_BLK = 256
PROJ_BLK = 512
ATTN_RESERVE = 4

W1 = 3 * A_W + DN_CONV_CH + DN_V
OFF_QKV = 3 * A_W
OFF_Z = OFF_QKV + DN_CONV_CH
W2 = XA_W + 3 * D_MODEL
OFF_G = XA_W
N_AB = 2 * DN_HEADS
P_Z = 3 * A_W
P_XQ = P_Z + DN_V
P_W = P_XQ + XA_W

VMEM_LIMIT = 60 * 1024 * 1024


def _cparams(sem):
    return pltpu.CompilerParams(dimension_semantics=sem, vmem_limit_bytes=VMEM_LIMIT)


def _resident(shape):
    return pl.BlockSpec(shape, lambda *_: (0,) * len(shape), pipeline_mode=pl.Buffered(1))


def _mm(a, b):
    return jnp.dot(a.astype(BF16), b.astype(BF16), preferred_element_type=F32)


def _mm_nt(a, b):
    return lax.dot_general(a.astype(BF16), b.astype(BF16), (((1,), (1,)), ((), ())),
                           preferred_element_type=F32)


def _mm_tn(a, b):
    return lax.dot_general(a.astype(BF16), b.astype(BF16), (((0,), (0,)), ((), ())),
                           preferred_element_type=F32)


def _rms(x, g):
    return x * lax.rsqrt(jnp.mean(x * x, axis=-1, keepdims=True) + EPS) * g


def _silu(x):
    return x * jax.nn.sigmoid(x)


def _softplus(x):
    return jnp.maximum(x, 0.0) + jnp.log1p(jnp.exp(-jnp.abs(x)))


STAGE_SLOTS = 4
STAGE_ROWS = 256


def _memkv_kernel(x_ref, g_ref, wkv_hbm, wb_hbm, wo_hbm, k_ref, v_ref, kb_ref, vb_ref, wbb_hbm, wob_hbm,
                  w_ref, wb_s, wo_s, stage, sem):
    first = pl.program_id(0) == 0
    ns = STAGE_SLOTS
    writebacks = ((wb_s, wbb_hbm, ns), (wo_s, wob_hbm, ns + 1))

    @pl.when(first)
    def _stage_weights():
        jobs = [(src, dst, r * STAGE_ROWS) for src, dst in ((wkv_hbm, w_ref), (wb_hbm, wb_s), (wo_hbm, wo_s))
                for r in range(dst.shape[0] // STAGE_ROWS)]

        def copy(n):
            src, _, lo = jobs[n]
            return pltpu.make_async_copy(src.at[pl.ds(lo, STAGE_ROWS), :], stage.at[n % ns], sem.at[n % ns])

        for n in range(min(ns, len(jobs))):
            copy(n).start()
        for n, (_, dst, lo) in enumerate(jobs):
            copy(n).wait()
            dst[lo:lo + STAGE_ROWS, :] = stage[n % ns].astype(BF16)
            if n + ns < len(jobs):
                copy(n + ns).start()
        for src, dst, s in writebacks:
            pltpu.make_async_copy(src, dst, sem.at[s]).start()

    xn = _rms(x_ref[...], g_ref[...]).astype(BF16)
    kv = jnp.dot(xn, w_ref[...], preferred_element_type=F32)
    k = kv[:, :XA_W]
    v = kv[:, XA_W:]
    tm = k.shape[0]
    for h in range(XA_HEADS):
        rows = pl.ds(h, tm, stride=XA_HEADS)
        k_ref[rows, :] = k[:, h * LANE:(h + 1) * LANE]
        v_ref[rows, :] = v[:, h * LANE:(h + 1) * LANE]
    kb_ref[...] = k.astype(BF16)
    vb_ref[...] = v.astype(BF16)

    @pl.when(first)
    def _finish_writeback():
        for src, dst, s in writebacks:
            pltpu.make_async_copy(src, dst, sem.at[s]).wait()


def _memkv(mem2d, gain, wkv32, wb32, wo32, tm):
    t, d = mem2d.shape
    assert all(w.shape[1] == d and w.shape[0] % STAGE_ROWS == 0 for w in (wkv32, wb32, wo32))
    blk = pl.BlockSpec((tm, XA_W), lambda i: (i, 0))
    blk_rows = pl.BlockSpec((tm * XA_HEADS, XA_DH), lambda i: (i, 0))
    anywhere = pl.BlockSpec(memory_space=pl.ANY)
    return pl.pallas_call(
        _memkv_kernel,
        grid=(t // tm,),
        in_specs=[pl.BlockSpec((tm, d), lambda i: (i, 0)),
                  pl.BlockSpec((1, d), lambda i: (0, 0)),
                  anywhere, anywhere, anywhere],
        out_specs=[blk_rows, blk_rows, blk, blk, anywhere, anywhere],
        out_shape=[jax.ShapeDtypeStruct((t * XA_HEADS, XA_DH), F32),
                   jax.ShapeDtypeStruct((t * XA_HEADS, XA_DH), F32),
                   jax.ShapeDtypeStruct((t, XA_W), BF16), jax.ShapeDtypeStruct((t, XA_W), BF16),
                   jax.ShapeDtypeStruct(wb32.shape, BF16), jax.ShapeDtypeStruct(wo32.shape, BF16)],
        scratch_shapes=[pltpu.VMEM(wkv32.shape, BF16), pltpu.VMEM(wb32.shape, BF16), pltpu.VMEM(wo32.shape, BF16),
                        pltpu.VMEM((STAGE_SLOTS, STAGE_ROWS, d), F32),
                        pltpu.SemaphoreType.DMA((STAGE_SLOTS + 2,))],
        compiler_params=_cparams(("arbitrary",)),
        name="memkv",
    )(mem2d, gain, wkv32, wb32, wo32)


def _log2(n):
    return n.bit_length() - 1


def _dn_masks(seg):
    r = lax.broadcasted_iota(jnp.int32, (CHUNK, CHUNK), 0)
    c = lax.broadcasted_iota(jnp.int32, (CHUNK, CHUNK), 1)
    ls = _log2(seg)
    same = (r >> ls) == (c >> ls)
    base = min(8, seg)
    lb = _log2(base)
    m = {
        "causal": (same & (r >= c)).astype(F32),
        "strict": (same & (r > c)).astype(F32),
        "eye": (r == c).astype(F32),
        "neg_diag": -((r >> lb) == (c >> lb)).astype(F32),
        "off": {},
        "base": base,
    }
    s = base
    while s < seg:
        l1, l2 = _log2(s), _log2(2 * s)
        m["off"][s] = (((r >> l2) == (c >> l2)) & ((r >> l1) != (c >> l1))).astype(F32)
        s *= 2
    return m


def _each(f, *lists):
    return [f(*args) for args in zip(*lists)]


def _low_rows(x, s):
    return jnp.concatenate([x[i + s:i + 2 * s] for i in range(0, x.shape[0], 2 * s)], axis=0)


def _merge_low(x, low, s):
    parts = []
    for j, i in enumerate(range(0, x.shape[0], 2 * s)):
        parts += [x[i:i + s], low[j * s:(j + 1) * s]]
    return jnp.concatenate(parts, axis=0)


def _spread_low(low, s):
    return _merge_low(jnp.zeros((2 * low.shape[0], low.shape[1]), low.dtype), low, s)


def _tri_inv(a_list, m, seg, tick):
    add = lambda x, y: x + y
    b = _each(lambda a: a * m["neg_diag"], a_list)
    p = _each(lambda x: m["eye"] + x, b)
    b2 = _each(_mm, b, b)
    tick()
    p = _each(add, p, _each(_mm, p, b2))
    tick()
    if m["base"] == 8:
        b4 = _each(_mm, b2, b2)
        tick()
        p = _each(add, p, _each(_mm, p, b4))
        tick()
    s = m["base"]
    while s < seg:
        low = lambda t, s=s: _low_rows(t, s)
        x = _each(_mm, _each(lambda a, s=s: low(a) * low(m["off"][s]), a_list), p)
        tick()
        r = _each(_mm, _each(low, p), _each(lambda y, s=s: _spread_low(y, s), x))
        p = _each(lambda t, y, s=s: _merge_low(t, low(t) - y, s), p, r)
        tick()
        s *= 2
    return p


def _seg_scan(x, seg, reverse):
    n = x.shape[1]
    pos = lax.broadcasted_iota(jnp.int32, x.shape, 1) & (seg - 1)
    s = 1
    while s < seg:
        shifted = pltpu.roll(x, n - s if reverse else s, 1)
        x = x + jnp.where(pos < seg - s if reverse else pos >= s, shifted, 0.0)
        s *= 2
    return x


def _dn_gates(ab, alog8, dtb8, seg):
    abt = ab.T[0:TAIL]
    g = -jnp.exp(alog8) * _softplus(abt + dtb8)
    d = _seg_scan(g, seg, False)
    dl = d + _seg_scan(g, seg, True) - g
    beta = jax.nn.sigmoid(abt)
    pad = jnp.zeros((CHUNK - 3 * TAIL, CHUNK), F32)
    return d, dl, jnp.concatenate([d, dl, beta, pad], axis=0).T


def _l2n(x, scale=1.0):
    return x * (lax.rsqrt(jnp.sum(x * x, axis=-1, keepdims=True) + EPS) * scale)


def _dn_intra(q, k, v, d_col, d_row, beta_col, m, seg, tick=lambda: None):
    q = _each(lambda x: _l2n(x, DN_DK ** -0.5), q)
    k = _each(_l2n, k)
    gamma = _each(lambda dc, dr: jnp.exp((dc - dr) * m["causal"]) * m["causal"], d_col, d_row)
    kk = _each(_mm_nt, k, k)
    a = _each(lambda bc, x, g: (bc * x) * g * m["strict"], beta_col, kk, gamma)
    t = _tri_inv(a, m, seg, tick)
    rhs = _each(lambda vv, kx, bc, dc: jnp.concatenate([vv * bc, kx * (bc * jnp.exp(dc))], axis=1),
                v, k, beta_col, d_col)
    sol = _each(_mm, t, rhs)
    u = [x[:, :DN_DV] for x in sol]
    w = [x[:, DN_DV:] for x in sol]
    qk = _each(lambda x, g: x * g, _each(_mm_nt, q, k), gamma)
    return q, k, u, w, qk


def _dn_out(o, z, dnorm):
    return _rms(o, dnorm) * _silu(z)


def _head_lists(qkv, z, d, cols):
    out = [[] for _ in range(8)]
    for h in range(DN_HEADS):
        beta_lane = 2 * TAIL + DN_HEADS + h
        vals = (qkv[:, h * LANE:(h + 1) * LANE],
                qkv[:, DN_QK + h * LANE:DN_QK + (h + 1) * LANE],
                qkv[:, 2 * DN_QK + h * LANE:2 * DN_QK + (h + 1) * LANE],
                None if z is None else z[:, h * LANE:(h + 1) * LANE],
                cols[:, h:h + 1], d[h:h + 1, :], cols[:, TAIL + h:TAIL + h + 1],
                cols[:, beta_lane:beta_lane + 1])
        for lst, val in zip(out, vals):
            lst.append(val)
    return out


def _causal_conv(x, tail_ref, b, wts, width):
    c, w = x.shape
    tiles = jnp.concatenate([tail_ref[b][None], x.reshape(c // TAIL, TAIL, w)], axis=0)
    sub = lax.broadcasted_iota(jnp.int32, (1, TAIL, 1), 1)
    acc = None
    for i in range(width):
        s = width - 1 - i
        if s == 0:
            y = tiles[1:]
        else:
            r = pltpu.roll(tiles, s, 1)
            y = jnp.where(sub >= s, r[1:], r[:-1])
        term = wts[i:i + 1][None] * y
        acc = term if acc is None else acc + term
    tail_ref[b] = tiles[c // TAIL]
    return acc.reshape(c, w)


def _front_prompt_kernel(x_ref, gmix_ref, w1_ref, w2_ref, wab_ref, mk_ref, mv_ref, caw_ref, dcw_ref,
                         alog_ref, dtb_ref, dnorm_ref, wb_ref, wo_ref,
                         x1_ref, ca_ref, dc_ref, s_ref, utail, qtail, ybuf, pbuf, gbuf, *, nb):
    c = CHUNK
    rows = nb * c
    t_idx = pl.program_id(1)

    @pl.when(t_idx == 0)
    def _():
        utail[...] = jnp.zeros(utail.shape, F32)
        qtail[...] = jnp.zeros(qtail.shape, F32)
        s_ref[...] = jnp.zeros(s_ref.shape, F32)

    x = x_ref[...].reshape(rows, D_MODEL)
    xn = _rms(x, gmix_ref[...]).astype(BF16)
    proj = lambda w_ref, lo, hi: jnp.dot(xn, w_ref[:, lo:hi], preferred_element_type=F32)

    queue = []

    def enqueue(dst, w_ref, src, dst_lo, width, act=None):
        def run():
            r = proj(w_ref, src, src + width)
            dst[:, dst_lo:dst_lo + width] = r if act is None else act(r)
        queue.append(run)

    for lo in range(0, 3 * A_W, PROJ_BLK):
        enqueue(pbuf, w1_ref, lo, lo, PROJ_BLK)
    enqueue(pbuf, w1_ref, OFF_Z, P_Z, DN_V)
    enqueue(pbuf, w2_ref, 0, P_XQ, XA_W)
    for lo in range(0, 3 * D_MODEL, GATE_BLK):
        enqueue(gbuf, w2_ref, OFF_G + lo, lo, GATE_BLK, jax.nn.sigmoid)
    queue.reverse()

    def tick(keep=ATTN_RESERVE):
        if len(queue) > keep:
            queue.pop()()

    m = _dn_masks(c)
    caw = caw_ref[...]
    dcw = dcw_ref[...]
    dnorm = dnorm_ref[...]
    pq_all = proj(w1_ref, OFF_QKV, OFF_Z)
    pab_all = jnp.dot(xn, wab_ref[...], preferred_element_type=F32)

    qkv_l, gates_l = [], []
    for b in range(nb):
        rb = slice(b * c, (b + 1) * c)
        tick()
        qkv_in = pq_all[rb]
        dc_ref[b] = qkv_in[c - (DN_CONV_K - 1):]
        qkv_l.append(_silu(_causal_conv(qkv_in, qtail, b, dcw, DN_CONV_K)))
        gates_l.append(_dn_gates(pab_all[rb], alog_ref[...], dtb_ref[...], c))

    for b in range(nb):
        rb = slice(b * c, (b + 1) * c)
        tick()
        u_in = pbuf[rb, A_W:2 * A_W] * pbuf[rb, 2 * A_W:3 * A_W]
        ca_ref[b] = u_in[c - (CONV_A_K - 1):]
        ybuf[rb, 0:A_W] = (pbuf[rb, 0:A_W] * _causal_conv(u_in, utail, b, caw, CONV_A_K)).astype(BF16)

    lists = [[] for _ in range(8)]
    for b in range(nb):
        d, _, cols = gates_l[b]
        for lst, val in zip(lists, _head_lists(qkv_l[b], None, d, cols)):
            lst.extend(val)
    idx = [(b, h) for b in range(nb) for h in range(DN_HEADS)]
    q, k, v, _, d_col, d_row, dl_col, beta_col = lists
    q, k, u, w, qk = _dn_intra(q, k, v, d_col, d_row, beta_col, m, c, tick)
    s_old = [s_ref[b, h] for b, h in idx]
    vn = _each(lambda ux, wx, s: ux - _mm(wx, s), u, w, s_old)
    tick()
    o = _each(lambda qx, dc, y, s, vx: _mm(jnp.concatenate([qx * jnp.exp(dc), y], axis=1),
                                           jnp.concatenate([s, vx], axis=0)), q, d_col, qk, s_old, vn)
    tick()
    s_new = _each(lambda s, dlc, kx, dc, vx: s * jnp.exp(dlc[0:1, :]) + _mm_tn(kx * jnp.exp(dlc - dc), vx),
                  s_old, dl_col, k, d_col, vn)
    for (b, h), sx, ox in zip(idx, s_new, o):
        s_ref[b, h] = sx
        zx = pbuf[b * c:(b + 1) * c, P_Z + h * LANE:P_Z + (h + 1) * LANE]
        ybuf[b * c:(b + 1) * c, A_W + h * LANE:A_W + (h + 1) * LANE] = _dn_out(ox, zx, dnorm).astype(BF16)

    for b in range(nb):
        tick(0)
        heads = range(XA_HEADS)
        sc = [_mm_nt(pbuf[b * c:(b + 1) * c, P_XQ + h * LANE:P_XQ + (h + 1) * LANE],
                     mk_ref[b, :, h * LANE:(h + 1) * LANE]) * (XA_DH ** -0.5) for h in heads]
        e = _each(lambda x_: jnp.exp(x_ - jnp.max(x_, axis=-1, keepdims=True)), sc)
        inv = _each(lambda x_: 1.0 / jnp.sum(x_, axis=-1, keepdims=True), e)
        for h, ex, ix in zip(heads, e, inv):
            ybuf[b * c:(b + 1) * c, A_W + DN_V + h * LANE:A_W + DN_V + (h + 1) * LANE] = (
                _mm(ex, mv_ref[b, :, h * LANE:(h + 1) * LANE]) * ix).astype(BF16)
    while queue:
        tick(0)

    merged = None
    for j, (lo, hi) in enumerate(((0, A_W), (A_W, A_W + DN_V), (A_W + DN_V, A_W + DN_V + XA_W))):
        term = gbuf[:, j * D_MODEL:(j + 1) * D_MODEL] * jnp.dot(ybuf[:, lo:hi], wb_ref[lo:hi, :],
                                                                preferred_element_type=F32)
        merged = term if merged is None else merged + term
    x1 = x + jnp.dot(merged.astype(BF16), wo_ref[...], preferred_element_type=F32)
    x1_ref[...] = x1.reshape(nb, c, D_MODEL)


def _front_prompt(x, gmix, w1, w2, wab, mkb, mvb, caw, dcw, alog_row, dtb_row, dnorm, wb, wo, nb):
    bsz, length, d = x.shape
    c = CHUNK
    return pl.pallas_call(
        functools.partial(_front_prompt_kernel, nb=nb),
        grid=(bsz // nb, length // c),
        in_specs=[pl.BlockSpec((nb, c, d), lambda g, t: (g, t, 0)),
                  _resident((1, d)), _resident((d, W1)), _resident(w2.shape), _resident(wab.shape),
                  pl.BlockSpec((nb, MEM_TOKENS, XA_W), lambda g, t: (g, 0, 0), pipeline_mode=pl.Buffered(1)),
                  pl.BlockSpec((nb, MEM_TOKENS, XA_W), lambda g, t: (g, 0, 0), pipeline_mode=pl.Buffered(1)),
                  _resident((CONV_A_K, A_W)), _resident((DN_CONV_K, DN_CONV_CH)),
                  _resident((TAIL, LANE)), _resident((TAIL, LANE)), _resident((1, LANE)),
                  _resident(wb.shape), _resident(wo.shape)],
        out_specs=[pl.BlockSpec((nb, c, d), lambda g, t: (g, t, 0)),
                   pl.BlockSpec((nb, CONV_A_K - 1, A_W), lambda g, t: (g, 0, 0)),
                   pl.BlockSpec((nb, DN_CONV_K - 1, DN_CONV_CH), lambda g, t: (g, 0, 0)),
                   pl.BlockSpec((nb, DN_HEADS, DN_DK, DN_DV), lambda g, t: (g, 0, 0, 0))],
        out_shape=[jax.ShapeDtypeStruct((bsz, length, d), F32),
                   jax.ShapeDtypeStruct((bsz, CONV_A_K - 1, A_W), F32),
                   jax.ShapeDtypeStruct((bsz, DN_CONV_K - 1, DN_CONV_CH), F32),
                   jax.ShapeDtypeStruct((bsz, DN_HEADS, DN_DK, DN_DV), F32)],
        scratch_shapes=[pltpu.VMEM((nb, TAIL, A_W), F32), pltpu.VMEM((nb, TAIL, DN_CONV_CH), F32),
                        pltpu.VMEM((nb * c, A_W + DN_V + XA_W), BF16), pltpu.VMEM((nb * c, P_W), F32),
                        pltpu.VMEM((nb * c, 3 * d), F32)],
        compiler_params=_cparams(("arbitrary", "arbitrary")),
        name="front_prompt",
    )(x, gmix, w1, w2, wab, mkb, mvb, caw, dcw, alog_row, dtb_row, dnorm, wb, wo)


def _proj_kernel(x_ref, g_ref, w1_ref, w2_ref, wab_ref, p1_ref, p2_ref, pab_ref, xq_ref):
    xn = _rms(x_ref[...], g_ref[...]).astype(BF16)
    p1_ref[...] = jnp.dot(xn, w1_ref[...], preferred_element_type=F32)
    p2 = jnp.dot(xn, w2_ref[...], preferred_element_type=F32)
    p2_ref[...] = p2
    pab_ref[...] = jnp.dot(xn, wab_ref[...], preferred_element_type=F32)
    for h in range(XA_HEADS):
        xq_ref[h] = p2[:, h * LANE:(h + 1) * LANE]


def _proj(x2d, gain, w1, w2, wab, tm):
    t, d = x2d.shape
    row = lambda n: pl.BlockSpec((tm, n), lambda i: (i, 0))
    return pl.pallas_call(
        _proj_kernel,
        grid=(t // tm,),
        in_specs=[row(d), _resident((1, d)), _resident((d, W1)), _resident(w2.shape), _resident(wab.shape)],
        out_specs=[row(W1), row(W2), row(LANE), pl.BlockSpec((XA_HEADS, tm, LANE), lambda i: (0, i, 0))],
        out_shape=[jax.ShapeDtypeStruct((t, W1), F32), jax.ShapeDtypeStruct((t, W2), F32),
                   jax.ShapeDtypeStruct((t, LANE), F32), jax.ShapeDtypeStruct((XA_HEADS, t, LANE), F32)],
        compiler_params=_cparams(("arbitrary",)),
        name="proj",
    )(x2d, gain, w1, w2, wab)


SEQ_S = 4
NB_S = CHUNK // SEQ_S
NB_ATTN_S = 16


def _seg_conv(x, e, wts, width):
    rows, w = x.shape
    xt = x.reshape(rows // TAIL, TAIL, w)
    et = e.reshape(rows // TAIL, TAIL, w)
    tmod = lax.broadcasted_iota(jnp.int32, (1, TAIL, 1), 1) & (SEQ_S - 1)
    acc = None
    for i in range(width):
        s = width - 1 - i
        term = xt if s == 0 else jnp.where(tmod >= s, pltpu.roll(xt, s, 1), 0.0)
        if i < width - 1:
            hist = et if i == 0 else pltpu.roll(et, TAIL - i, 1)
            term = term + jnp.where(tmod < SEQ_S - i, hist, 0.0)
        term = wts[i:i + 1][None] * term
        acc = term if acc is None else acc + term
    return acc.reshape(rows, w)


def _branch_sample_kernel(pa_ref, pq_ref, pz_ref, pab_ref, ea_ref, eq_ref, s0_ref, caw_ref, dcw_ref,
                          alog_ref, dtb_ref, dnorm_ref, mexp2_ref, mexpt_ref, yad_ref, u_ref, s_ref):
    c = CHUNK
    m = _dn_masks(SEQ_S)

    pa = pa_ref[...]
    u_in = pa[:, A_W:2 * A_W] * pa[:, 2 * A_W:3 * A_W]
    u_ref[...] = u_in
    conv = _seg_conv(u_in, ea_ref[...], caw_ref[...], CONV_A_K)
    yad_ref[:, 0:A_W] = (pa[:, 0:A_W] * conv).astype(BF16)

    qkv = _silu(_seg_conv(pq_ref[...], eq_ref[...], dcw_ref[...], DN_CONV_K))
    d, dl, cols = _dn_gates(pab_ref[...], alog_ref[...], dtb_ref[...], SEQ_S)
    dec_t = jnp.exp(dl)
    dnorm = dnorm_ref[...]

    wide = NB_S * DN_DK
    mexp2 = mexp2_ref[...]
    mexp_t = mexpt_ref[...]

    q, k, v, z, d_col, d_row, dl_col, beta_col = _head_lists(qkv, pz_ref[...], d, cols)
    q, k, u, w, qk = _dn_intra(q, k, v, d_col, d_row, beta_col, m, SEQ_S)
    heads = list(range(DN_HEADS))
    s_old = [s0_ref[:, h].reshape(wide, DN_DV) for h in heads]
    x_exp = _each(lambda wx, qx, dc: jnp.concatenate(
        [jnp.concatenate([wx, qx * jnp.exp(dc)], axis=0).astype(BF16)] * NB_S, axis=1) * mexp2, w, q, d_col)
    ws = _each(_mm, x_exp, s_old)
    vn = _each(lambda ux, x: ux - x[:c], u, ws)
    o = _each(lambda x, y, vx: x[c:] + _mm(y, vx), ws, qk, vn)
    k_exp = _each(lambda kx, dlc, dc: jnp.concatenate(
        [(kx * jnp.exp(dlc - dc)).T.astype(BF16)] * NB_S, axis=0) * mexp_t, k, dl_col, d_col)
    dec = [jnp.concatenate([jnp.broadcast_to(dec_t[h:h + 1, SEQ_S * b:SEQ_S * b + 1], (DN_DK, DN_DV))
                            for b in range(NB_S)], axis=0) for h in heads]
    s_new = _each(lambda s, dx, kx, vx: s * dx + _mm(kx, vx), s_old, dec, k_exp, vn)
    for h, sx, ox, zx in zip(heads, s_new, o, z):
        s_ref[:, h] = sx.reshape(NB_S, DN_DK, DN_DV)
        yad_ref[:, A_W + h * LANE:A_W + (h + 1) * LANE] = _dn_out(ox, zx, dnorm).astype(BF16)


def _branch_sample(p1, pab, ea, eq, state, caw, dcw, alog_row, dtb_row, dnorm):
    t = p1.shape[0]
    c = CHUNK
    full = lambda shape: pl.BlockSpec(shape, lambda i: (0,) * len(shape))
    wide = NB_S * DN_DK
    owner = np.arange(wide) // DN_DK
    seq = (np.arange(2 * c) % c) // SEQ_S
    mexp2 = jnp.asarray(seq[:, None] == owner[None, :], BF16)
    mexp_t = jnp.asarray(owner[:, None] == seq[None, :c], BF16)
    return pl.pallas_call(
        _branch_sample_kernel,
        grid=(t // c,),
        in_specs=[pl.BlockSpec((c, 3 * A_W), lambda i: (i, 0)),
                  pl.BlockSpec((c, DN_CONV_CH), lambda i: (i, OFF_QKV // DN_CONV_CH)),
                  pl.BlockSpec((c, DN_V), lambda i: (i, OFF_Z // DN_V)),
                  pl.BlockSpec((c, LANE), lambda i: (i, 0)),
                  pl.BlockSpec((c, A_W), lambda i: (i, 0)),
                  pl.BlockSpec((c, DN_CONV_CH), lambda i: (i, 0)),
                  pl.BlockSpec((NB_S, DN_HEADS, DN_DK, DN_DV), lambda i: (i, 0, 0, 0)),
                  full((CONV_A_K, A_W)), full((DN_CONV_K, DN_CONV_CH)),
                  full((TAIL, LANE)), full((TAIL, LANE)), full((1, LANE)),
                  _resident(mexp2.shape), _resident(mexp_t.shape)],
        out_specs=[pl.BlockSpec((c, A_W + DN_V), lambda i: (i, 0)),
                   pl.BlockSpec((c, A_W), lambda i: (i, 0)),
                   pl.BlockSpec((NB_S, DN_HEADS, DN_DK, DN_DV), lambda i: (i, 0, 0, 0))],
        out_shape=[jax.ShapeDtypeStruct((t, A_W + DN_V), BF16),
                   jax.ShapeDtypeStruct((t, A_W), F32),
                   jax.ShapeDtypeStruct(state.shape, F32)],
        compiler_params=_cparams(("arbitrary",)),
        name="branch_sample",
    )(p1, p1, p1, pab, ea, eq, state, caw, dcw, alog_row, dtb_row, dnorm, mexp2, mexp_t)


def _attn_sample_kernel(q_ref, k_ref, v_ref, o_ref):
    for h in range(XA_HEADS):
        rows = pl.ds(h, MEM_TOKENS, stride=XA_HEADS)
        s = jnp.einsum("bqd,bkd->bqk", q_ref[h].astype(BF16), k_ref[:, rows, :].astype(BF16),
                       preferred_element_type=F32) * (XA_DH ** -0.5)
        e = jnp.exp(s - jnp.max(s, axis=-1, keepdims=True))
        p = e / jnp.sum(e, axis=-1, keepdims=True)
        o_ref[h] = jnp.einsum("bqk,bkd->bqd", p.astype(BF16), v_ref[:, rows, :].astype(BF16),
                              preferred_element_type=F32)


def _attn_sample(q4, ck, cv, nb):
    _, bsz, length, _ = q4.shape
    head_major = pl.BlockSpec((XA_HEADS, nb, length, XA_DH), lambda i: (0, i, 0, 0))
    return pl.pallas_call(
        _attn_sample_kernel,
        grid=(bsz // nb,),
        in_specs=[head_major,
                  pl.BlockSpec((nb, MEM_TOKENS * XA_HEADS, XA_DH), lambda i: (i, 0, 0)),
                  pl.BlockSpec((nb, MEM_TOKENS * XA_HEADS, XA_DH), lambda i: (i, 0, 0))],
        out_specs=head_major,
        out_shape=jax.ShapeDtypeStruct(q4.shape, F32),
        compiler_params=_cparams(("arbitrary",)),
        name="attn_sample",
    )(q4, ck, cv)


def _merge_kernel(yad_ref, ym_ref, p2_ref, x_ref, wb_ref, wo_ref, o_ref):
    yad = yad_ref[...]
    ym = jnp.concatenate([ym_ref[h] for h in range(XA_HEADS)], axis=1).astype(BF16)
    gate = lambda j: jax.nn.sigmoid(p2_ref[:, OFF_G + j * D_MODEL:OFF_G + (j + 1) * D_MODEL])
    merged = (gate(0) * jnp.dot(yad[:, :A_W], wb_ref[0:A_W, :], preferred_element_type=F32)
              + gate(1) * jnp.dot(yad[:, A_W:], wb_ref[A_W:A_W + DN_V, :], preferred_element_type=F32)
              + gate(2) * jnp.dot(ym, wb_ref[A_W + DN_V:, :], preferred_element_type=F32))
    o_ref[...] = x_ref[...] + jnp.dot(merged.astype(BF16), wo_ref[...], preferred_element_type=F32)


MXU_K = 256
FF_EDGES = (0, 6 * MXU_K, D_FF)
FF_SPLIT = len(FF_EDGES) - 1
FF_SUB = 256
TM_FF = 4 * FF_SUB


def _ffn_kernel(x_ref, gf_ref, wu_ref, wd_ref, gl_ref, o_ref):
    n_sub = x_ref.shape[0] // FF_SUB

    def block(xn, acc, j):
        lo, hi = FF_EDGES[j], FF_EDGES[j + 1]
        gate = jnp.dot(xn, wu_ref[:, lo:hi], preferred_element_type=F32)
        up = jnp.dot(xn, wu_ref[:, D_FF + lo:D_FF + hi], preferred_element_type=F32)
        hid = (_silu(gate) * up).astype(BF16)
        return acc + jnp.dot(hid, wd_ref[lo:hi, :], preferred_element_type=F32)

    state = [None] * n_sub
    for i in range(n_sub + 1):
        if i < n_sub:
            x = x_ref[i * FF_SUB:(i + 1) * FF_SUB, :]
            xn = _rms(x, gf_ref[...]).astype(BF16)
            state[i] = (xn, block(xn, x, 0))
        if i > 0:
            xn, acc = state[i - 1]
            for j in range(1, FF_SPLIT):
                acc = block(xn, acc, j)
            o_ref[(i - 1) * FF_SUB:i * FF_SUB, :] = _rms(acc, gl_ref[...])


STAGE_U = 64
STAGE_D = D_FF // 8


def _ffn_prompt_kernel(x_ref, gf_ref, wu_hbm, wd_hbm, gl_ref, o_ref, wub_hbm, wdb_hbm,
                       wu_s, wd_s, stg_u, stg_d, sem):
    first = pl.program_id(0) == 0
    ns = STAGE_SLOTS
    writebacks = ((wu_s, wub_hbm, 2 * ns), (wd_s, wdb_hbm, 2 * ns + 1))

    @pl.when(first)
    def _stage_weights():
        jobs = ([(wu_hbm, wu_s, stg_u, 0, r * STAGE_U, STAGE_U) for r in range(wu_s.shape[0] // STAGE_U)]
                + [(wd_hbm, wd_s, stg_d, ns, r * STAGE_D, STAGE_D) for r in range(wd_s.shape[0] // STAGE_D)])

        def copy(n):
            src, _, stg, s0, lo, sz = jobs[n]
            return pltpu.make_async_copy(src.at[pl.ds(lo, sz), :], stg.at[n % ns], sem.at[s0 + n % ns])

        for n in range(min(ns, len(jobs))):
            copy(n).start()
        for n, (_, dst, stg, _, lo, sz) in enumerate(jobs):
            copy(n).wait()
            dst[lo:lo + sz, :] = stg[n % ns].astype(BF16)
            if n + ns < len(jobs):
                copy(n + ns).start()
        for src, dst, s in writebacks:
            pltpu.make_async_copy(src, dst, sem.at[s]).start()

    _ffn_kernel(x_ref, gf_ref, wu_s, wd_s, gl_ref, o_ref)

    @pl.when(first)
    def _finish_writeback():
        for src, dst, s in writebacks:
            pltpu.make_async_copy(src, dst, sem.at[s]).wait()


def _ffn(x2d, gf, wu32, wd32, gl, tm):
    t, d = x2d.shape
    assert wu32.shape[0] % (STAGE_SLOTS * STAGE_U) == 0 and wd32.shape[0] % (STAGE_SLOTS * STAGE_D) == 0
    anywhere = pl.BlockSpec(memory_space=pl.ANY)
    return pl.pallas_call(
        _ffn_prompt_kernel,
        grid=(t // tm,),
        in_specs=[pl.BlockSpec((tm, d), lambda i: (i, 0)), _resident((1, d)), anywhere, anywhere,
                  _resident((1, d))],
        out_specs=[pl.BlockSpec((tm, d), lambda i: (i, 0)), anywhere, anywhere],
        out_shape=[jax.ShapeDtypeStruct((t, d), F32), jax.ShapeDtypeStruct(wu32.shape, BF16),
                   jax.ShapeDtypeStruct(wd32.shape, BF16)],
        scratch_shapes=[pltpu.VMEM(wu32.shape, BF16), pltpu.VMEM(wd32.shape, BF16),
                        pltpu.VMEM((STAGE_SLOTS, STAGE_U, wu32.shape[1]), F32),
                        pltpu.VMEM((STAGE_SLOTS, STAGE_D, wd32.shape[1]), F32),
                        pltpu.SemaphoreType.DMA((2 * STAGE_SLOTS + 2,))],
        compiler_params=_cparams(("arbitrary",)),
        name="ffn",
    )(x2d, gf, wu32, wd32, gl)


def _tail_sample_kernel(yad_ref, ym_ref, p2_ref, x_ref, wb_ref, wo_ref, gf_ref, wu_ref, wd_ref, gl_ref,
                        o_ref, x1_buf):
    _merge_kernel(yad_ref, ym_ref, p2_ref, x_ref, wb_ref, wo_ref, x1_buf)
    _ffn_kernel(x1_buf, gf_ref, wu_ref, wd_ref, gl_ref, o_ref)


def _tail_sample(yad, ym, p2, x2d, wb, wo, gf, wu, wd, gl):
    t, d = x2d.shape
    args = (yad, ym, p2, x2d, wb, wo, gf, wu, wd, gl)
    return pl.pallas_call(
        _tail_sample_kernel,
        grid=(1,),
        in_specs=[_resident(a.shape) for a in args],
        out_specs=pl.BlockSpec((t, d), lambda i: (0, 0)),
        out_shape=jax.ShapeDtypeStruct((t, d), F32),
        scratch_shapes=[pltpu.VMEM((t, d), F32)],
        compiler_params=_cparams(("arbitrary",)),
        name="tail_sample",
    )(*args)


def _head_rows(v):
    col = jnp.zeros((TAIL, 1), F32).at[:v.shape[0], 0].set(v.astype(F32))
    return jnp.broadcast_to(col, (TAIL, LANE))


def kernel(x_prompt, x_sample, mem_prompt, state_conv_a, state_dn_conv, state_dn, cache_mem_k, cache_mem_v,
           norm_mix, w_in, conv_a_w, dn_conv_w, dn_a_log, dn_dt_bias, dn_norm, norm_mem, w_mem_kv, w_branch,
           w_o, norm_ffn, w_ffn_up, w_ffn_down, norm_final):
    bp, lp, d = x_prompt.shape
    bs, ls, _ = x_sample.shape
    assert norm_mix.shape[0] == 1 and ls == SEQ_S and lp % CHUNK == 0 and (bs * ls) % CHUNK == 0
    assert w_in.shape[2] == W1 + N_AB + W2

    w = w_in[0]
    w1 = w.astype(BF16)
    w2 = w1[:, W1 + N_AB:]
    wab = jnp.pad(w1[:, W1:W1 + N_AB], ((0, 0), (0, LANE - N_AB)))
    g_mix = norm_mix[0][None, :]
    g_ffn = norm_ffn[0][None, :]
    g_fin = norm_final[None, :]
    g_mem = norm_mem[0][None, :]
    caw = conv_a_w[0]
    dcw = dn_conv_w[0]
    alog_row = _head_rows(dn_a_log[0])
    dtb_row = _head_rows(dn_dt_bias[0])
    dnorm = dn_norm[0][None, :]

    tp = bp * lp
    mk, mv, mkb, mvb, wb, wo = _memkv(mem_prompt.reshape(bp * MEM_TOKENS, d), g_mem, w_mem_kv[0], w_branch[0],
                                      w_o[0], TM)
    x1_p, ca_p, dc_p, s_p = _front_prompt(x_prompt, g_mix, w1, w2, wab, mkb.reshape(bp, MEM_TOKENS, XA_W),
                                          mvb.reshape(bp, MEM_TOKENS, XA_W), caw, dcw, alog_row, dtb_row, dnorm,
                                          wb, wo, NB_P)

    ts = bs * ls
    xs2 = x_sample.reshape(ts, d)
    p1_s, p2_s, pab_s, xq_s = _proj(xs2, g_mix, w1, w2, wab, CHUNK)
    ea = jnp.pad(state_conv_a[0], ((0, 0), (0, ls - (CONV_A_K - 1)), (0, 0))).reshape(ts, A_W)
    eq = jnp.pad(state_dn_conv[0], ((0, 0), (0, ls - (DN_CONV_K - 1)), (0, 0))).reshape(ts, DN_CONV_CH)
    yad_s, u_s, s_s = _branch_sample(p1_s, pab_s, ea, eq, state_dn[0], caw, dcw, alog_row, dtb_row, dnorm)
    ym_s = _attn_sample(xq_s.reshape(XA_HEADS, bs, ls, XA_DH),
                        cache_mem_k.reshape(bs, MEM_TOKENS * XA_HEADS, XA_DH),
                        cache_mem_v.reshape(bs, MEM_TOKENS * XA_HEADS, XA_DH), NB_ATTN_S)
    ca_s = u_s.reshape(bs, ls, A_W)[:, ls - (CONV_A_K - 1):]
    dc_s = p1_s[:, OFF_QKV:OFF_Z].reshape(bs, ls, DN_CONV_CH)[:, ls - (DN_CONV_K - 1):]

    y_p, wu, wd = _ffn(x1_p.reshape(tp, d), g_ffn, w_ffn_up[0], w_ffn_down[0], g_fin, TM_FF)
    y_p = y_p.reshape(bp, lp, d)
    y_s = _tail_sample(yad_s, ym_s.reshape(XA_HEADS, ts, XA_DH), p2_s, xs2, wb, wo, g_ffn, wu, wd, g_fin)
    y_s = y_s.reshape(bs, ls, d)

    return (y_p, y_s, ca_p[None], dc_p[None], s_p[None],
            mk.reshape(1, bp, MEM_TOKENS, XA_HEADS, XA_DH), mv.reshape(1, bp, MEM_TOKENS, XA_HEADS, XA_DH),
            ca_s[None], dc_s[None], s_s[None])
```

```python
import functools

import jax
import jax.numpy as jnp
import numpy as np
from jax import lax
from jax.experimental import pallas as pl
from jax.experimental.pallas import tpu as pltpu

F32 = jnp.float32
BF16 = jnp.bfloat16

D_MODEL = 1024
A_W = 512
CONV_A_K = 3
DN_HEADS = 4
DN_DK = 128
DN_DV = 128
DN_QK = DN_HEADS * DN_DK
DN_V = DN_HEADS * DN_DV
DN_CONV_CH = 2 * DN_QK + DN_V
DN_CONV_K = 4
MEM_TOKENS = 256
XA_HEADS = 4
XA_DH = 128
XA_W = XA_HEADS * XA_DH
D_FF = 2816
EPS = 1e-6

LANE = 128
CHUNK = 128
TAIL = 8
NB_P = 4
TM = 512
GATE_BLK = 256
PROJ_BLK = 512
ATTN_RESERVE = 4

W1 = 3 * A_W + DN_CONV_CH + DN_V
OFF_QKV = 3 * A_W
OFF_Z = OFF_QKV + DN_CONV_CH
W2 = XA_W + 3 * D_MODEL
OFF_G = XA_W
N_AB = 2 * DN_HEADS
P_Z = 3 * A_W
P_XQ = P_Z + DN_V
P_W = P_XQ + XA_W

VMEM_LIMIT = 60 * 1024 * 1024


def _cparams(sem):
    return pltpu.CompilerParams(dimension_semantics=sem, vmem_limit_bytes=VMEM_LIMIT)


def _resident(shape):
    return pl.BlockSpec(shape, lambda *_: (0,) * len(shape), pipeline_mode=pl.Buffered(1))


def _mm(a, b):
    return jnp.dot(a.astype(BF16), b.astype(BF16), preferred_element_type=F32)


def _mm_nt(a, b):
    return lax.dot_general(a.astype(BF16), b.astype(BF16), (((1,), (1,)), ((), ())),
                           preferred_element_type=F32)


def _mm_tn(a, b):
    return lax.dot_general(a.astype(BF16), b.astype(BF16), (((0,), (0,)), ((), ())),
                           preferred_element_type=F32)


def _rms(x, g):
    return x * lax.rsqrt(jnp.mean(x * x, axis=-1, keepdims=True) + EPS) * g


def _silu(x):
    return x * jax.nn.sigmoid(x)


def _softplus(x):
    return jnp.maximum(x, 0.0) + jnp.log1p(jnp.exp(-jnp.abs(x)))


STAGE_SLOTS = 4
STAGE_ROWS = 256


def _memkv_kernel(x_ref, g_ref, wkv_hbm, wb_hbm, wo_hbm, k_ref, v_ref, kb_ref, vb_ref, wbb_hbm, wob_hbm,
                  w_ref, wb_s, wo_s, stage, sem):
    first = pl.program_id(0) == 0
    ns = STAGE_SLOTS
    writebacks = ((wb_s, wbb_hbm, ns), (wo_s, wob_hbm, ns + 1))

    @pl.when(first)
    def _stage_weights():
        jobs = [(src, dst, r * STAGE_ROWS) for src, dst in ((wkv_hbm, w_ref), (wb_hbm, wb_s), (wo_hbm, wo_s))
                for r in range(dst.shape[0] // STAGE_ROWS)]

        def copy(n):
            src, _, lo = jobs[n]
            return pltpu.make_async_copy(src.at[pl.ds(lo, STAGE_ROWS), :], stage.at[n % ns], sem.at[n % ns])

        for n in range(min(ns, len(jobs))):
            copy(n).start()
        for n, (_, dst, lo) in enumerate(jobs):
            copy(n).wait()
            dst[lo:lo + STAGE_ROWS, :] = stage[n % ns].astype(BF16)
            if n + ns < len(jobs):
                copy(n + ns).start()
        for src, dst, s in writebacks:
            pltpu.make_async_copy(src, dst, sem.at[s]).start()

    xn = _rms(x_ref[...], g_ref[...]).astype(BF16)
    kv = jnp.dot(xn, w_ref[...], preferred_element_type=F32)
    k = kv[:, :XA_W]
    v = kv[:, XA_W:]
    tm = k.shape[0]
    for h in range(XA_HEADS):
        rows = pl.ds(h, tm, stride=XA_HEADS)
        k_ref[rows, :] = k[:, h * LANE:(h + 1) * LANE]
        v_ref[rows, :] = v[:, h * LANE:(h + 1) * LANE]
    kb_ref[...] = k.astype(BF16)
    vb_ref[...] = v.astype(BF16)

    @pl.when(first)
    def _finish_writeback():
        for src, dst, s in writebacks:
            pltpu.make_async_copy(src, dst, sem.at[s]).wait()


def _memkv(mem2d, gain, wkv32, wb32, wo32, tm):
    t, d = mem2d.shape
    assert all(w.shape[1] == d and w.shape[0] % STAGE_ROWS == 0 for w in (wkv32, wb32, wo32))
    blk = pl.BlockSpec((tm, XA_W), lambda i: (i, 0))
    blk_rows = pl.BlockSpec((tm * XA_HEADS, XA_DH), lambda i: (i, 0))
    anywhere = pl.BlockSpec(memory_space=pl.ANY)
    return pl.pallas_call(
        _memkv_kernel,
        grid=(t // tm,),
        in_specs=[pl.BlockSpec((tm, d), lambda i: (i, 0)),
                  pl.BlockSpec((1, d), lambda i: (0, 0)),
                  anywhere, anywhere, anywhere],
        out_specs=[blk_rows, blk_rows, blk, blk, anywhere, anywhere],
        out_shape=[jax.ShapeDtypeStruct((t * XA_HEADS, XA_DH), F32),
                   jax.ShapeDtypeStruct((t * XA_HEADS, XA_DH), F32),
                   jax.ShapeDtypeStruct((t, XA_W), BF16), jax.ShapeDtypeStruct((t, XA_W), BF16),
                   jax.ShapeDtypeStruct(wb32.shape, BF16), jax.ShapeDtypeStruct(wo32.shape, BF16)],
        scratch_shapes=[pltpu.VMEM(wkv32.shape, BF16), pltpu.VMEM(wb32.shape, BF16), pltpu.VMEM(wo32.shape, BF16),
                        pltpu.VMEM((STAGE_SLOTS, STAGE_ROWS, d), F32),
                        pltpu.SemaphoreType.DMA((STAGE_SLOTS + 2,))],
        compiler_params=_cparams(("arbitrary",)),
        name="memkv",
    )(mem2d, gain, wkv32, wb32, wo32)


def _log2(n):
    return n.bit_length() - 1


def _dn_masks(seg):
    r = lax.broadcasted_iota(jnp.int32, (CHUNK, CHUNK), 0)
    c = lax.broadcasted_iota(jnp.int32, (CHUNK, CHUNK), 1)
    ls = _log2(seg)
    same = (r >> ls) == (c >> ls)
    base = min(8, seg)
    lb = _log2(base)
    m = {
        "causal": (same & (r >= c)).astype(F32),
        "strict": (same & (r > c)).astype(F32),
        "eye": (r == c).astype(F32),
        "neg_diag": -((r >> lb) == (c >> lb)).astype(F32),
        "off": {},
        "base": base,
    }
    s = base
    while s < seg:
        l1, l2 = _log2(s), _log2(2 * s)
        m["off"][s] = (((r >> l2) == (c >> l2)) & ((r >> l1) != (c >> l1))).astype(F32)
        s *= 2
    return m


def _each(f, *lists):
    return [f(*args) for args in zip(*lists)]


def _low_rows(x, s):
    return jnp.concatenate([x[i + s:i + 2 * s] for i in range(0, x.shape[0], 2 * s)], axis=0)


def _merge_low(x, low, s):
    parts = []
    for j, i in enumerate(range(0, x.shape[0], 2 * s)):
        parts += [x[i:i + s], low[j * s:(j + 1) * s]]
    return jnp.concatenate(parts, axis=0)


def _spread_low(low, s):
    return _merge_low(jnp.zeros((2 * low.shape[0], low.shape[1]), low.dtype), low, s)


def _tri_inv(a_list, m, seg, tick):
    add = lambda x, y: x + y
    b = _each(lambda a: a * m["neg_diag"], a_list)
    p = _each(lambda x: m["eye"] + x, b)
    b2 = _each(_mm, b, b)
    tick()
    p = _each(add, p, _each(_mm, p, b2))
    tick()
    if m["base"] == 8:
        b4 = _each(_mm, b2, b2)
        tick()
        p = _each(add, p, _each(_mm, p, b4))
        tick()
    s = m["base"]
    while s < seg:
        low = lambda t, s=s: _low_rows(t, s)
        x = _each(_mm, _each(lambda a, s=s: low(a) * low(m["off"][s]), a_list), p)
        tick()
        r = _each(_mm, _each(low, p), _each(lambda y, s=s: _spread_low(y, s), x))
        p = _each(lambda t, y, s=s: _merge_low(t, low(t) - y, s), p, r)
        tick()
        s *= 2
    return p


def _seg_scan(x, seg, reverse):
    n = x.shape[1]
    pos = lax.broadcasted_iota(jnp.int32, x.shape, 1) & (seg - 1)
    s = 1
    while s < seg:
        shifted = pltpu.roll(x, n - s if reverse else s, 1)
        x = x + jnp.where(pos < seg - s if reverse else pos >= s, shifted, 0.0)
        s *= 2
    return x


def _dn_gates(ab, alog8, dtb8, seg):
    abt = ab.T[0:TAIL]
    g = -jnp.exp(alog8) * _softplus(abt + dtb8)
    d = _seg_scan(g, seg, False)
    dl = d + _seg_scan(g, seg, True) - g
    beta = jax.nn.sigmoid(abt)
    pad = jnp.zeros((CHUNK - 3 * TAIL, CHUNK), F32)
    return d, dl, jnp.concatenate([d, dl, beta, pad], axis=0).T


def _l2n(x, scale=1.0):
    return x * (lax.rsqrt(jnp.sum(x * x, axis=-1, keepdims=True) + EPS) * scale)


def _dn_intra(q, k, v, d_col, d_row, beta_col, m, seg, tick=lambda: None):
    q = _each(lambda x: _l2n(x, DN_DK ** -0.5), q)
    k = _each(_l2n, k)
    gamma = _each(lambda dc, dr: jnp.exp((dc - dr) * m["causal"]) * m["causal"], d_col, d_row)
    kk = _each(_mm_nt, k, k)
    a = _each(lambda bc, x, g: (bc * x) * g * m["strict"], beta_col, kk, gamma)
    t = _tri_inv(a, m, seg, tick)
    rhs = _each(lambda vv, kx, bc, dc: jnp.concatenate([vv * bc, kx * (bc * jnp.exp(dc))], axis=1),
                v, k, beta_col, d_col)
    sol = _each(_mm, t, rhs)
    u = [x[:, :DN_DV] for x in sol]
    w = [x[:, DN_DV:] for x in sol]
    qk = _each(lambda x, g: x * g, _each(_mm_nt, q, k), gamma)
    return q, k, u, w, qk


def _dn_out(o, z, dnorm):
    return _rms(o, dnorm) * _silu(z)


def _head_lists(qkv, z, d, cols):
    out = [[] for _ in range(8)]
    for h in range(DN_HEADS):
        beta_lane = 2 * TAIL + DN_HEADS + h
        vals = (qkv[:, h * LANE:(h + 1) * LANE],
                qkv[:, DN_QK + h * LANE:DN_QK + (h + 1) * LANE],
                qkv[:, 2 * DN_QK + h * LANE:2 * DN_QK + (h + 1) * LANE],
                None if z is None else z[:, h * LANE:(h + 1) * LANE],
                cols[:, h:h + 1], d[h:h + 1, :], cols[:, TAIL + h:TAIL + h + 1],
                cols[:, beta_lane:beta_lane + 1])
        for lst, val in zip(out, vals):
            lst.append(val)
    return out


def _causal_conv(x, tail_ref, b, wts, width):
    c, w = x.shape
    tiles = jnp.concatenate([tail_ref[b][None], x.reshape(c // TAIL, TAIL, w)], axis=0)
    sub = lax.broadcasted_iota(jnp.int32, (1, TAIL, 1), 1)
    acc = None
    for i in range(width):
        s = width - 1 - i
        if s == 0:
            y = tiles[1:]
        else:
            r = pltpu.roll(tiles, s, 1)
            y = jnp.where(sub >= s, r[1:], r[:-1])
        term = wts[i:i + 1][None] * y
        acc = term if acc is None else acc + term
    tail_ref[b] = tiles[c // TAIL]
    return acc.reshape(c, w)


def _front_prompt_kernel(x_ref, gmix_ref, w1_ref, w2_ref, wab_ref, mk_ref, mv_ref, caw_ref, dcw_ref,
                         alog_ref, dtb_ref, dnorm_ref, wb_ref, wo_ref,
                         x1_ref, ca_ref, dc_ref, s_ref, utail, qtail, ybuf, pbuf, gbuf, *, nb):
    c = CHUNK
    rows = nb * c
    t_idx = pl.program_id(1)

    @pl.when(t_idx == 0)
    def _():
        utail[...] = jnp.zeros(utail.shape, F32)
        qtail[...] = jnp.zeros(qtail.shape, F32)
        s_ref[...] = jnp.zeros(s_ref.shape, F32)

    x = x_ref[...].reshape(rows, D_MODEL)
    xn = _rms(x, gmix_ref[...]).astype(BF16)
    proj = lambda w_ref, lo, hi: jnp.dot(xn, w_ref[:, lo:hi], preferred_element_type=F32)

    queue = []

    def enqueue(dst, w_ref, src, dst_lo, width, act=None):
        def run():
            r = proj(w_ref, src, src + width)
            dst[:, dst_lo:dst_lo + width] = r if act is None else act(r)
        queue.append(run)

    for lo in range(0, 3 * A_W, PROJ_BLK):
        enqueue(pbuf, w1_ref, lo, lo, PROJ_BLK)
    enqueue(pbuf, w1_ref, OFF_Z, P_Z, DN_V)
    enqueue(pbuf, w2_ref, 0, P_XQ, XA_W)
    for lo in range(0, 3 * D_MODEL, GATE_BLK):
        enqueue(gbuf, w2_ref, OFF_G + lo, lo, GATE_BLK, jax.nn.sigmoid)
    queue.reverse()

    def tick(keep=ATTN_RESERVE):
        if len(queue) > keep:
            queue.pop()()

    m = _dn_masks(c)
    caw = caw_ref[...]
    dcw = dcw_ref[...]
    dnorm = dnorm_ref[...]
    pq_all = proj(w1_ref, OFF_QKV, OFF_Z)
    pab_all = jnp.dot(xn, wab_ref[...], preferred_element_type=F32)

    qkv_l, gates_l = [], []
    for b in range(nb):
        rb = slice(b * c, (b + 1) * c)
        tick()
        qkv_in = pq_all[rb]
        dc_ref[b] = qkv_in[c - (DN_CONV_K - 1):]
        qkv_l.append(_silu(_causal_conv(qkv_in, qtail, b, dcw, DN_CONV_K)))
        gates_l.append(_dn_gates(pab_all[rb], alog_ref[...], dtb_ref[...], c))

    for b in range(nb):
        rb = slice(b * c, (b + 1) * c)
        tick()
        u_in = pbuf[rb, A_W:2 * A_W] * pbuf[rb, 2 * A_W:3 * A_W]
        ca_ref[b] = u_in[c - (CONV_A_K - 1):]
        ybuf[rb, 0:A_W] = (pbuf[rb, 0:A_W] * _causal_conv(u_in, utail, b, caw, CONV_A_K)).astype(BF16)

    lists = [[] for _ in range(8)]
    for b in range(nb):
        d, _, cols = gates_l[b]
        for lst, val in zip(lists, _head_lists(qkv_l[b], None, d, cols)):
            lst.extend(val)
    idx = [(b, h) for b in range(nb) for h in range(DN_HEADS)]
    q, k, v, _, d_col, d_row, dl_col, beta_col = lists
    q, k, u, w, qk = _dn_intra(q, k, v, d_col, d_row, beta_col, m, c, tick)
    s_old = [s_ref[b, h] for b, h in idx]
    vn = _each(lambda ux, wx, s: ux - _mm(wx, s), u, w, s_old)
    tick()
    o = _each(lambda qx, dc, y, s, vx: _mm(jnp.concatenate([qx * jnp.exp(dc), y], axis=1),
                                           jnp.concatenate([s, vx], axis=0)), q, d_col, qk, s_old, vn)
    tick()
    s_new = _each(lambda s, dlc, kx, dc, vx: s * jnp.exp(dlc[0:1, :]) + _mm_tn(kx * jnp.exp(dlc - dc), vx),
                  s_old, dl_col, k, d_col, vn)
    for (b, h), sx, ox in zip(idx, s_new, o):
        s_ref[b, h] = sx
        zx = pbuf[b * c:(b + 1) * c, P_Z + h * LANE:P_Z + (h + 1) * LANE]
        ybuf[b * c:(b + 1) * c, A_W + h * LANE:A_W + (h + 1) * LANE] = _dn_out(ox, zx, dnorm).astype(BF16)

    for b in range(nb):
        tick(0)
        heads = range(XA_HEADS)
        sc = [_mm_nt(pbuf[b * c:(b + 1) * c, P_XQ + h * LANE:P_XQ + (h + 1) * LANE],
                     mk_ref[b, :, h * LANE:(h + 1) * LANE]) * (XA_DH ** -0.5) for h in heads]
        e = _each(lambda x_: jnp.exp(x_ - jnp.max(x_, axis=-1, keepdims=True)), sc)
        inv = _each(lambda x_: 1.0 / jnp.sum(x_, axis=-1, keepdims=True), e)
        for h, ex, ix in zip(heads, e, inv):
            ybuf[b * c:(b + 1) * c, A_W + DN_V + h * LANE:A_W + DN_V + (h + 1) * LANE] = (
                _mm(ex, mv_ref[b, :, h * LANE:(h + 1) * LANE]) * ix).astype(BF16)
    while queue:
        tick(0)

    merged = None
    for j, (lo, hi) in enumerate(((0, A_W), (A_W, A_W + DN_V), (A_W + DN_V, A_W + DN_V + XA_W))):
        term = gbuf[:, j * D_MODEL:(j + 1) * D_MODEL] * jnp.dot(ybuf[:, lo:hi], wb_ref[lo:hi, :],
                                                                preferred_element_type=F32)
        merged = term if merged is None else merged + term
    x1 = x + jnp.dot(merged.astype(BF16), wo_ref[...], preferred_element_type=F32)
    x1_ref[...] = x1.reshape(nb, c, D_MODEL)


def _front_prompt(x, gmix, w1, w2, wab, mkb, mvb, caw, dcw, alog_row, dtb_row, dnorm, wb, wo, nb):
    bsz, length, d = x.shape
    c = CHUNK
    return pl.pallas_call(
        functools.partial(_front_prompt_kernel, nb=nb),
        grid=(bsz // nb, length // c),
        in_specs=[pl.BlockSpec((nb, c, d), lambda g, t: (g, t, 0)),
                  _resident((1, d)), _resident((d, W1)), _resident(w2.shape), _resident(wab.shape),
                  pl.BlockSpec((nb, MEM_TOKENS, XA_W), lambda g, t: (g, 0, 0), pipeline_mode=pl.Buffered(1)),
                  pl.BlockSpec((nb, MEM_TOKENS, XA_W), lambda g, t: (g, 0, 0), pipeline_mode=pl.Buffered(1)),
                  _resident((CONV_A_K, A_W)), _resident((DN_CONV_K, DN_CONV_CH)),
                  _resident((TAIL, LANE)), _resident((TAIL, LANE)), _resident((1, LANE)),
                  _resident(wb.shape), _resident(wo.shape)],
        out_specs=[pl.BlockSpec((nb, c, d), lambda g, t: (g, t, 0)),
                   pl.BlockSpec((nb, CONV_A_K - 1, A_W), lambda g, t: (g, 0, 0)),
                   pl.BlockSpec((nb, DN_CONV_K - 1, DN_CONV_CH), lambda g, t: (g, 0, 0)),
                   pl.BlockSpec((nb, DN_HEADS, DN_DK, DN_DV), lambda g, t: (g, 0, 0, 0))],
        out_shape=[jax.ShapeDtypeStruct((bsz, length, d), F32),
                   jax.ShapeDtypeStruct((bsz, CONV_A_K - 1, A_W), F32),
                   jax.ShapeDtypeStruct((bsz, DN_CONV_K - 1, DN_CONV_CH), F32),
                   jax.ShapeDtypeStruct((bsz, DN_HEADS, DN_DK, DN_DV), F32)],
        scratch_shapes=[pltpu.VMEM((nb, TAIL, A_W), F32), pltpu.VMEM((nb, TAIL, DN_CONV_CH), F32),
                        pltpu.VMEM((nb * c, A_W + DN_V + XA_W), BF16), pltpu.VMEM((nb * c, P_W), F32),
                        pltpu.VMEM((nb * c, 3 * d), F32)],
        compiler_params=_cparams(("arbitrary", "arbitrary")),
        name="front_prompt",
    )(x, gmix, w1, w2, wab, mkb, mvb, caw, dcw, alog_row, dtb_row, dnorm, wb, wo)


def _proj_kernel(x_ref, g_ref, w1_ref, w2_ref, wab_ref, p1_ref, p2_ref, pab_ref, xq_ref):
    xn = _rms(x_ref[...], g_ref[...]).astype(BF16)
    p1_ref[...] = jnp.dot(xn, w1_ref[...], preferred_element_type=F32)
    p2 = jnp.dot(xn, w2_ref[...], preferred_element_type=F32)
    p2_ref[...] = p2
    pab_ref[...] = jnp.dot(xn, wab_ref[...], preferred_element_type=F32)
    for h in range(XA_HEADS):
        xq_ref[h] = p2[:, h * LANE:(h + 1) * LANE]


def _proj(x2d, gain, w1, w2, wab, tm):
    t, d = x2d.shape
    row = lambda n: pl.BlockSpec((tm, n), lambda i: (i, 0))
    return pl.pallas_call(
        _proj_kernel,
        grid=(t // tm,),
        in_specs=[row(d), _resident((1, d)), _resident((d, W1)), _resident(w2.shape), _resident(wab.shape)],
        out_specs=[row(W1), row(W2), row(LANE), pl.BlockSpec((XA_HEADS, tm, LANE), lambda i: (0, i, 0))],
        out_shape=[jax.ShapeDtypeStruct((t, W1), F32), jax.ShapeDtypeStruct((t, W2), F32),
                   jax.ShapeDtypeStruct((t, LANE), F32), jax.ShapeDtypeStruct((XA_HEADS, t, LANE), F32)],
        compiler_params=_cparams(("arbitrary",)),
        name="proj",
    )(x2d, gain, w1, w2, wab)


SEQ_S = 4
NB_S = CHUNK // SEQ_S
NB_ATTN_S = 16


def _seg_conv(x, e, wts, width):
    rows, w = x.shape
    xt = x.reshape(rows // TAIL, TAIL, w)
    et = e.reshape(rows // TAIL, TAIL, w)
    tmod = lax.broadcasted_iota(jnp.int32, (1, TAIL, 1), 1) & (SEQ_S - 1)
    acc = None
    for i in range(width):
        s = width - 1 - i
        term = xt if s == 0 else jnp.where(tmod >= s, pltpu.roll(xt, s, 1), 0.0)
        if i < width - 1:
            hist = et if i == 0 else pltpu.roll(et, TAIL - i, 1)
            term = term + jnp.where(tmod < SEQ_S - i, hist, 0.0)
        term = wts[i:i + 1][None] * term
        acc = term if acc is None else acc + term
    return acc.reshape(rows, w)


def _branch_sample_kernel(pa_ref, pq_ref, pz_ref, pab_ref, ea_ref, eq_ref, s0_ref, caw_ref, dcw_ref,
                          alog_ref, dtb_ref, dnorm_ref, mexp2_ref, mexpt_ref, yad_ref, u_ref, s_ref):
    c = CHUNK
    m = _dn_masks(SEQ_S)

    pa = pa_ref[...]
    u_in = pa[:, A_W:2 * A_W] * pa[:, 2 * A_W:3 * A_W]
    u_ref[...] = u_in
    conv = _seg_conv(u_in, ea_ref[...], caw_ref[...], CONV_A_K)
    yad_ref[:, 0:A_W] = (pa[:, 0:A_W] * conv).astype(BF16)

    qkv = _silu(_seg_conv(pq_ref[...], eq_ref[...], dcw_ref[...], DN_CONV_K))
    d, dl, cols = _dn_gates(pab_ref[...], alog_ref[...], dtb_ref[...], SEQ_S)
    dec_t = jnp.exp(dl)
    dnorm = dnorm_ref[...]

    wide = NB_S * DN_DK
    mexp2 = mexp2_ref[...]
    mexp_t = mexpt_ref[...]

    q, k, v, z, d_col, d_row, dl_col, beta_col = _head_lists(qkv, pz_ref[...], d, cols)
    q, k, u, w, qk = _dn_intra(q, k, v, d_col, d_row, beta_col, m, SEQ_S)
    heads = list(range(DN_HEADS))
    s_old = [s0_ref[:, h].reshape(wide, DN_DV) for h in heads]
    x_exp = _each(lambda wx, qx, dc: jnp.concatenate(
        [jnp.concatenate([wx, qx * jnp.exp(dc)], axis=0).astype(BF16)] * NB_S, axis=1) * mexp2, w, q, d_col)
    ws = _each(_mm, x_exp, s_old)
    vn = _each(lambda ux, x: ux - x[:c], u, ws)
    o = _each(lambda x, y, vx: x[c:] + _mm(y, vx), ws, qk, vn)
    k_exp = _each(lambda kx, dlc, dc: jnp.concatenate(
        [(kx * jnp.exp(dlc - dc)).T.astype(BF16)] * NB_S, axis=0) * mexp_t, k, dl_col, d_col)
    dec = [jnp.concatenate([jnp.broadcast_to(dec_t[h:h + 1, SEQ_S * b:SEQ_S * b + 1], (DN_DK, DN_DV))
                            for b in range(NB_S)], axis=0) for h in heads]
    s_new = _each(lambda s, dx, kx, vx: s * dx + _mm(kx, vx), s_old, dec, k_exp, vn)
    for h, sx, ox, zx in zip(heads, s_new, o, z):
        s_ref[:, h] = sx.reshape(NB_S, DN_DK, DN_DV)
        yad_ref[:, A_W + h * LANE:A_W + (h + 1) * LANE] = _dn_out(ox, zx, dnorm).astype(BF16)


def _branch_sample(p1, pab, ea, eq, state, caw, dcw, alog_row, dtb_row, dnorm):
    t = p1.shape[0]
    c = CHUNK
    full = lambda shape: pl.BlockSpec(shape, lambda i: (0,) * len(shape))
    wide = NB_S * DN_DK
    owner = np.arange(wide) // DN_DK
    seq = (np.arange(2 * c) % c) // SEQ_S
    mexp2 = jnp.asarray(seq[:, None] == owner[None, :], BF16)
    mexp_t = jnp.asarray(owner[:, None] == seq[None, :c], BF16)
    return pl.pallas_call(
        _branch_sample_kernel,
        grid=(t // c,),
        in_specs=[pl.BlockSpec((c, 3 * A_W), lambda i: (i, 0)),
                  pl.BlockSpec((c, DN_CONV_CH), lambda i: (i, OFF_QKV // DN_CONV_CH)),
                  pl.BlockSpec((c, DN_V), lambda i: (i, OFF_Z // DN_V)),
                  pl.BlockSpec((c, LANE), lambda i: (i, 0)),
                  pl.BlockSpec((c, A_W), lambda i: (i, 0)),
                  pl.BlockSpec((c, DN_CONV_CH), lambda i: (i, 0)),
                  pl.BlockSpec((NB_S, DN_HEADS, DN_DK, DN_DV), lambda i: (i, 0, 0, 0)),
                  full((CONV_A_K, A_W)), full((DN_CONV_K, DN_CONV_CH)),
                  full((TAIL, LANE)), full((TAIL, LANE)), full((1, LANE)),
                  _resident(mexp2.shape), _resident(mexp_t.shape)],
        out_specs=[pl.BlockSpec((c, A_W + DN_V), lambda i: (i, 0)),
                   pl.BlockSpec((c, A_W), lambda i: (i, 0)),
                   pl.BlockSpec((NB_S, DN_HEADS, DN_DK, DN_DV), lambda i: (i, 0, 0, 0))],
        out_shape=[jax.ShapeDtypeStruct((t, A_W + DN_V), BF16),
                   jax.ShapeDtypeStruct((t, A_W), F32),
                   jax.ShapeDtypeStruct(state.shape, F32)],
        compiler_params=_cparams(("arbitrary",)),
        name="branch_sample",
    )(p1, p1, p1, pab, ea, eq, state, caw, dcw, alog_row, dtb_row, dnorm, mexp2, mexp_t)


def _attn_sample_kernel(q_ref, k_ref, v_ref, o_ref):
    for h in range(XA_HEADS):
        rows = pl.ds(h, MEM_TOKENS, stride=XA_HEADS)
        s = jnp.einsum("bqd,bkd->bqk", q_ref[h].astype(BF16), k_ref[:, rows, :].astype(BF16),
                       preferred_element_type=F32) * (XA_DH ** -0.5)
        e = jnp.exp(s - jnp.max(s, axis=-1, keepdims=True))
        p = e / jnp.sum(e, axis=-1, keepdims=True)
        o_ref[h] = jnp.einsum("bqk,bkd->bqd", p.astype(BF16), v_ref[:, rows, :].astype(BF16),
                              preferred_element_type=F32)


def _attn_sample(q4, ck, cv, nb):
    _, bsz, length, _ = q4.shape
    head_major = pl.BlockSpec((XA_HEADS, nb, length, XA_DH), lambda i: (0, i, 0, 0))
    return pl.pallas_call(
        _attn_sample_kernel,
        grid=(bsz // nb,),
        in_specs=[head_major,
                  pl.BlockSpec((nb, MEM_TOKENS * XA_HEADS, XA_DH), lambda i: (i, 0, 0)),
                  pl.BlockSpec((nb, MEM_TOKENS * XA_HEADS, XA_DH), lambda i: (i, 0, 0))],
        out_specs=head_major,
        out_shape=jax.ShapeDtypeStruct(q4.shape, F32),
        compiler_params=_cparams(("arbitrary",)),
        name="attn_sample",
    )(q4, ck, cv)


def _merge_kernel(yad_ref, ym_ref, p2_ref, x_ref, wb_ref, wo_ref, o_ref):
    yad = yad_ref[...]
    ym = jnp.concatenate([ym_ref[h] for h in range(XA_HEADS)], axis=1).astype(BF16)
    gate = lambda j: jax.nn.sigmoid(p2_ref[:, OFF_G + j * D_MODEL:OFF_G + (j + 1) * D_MODEL])
    merged = (gate(0) * jnp.dot(yad[:, :A_W], wb_ref[0:A_W, :], preferred_element_type=F32)
              + gate(1) * jnp.dot(yad[:, A_W:], wb_ref[A_W:A_W + DN_V, :], preferred_element_type=F32)
              + gate(2) * jnp.dot(ym, wb_ref[A_W + DN_V:, :], preferred_element_type=F32))
    o_ref[...] = x_ref[...] + jnp.dot(merged.astype(BF16), wo_ref[...], preferred_element_type=F32)


MXU_K = 256
FF_EDGES = (0, 6 * MXU_K, D_FF)
FF_SPLIT = len(FF_EDGES) - 1
FF_SUB = 256
TM_FF = 4 * FF_SUB


def _ffn_kernel(x_ref, gf_ref, wu_ref, wd_ref, gl_ref, o_ref):
    n_sub = x_ref.shape[0] // FF_SUB

    def block(xn, acc, j):
        lo, hi = FF_EDGES[j], FF_EDGES[j + 1]
        gate = jnp.dot(xn, wu_ref[:, lo:hi], preferred_element_type=F32)
        up = jnp.dot(xn, wu_ref[:, D_FF + lo:D_FF + hi], preferred_element_type=F32)
        hid = (_silu(gate) * up).astype(BF16)
        return acc + jnp.dot(hid, wd_ref[lo:hi, :], preferred_element_type=F32)

    state = [None] * n_sub
    for i in range(n_sub + 1):
        if i < n_sub:
            x = x_ref[i * FF_SUB:(i + 1) * FF_SUB, :]
            xn = _rms(x, gf_ref[...]).astype(BF16)
            state[i] = (xn, block(xn, x, 0))
        if i > 0:
            xn, acc = state[i - 1]
            for j in range(1, FF_SPLIT):
                acc = block(xn, acc, j)
            o_ref[(i - 1) * FF_SUB:i * FF_SUB, :] = _rms(acc, gl_ref[...])


STAGE_U = 64
STAGE_D = D_FF // 8


def _ffn_prompt_kernel(x_ref, gf_ref, wu_hbm, wd_hbm, gl_ref, o_ref, wub_hbm, wdb_hbm,
                       wu_s, wd_s, stg_u, stg_d, sem):
    first = pl.program_id(0) == 0
    ns = STAGE_SLOTS
    writebacks = ((wu_s, wub_hbm, 2 * ns), (wd_s, wdb_hbm, 2 * ns + 1))

    @pl.when(first)
    def _stage_weights():
        jobs = ([(wu_hbm, wu_s, stg_u, 0, r * STAGE_U, STAGE_U) for r in range(wu_s.shape[0] // STAGE_U)]
                + [(wd_hbm, wd_s, stg_d, ns, r * STAGE_D, STAGE_D) for r in range(wd_s.shape[0] // STAGE_D)])

        def copy(n):
            src, _, stg, s0, lo, sz = jobs[n]
            return pltpu.make_async_copy(src.at[pl.ds(lo, sz), :], stg.at[n % ns], sem.at[s0 + n % ns])

        for n in range(min(ns, len(jobs))):
            copy(n).start()
        for n, (_, dst, stg, _, lo, sz) in enumerate(jobs):
            copy(n).wait()
            dst[lo:lo + sz, :] = stg[n % ns].astype(BF16)
            if n + ns < len(jobs):
                copy(n + ns).start()
        for src, dst, s in writebacks:
            pltpu.make_async_copy(src, dst, sem.at[s]).start()

    _ffn_kernel(x_ref, gf_ref, wu_s, wd_s, gl_ref, o_ref)

    @pl.when(first)
    def _finish_writeback():
        for src, dst, s in writebacks:
            pltpu.make_async_copy(src, dst, sem.at[s]).wait()


def _ffn(x2d, gf, wu32, wd32, gl, tm):
    t, d = x2d.shape
    assert wu32.shape[0] % (STAGE_SLOTS * STAGE_U) == 0 and wd32.shape[0] % (STAGE_SLOTS * STAGE_D) == 0
    anywhere = pl.BlockSpec(memory_space=pl.ANY)
    return pl.pallas_call(
        _ffn_prompt_kernel,
        grid=(t // tm,),
        in_specs=[pl.BlockSpec((tm, d), lambda i: (i, 0)), _resident((1, d)), anywhere, anywhere,
                  _resident((1, d))],
        out_specs=[pl.BlockSpec((tm, d), lambda i: (i, 0)), anywhere, anywhere],
        out_shape=[jax.ShapeDtypeStruct((t, d), F32), jax.ShapeDtypeStruct(wu32.shape, BF16),
                   jax.ShapeDtypeStruct(wd32.shape, BF16)],
        scratch_shapes=[pltpu.VMEM(wu32.shape, BF16), pltpu.VMEM(wd32.shape, BF16),
                        pltpu.VMEM((STAGE_SLOTS, STAGE_U, wu32.shape[1]), F32),
                        pltpu.VMEM((STAGE_SLOTS, STAGE_D, wd32.shape[1]), F32),
                        pltpu.SemaphoreType.DMA((2 * STAGE_SLOTS + 2,))],
        compiler_params=_cparams(("arbitrary",)),
        name="ffn",
    )(x2d, gf, wu32, wd32, gl)


def _tail_sample_kernel(yad_ref, ym_ref, p2_ref, x_ref, wb_ref, wo_ref, gf_ref, wu_hbm, wd_hbm, gl_ref,
                        o_ref, x1_buf, wu_s, wd_s, sem):
    fetches = [pltpu.make_async_copy(wu_hbm, wu_s, sem.at[0]), pltpu.make_async_copy(wd_hbm, wd_s, sem.at[1])]
    for f in fetches:
        f.start()
    _merge_kernel(yad_ref, ym_ref, p2_ref, x_ref, wb_ref, wo_ref, x1_buf)
    for f in fetches:
        f.wait()
    _ffn_kernel(x1_buf, gf_ref, wu_s, wd_s, gl_ref, o_ref)


def _tail_sample(yad, ym, p2, x2d, wb, wo, gf, wu, wd, gl):
    t, d = x2d.shape
    args = (yad, ym, p2, x2d, wb, wo, gf, wu, wd, gl)
    return pl.pallas_call(
        _tail_sample_kernel,
        grid=(1,),
        in_specs=[pl.BlockSpec(memory_space=pl.ANY) if a is wu or a is wd else _resident(a.shape) for a in args],
        out_specs=pl.BlockSpec((t, d), lambda i: (0, 0)),
        out_shape=jax.ShapeDtypeStruct((t, d), F32),
        scratch_shapes=[pltpu.VMEM((t, d), F32), pltpu.VMEM(wu.shape, BF16), pltpu.VMEM(wd.shape, BF16),
                        pltpu.SemaphoreType.DMA((2,))],
        compiler_params=_cparams(("arbitrary",)),
        name="tail_sample",
    )(*args)


def _head_rows(v):
    col = jnp.zeros((TAIL, 1), F32).at[:v.shape[0], 0].set(v.astype(F32))
    return jnp.broadcast_to(col, (TAIL, LANE))


def kernel(x_prompt, x_sample, mem_prompt, state_conv_a, state_dn_conv, state_dn, cache_mem_k, cache_mem_v,
           norm_mix, w_in, conv_a_w, dn_conv_w, dn_a_log, dn_dt_bias, dn_norm, norm_mem, w_mem_kv, w_branch,
           w_o, norm_ffn, w_ffn_up, w_ffn_down, norm_final):
    bp, lp, d = x_prompt.shape
    bs, ls, _ = x_sample.shape
    assert norm_mix.shape[0] == 1 and ls == SEQ_S and lp % CHUNK == 0 and (bs * ls) % CHUNK == 0
    assert w_in.shape[2] == W1 + N_AB + W2

    w = w_in[0]
    w1 = w.astype(BF16)
    w2 = w1[:, W1 + N_AB:]
    wab = jnp.pad(w1[:, W1:W1 + N_AB], ((0, 0), (0, LANE - N_AB)))
    g_mix = norm_mix[0][None, :]
    g_ffn = norm_ffn[0][None, :]
    g_fin = norm_final[None, :]
    g_mem = norm_mem[0][None, :]
    caw = conv_a_w[0]
    dcw = dn_conv_w[0]
    alog_row = _head_rows(dn_a_log[0])
    dtb_row = _head_rows(dn_dt_bias[0])
    dnorm = dn_norm[0][None, :]

    tp = bp * lp
    mk, mv, mkb, mvb, wb, wo = _memkv(mem_prompt.reshape(bp * MEM_TOKENS, d), g_mem, w_mem_kv[0], w_branch[0],
                                      w_o[0], TM)
    x1_p, ca_p, dc_p, s_p = _front_prompt(x_prompt, g_mix, w1, w2, wab, mkb.reshape(bp, MEM_TOKENS, XA_W),
                                          mvb.reshape(bp, MEM_TOKENS, XA_W), caw, dcw, alog_row, dtb_row, dnorm,
                                          wb, wo, NB_P)

    ts = bs * ls
    xs2 = x_sample.reshape(ts, d)
    p1_s, p2_s, pab_s, xq_s = _proj(xs2, g_mix, w1, w2, wab, CHUNK)
    ea = jnp.pad(state_conv_a[0], ((0, 0), (0, ls - (CONV_A_K - 1)), (0, 0))).reshape(ts, A_W)
    eq = jnp.pad(state_dn_conv[0], ((0, 0), (0, ls - (DN_CONV_K - 1)), (0, 0))).reshape(ts, DN_CONV_CH)
    yad_s, u_s, s_s = _branch_sample(p1_s, pab_s, ea, eq, state_dn[0], caw, dcw, alog_row, dtb_row, dnorm)
    ym_s = _attn_sample(xq_s.reshape(XA_HEADS, bs, ls, XA_DH),
                        cache_mem_k.reshape(bs, MEM_TOKENS * XA_HEADS, XA_DH),
                        cache_mem_v.reshape(bs, MEM_TOKENS * XA_HEADS, XA_DH), NB_ATTN_S)
    ca_s = u_s.reshape(bs, ls, A_W)[:, ls - (CONV_A_K - 1):]
    dc_s = p1_s[:, OFF_QKV:OFF_Z].reshape(bs, ls, DN_CONV_CH)[:, ls - (DN_CONV_K - 1):]

    y_p, wu, wd = _ffn(x1_p.reshape(tp, d), g_ffn, w_ffn_up[0], w_ffn_down[0], g_fin, TM_FF)
    y_p = y_p.reshape(bp, lp, d)
    y_s = _tail_sample(yad_s, ym_s.reshape(XA_HEADS, ts, XA_DH), p2_s, xs2, wb, wo, g_ffn, wu, wd, g_fin)
    y_s = y_s.reshape(bs, ls, d)

    return (y_p, y_s, ca_p[None], dc_p[None], s_p[None],
            mk.reshape(1, bp, MEM_TOKENS, XA_HEADS, XA_DH), mv.reshape(1, bp, MEM_TOKENS, XA_HEADS, XA_DH),
            ca_s[None], dc_s[None], s_s[None])
```
